```python
import jax, jax.numpy as jnp
from jax import lax
import numpy as np

D_MODEL = 1024
BATCH = 8
SEQ = 16384
DEPTH = 4

N_MIXERS = 2
N_HEADS = 16
HEAD_DIM = D_MODEL // N_HEADS
Q_BLOCK = 128
SGU_CHUNK = 128
SGU_WIDTH = 2 * D_MODEL
SGU_GROUPS = 16
SGU_GROUP_DIM = SGU_WIDTH // SGU_GROUPS
FFN_HIDDEN = ((8 * D_MODEL // 3 + 255) // 256) * 256
N_ATTN_LAYERS = (DEPTH + 1) // 2
N_SGU_LAYERS = DEPTH // 2
NORM_EPS = 1e-6
LN_EPS = 1e-5

kernel_name = "fox_gmlp_interleaved_hybrid"


def rms_norm(x, w):
    xf = x.astype(jnp.float32)
    y = xf * lax.rsqrt(jnp.mean(xf * xf, axis=-1, keepdims=True) + NORM_EPS)
    return (y * w.astype(jnp.float32)).astype(x.dtype)


def forgetting_attention(h, w_in, b_f, w_out):
    B, S, _ = h.shape
    proj = h @ w_in
    q, k, v, f_logit = jnp.split(proj, [D_MODEL, 2 * D_MODEL, 3 * D_MODEL], axis=-1)
    q = q.reshape(B, S, N_HEADS, HEAD_DIM).transpose(0, 2, 1, 3)
    k = k.reshape(B, S, N_HEADS, HEAD_DIM).transpose(0, 2, 1, 3)
    v = v.reshape(B, S, N_HEADS, HEAD_DIM).transpose(0, 2, 1, 3)
    log_f = jax.nn.log_sigmoid((f_logit + b_f).astype(jnp.float32))
    c = jnp.cumsum(log_f, axis=1).transpose(0, 2, 1)
    n_blk = S // Q_BLOCK
    q_blk = q.reshape(B, N_HEADS, n_blk, Q_BLOCK, HEAD_DIM).transpose(2, 0, 1, 3, 4)
    c_blk = c.reshape(B, N_HEADS, n_blk, Q_BLOCK).transpose(2, 0, 1, 3)
    k_pos = jnp.arange(S)
    scale = HEAD_DIM ** -0.5

    def attend(args):
        qb, cb, start = args
        s = jnp.einsum('bhqd,bhkd->bhqk', qb, k, preferred_element_type=jnp.float32) * scale
        s = s + cb[..., :, None] - c[:, :, None, :]
        q_pos = start + jnp.arange(Q_BLOCK)
        s = jnp.where(k_pos[None, :] <= q_pos[:, None], s, -jnp.inf)
        p = jax.nn.softmax(s, axis=-1).astype(v.dtype)
        return jnp.einsum('bhqk,bhkd->bhqd', p, v)

    o = lax.map(attend, (q_blk, c_blk, jnp.arange(n_blk) * Q_BLOCK))
    o = o.transpose(1, 0, 3, 2, 4).reshape(B, S, D_MODEL)
    return o @ w_out


def spatial_gating_mixer(h, w_in, ln_g, ln_b, w_s, b_s, w_out):
    B, S, _ = h.shape
    z = jax.nn.gelu(h @ w_in, approximate=False)
    u, v = jnp.split(z, 2, axis=-1)
    vf = v.astype(jnp.float32)
    mu = jnp.mean(vf, axis=-1, keepdims=True)
    var = jnp.mean(jnp.square(vf - mu), axis=-1, keepdims=True)
    vn = ((vf - mu) * lax.rsqrt(var + LN_EPS) * ln_g.astype(jnp.float32)
          + ln_b.astype(jnp.float32)).astype(v.dtype)
    vc = vn.reshape(B, S // SGU_CHUNK, SGU_CHUNK, SGU_GROUPS, SGU_GROUP_DIM)
    w_causal = jnp.tril(w_s)
    mixed = jnp.einsum('gts,bcsgd->bctgd', w_causal, vc) + b_s.T[:, :, None]
    gated = u * mixed.reshape(B, S, SGU_WIDTH)
    return gated @ w_out


def swiglu_ffn(h, w_in, w_out):
    g, u = jnp.split(h @ w_in, 2, axis=-1)
    return (jax.nn.silu(g) * u) @ w_out


def _fwd_setup_inputs(seed: int = 0) -> dict:
    key = jax.random.key(seed)
    ks = jax.random.split(key, 20)
    f32 = jnp.float32
    D = D_MODEL

    def nrm(k, shape, fan_in):
        return jax.random.normal(k, shape, f32) * (fan_in ** -0.5)

    def gain(k, shape):
        return 1.0 + 0.05 * jax.random.normal(k, shape, f32)

    x = jax.random.normal(ks[0], (BATCH, SEQ, D), f32)
    mixer_norm_w = gain(ks[1], (DEPTH, D))
    attn_w_in = nrm(ks[2], (N_ATTN_LAYERS, D, 3 * D + N_HEADS), D)
    attn_b_f = jax.random.uniform(ks[3], (N_ATTN_LAYERS, N_HEADS), f32, 1.0, 6.0)
    attn_w_out = nrm(ks[4], (N_ATTN_LAYERS, D, D), D)
    sgu_w_in = nrm(ks[5], (N_SGU_LAYERS, D, 2 * SGU_WIDTH), D)
    sgu_ln_g = gain(ks[6], (N_SGU_LAYERS, SGU_WIDTH))
    sgu_ln_b = 0.02 * jax.random.normal(ks[7], (N_SGU_LAYERS, SGU_WIDTH), f32)
    sgu_w_s = nrm(ks[8], (N_SGU_LAYERS, SGU_GROUPS, SGU_CHUNK, SGU_CHUNK), SGU_CHUNK)
    sgu_b_s = 1.0 + 0.1 * jax.random.normal(ks[9], (N_SGU_LAYERS, SGU_GROUPS, SGU_CHUNK), f32)
    sgu_w_out = nrm(ks[10], (N_SGU_LAYERS, SGU_WIDTH, D), SGU_WIDTH)
    ffn_norm_w = gain(ks[11], (DEPTH, D))
    ffn_w_in = nrm(ks[12], (DEPTH, D, 2 * FFN_HIDDEN), D)
    ffn_w_out = nrm(ks[13], (DEPTH, FFN_HIDDEN, D), FFN_HIDDEN)
    final_norm_w = gain(ks[14], (D,))
    return {"x": x, "mixer_norm_w": mixer_norm_w, "attn_w_in": attn_w_in,
            "attn_b_f": attn_b_f, "attn_w_out": attn_w_out, "sgu_w_in": sgu_w_in,
            "sgu_ln_g": sgu_ln_g, "sgu_ln_b": sgu_ln_b, "sgu_w_s": sgu_w_s,
            "sgu_b_s": sgu_b_s, "sgu_w_out": sgu_w_out, "ffn_norm_w": ffn_norm_w,
            "ffn_w_in": ffn_w_in, "ffn_w_out": ffn_w_out, "final_norm_w": final_norm_w}


def _fwd_reference(x, mixer_norm_w, attn_w_in, attn_b_f, attn_w_out, sgu_w_in, sgu_ln_g,
              sgu_ln_b, sgu_w_s, sgu_b_s, sgu_w_out, ffn_norm_w, ffn_w_in, ffn_w_out,
              final_norm_w):
    for i in range(DEPTH):
        h = rms_norm(x, mixer_norm_w[i])
        j = i // N_MIXERS
        if i % N_MIXERS == 0:
            x = x + forgetting_attention(h, attn_w_in[j], attn_b_f[j], attn_w_out[j])
        else:
            x = x + spatial_gating_mixer(h, sgu_w_in[j], sgu_ln_g[j], sgu_ln_b[j],
                                         sgu_w_s[j], sgu_b_s[j], sgu_w_out[j])
        x = x + swiglu_ffn(rms_norm(x, ffn_norm_w[i]), ffn_w_in[i], ffn_w_out[i])
    return rms_norm(x, final_norm_w)


import jax as _jax
import jax.numpy as _jnp

TWIN_FORMAT = 'train_step'
FWD_PARAMS = ['x', 'mixer_norm_w', 'attn_w_in', 'attn_b_f', 'attn_w_out', 'sgu_w_in', 'sgu_ln_g', 'sgu_ln_b', 'sgu_w_s', 'sgu_b_s', 'sgu_w_out', 'ffn_norm_w', 'ffn_w_in', 'ffn_w_out', 'final_norm_w']
TWIN_WEIGHTS = ['mixer_norm_w', 'attn_w_in', 'attn_b_f', 'attn_w_out', 'sgu_w_in', 'sgu_ln_g', 'sgu_ln_b', 'sgu_w_s', 'sgu_b_s', 'sgu_w_out', 'ffn_norm_w', 'ffn_w_in', 'ffn_w_out', 'final_norm_w']
TWIN_DIFF_INPUT = 'x'
TWIN_INPUTS = ['x', 'mixer_norm_w', 'attn_w_in', 'attn_b_f', 'attn_w_out', 'sgu_w_in', 'sgu_ln_g', 'sgu_ln_b', 'sgu_w_s', 'sgu_b_s', 'sgu_w_out', 'ffn_norm_w', 'ffn_w_in', 'ffn_w_out', 'final_norm_w', 'loss_target', 'm_mixer_norm_w', 'm_attn_w_in', 'm_attn_b_f', 'm_attn_w_out', 'm_sgu_w_in', 'm_sgu_ln_g', 'm_sgu_ln_b', 'm_sgu_w_s', 'm_sgu_b_s', 'm_sgu_w_out', 'm_ffn_norm_w', 'm_ffn_w_in', 'm_ffn_w_out', 'm_final_norm_w', 'v_mixer_norm_w', 'v_attn_w_in', 'v_attn_b_f', 'v_attn_w_out', 'v_sgu_w_in', 'v_sgu_ln_g', 'v_sgu_ln_b', 'v_sgu_w_s', 'v_sgu_b_s', 'v_sgu_w_out', 'v_ffn_norm_w', 'v_ffn_w_in', 'v_ffn_w_out', 'v_final_norm_w']
TWIN_OUTPUTS = ['loss', 'grad_x', 'grad_mixer_norm_w', 'grad_attn_w_in', 'grad_attn_b_f', 'grad_attn_w_out', 'grad_sgu_w_in', 'grad_sgu_ln_g', 'grad_sgu_ln_b', 'grad_sgu_w_s', 'grad_sgu_b_s', 'grad_sgu_w_out', 'grad_ffn_norm_w', 'grad_ffn_w_in', 'grad_ffn_w_out', 'grad_final_norm_w', 'delta_mixer_norm_w', 'delta_attn_w_in', 'delta_attn_b_f', 'delta_attn_w_out', 'delta_sgu_w_in', 'delta_sgu_ln_g', 'delta_sgu_ln_b', 'delta_sgu_w_s', 'delta_sgu_b_s', 'delta_sgu_w_out', 'delta_ffn_norm_w', 'delta_ffn_w_in', 'delta_ffn_w_out', 'delta_final_norm_w', 'new_m_mixer_norm_w', 'new_m_attn_w_in', 'new_m_attn_b_f', 'new_m_attn_w_out', 'new_m_sgu_w_in', 'new_m_sgu_ln_g', 'new_m_sgu_ln_b', 'new_m_sgu_w_s', 'new_m_sgu_b_s', 'new_m_sgu_w_out', 'new_m_ffn_norm_w', 'new_m_ffn_w_in', 'new_m_ffn_w_out', 'new_m_final_norm_w', 'new_v_mixer_norm_w', 'new_v_attn_w_in', 'new_v_attn_b_f', 'new_v_attn_w_out', 'new_v_sgu_w_in', 'new_v_sgu_ln_g', 'new_v_sgu_ln_b', 'new_v_sgu_w_s', 'new_v_sgu_b_s', 'new_v_sgu_w_out', 'new_v_ffn_norm_w', 'new_v_ffn_w_in', 'new_v_ffn_w_out', 'new_v_final_norm_w']
TWIN_LEAF_KINDS = {'loss': 'loss', 'grad_x': 'grad_x', 'grad_mixer_norm_w': 'grad_w', 'grad_attn_w_in': 'grad_w', 'grad_attn_b_f': 'grad_w', 'grad_attn_w_out': 'grad_w', 'grad_sgu_w_in': 'grad_w', 'grad_sgu_ln_g': 'grad_w', 'grad_sgu_ln_b': 'grad_w', 'grad_sgu_w_s': 'grad_w', 'grad_sgu_b_s': 'grad_w', 'grad_sgu_w_out': 'grad_w', 'grad_ffn_norm_w': 'grad_w', 'grad_ffn_w_in': 'grad_w', 'grad_ffn_w_out': 'grad_w', 'grad_final_norm_w': 'grad_w', 'delta_mixer_norm_w': 'delta_w', 'delta_attn_w_in': 'delta_w', 'delta_attn_b_f': 'delta_w', 'delta_attn_w_out': 'delta_w', 'delta_sgu_w_in': 'delta_w', 'delta_sgu_ln_g': 'delta_w', 'delta_sgu_ln_b': 'delta_w', 'delta_sgu_w_s': 'delta_w', 'delta_sgu_b_s': 'delta_w', 'delta_sgu_w_out': 'delta_w', 'delta_ffn_norm_w': 'delta_w', 'delta_ffn_w_in': 'delta_w', 'delta_ffn_w_out': 'delta_w', 'delta_final_norm_w': 'delta_w', 'new_m_mixer_norm_w': 'new_m', 'new_m_attn_w_in': 'new_m', 'new_m_attn_b_f': 'new_m', 'new_m_attn_w_out': 'new_m', 'new_m_sgu_w_in': 'new_m', 'new_m_sgu_ln_g': 'new_m', 'new_m_sgu_ln_b': 'new_m', 'new_m_sgu_w_s': 'new_m', 'new_m_sgu_b_s': 'new_m', 'new_m_sgu_w_out': 'new_m', 'new_m_ffn_norm_w': 'new_m', 'new_m_ffn_w_in': 'new_m', 'new_m_ffn_w_out': 'new_m', 'new_m_final_norm_w': 'new_m', 'new_v_mixer_norm_w': 'new_v', 'new_v_attn_w_in': 'new_v', 'new_v_attn_b_f': 'new_v', 'new_v_attn_w_out': 'new_v', 'new_v_sgu_w_in': 'new_v', 'new_v_sgu_ln_g': 'new_v', 'new_v_sgu_ln_b': 'new_v', 'new_v_sgu_w_s': 'new_v', 'new_v_sgu_b_s': 'new_v', 'new_v_sgu_w_out': 'new_v', 'new_v_ffn_norm_w': 'new_v', 'new_v_ffn_w_in': 'new_v', 'new_v_ffn_w_out': 'new_v', 'new_v_final_norm_w': 'new_v'}


def _forward(args):
    return _fwd_reference(*[args[k] for k in FWD_PARAMS])


def _output_shape():
    def fwd():
        inp = _fwd_setup_inputs(0)
        return _fwd_reference(*[inp[k] for k in FWD_PARAMS])
    out = _jax.eval_shape(fwd)
    return out.shape, out.dtype

N_MICROBATCH = 1
ADAM_LR = 0.001
ADAM_B1 = 0.9
ADAM_B2 = 0.999
ADAM_EPS = 1e-08
ADAM_WD = 0.01
ADAM_STEP = 10
PER_EXAMPLE_BATCH_AXIS = {'x': 0, 'loss_target': 0}
SHARED_INPUTS = []
_WEIGHT_DTYPES = {'mixer_norm_w': _jnp.float32, 'attn_w_in': _jnp.float32, 'attn_b_f': _jnp.float32, 'attn_w_out': _jnp.float32, 'sgu_w_in': _jnp.float32, 'sgu_ln_g': _jnp.float32, 'sgu_ln_b': _jnp.float32, 'sgu_w_s': _jnp.float32, 'sgu_b_s': _jnp.float32, 'sgu_w_out': _jnp.float32, 'ffn_norm_w': _jnp.float32, 'ffn_w_in': _jnp.float32, 'ffn_w_out': _jnp.float32, 'final_norm_w': _jnp.float32}
MOMENT_SCALE = {'mixer_norm_w': 2.297073e-01, 'attn_w_in': 1.209924e-01, 'attn_b_f': 1.327126e+00, 'attn_w_out': 1.474780e-01, 'sgu_w_in': 1.316287e-01, 'sgu_ln_g': 8.670977e-02, 'sgu_ln_b': 8.759905e-02, 'sgu_w_s': 8.402935e-02, 'sgu_b_s': 1.270097e-01, 'sgu_w_out': 3.079239e-01, 'ffn_norm_w': 2.512439e-01, 'ffn_w_in': 1.063438e-01, 'ffn_w_out': 1.745893e-01, 'final_norm_w': 1.285774e+02}


def _to_microbatches(a, axis):
    t = _jnp.moveaxis(a, axis, 0)
    t = t.reshape((N_MICROBATCH, t.shape[0] // N_MICROBATCH) + t.shape[1:])
    return _jnp.moveaxis(t, 1, axis + 1)


def setup_inputs(seed: int = 0) -> dict:
    inp = _fwd_setup_inputs(seed)
    key = _jax.random.fold_in(_jax.random.key(seed), 7919)
    shape, _ = _output_shape()
    out = dict(inp)
    out["loss_target"] = _jax.random.normal(_jax.random.fold_in(key, 0), shape, _jnp.float32)
    for i, name in enumerate(TWIN_WEIGHTS):
        w = inp[name].astype(_jnp.float32)
        if MOMENT_SCALE is None:
            s = _jnp.sqrt(_jnp.mean(_jnp.square(w)) + 1e-30)
        else:
            s = MOMENT_SCALE[name]
        km, kv = _jax.random.split(_jax.random.fold_in(key, i + 1))
        out[name] = w
        out["m_" + name] = s * _jax.random.normal(km, w.shape, _jnp.float32)
        out["v_" + name] = (s * s) * _jax.random.uniform(kv, w.shape, _jnp.float32, 0.5, 1.5)
    if N_MICROBATCH > 1:
        for name, axis in PER_EXAMPLE_BATCH_AXIS.items():
            out[name] = _to_microbatches(out[name], axis)
    return {'x': out['x'], 'mixer_norm_w': out['mixer_norm_w'], 'attn_w_in': out['attn_w_in'], 'attn_b_f': out['attn_b_f'], 'attn_w_out': out['attn_w_out'], 'sgu_w_in': out['sgu_w_in'], 'sgu_ln_g': out['sgu_ln_g'], 'sgu_ln_b': out['sgu_ln_b'], 'sgu_w_s': out['sgu_w_s'], 'sgu_b_s': out['sgu_b_s'], 'sgu_w_out': out['sgu_w_out'], 'ffn_norm_w': out['ffn_norm_w'], 'ffn_w_in': out['ffn_w_in'], 'ffn_w_out': out['ffn_w_out'], 'final_norm_w': out['final_norm_w'], 'loss_target': out['loss_target'], 'm_mixer_norm_w': out['m_mixer_norm_w'], 'm_attn_w_in': out['m_attn_w_in'], 'm_attn_b_f': out['m_attn_b_f'], 'm_attn_w_out': out['m_attn_w_out'], 'm_sgu_w_in': out['m_sgu_w_in'], 'm_sgu_ln_g': out['m_sgu_ln_g'], 'm_sgu_ln_b': out['m_sgu_ln_b'], 'm_sgu_w_s': out['m_sgu_w_s'], 'm_sgu_b_s': out['m_sgu_b_s'], 'm_sgu_w_out': out['m_sgu_w_out'], 'm_ffn_norm_w': out['m_ffn_norm_w'], 'm_ffn_w_in': out['m_ffn_w_in'], 'm_ffn_w_out': out['m_ffn_w_out'], 'm_final_norm_w': out['m_final_norm_w'], 'v_mixer_norm_w': out['v_mixer_norm_w'], 'v_attn_w_in': out['v_attn_w_in'], 'v_attn_b_f': out['v_attn_b_f'], 'v_attn_w_out': out['v_attn_w_out'], 'v_sgu_w_in': out['v_sgu_w_in'], 'v_sgu_ln_g': out['v_sgu_ln_g'], 'v_sgu_ln_b': out['v_sgu_ln_b'], 'v_sgu_w_s': out['v_sgu_w_s'], 'v_sgu_b_s': out['v_sgu_b_s'], 'v_sgu_w_out': out['v_sgu_w_out'], 'v_ffn_norm_w': out['v_ffn_norm_w'], 'v_ffn_w_in': out['v_ffn_w_in'], 'v_ffn_w_out': out['v_ffn_w_out'], 'v_final_norm_w': out['v_final_norm_w']}


def _loss(weights, diff, rest, loss_target):
    with _jax.named_scope("forward"):
        args = {**rest, TWIN_DIFF_INPUT: diff, **{k: w.astype(_WEIGHT_DTYPES[k]) for k, w in weights.items()}}
        y = _forward(args)
    with _jax.named_scope("loss_head"):
        err = _jnp.square(y.astype(_jnp.float32) - loss_target)
        return 0.5 * _jnp.sum(_jnp.mean(err, axis=-1)) if err.ndim else 0.5 * err


def _adamw(w, g, m, v):
    m = ADAM_B1 * m + (1.0 - ADAM_B1) * g
    v = ADAM_B2 * v + (1.0 - ADAM_B2) * _jnp.square(g)
    m_hat = m / (1.0 - ADAM_B1 ** ADAM_STEP)
    v_hat = v / (1.0 - ADAM_B2 ** ADAM_STEP)
    delta = -ADAM_LR * (m_hat / (_jnp.sqrt(v_hat) + ADAM_EPS) + ADAM_WD * w)
    return delta, m, v


def reference(x, mixer_norm_w, attn_w_in, attn_b_f, attn_w_out, sgu_w_in, sgu_ln_g, sgu_ln_b, sgu_w_s, sgu_b_s, sgu_w_out, ffn_norm_w, ffn_w_in, ffn_w_out, final_norm_w, loss_target, m_mixer_norm_w, m_attn_w_in, m_attn_b_f, m_attn_w_out, m_sgu_w_in, m_sgu_ln_g, m_sgu_ln_b, m_sgu_w_s, m_sgu_b_s, m_sgu_w_out, m_ffn_norm_w, m_ffn_w_in, m_ffn_w_out, m_final_norm_w, v_mixer_norm_w, v_attn_w_in, v_attn_b_f, v_attn_w_out, v_sgu_w_in, v_sgu_ln_g, v_sgu_ln_b, v_sgu_w_s, v_sgu_b_s, v_sgu_w_out, v_ffn_norm_w, v_ffn_w_in, v_ffn_w_out, v_final_norm_w):
    given = dict(x=x, mixer_norm_w=mixer_norm_w, attn_w_in=attn_w_in, attn_b_f=attn_b_f, attn_w_out=attn_w_out, sgu_w_in=sgu_w_in, sgu_ln_g=sgu_ln_g, sgu_ln_b=sgu_ln_b, sgu_w_s=sgu_w_s, sgu_b_s=sgu_b_s, sgu_w_out=sgu_w_out, ffn_norm_w=ffn_norm_w, ffn_w_in=ffn_w_in, ffn_w_out=ffn_w_out, final_norm_w=final_norm_w, loss_target=loss_target, m_mixer_norm_w=m_mixer_norm_w, m_attn_w_in=m_attn_w_in, m_attn_b_f=m_attn_b_f, m_attn_w_out=m_attn_w_out, m_sgu_w_in=m_sgu_w_in, m_sgu_ln_g=m_sgu_ln_g, m_sgu_ln_b=m_sgu_ln_b, m_sgu_w_s=m_sgu_w_s, m_sgu_b_s=m_sgu_b_s, m_sgu_w_out=m_sgu_w_out, m_ffn_norm_w=m_ffn_norm_w, m_ffn_w_in=m_ffn_w_in, m_ffn_w_out=m_ffn_w_out, m_final_norm_w=m_final_norm_w, v_mixer_norm_w=v_mixer_norm_w, v_attn_w_in=v_attn_w_in, v_attn_b_f=v_attn_b_f, v_attn_w_out=v_attn_w_out, v_sgu_w_in=v_sgu_w_in, v_sgu_ln_g=v_sgu_ln_g, v_sgu_ln_b=v_sgu_ln_b, v_sgu_w_s=v_sgu_w_s, v_sgu_b_s=v_sgu_b_s, v_sgu_w_out=v_sgu_w_out, v_ffn_norm_w=v_ffn_norm_w, v_ffn_w_in=v_ffn_w_in, v_ffn_w_out=v_ffn_w_out, v_final_norm_w=v_final_norm_w)
    weights = {n: given[n] for n in TWIN_WEIGHTS}
    shared = {n: given[n] for n in SHARED_INPUTS}
    per_example = {n: given[n] for n in ['x']}
    grad_fn = _jax.value_and_grad(_loss, argnums=(0, 1))

    def one_microbatch(ex, loss_target):
        ex = dict(ex)
        diff = ex.pop(TWIN_DIFF_INPUT)
        return grad_fn(weights, diff, {**shared, **ex}, loss_target)

    if N_MICROBATCH == 1:
        loss, (grad_w, grad_x) = one_microbatch(per_example, given["loss_target"])
    else:
        def body(carry, xs):
            loss_sum, grad_sum = carry
            l_k, (gw_k, gx_k) = one_microbatch(xs[0], xs[1])
            with _jax.named_scope("update"):
                return (loss_sum + l_k, _jax.tree.map(_jnp.add, grad_sum, gw_k)), gx_k

        init = (_jnp.zeros((), _jnp.float32), _jax.tree.map(_jnp.zeros_like, weights))
        (loss, grad_w), grad_x = _jax.lax.scan(body, init, (per_example, given["loss_target"]))
    with _jax.named_scope("update"):
        delta_w, new_m, new_v = {}, {}, {}
        for n in TWIN_WEIGHTS:
            delta_w[n], new_m[n], new_v[n] = _adamw(weights[n], grad_w[n], given["m_" + n], given["v_" + n])
    return (loss, grad_x, *[grad_w[n] for n in TWIN_WEIGHTS], *[delta_w[n] for n in TWIN_WEIGHTS],
            *[new_m[n] for n in TWIN_WEIGHTS], *[new_v[n] for n in TWIN_WEIGHTS])
```

```python
import functools

import jax
import jax.numpy as jnp
from jax import lax
from jax.experimental import pallas as pl
from jax.experimental.pallas import tpu as pltpu

F32 = jnp.float32
BF16 = jnp.bfloat16

D_MODEL = 1024
HEAD_DIM = 64
N_PAIR = 8
LANES = 128
SGU_W = 2048
SGU_G = 16
CHUNK = 128
FFN_H = 2816
FFN_TILE = 256
N_FFN_TILE = FFN_H // FFN_TILE
NORM_EPS = 1e-6
LN_EPS = 1e-5
QK_SCALE = 0.125
ATT_BLOCK = 256
N_DEV = 8
ADAM_LR = 0.001
ADAM_B1 = 0.9
ADAM_B2 = 0.999
ADAM_EPS = 1e-08
ADAM_WD = 0.01
ADAM_STEP = 10
MESH = pl.DeviceIdType.MESH
SQRT_HALF = 0.7071067811865476
INV_SQRT_2PI = 0.3989422804014327

NT_DIMS = (((1,), (1,)), ((), ()))
TN_DIMS = (((0,), (0,)), ((), ()))


def _gelu(x):
    return 0.5 * x * (1.0 + lax.erf(x * SQRT_HALF))


def _gelu_grad(x):
    return 0.5 * (1.0 + lax.erf(x * SQRT_HALF)) + x * jnp.exp(-0.5 * x * x) * INV_SQRT_2PI


def _lane_col(v, lane):
    idx = lax.broadcasted_iota(jnp.int32, v.shape, 1)
    return jnp.sum(jnp.where(idx == lane, v, 0.0), axis=1, keepdims=True)


def _params(sem, vmem_mb):
    return pltpu.CompilerParams(dimension_semantics=sem, vmem_limit_bytes=vmem_mb << 20)


def _cat_groups(ref, n):
    if n == 1:
        return ref[0]
    return jnp.concatenate([ref[t] for t in range(n)], axis=1)


def _rowcall(name, body, T, tm, ins, outs, vmem_mb, scratch=()):
    def spec(shape, kind):
        shape = tuple(shape)
        if kind == "row":
            return pl.BlockSpec((tm,) + shape[1:], lambda i: (i,) + (0,) * (len(shape) - 1))
        if kind == "grp":
            return pl.BlockSpec((shape[0], tm, shape[2]), lambda i: (0, i, 0))
        return pl.BlockSpec(shape, lambda i: (0,) * len(shape))

    return pl.pallas_call(
        body,
        name=name,
        grid=(T // tm,),
        in_specs=[spec(a.shape, k) for a, k in ins],
        out_specs=[spec(s, k) for s, _, k in outs],
        out_shape=[jax.ShapeDtypeStruct(tuple(s), d) for s, d, _ in outs],
        scratch_shapes=list(scratch),
        compiler_params=_params(("arbitrary",), vmem_mb),
    )(*[a for a, _ in ins])


def _norm_matmul(x, nw, w, out_dtype, name, tn, groups=False):
    T, N = x.shape[0], w.shape[1]
    tm = min(512, T)

    def body(x_ref, nw_ref, w_ref, o_ref, h_ref, h_scr):
        @pl.when(pl.program_id(1) == 0)
        def _():
            xv = x_ref[...]
            r = lax.rsqrt(jnp.mean(xv * xv, axis=-1, keepdims=True) + NORM_EPS)
            hv = (xv * r * nw_ref[...]).astype(BF16)
            h_scr[...] = hv
            h_ref[...] = hv

        acc = jnp.dot(h_scr[...], w_ref[...], preferred_element_type=F32)
        if groups:
            for t in range(tn // LANES):
                o_ref[t] = acc[:, LANES * t:LANES * (t + 1)].astype(out_dtype)
        else:
            o_ref[...] = acc.astype(out_dtype)

    if groups:
        o_shape = (N // LANES, T, LANES)
        o_spec = pl.BlockSpec((tn // LANES, tm, LANES), lambda i, j: (j, i, 0))
    else:
        o_shape = (T, N)
        o_spec = pl.BlockSpec((tm, tn), lambda i, j: (i, j))
    return pl.pallas_call(
        body,
        name=name,
        grid=(T // tm, N // tn),
        in_specs=[
            pl.BlockSpec((tm, D_MODEL), lambda i, j: (i, 0)),
            pl.BlockSpec((1, D_MODEL), lambda i, j: (0, 0)),
            pl.BlockSpec((D_MODEL, tn), lambda i, j: (0, j)),
        ],
        out_specs=[o_spec, pl.BlockSpec((tm, D_MODEL), lambda i, j: (i, 0))],
        out_shape=[jax.ShapeDtypeStruct(o_shape, out_dtype), jax.ShapeDtypeStruct((T, D_MODEL), BF16)],
        scratch_shapes=[pltpu.VMEM((tm, D_MODEL), BF16)],
        compiler_params=_params(("arbitrary", "arbitrary"), 32),
    )(x, nw, w)


def _matmul_tn(a, g, name, tk, tn, a_grp=False, g_grp=False):
    T = a.shape[1] if a_grp else a.shape[0]
    K = a.shape[0] * LANES if a_grp else a.shape[1]
    N = g.shape[0] * LANES if g_grp else g.shape[1]
    tm = min(512, T)

    def body(a_ref, g_ref, o_ref):
        @pl.when(pl.program_id(2) == 0)
        def _():
            o_ref[...] = jnp.zeros(o_ref.shape, F32)

        av = _cat_groups(a_ref, tk // LANES) if a_grp else a_ref[...]
        gv = _cat_groups(g_ref, tn // LANES) if g_grp else g_ref[...]
        o_ref[...] += lax.dot_general(av.astype(BF16), gv.astype(BF16), TN_DIMS, preferred_element_type=F32)

    if a_grp:
        a_spec = pl.BlockSpec((tk // LANES, tm, LANES), lambda k, n, m: (k, m, 0))
    else:
        a_spec = pl.BlockSpec((tm, tk), lambda k, n, m: (m, k))
    if g_grp:
        g_spec = pl.BlockSpec((tn // LANES, tm, LANES), lambda k, n, m: (n, m, 0))
    else:
        g_spec = pl.BlockSpec((tm, tn), lambda k, n, m: (m, n))
    return pl.pallas_call(
        body,
        name=name,
        grid=(K // tk, N // tn, T // tm),
        in_specs=[a_spec, g_spec],
        out_specs=pl.BlockSpec((tk, tn), lambda k, n, m: (k, n)),
        out_shape=jax.ShapeDtypeStruct((K, N), F32),
        compiler_params=_params(("parallel", "parallel", "arbitrary"), 40),
    )(a, g)


def _matmul_rms_bwd(a_list, wt_list, x, nw, dres, name, tm, vmem_mb):
    T = x.shape[0]
    n = len(a_list)

    def body(*refs):
        a_refs, w_refs = refs[:n], refs[n:2 * n]
        x_ref, nw_ref, dres_ref, dx_ref, dnw_ref = refs[2 * n:]

        @pl.when(pl.program_id(0) == 0)
        def _():
            dnw_ref[...] = jnp.zeros(dnw_ref.shape, F32)

        dh = None
        for (arr, kind), a_ref, w_ref in zip(a_list, a_refs, w_refs):
            av = _cat_groups(a_ref, arr.shape[0]) if kind == "grp" else a_ref[...]
            part = jnp.dot(av.astype(BF16), w_ref[...], preferred_element_type=F32)
            dh = part if dh is None else dh + part
        xv = x_ref[...]
        r = lax.rsqrt(jnp.mean(xv * xv, axis=-1, keepdims=True) + NORM_EPS)
        xn = xv * r
        dnw_ref[...] += jnp.sum(dh * xn, axis=0, keepdims=True)
        dyw = dh * nw_ref[...]
        dx_ref[...] = dres_ref[...] + r * (dyw - xn * jnp.mean(dyw * xn, axis=-1, keepdims=True))

    ins = list(a_list) + [(w, "full") for w in wt_list] + [(x, "row"), (nw, "full"), (dres, "row")]
    outs = [((T, D_MODEL), F32, "row"), ((1, D_MODEL), F32, "full")]
    return _rowcall(name, body, T, tm, ins, outs, vmem_mb)


def _fgate_fwd(fl3, bcol, name):
    n_chunk = fl3.shape[0]

    def body(fl_ref, b_ref, c_ref):
        r = lax.broadcasted_iota(jnp.int32, (CHUNK, CHUNK), 0)
        t = lax.broadcasted_iota(jnp.int32, (CHUNK, CHUNK), 1)
        tri = jnp.where(r <= t, 1.0, 0.0).astype(BF16)

        def chunk(i, carry):
            z = fl_ref[i] + b_ref[...]
            lf = jnp.minimum(z, 0.0) - jnp.log(1.0 + jnp.exp(-jnp.abs(z)))
            hi = lf.astype(BF16)
            r1 = lf - hi.astype(F32)
            mid = r1.astype(BF16)
            low = (r1 - mid.astype(F32)).astype(BF16)
            cs = (jnp.dot(hi, tri, preferred_element_type=F32) + jnp.dot(mid, tri, preferred_element_type=F32)
                  + jnp.dot(low, tri, preferred_element_type=F32)) + carry
            c_ref[i] = cs
            return _lane_col(cs, CHUNK - 1)

        lax.fori_loop(0, n_chunk, chunk, jnp.zeros((2 * N_PAIR, 1), F32))

    return pl.pallas_call(
        body, name=name, out_shape=jax.ShapeDtypeStruct(fl3.shape, F32),
        compiler_params=pltpu.CompilerParams(vmem_limit_bytes=16 << 20),
    )(fl3, bcol)


def _fgate_bwd(dc3, fl3, bcol, name):
    n_chunk = fl3.shape[0]

    def body(dc_ref, fl_ref, b_ref, dfl_ref, db_ref):
        tt = lax.broadcasted_iota(jnp.int32, (CHUNK, CHUNK), 0)
        rr = lax.broadcasted_iota(jnp.int32, (CHUNK, CHUNK), 1)
        tri = jnp.where(tt >= rr, 1.0, 0.0).astype(BF16)

        def chunk(k, carry):
            tail, acc = carry
            i = n_chunk - 1 - k
            dc = dc_ref[i]
            hi = dc.astype(BF16)
            r1 = dc - hi.astype(F32)
            mid = r1.astype(BF16)
            low = (r1 - mid.astype(F32)).astype(BF16)
            dlf = (jnp.dot(hi, tri, preferred_element_type=F32) + jnp.dot(mid, tri, preferred_element_type=F32)
                   + jnp.dot(low, tri, preferred_element_type=F32)) + tail
            z = fl_ref[i] + b_ref[...]
            dfl = dlf / (1.0 + jnp.exp(z))
            dfl_ref[i] = dfl
            return _lane_col(dlf, 0), acc + dfl

        _, acc = lax.fori_loop(0, n_chunk, chunk,
                               (jnp.zeros((2 * N_PAIR, 1), F32), jnp.zeros((2 * N_PAIR, CHUNK), F32)))
        db_ref[...] = jnp.broadcast_to(jnp.sum(acc, axis=1, keepdims=True), db_ref.shape)

    return pl.pallas_call(
        body, name=name,
        out_shape=[jax.ShapeDtypeStruct(fl3.shape, F32), jax.ShapeDtypeStruct((2 * N_PAIR, LANES), F32)],
        compiler_params=pltpu.CompilerParams(vmem_limit_bytes=16 << 20),
    )(dc3, fl3, bcol)


def _attn_fwd(qkv3, c4, name):
    T = qkv3.shape[1]
    tb = min(ATT_BLOCK, T)

    def body(q_ref, k_ref, v_ref, c_ref, o_ref, lse_ref):
        i = pl.program_id(1)
        lo = lax.broadcasted_iota(jnp.int32, (tb, LANES), 1) < HEAD_DIM
        row = lax.broadcasted_iota(jnp.int32, (tb, tb), 0)
        col = lax.broadcasted_iota(jnp.int32, (tb, tb), 1)
        q2 = q_ref[0] * QK_SCALE
        zero = jnp.zeros_like(q2)
        qh = (jnp.where(lo, q2, zero), jnp.where(lo, zero, q2))

        def step(j, carry, masked):
            m_old, l_old, acc = carry[:2], carry[2:4], carry[4]
            off = pl.multiple_of(j * tb, tb)
            kb = k_ref[0, pl.ds(off, tb), :]
            vb = v_ref[0, pl.ds(off, tb), :]
            m_new, l_new, upd = [], [], []
            for hh in range(2):
                s = lax.dot_general(qh[hh], kb, NT_DIMS, preferred_element_type=F32)
                s = s - c_ref[0, j, hh:hh + 1, :]
                if masked:
                    s = jnp.where(row >= col, s, -jnp.inf)
                m = jnp.maximum(m_old[hh], jnp.max(s, axis=1, keepdims=True))
                p = jnp.exp(s - m)
                alpha = jnp.exp(m_old[hh] - m)
                m_new.append(m)
                l_new.append(alpha * l_old[hh] + jnp.sum(p, axis=1, keepdims=True))
                upd.append(alpha * acc + jnp.dot(p.astype(BF16), vb, preferred_element_type=F32))
            return (m_new[0], m_new[1], l_new[0], l_new[1], jnp.where(lo, upd[0], upd[1]))

        ninf = jnp.full((tb, 1), -jnp.inf, F32)
        zcol = jnp.zeros((tb, 1), F32)
        carry = lax.fori_loop(0, i, lambda j, c: step(j, c, False),
                              (ninf, ninf, zcol, zcol, jnp.zeros((tb, LANES), F32)))
        m0, m1, l0, l1, acc = step(i, carry, True)
        o_ref[0] = (acc * jnp.where(lo, 1.0 / l0, 1.0 / l1)).astype(BF16)
        lse_ref[0] = jnp.where(lo, m0 + jnp.log(l0), m1 + jnp.log(l1))

    blk = lambda base: pl.BlockSpec((1, T, LANES), lambda h, i: (base + h, 0, 0))
    tile = pl.BlockSpec((1, tb, LANES), lambda h, i: (h, i, 0))
    return pl.pallas_call(
        body,
        name=name,
        grid=(N_PAIR, T // tb),
        in_specs=[tile, blk(N_PAIR), blk(2 * N_PAIR), pl.BlockSpec((1, T // tb, 2, tb), lambda h, i: (h, 0, 0, 0))],
        out_specs=[tile, tile],
        out_shape=[jax.ShapeDtypeStruct((N_PAIR, T, LANES), BF16), jax.ShapeDtypeStruct((N_PAIR, T, LANES), F32)],
        compiler_params=_params(("parallel", "arbitrary"), 40),
    )(qkv3, qkv3, qkv3, c4)


def _attn_bwd(qkv3, do3, st3, c4, name):
    T = qkv3.shape[1]
    tb = min(ATT_BLOCK, T)
    nb = T // tb

    def body(q_ref, do_ref, st_ref, k_ref, v_ref, c_ref, dq_hbm, rs_hbm, dk_ref, dv_ref, cs_ref,
             dq_acc, rs_acc, dk_acc, dv_acc):
        h, j = pl.program_id(0), pl.program_id(1)
        lo = lax.broadcasted_iota(jnp.int32, (tb, LANES), 1) < HEAD_DIM
        hi = jnp.logical_not(lo)
        row = lax.broadcasted_iota(jnp.int32, (tb, tb), 0)
        col = lax.broadcasted_iota(jnp.int32, (tb, tb), 1)

        @pl.when(j == 0)
        def _():
            dq_acc[...] = jnp.zeros(dq_acc.shape, F32)
            rs_acc[...] = jnp.zeros(rs_acc.shape, F32)

        kb = k_ref[0]
        vb = v_ref[0]
        ks = kb * QK_SCALE
        one = jnp.ones_like(ks)
        zero = jnp.zeros_like(ks)
        ks1 = (jnp.where(lo, ks, one), jnp.where(hi, ks, one))
        dk_acc[...] = jnp.zeros(dk_acc.shape, F32)
        dv_acc[...] = jnp.zeros(dv_acc.shape, F32)
        crow = [c_ref[0, j, hh:hh + 1, :] for hh in range(2)]

        def step(i, masked):
            off = pl.multiple_of(i * tb, tb)
            qb = q_ref[0, pl.ds(off, tb), :] * QK_SCALE
            dob = do_ref[0, pl.ds(off, tb), :]
            st = st_ref[0, pl.ds(off, tb), :]
            dqs = []
            for hh, sel in enumerate((lo, hi)):
                lse = _lane_col(st, HEAD_DIM * hh)
                dd = _lane_col(st, HEAD_DIM * hh + 32)
                s = lax.dot_general(jnp.where(sel, qb, zero), kb, NT_DIMS, preferred_element_type=F32) - crow[hh]
                if masked:
                    s = jnp.where(row >= col, s, -jnp.inf)
                p = jnp.exp(s - lse)
                dv_acc[hh] += lax.dot_general(p.astype(BF16), dob, TN_DIMS, preferred_element_type=F32)
                dp = lax.dot_general(jnp.where(sel, dob, zero), vb, NT_DIMS, preferred_element_type=F32)
                dsb = (p * (dp - dd)).astype(BF16)
                dk_acc[hh] += lax.dot_general(dsb, jnp.where(sel, qb, one), TN_DIMS, preferred_element_type=F32)
                dqs.append(jnp.dot(dsb, ks1[hh], preferred_element_type=F32))
            dq_acc[pl.ds(off, tb), :] += jnp.where(lo, dqs[0], dqs[1])
            rs_acc[pl.ds(off, tb), :] += jnp.where(lo, dqs[1], dqs[0])

        step(j, True)

        def loop_body(i, carry):
            step(i, False)
            return carry

        lax.fori_loop(j + 1, nb, loop_body, 0)
        dk_ref[0] = jnp.where(lo, dk_acc[0], dk_acc[1]).astype(BF16)
        cs_ref[0] = jnp.where(lo, dk_acc[1], dk_acc[0])
        dv_ref[0] = jnp.where(lo, dv_acc[0], dv_acc[1]).astype(BF16)

        @pl.when(j == nb - 1)
        def _():
            pltpu.sync_copy(dq_acc, dq_hbm.at[h])
            pltpu.sync_copy(rs_acc, rs_hbm.at[h])

    res = lambda base: pl.BlockSpec((1, T, LANES), lambda h, j: (base + h, 0, 0), pipeline_mode=pl.Buffered(1))
    tile = lambda base: pl.BlockSpec((1, tb, LANES), lambda h, j: (base + h, j, 0))
    hbm = pl.BlockSpec(memory_space=pl.ANY)
    return pl.pallas_call(
        body,
        name=name,
        grid=(N_PAIR, nb),
        in_specs=[res(0), res(0), res(0), tile(N_PAIR), tile(2 * N_PAIR),
                  pl.BlockSpec((1, nb, 2, tb), lambda h, j: (h, 0, 0, 0))],
        out_specs=[hbm, hbm, tile(0), tile(0), tile(0)],
        out_shape=[
            jax.ShapeDtypeStruct((N_PAIR, T, LANES), F32),
            jax.ShapeDtypeStruct((N_PAIR, T, LANES), F32),
            jax.ShapeDtypeStruct((N_PAIR, T, LANES), BF16),
            jax.ShapeDtypeStruct((N_PAIR, T, LANES), BF16),
            jax.ShapeDtypeStruct((N_PAIR, T, LANES), F32),
        ],
        scratch_shapes=[pltpu.VMEM((T, LANES), F32), pltpu.VMEM((T, LANES), F32),
                        pltpu.VMEM((2, tb, LANES), F32), pltpu.VMEM((2, tb, LANES), F32)],
        compiler_params=_params(("arbitrary", "arbitrary"), 56),
    )(qkv3, do3, st3, qkv3, qkv3, c4)


def _attn_out(o3, w, x, name):
    T = x.shape[0]

    def body(o_ref, w_ref, x_ref, out_ref):
        out_ref[...] = x_ref[...] + jnp.dot(_cat_groups(o_ref, N_PAIR), w_ref[...], preferred_element_type=F32)

    return _rowcall(name, body, T, min(512, T), [(o3, "grp"), (w, "full"), (x, "row")],
                    [((T, D_MODEL), F32, "row")], 24)[0]


def _attn_dout(dx, wt, o3, lse3, name):
    T = dx.shape[0]
    tm = min(512, T)

    def body(dx_ref, w_ref, o_ref, lse_ref, do_ref, st_ref):
        do = jnp.dot(dx_ref[...].astype(BF16), w_ref[...], preferred_element_type=F32).astype(BF16)
        lane = lax.broadcasted_iota(jnp.int32, (tm, LANES), 1)
        lo = lane < HEAD_DIM
        is_lse = jnp.bitwise_and(lane, HEAD_DIM - 1) < 32
        for t in range(N_PAIR):
            d = do[:, LANES * t:LANES * (t + 1)]
            do_ref[t] = d
            prod = d.astype(F32) * o_ref[t].astype(F32)
            d0 = jnp.sum(jnp.where(lo, prod, 0.0), axis=1, keepdims=True)
            d1 = jnp.sum(jnp.where(lo, 0.0, prod), axis=1, keepdims=True)
            st_ref[t] = jnp.where(is_lse, lse_ref[t], jnp.where(lo, d0, d1))

    return _rowcall(name, body, T, tm, [(dx, "row"), (wt, "full"), (o3, "grp"), (lse3, "grp")],
                    [((N_PAIR, T, LANES), BF16, "grp"), ((N_PAIR, T, LANES), F32, "grp")], 32)


def _ffn_out(gu, w, x, name):
    T = x.shape[0]
    tm = min(256, T)

    def body(gu_ref, w_ref, x_ref, out_ref, hid_ref):
        acc = x_ref[...]
        for j in range(N_FFN_TILE):
            g = gu_ref[:, 2 * FFN_TILE * j:2 * FFN_TILE * j + FFN_TILE].astype(F32)
            u = gu_ref[:, 2 * FFN_TILE * j + FFN_TILE:2 * FFN_TILE * (j + 1)].astype(F32)
            hj = (g * jax.nn.sigmoid(g) * u).astype(BF16)
            hid_ref[:, FFN_TILE * j:FFN_TILE * (j + 1)] = hj
            acc = acc + jnp.dot(hj, w_ref[FFN_TILE * j:FFN_TILE * (j + 1), :], preferred_element_type=F32)
        out_ref[...] = acc

    return _rowcall(name, body, T, tm, [(gu, "row"), (w, "full"), (x, "row")],
                    [((T, D_MODEL), F32, "row"), ((T, FFN_H), BF16, "row")], 40)


def _ffn_dgu(dx, wt, gu, name):
    T = dx.shape[0]
    tm = min(256, T)

    def body(dx_ref, w_ref, gu_ref, dgu_ref):
        dxb = dx_ref[...].astype(BF16)
        for j in range(N_FFN_TILE):
            dh = jnp.dot(dxb, w_ref[:, FFN_TILE * j:FFN_TILE * (j + 1)], preferred_element_type=F32)
            g = gu_ref[:, 2 * FFN_TILE * j:2 * FFN_TILE * j + FFN_TILE].astype(F32)
            u = gu_ref[:, 2 * FFN_TILE * j + FFN_TILE:2 * FFN_TILE * (j + 1)].astype(F32)
            sg = jax.nn.sigmoid(g)
            dgu_ref[:, 2 * FFN_TILE * j:2 * FFN_TILE * j + FFN_TILE] = (
                dh * u * (sg * (1.0 + g * (1.0 - sg)))).astype(BF16)
            dgu_ref[:, 2 * FFN_TILE * j + FFN_TILE:2 * FFN_TILE * (j + 1)] = (dh * (g * sg)).astype(BF16)

    return _rowcall(name, body, T, tm, [(dx, "row"), (wt, "full"), (gu, "row")],
                    [((T, 2 * FFN_H), BF16, "row")], 40)[0]


def _sgu_core(a, ln_g, ln_b, w_s, bst, w, x, name):
    T = x.shape[0]
    tm = min(256, T)

    def body(a_ref, lg_ref, lb_ref, ws_ref, bs_ref, w_ref, x_ref, out_ref, gated_ref, vn_ref, mixed_ref):
        v = _gelu(a_ref[:, SGU_W:].astype(F32))
        mu = jnp.mean(v, axis=-1, keepdims=True)
        vc = v - mu
        rstd = lax.rsqrt(jnp.mean(vc * vc, axis=-1, keepdims=True) + LN_EPS)
        vn_ref[...] = (vc * rstd * lg_ref[...] + lb_ref[...]).astype(BF16)
        tt = lax.broadcasted_iota(jnp.int32, (CHUNK, CHUNK), 0)
        ss = lax.broadcasted_iota(jnp.int32, (CHUNK, CHUNK), 1)
        for g in range(SGU_G):
            wg = jnp.where(tt >= ss, ws_ref[g], 0.0).astype(BF16)
            bcol = _lane_col(bs_ref[...], g)
            cols = slice(CHUNK * g, CHUNK * (g + 1))
            for c in range(tm // CHUNK):
                rows = slice(CHUNK * c, CHUNK * (c + 1))
                mixed = jnp.dot(wg, vn_ref[rows, cols], preferred_element_type=F32) + bcol
                u = _gelu(a_ref[rows, cols].astype(F32))
                mixed_ref[rows, cols] = mixed.astype(BF16)
                gated_ref[rows, cols] = (u * mixed).astype(BF16)
        out_ref[...] = x_ref[...] + jnp.dot(gated_ref[...], w_ref[...], preferred_element_type=F32)

    ins = [(a, "row"), (ln_g, "full"), (ln_b, "full"), (w_s, "full"), (bst, "full"), (w, "full"), (x, "row")]
    outs = [((T, D_MODEL), F32, "row")] + [((T, SGU_W), BF16, "row")] * 3
    return _rowcall(name, body, T, tm, ins, outs, 40)


def _sgu_core_bwd(dx, wt, a, vn, mixed, ln_g, w_s, name):
    T = dx.shape[0]
    tm = min(256, T)

    def body(dx_ref, wt_ref, a_ref, vn_ref, mx_ref, lg_ref, ws_ref,
             da_ref, dws_ref, dba_ref, dlg_ref, dlb_ref, dg_scr, dvn_scr):
        @pl.when(pl.program_id(0) == 0)
        def _():
            dws_ref[...] = jnp.zeros(dws_ref.shape, F32)
            dba_ref[...] = jnp.zeros(dba_ref.shape, F32)
            dlg_ref[...] = jnp.zeros(dlg_ref.shape, F32)
            dlb_ref[...] = jnp.zeros(dlb_ref.shape, F32)

        dg_scr[...] = jnp.dot(dx_ref[...].astype(BF16), wt_ref[...], preferred_element_type=F32)
        tt = lax.broadcasted_iota(jnp.int32, (CHUNK, CHUNK), 0)
        ss = lax.broadcasted_iota(jnp.int32, (CHUNK, CHUNK), 1)
        tril = tt >= ss
        for g in range(SGU_G):
            wg = jnp.where(tril, ws_ref[g], 0.0).astype(BF16)
            cols = slice(CHUNK * g, CHUNK * (g + 1))
            for c in range(tm // CHUNK):
                rows = slice(CHUNK * c, CHUNK * (c + 1))
                dgb = dg_scr[rows, cols]
                au = a_ref[rows, cols].astype(F32)
                dmx = dgb * _gelu(au)
                da_ref[rows, cols] = (dgb * mx_ref[rows, cols].astype(F32) * _gelu_grad(au)).astype(BF16)
                dmb = dmx.astype(BF16)
                dvn_scr[rows, cols] = lax.dot_general(wg, dmb, TN_DIMS, preferred_element_type=F32)
                dws_ref[g] += jnp.where(
                    tril, lax.dot_general(dmb, vn_ref[rows, cols], NT_DIMS, preferred_element_type=F32), 0.0)
                dba_ref[:, cols] += dmx
        av = a_ref[:, SGU_W:].astype(F32)
        v = _gelu(av)
        mu = jnp.mean(v, axis=-1, keepdims=True)
        vc = v - mu
        rstd = lax.rsqrt(jnp.mean(vc * vc, axis=-1, keepdims=True) + LN_EPS)
        xhat = vc * rstd
        dvn = dvn_scr[...]
        dlg_ref[...] += jnp.sum(dvn * xhat, axis=0, keepdims=True)
        dlb_ref[...] += jnp.sum(dvn, axis=0, keepdims=True)
        dxh = dvn * lg_ref[...]
        dv = rstd * (dxh - jnp.mean(dxh, axis=-1, keepdims=True)
                     - xhat * jnp.mean(dxh * xhat, axis=-1, keepdims=True))
        da_ref[:, SGU_W:] = (dv * _gelu_grad(av)).astype(BF16)

    ins = [(dx, "row"), (wt, "full"), (a, "row"), (vn, "row"), (mixed, "row"), (ln_g, "full"), (w_s, "full")]
    outs = [((T, 2 * SGU_W), BF16, "row"), ((SGU_G, CHUNK, CHUNK), F32, "full"), ((CHUNK, SGU_W), F32, "full"),
            ((1, SGU_W), F32, "full"), ((1, SGU_W), F32, "full")]
    return _rowcall(name, body, T, tm, ins, outs, 40,
                    scratch=[pltpu.VMEM((tm, SGU_W), F32), pltpu.VMEM((tm, SGU_W), F32)])


def _loss_head(x, wf, tgt, name):
    T = x.shape[0]
    tm = min(512, T)

    def body(x_ref, wf_ref, tgt_ref, dx_ref, dwf_ref, loss_ref):
        @pl.when(pl.program_id(0) == 0)
        def _():
            dwf_ref[...] = jnp.zeros(dwf_ref.shape, F32)
            loss_ref[...] = jnp.zeros(loss_ref.shape, F32)

        xv = x_ref[...]
        r = lax.rsqrt(jnp.mean(xv * xv, axis=-1, keepdims=True) + NORM_EPS)
        xn = xv * r
        err = xn * wf_ref[...] - tgt_ref[...]
        loss_ref[...] += 0.5 * jnp.sum(jnp.mean(err * err, axis=-1, keepdims=True), axis=0, keepdims=True)
        dy = err * (1.0 / D_MODEL)
        dwf_ref[...] += jnp.sum(dy * xn, axis=0, keepdims=True)
        dyw = dy * wf_ref[...]
        dx_ref[...] = r * (dyw - xn * jnp.mean(dyw * xn, axis=-1, keepdims=True))

    return _rowcall(name, body, T, tm, [(x, "row"), (wf, "full"), (tgt, "row")],
                    [((T, D_MODEL), F32, "row"), ((1, D_MODEL), F32, "full"), ((8, LANES), F32, "full")], 32)


def _peers():
    x, y, c = lax.axis_index("x"), lax.axis_index("y"), lax.axis_index("c")
    peers = []
    for p in range(1, N_DEV):
        px = 1 - x if p & 4 else x
        py = 1 - y if p & 2 else y
        pc = 1 - c if p & 1 else c
        peers.append((4 * px + 2 * py + pc, (px, py, pc)))
    return 4 * x + 2 * y + c, peers


def _all_gather(arrs, name):
    n = len(arrs)
    hbm = pl.BlockSpec(memory_space=pl.ANY)

    def body(*refs):
        ins, outs = refs[:n], refs[n:2 * n]
        send_sems, recv_sems, local_sems = refs[2 * n:]
        me, peers = _peers()
        sends, recvs, locals_ = [], [], []
        for t in range(n):
            cp = pltpu.make_async_copy(ins[t], outs[t].at[me], local_sems.at[t])
            cp.start()
            locals_.append(cp)
            for k, (pidx, pid) in enumerate(peers):
                s = t * (N_DEV - 1) + k
                send = pltpu.make_async_remote_copy(
                    src_ref=ins[t], dst_ref=outs[t].at[me], send_sem=send_sems.at[s], recv_sem=recv_sems.at[s],
                    device_id=pid, device_id_type=MESH)
                send.start()
                sends.append(send)
                recvs.append(pltpu.make_async_remote_copy(
                    src_ref=ins[t], dst_ref=outs[t].at[pidx], send_sem=send_sems.at[s], recv_sem=recv_sems.at[s],
                    device_id=pid, device_id_type=MESH))
        for r in recvs:
            r.wait_recv()
        for s in sends:
            s.wait_send()
        for cp in locals_:
            cp.wait()

    return pl.pallas_call(
        body,
        name=name,
        in_specs=[hbm] * n,
        out_specs=[hbm] * n,
        out_shape=[jax.ShapeDtypeStruct((N_DEV,) + a.shape, a.dtype) for a in arrs],
        scratch_shapes=[pltpu.SemaphoreType.DMA((n * (N_DEV - 1),)), pltpu.SemaphoreType.DMA((n * (N_DEV - 1),)),
                        pltpu.SemaphoreType.DMA((n,))],
    )(*arrs)


def _exchange(arrs, name):
    n = len(arrs)
    hbm = pl.BlockSpec(memory_space=pl.ANY)

    def body(*refs):
        ins, outs = refs[:n], refs[n:2 * n]
        send_sems, recv_sems, local_sems = refs[2 * n:]
        me, peers = _peers()
        sends, recvs, locals_ = [], [], []
        for t in range(n):
            cp = pltpu.make_async_copy(ins[t].at[me], outs[t].at[me], local_sems.at[t])
            cp.start()
            locals_.append(cp)
            for k, (pidx, pid) in enumerate(peers):
                s = t * (N_DEV - 1) + k
                send = pltpu.make_async_remote_copy(
                    src_ref=ins[t].at[pidx], dst_ref=outs[t].at[me], send_sem=send_sems.at[s],
                    recv_sem=recv_sems.at[s], device_id=pid, device_id_type=MESH)
                send.start()
                sends.append(send)
                recvs.append(pltpu.make_async_remote_copy(
                    src_ref=ins[t].at[pidx], dst_ref=outs[t].at[pidx], send_sem=send_sems.at[s],
                    recv_sem=recv_sems.at[s], device_id=pid, device_id_type=MESH))
        for r in recvs:
            r.wait_recv()
        for s in sends:
            s.wait_send()
        for cp in locals_:
            cp.wait()

    return pl.pallas_call(
        body,
        name=name,
        in_specs=[hbm] * n,
        out_specs=[hbm] * n,
        out_shape=[jax.ShapeDtypeStruct(a.shape, a.dtype) for a in arrs],
        scratch_shapes=[pltpu.SemaphoreType.DMA((n * (N_DEV - 1),)), pltpu.SemaphoreType.DMA((n * (N_DEV - 1),)),
                        pltpu.SemaphoreType.DMA((n,))],
    )(*arrs)


def _row_tile(rows, cap):
    best = None
    for t in range(16, cap + 1, 16):
        if rows % t == 0:
            best = t
    assert best is not None, rows
    return best


def _adam_sum(parts, w, m, v, name):
    R, C = w.shape
    tr = _row_tile(R, 128)

    def body(p_ref, w_ref, m_ref, v_ref, g_ref, d_ref, nm_ref, nv_ref):
        g = p_ref[0].astype(F32)
        for s in range(1, N_DEV):
            g = g + p_ref[s].astype(F32)
        mm = ADAM_B1 * m_ref[...] + (1.0 - ADAM_B1) * g
        vv = ADAM_B2 * v_ref[...] + (1.0 - ADAM_B2) * (g * g)
        m_hat = mm / (1.0 - ADAM_B1 ** ADAM_STEP)
        v_hat = vv / (1.0 - ADAM_B2 ** ADAM_STEP)
        g_ref[...] = g
        d_ref[...] = -ADAM_LR * (m_hat / (jnp.sqrt(v_hat) + ADAM_EPS) + ADAM_WD * w_ref[...])
        nm_ref[...] = mm
        nv_ref[...] = vv

    mat = pl.BlockSpec((tr, C), lambda i: (i, 0))
    return pl.pallas_call(
        body,
        name=name,
        grid=(R // tr,),
        in_specs=[pl.BlockSpec((N_DEV, tr, C), lambda i: (0, i, 0)), mat, mat, mat],
        out_specs=[mat] * 4,
        out_shape=[jax.ShapeDtypeStruct((R, C), F32)] * 4,
        compiler_params=_params(("parallel",), 32),
    )(parts, w, m, v)


def _cols_from_gathered(g):
    _, L, K, n = g.shape
    return jnp.transpose(g, (1, 2, 0, 3)).reshape(L, K, N_DEV * n)


def _rows_from_gathered(g):
    _, L, k, N = g.shape
    return jnp.transpose(g, (1, 0, 2, 3)).reshape(L, N_DEV * k, N)


def _cols_to_blocks(dw):
    L, K, N = dw.shape
    n = N // N_DEV
    return jnp.transpose(dw.reshape(L, K, N_DEV, n), (2, 0, 1, 3)).reshape(N_DEV, L * K, n)


def _rows_to_blocks(dw):
    L, K, N = dw.shape
    k = K // N_DEV
    return jnp.transpose(dw.reshape(L, N_DEV, k, N), (1, 0, 2, 3)).reshape(N_DEV, L * k, N)


def _ffn_interleave(w):
    lead = w.shape[:-1]
    t = w.reshape(lead + (2, N_FFN_TILE, FFN_TILE))
    return jnp.swapaxes(t, -3, -2).reshape(lead + (2 * FFN_H,))


def _ffn_deinterleave(w):
    lead = w.shape[:-1]
    t = w.reshape(lead + (N_FFN_TILE, 2, FFN_TILE))
    return jnp.swapaxes(t, -3, -2).reshape(lead + (2 * FFN_H,))


def _pad_rows(a, rows=8):
    a = a.reshape(-1, a.shape[-1])
    return jnp.pad(a, ((0, rows - a.shape[0]), (0, 0)))


SMALL_ROWS = 6 * 8 + 2 * SGU_G * CHUNK * CHUNK // D_MODEL


def _pack_small(mixer, ffn, final, b_f, extra, w_s, b_s):
    bf_row = jnp.pad(b_f.reshape(1, -1), ((0, 0), (0, D_MODEL - b_f.size)))
    bs_rows = jnp.pad(b_s.reshape(4, -1), ((0, 0), (0, D_MODEL - b_s.size // 4)))
    return jnp.concatenate([
        _pad_rows(mixer), _pad_rows(ffn), _pad_rows(final.reshape(1, -1)), _pad_rows(bf_row),
        _pad_rows(extra), _pad_rows(bs_rows), w_s.reshape(-1, D_MODEL)], axis=0)


def _unpack_small(p):
    mixer, ffn, final = p[0:4], p[8:12], p[16]
    b_f = p[24, :32].reshape(2, 2 * N_PAIR)
    extra = p[32]
    b_s = p[40:44, :2 * SGU_G * CHUNK // 4].reshape(2, SGU_G, CHUNK)
    w_s = p[48:].reshape(2, SGU_G, CHUNK, CHUNK)
    return mixer, ffn, final, b_f, extra, w_s, b_s


def kernel(x, mixer_norm_w, attn_w_in, attn_b_f, attn_w_out, sgu_w_in, sgu_ln_g, sgu_ln_b, sgu_w_s, sgu_b_s, sgu_w_out, ffn_norm_w, ffn_w_in, ffn_w_out, final_norm_w, loss_target, m_mixer_norm_w, m_attn_w_in, m_attn_b_f, m_attn_w_out, m_sgu_w_in, m_sgu_ln_g, m_sgu_ln_b, m_sgu_w_s, m_sgu_b_s, m_sgu_w_out, m_ffn_norm_w, m_ffn_w_in, m_ffn_w_out, m_final_norm_w, v_mixer_norm_w, v_attn_w_in, v_attn_b_f, v_attn_w_out, v_sgu_w_in, v_sgu_ln_g, v_sgu_ln_b, v_sgu_w_s, v_sgu_b_s, v_sgu_w_out, v_ffn_norm_w, v_ffn_w_in, v_ffn_w_out, v_final_norm_w):
    T = x.shape[1]
    tb = min(ATT_BLOCK, T)
    xs, tgt = x[0], loss_target[0]

    shards = [attn_w_in, attn_w_out, sgu_w_in, sgu_w_out, ffn_w_in, ffn_w_out, sgu_ln_g, sgu_ln_b]
    gathered = _all_gather([s.astype(BF16) for s in shards[:6]] + shards[6:], "gather_weights")
    w_attn_in = _cols_from_gathered(gathered[0])
    w_attn_out = _rows_from_gathered(gathered[1])
    w_sgu_in = _cols_from_gathered(gathered[2])
    w_sgu_out = _rows_from_gathered(gathered[3])
    w_ffn_in = _ffn_interleave(_cols_from_gathered(gathered[4]))
    w_ffn_out = _rows_from_gathered(gathered[5])
    ln_g = jnp.transpose(gathered[6], (1, 0, 2)).reshape(2, 1, SGU_W)
    ln_b = jnp.transpose(gathered[7], (1, 0, 2)).reshape(2, 1, SGU_W)
    w_qkv = w_attn_in[:, :, :3 * D_MODEL]
    w_f = jnp.pad(w_attn_in[:, :, 3 * D_MODEL:], ((0, 0), (0, 0), (0, LANES - 2 * N_PAIR)))
    tr = lambda w: jnp.swapaxes(w, -1, -2)
    w_qkv_t, w_f_t, w_attn_out_t = tr(w_qkv), tr(w_f), tr(w_attn_out)
    w_sgu_in_t, w_sgu_out_t, w_ffn_in_t, w_ffn_out_t = tr(w_sgu_in), tr(w_sgu_out), tr(w_ffn_in), tr(w_ffn_out)
    mixer_nw = mixer_norm_w.reshape(4, 1, D_MODEL)
    ffn_nw = ffn_norm_w.reshape(4, 1, D_MODEL)
    b_col = attn_b_f.reshape(2, 2 * N_PAIR, 1)
    bs_t = jnp.swapaxes(sgu_b_s, 1, 2)

    saved = []
    xr = xs
    for i in range(4):
        j = i // 2
        if i % 2 == 0:
            qkv3, h = _norm_matmul(xr, mixer_nw[i], w_qkv[j], BF16, f"attn_qkv_{j}", 512, groups=True)
            fl, _ = _norm_matmul(xr, mixer_nw[i], w_f[j], F32, f"attn_gate_{j}", LANES)
            fl3 = jnp.transpose(fl[:, :2 * N_PAIR].reshape(T // CHUNK, CHUNK, 2 * N_PAIR), (0, 2, 1))
            c_chunks = _fgate_fwd(fl3, b_col[j], f"fgate_fwd_{j}")
            c4 = jnp.transpose(jnp.transpose(c_chunks, (1, 0, 2)).reshape(N_PAIR, 2, T // tb, tb), (0, 2, 1, 3))
            o3, lse3 = _attn_fwd(qkv3, c4, f"attn_fwd_{j}")
            xm = _attn_out(o3, w_attn_out[j], xr, f"attn_out_{j}")
            mix_saved = (xr, h, qkv3, fl3, c4, o3, lse3)
        else:
            a, h = _norm_matmul(xr, mixer_nw[i], w_sgu_in[j], BF16, f"sgu_in_{j}", 512)
            xm, gated, vn, mixed = _sgu_core(a, ln_g[j], ln_b[j], sgu_w_s[j], bs_t[j], w_sgu_out[j], xr,
                                             f"sgu_core_{j}")
            mix_saved = (xr, h, a, gated, vn, mixed)
        gu, h2 = _norm_matmul(xm, ffn_nw[i], w_ffn_in[i], BF16, f"ffn_in_{i}", 512)
        xo, hid = _ffn_out(gu, w_ffn_out[i], xm, f"ffn_out_{i}")
        saved.append((mix_saved, (xm, h2, gu, hid)))
        xr = xo
    dx, d_final, loss_part = _loss_head(xr, final_norm_w.reshape(1, D_MODEL), tgt, "loss_head")

    d_mixer_nw, d_ffn_nw = [None] * 4, [None] * 4
    d_attn_in, d_attn_out, d_bf, d_sgu_in, d_sgu_out = [None] * 2, [None] * 2, [None] * 2, [None] * 2, [None] * 2
    d_ws, d_bs, d_lg, d_lb = [None] * 2, [None] * 2, [None] * 2, [None] * 2
    d_ffn_in, d_ffn_out = [None] * 4, [None] * 4
    for i in reversed(range(4)):
        j = i // 2
        mix_saved, (xm, h2, gu, hid) = saved[i]
        dgu = _ffn_dgu(dx, w_ffn_out_t[i], gu, f"ffn_dgu_{i}")
        d_ffn_out[i] = _matmul_tn(hid, dx, f"ffn_dwout_{i}", FFN_H // 2, D_MODEL)
        d_ffn_in[i] = _matmul_tn(h2, dgu, f"ffn_dwin_{i}", D_MODEL, 512)
        dx, d_ffn_nw[i] = _matmul_rms_bwd([(dgu, "row")], [w_ffn_in_t[i]], xm, ffn_nw[i], dx, f"ffn_dx_{i}", 256, 48)
        if i % 2 == 0:
            xr, h, qkv3, fl3, c4, o3, lse3 = mix_saved
            do3, st3 = _attn_dout(dx, w_attn_out_t[j], o3, lse3, f"attn_dout_{j}")
            d_attn_out[j] = _matmul_tn(o3, dx, f"attn_dwout_{j}", D_MODEL, D_MODEL, a_grp=True)
            dq3, rs3, dk3, dv3, cs3 = _attn_bwd(qkv3, do3, st3, c4, f"attn_bwd_{j}")
            dc_pair = jnp.stack([rs3[:, :, HEAD_DIM] - cs3[:, :, HEAD_DIM], rs3[:, :, 0] - cs3[:, :, 0]], axis=1)
            dc_chunks = jnp.transpose(dc_pair.reshape(2 * N_PAIR, T // CHUNK, CHUNK), (1, 0, 2))
            dfl3, db = _fgate_bwd(dc_chunks, fl3, b_col[j], f"fgate_bwd_{j}")
            d_bf[j] = db[:, 0]
            dfl = jnp.transpose(dfl3, (0, 2, 1)).reshape(T, 2 * N_PAIR)
            dfl = jnp.pad(dfl.astype(BF16), ((0, 0), (0, LANES - 2 * N_PAIR)))
            d_qkv = [_matmul_tn(h, d3, f"attn_dw{nm}_{j}", D_MODEL, D_MODEL, g_grp=True)
                     for nm, d3 in (("q", dq3), ("k", dk3), ("v", dv3))]
            d_f = _matmul_tn(h, dfl, f"attn_dwf_{j}", D_MODEL, LANES)[:, :2 * N_PAIR]
            d_attn_in[j] = jnp.concatenate(d_qkv + [d_f], axis=1)
            wts = [w_qkv_t[j, k * D_MODEL:(k + 1) * D_MODEL] for k in range(3)] + [w_f_t[j]]
            dx, d_mixer_nw[i] = _matmul_rms_bwd(
                [(dq3, "grp"), (dk3, "grp"), (dv3, "grp"), (dfl, "row")], wts, xr, mixer_nw[i], dx,
                f"attn_dx_{j}", 256, 40)
        else:
            xr, h, a, gated, vn, mixed = mix_saved
            da, d_ws[j], dba, d_lg[j], d_lb[j] = _sgu_core_bwd(dx, w_sgu_out_t[j], a, vn, mixed, ln_g[j], sgu_w_s[j],
                                                               f"sgu_core_bwd_{j}")
            d_bs[j] = jnp.sum(dba.reshape(CHUNK, SGU_G, CHUNK), axis=-1).T
            d_sgu_out[j] = _matmul_tn(gated, dx, f"sgu_dwout_{j}", D_MODEL, D_MODEL)
            d_sgu_in[j] = _matmul_tn(h, da, f"sgu_dwin_{j}", D_MODEL, 512)
            dx, d_mixer_nw[i] = _matmul_rms_bwd([(da, "row")], [w_sgu_in_t[j]], xr, mixer_nw[i], dx,
                                                f"sgu_dx_{j}", 256, 40)
    grad_x = dx[None]

    rows4 = lambda parts: jnp.concatenate(parts, axis=1).reshape(4, D_MODEL)
    small_g = _pack_small(rows4(d_mixer_nw), rows4(d_ffn_nw), d_final[0], jnp.stack(d_bf),
                          loss_part[0:1, 0:1] * jnp.ones((1, D_MODEL), F32), jnp.stack(d_ws), jnp.stack(d_bs))
    zero_row = jnp.zeros((1, D_MODEL), F32)
    pack = lambda pre: _pack_small(pre[0], pre[1], pre[2], pre[3], zero_row, pre[4], pre[5])
    small_w = pack((mixer_norm_w, ffn_norm_w, final_norm_w, attn_b_f, sgu_w_s, sgu_b_s))
    small_m = pack((m_mixer_norm_w, m_ffn_norm_w, m_final_norm_w, m_attn_b_f, m_sgu_w_s, m_sgu_b_s))
    small_v = pack((v_mixer_norm_w, v_ffn_norm_w, v_final_norm_w, v_attn_b_f, v_sgu_w_s, v_sgu_b_s))
    small_all = _all_gather([small_g], "gather_small_grads")[0]
    small_out = [_unpack_small(p) for p in _adam_sum(small_all, small_w, small_m, small_v, "adam_small")]
    loss = small_out[0][4][0]

    blocks = [
        _cols_to_blocks(jnp.stack(d_attn_in)), _rows_to_blocks(jnp.stack(d_attn_out)),
        _cols_to_blocks(jnp.stack(d_sgu_in)), _rows_to_blocks(jnp.stack(d_sgu_out)),
        _cols_to_blocks(_ffn_deinterleave(jnp.stack(d_ffn_in))), _rows_to_blocks(jnp.stack(d_ffn_out)),
        jnp.stack(d_lg).reshape(2, N_DEV, 1, SGU_W // N_DEV).transpose(1, 0, 2, 3).reshape(N_DEV, 2, SGU_W // N_DEV),
        jnp.stack(d_lb).reshape(2, N_DEV, 1, SGU_W // N_DEV).transpose(1, 0, 2, 3).reshape(N_DEV, 2, SGU_W // N_DEV),
    ]
    received = _exchange([b.astype(BF16) for b in blocks[:6]] + blocks[6:], "exchange_grads")
    names = ["attn_w_in", "attn_w_out", "sgu_w_in", "sgu_w_out", "ffn_w_in", "ffn_w_out"]
    ws = [attn_w_in, attn_w_out, sgu_w_in, sgu_w_out, ffn_w_in, ffn_w_out]
    ms = [m_attn_w_in, m_attn_w_out, m_sgu_w_in, m_sgu_w_out, m_ffn_w_in, m_ffn_w_out]
    vs = [v_attn_w_in, v_attn_w_out, v_sgu_w_in, v_sgu_w_out, v_ffn_w_in, v_ffn_w_out]
    big_out = {}
    for nm, rec, w, m, v in zip(names, received[:6], ws, ms, vs):
        flat = lambda a: a.reshape(-1, a.shape[-1])
        big_out[nm] = [o.reshape(w.shape) for o in _adam_sum(rec, flat(w), flat(m), flat(v), f"adam_{nm}")]
    pad8 = lambda a: jnp.pad(a, [(0, 0)] * (a.ndim - 2) + [(0, 8 - a.shape[-2]), (0, 0)])
    ln_parts = jnp.concatenate([pad8(received[6]), pad8(received[7])], axis=1)
    ln_pack = lambda g, b: jnp.concatenate([pad8(g), pad8(b)], axis=0)
    ln_out = _adam_sum(ln_parts, ln_pack(sgu_ln_g, sgu_ln_b), ln_pack(m_sgu_ln_g, m_sgu_ln_b),
                       ln_pack(v_sgu_ln_g, v_sgu_ln_b), "adam_sgu_ln")

    def leaf(kind):
        mixer, ffn, final, b_f, _, w_s, b_s = small_out[kind]
        o = lambda nm: big_out[nm][kind]
        return [mixer, o("attn_w_in"), b_f, o("attn_w_out"), o("sgu_w_in"), ln_out[kind][0:2], ln_out[kind][8:10],
                w_s, b_s, o("sgu_w_out"), ffn, o("ffn_w_in"), o("ffn_w_out"), final]

    return (loss, grad_x, *leaf(0), *leaf(1), *leaf(2), *leaf(3))
```

```python
import functools

import jax
import jax.numpy as jnp
from jax import lax
from jax.experimental import pallas as pl
from jax.experimental.pallas import tpu as pltpu

F32 = jnp.float32
BF16 = jnp.bfloat16

D_MODEL = 1024
HEAD_DIM = 64
N_PAIR = 8
LANES = 128
SGU_W = 2048
SGU_G = 16
CHUNK = 128
FFN_H = 2816
FFN_TILE = 256
N_FFN_TILE = FFN_H // FFN_TILE
NORM_EPS = 1e-6
LN_EPS = 1e-5
QK_SCALE = 0.125
ATT_BLOCK = 512
N_DEV = 8
ADAM_LR = 0.001
ADAM_B1 = 0.9
ADAM_B2 = 0.999
ADAM_EPS = 1e-08
ADAM_WD = 0.01
ADAM_STEP = 10
MESH = pl.DeviceIdType.MESH
SQRT_HALF = 0.7071067811865476
INV_SQRT_2PI = 0.3989422804014327

NT_DIMS = (((1,), (1,)), ((), ()))
TN_DIMS = (((0,), (0,)), ((), ()))


def _gelu(x):
    return 0.5 * x * (1.0 + lax.erf(x * SQRT_HALF))


def _gelu_grad(x):
    return 0.5 * (1.0 + lax.erf(x * SQRT_HALF)) + x * jnp.exp(-0.5 * x * x) * INV_SQRT_2PI


def _lane_col(v, lane):
    idx = lax.broadcasted_iota(jnp.int32, v.shape, 1)
    return jnp.sum(jnp.where(idx == lane, v, 0.0), axis=1, keepdims=True)


def _params(sem, vmem_mb):
    return pltpu.CompilerParams(dimension_semantics=sem, vmem_limit_bytes=vmem_mb << 20)


def _cat_groups(ref, n):
    if n == 1:
        return ref[0]
    return jnp.concatenate([ref[t] for t in range(n)], axis=1)


def _rowcall(name, body, T, tm, ins, outs, vmem_mb, scratch=()):
    def spec(shape, kind):
        shape = tuple(shape)
        if kind == "row":
            return pl.BlockSpec((tm,) + shape[1:], lambda i: (i,) + (0,) * (len(shape) - 1))
        if kind == "grp":
            return pl.BlockSpec((shape[0], tm, shape[2]), lambda i: (0, i, 0))
        return pl.BlockSpec(shape, lambda i: (0,) * len(shape))

    return pl.pallas_call(
        body,
        name=name,
        grid=(T // tm,),
        in_specs=[spec(a.shape, k) for a, k in ins],
        out_specs=[spec(s, k) for s, _, k in outs],
        out_shape=[jax.ShapeDtypeStruct(tuple(s), d) for s, d, _ in outs],
        scratch_shapes=list(scratch),
        compiler_params=_params(("arbitrary",), vmem_mb),
    )(*[a for a, _ in ins])


def _norm_matmul(x, nw, w, out_dtype, name, tn, groups=False):
    T, N = x.shape[0], w.shape[1]
    tm = min(512, T)

    def body(x_ref, nw_ref, w_ref, o_ref, h_ref, h_scr):
        @pl.when(pl.program_id(1) == 0)
        def _():
            xv = x_ref[...]
            r = lax.rsqrt(jnp.mean(xv * xv, axis=-1, keepdims=True) + NORM_EPS)
            hv = (xv * r * nw_ref[...]).astype(BF16)
            h_scr[...] = hv
            h_ref[...] = hv

        acc = jnp.dot(h_scr[...], w_ref[...], preferred_element_type=F32)
        if groups:
            for t in range(tn // LANES):
                o_ref[t] = acc[:, LANES * t:LANES * (t + 1)].astype(out_dtype)
        else:
            o_ref[...] = acc.astype(out_dtype)

    if groups:
        o_shape = (N // LANES, T, LANES)
        o_spec = pl.BlockSpec((tn // LANES, tm, LANES), lambda i, j: (j, i, 0))
    else:
        o_shape = (T, N)
        o_spec = pl.BlockSpec((tm, tn), lambda i, j: (i, j))
    return pl.pallas_call(
        body,
        name=name,
        grid=(T // tm, N // tn),
        in_specs=[
            pl.BlockSpec((tm, D_MODEL), lambda i, j: (i, 0)),
            pl.BlockSpec((1, D_MODEL), lambda i, j: (0, 0)),
            pl.BlockSpec((D_MODEL, tn), lambda i, j: (0, j)),
        ],
        out_specs=[o_spec, pl.BlockSpec((tm, D_MODEL), lambda i, j: (i, 0))],
        out_shape=[jax.ShapeDtypeStruct(o_shape, out_dtype), jax.ShapeDtypeStruct((T, D_MODEL), BF16)],
        scratch_shapes=[pltpu.VMEM((tm, D_MODEL), BF16)],
        compiler_params=_params(("arbitrary", "arbitrary"), 32),
    )(x, nw, w)


def _matmul_tn(a, g, name, tk, tn, a_grp=False, g_grp=False):
    T = a.shape[1] if a_grp else a.shape[0]
    K = a.shape[0] * LANES if a_grp else a.shape[1]
    N = g.shape[0] * LANES if g_grp else g.shape[1]
    tm = min(512, T)

    def body(a_ref, g_ref, o_ref):
        @pl.when(pl.program_id(2) == 0)
        def _():
            o_ref[...] = jnp.zeros(o_ref.shape, F32)

        av = _cat_groups(a_ref, tk // LANES) if a_grp else a_ref[...]
        gv = _cat_groups(g_ref, tn // LANES) if g_grp else g_ref[...]
        o_ref[...] += lax.dot_general(av.astype(BF16), gv.astype(BF16), TN_DIMS, preferred_element_type=F32)

    if a_grp:
        a_spec = pl.BlockSpec((tk // LANES, tm, LANES), lambda k, n, m: (k, m, 0))
    else:
        a_spec = pl.BlockSpec((tm, tk), lambda k, n, m: (m, k))
    if g_grp:
        g_spec = pl.BlockSpec((tn // LANES, tm, LANES), lambda k, n, m: (n, m, 0))
    else:
        g_spec = pl.BlockSpec((tm, tn), lambda k, n, m: (m, n))
    return pl.pallas_call(
        body,
        name=name,
        grid=(K // tk, N // tn, T // tm),
        in_specs=[a_spec, g_spec],
        out_specs=pl.BlockSpec((tk, tn), lambda k, n, m: (k, n)),
        out_shape=jax.ShapeDtypeStruct((K, N), F32),
        compiler_params=_params(("parallel", "parallel", "arbitrary"), 40),
    )(a, g)


def _matmul_rms_bwd(a_list, wt_list, x, nw, dres, name, tm, vmem_mb):
    T = x.shape[0]
    n = len(a_list)

    def body(*refs):
        a_refs, w_refs = refs[:n], refs[n:2 * n]
        x_ref, nw_ref, dres_ref, dx_ref, dnw_ref = refs[2 * n:]

        @pl.when(pl.program_id(0) == 0)
        def _():
            dnw_ref[...] = jnp.zeros(dnw_ref.shape, F32)

        dh = None
        for (arr, kind), a_ref, w_ref in zip(a_list, a_refs, w_refs):
            av = _cat_groups(a_ref, arr.shape[0]) if kind == "grp" else a_ref[...]
            part = jnp.dot(av.astype(BF16), w_ref[...], preferred_element_type=F32)
            dh = part if dh is None else dh + part
        xv = x_ref[...]
        r = lax.rsqrt(jnp.mean(xv * xv, axis=-1, keepdims=True) + NORM_EPS)
        xn = xv * r
        dnw_ref[...] += jnp.sum(dh * xn, axis=0, keepdims=True)
        dyw = dh * nw_ref[...]
        dx_ref[...] = dres_ref[...] + r * (dyw - xn * jnp.mean(dyw * xn, axis=-1, keepdims=True))

    ins = list(a_list) + [(w, "full") for w in wt_list] + [(x, "row"), (nw, "full"), (dres, "row")]
    outs = [((T, D_MODEL), F32, "row"), ((1, D_MODEL), F32, "full")]
    return _rowcall(name, body, T, tm, ins, outs, vmem_mb)


def _fgate_fwd(fl3, bcol, name):
    n_chunk = fl3.shape[0]

    def body(fl_ref, b_ref, c_ref):
        r = lax.broadcasted_iota(jnp.int32, (CHUNK, CHUNK), 0)
        t = lax.broadcasted_iota(jnp.int32, (CHUNK, CHUNK), 1)
        tri = jnp.where(r <= t, 1.0, 0.0).astype(BF16)

        def chunk(i, carry):
            z = fl_ref[i] + b_ref[...]
            lf = jnp.minimum(z, 0.0) - jnp.log(1.0 + jnp.exp(-jnp.abs(z)))
            hi = lf.astype(BF16)
            r1 = lf - hi.astype(F32)
            mid = r1.astype(BF16)
            low = (r1 - mid.astype(F32)).astype(BF16)
            cs = (jnp.dot(hi, tri, preferred_element_type=F32) + jnp.dot(mid, tri, preferred_element_type=F32)
                  + jnp.dot(low, tri, preferred_element_type=F32)) + carry
            c_ref[i] = cs
            return _lane_col(cs, CHUNK - 1)

        lax.fori_loop(0, n_chunk, chunk, jnp.zeros((2 * N_PAIR, 1), F32))

    return pl.pallas_call(
        body, name=name, out_shape=jax.ShapeDtypeStruct(fl3.shape, F32),
        compiler_params=pltpu.CompilerParams(vmem_limit_bytes=16 << 20),
    )(fl3, bcol)


def _fgate_bwd(dc3, fl3, bcol, name):
    n_chunk = fl3.shape[0]

    def body(dc_ref, fl_ref, b_ref, dfl_ref, db_ref):
        tt = lax.broadcasted_iota(jnp.int32, (CHUNK, CHUNK), 0)
        rr = lax.broadcasted_iota(jnp.int32, (CHUNK, CHUNK), 1)
        tri = jnp.where(tt >= rr, 1.0, 0.0).astype(BF16)

        def chunk(k, carry):
            tail, acc = carry
            i = n_chunk - 1 - k
            dc = dc_ref[i]
            hi = dc.astype(BF16)
            r1 = dc - hi.astype(F32)
            mid = r1.astype(BF16)
            low = (r1 - mid.astype(F32)).astype(BF16)
            dlf = (jnp.dot(hi, tri, preferred_element_type=F32) + jnp.dot(mid, tri, preferred_element_type=F32)
                   + jnp.dot(low, tri, preferred_element_type=F32)) + tail
            z = fl_ref[i] + b_ref[...]
            dfl = dlf / (1.0 + jnp.exp(z))
            dfl_ref[i] = dfl
            return _lane_col(dlf, 0), acc + dfl

        _, acc = lax.fori_loop(0, n_chunk, chunk,
                               (jnp.zeros((2 * N_PAIR, 1), F32), jnp.zeros((2 * N_PAIR, CHUNK), F32)))
        db_ref[...] = jnp.broadcast_to(jnp.sum(acc, axis=1, keepdims=True), db_ref.shape)

    return pl.pallas_call(
        body, name=name,
        out_shape=[jax.ShapeDtypeStruct(fl3.shape, F32), jax.ShapeDtypeStruct((2 * N_PAIR, LANES), F32)],
        compiler_params=pltpu.CompilerParams(vmem_limit_bytes=16 << 20),
    )(dc3, fl3, bcol)


def _attn_fwd(qkv3, c4, name):
    T = qkv3.shape[1]
    tb = min(ATT_BLOCK, T)

    def body(q_ref, k_ref, v_ref, c_ref, o_ref, lse_ref):
        i = pl.program_id(1)
        lo = lax.broadcasted_iota(jnp.int32, (tb, LANES), 1) < HEAD_DIM
        row = lax.broadcasted_iota(jnp.int32, (tb, tb), 0)
        col = lax.broadcasted_iota(jnp.int32, (tb, tb), 1)
        q2 = q_ref[0] * QK_SCALE
        zero = jnp.zeros_like(q2)
        qh = (jnp.where(lo, q2, zero), jnp.where(lo, zero, q2))

        def step(j, carry, masked):
            m_old, l_old, acc = carry[:2], carry[2:4], carry[4]
            off = pl.multiple_of(j * tb, tb)
            kb = k_ref[0, pl.ds(off, tb), :]
            vb = v_ref[0, pl.ds(off, tb), :]
            m_new, l_new, upd = [], [], []
            for hh in range(2):
                s = lax.dot_general(qh[hh], kb, NT_DIMS, preferred_element_type=F32)
                s = s - c_ref[0, j, hh:hh + 1, :]
                if masked:
                    s = jnp.where(row >= col, s, -jnp.inf)
                m = jnp.maximum(m_old[hh], jnp.max(s, axis=1, keepdims=True))
                p = jnp.exp(s - m)
                alpha = jnp.exp(m_old[hh] - m)
                m_new.append(m)
                l_new.append(alpha * l_old[hh] + jnp.sum(p, axis=1, keepdims=True))
                upd.append(alpha * acc + jnp.dot(p.astype(BF16), vb, preferred_element_type=F32))
            return (m_new[0], m_new[1], l_new[0], l_new[1], jnp.where(lo, upd[0], upd[1]))

        ninf = jnp.full((tb, 1), -jnp.inf, F32)
        zcol = jnp.zeros((tb, 1), F32)
        carry = lax.fori_loop(0, i, lambda j, c: step(j, c, False),
                              (ninf, ninf, zcol, zcol, jnp.zeros((tb, LANES), F32)))
        m0, m1, l0, l1, acc = step(i, carry, True)
        o_ref[0] = (acc * jnp.where(lo, 1.0 / l0, 1.0 / l1)).astype(BF16)
        lse_ref[0] = jnp.where(lo, m0 + jnp.log(l0), m1 + jnp.log(l1))

    blk = lambda base: pl.BlockSpec((1, T, LANES), lambda h, i: (base + h, 0, 0))
    tile = pl.BlockSpec((1, tb, LANES), lambda h, i: (h, i, 0))
    return pl.pallas_call(
        body,
        name=name,
        grid=(N_PAIR, T // tb),
        in_specs=[tile, blk(N_PAIR), blk(2 * N_PAIR), pl.BlockSpec((1, T // tb, 2, tb), lambda h, i: (h, 0, 0, 0))],
        out_specs=[tile, tile],
        out_shape=[jax.ShapeDtypeStruct((N_PAIR, T, LANES), BF16), jax.ShapeDtypeStruct((N_PAIR, T, LANES), F32)],
        compiler_params=_params(("parallel", "arbitrary"), 40),
    )(qkv3, qkv3, qkv3, c4)


def _attn_bwd(qkv3, do3, st3, c4, name):
    T = qkv3.shape[1]
    tb = min(ATT_BLOCK, T)
    nb = T // tb

    def body(q_ref, do_ref, st_ref, k_ref, v_ref, c_ref, dq_hbm, rs_hbm, dk_ref, dv_ref, cs_ref,
             dq_acc, rs_acc, dk_acc, dv_acc):
        h, j = pl.program_id(0), pl.program_id(1)
        lo = lax.broadcasted_iota(jnp.int32, (tb, LANES), 1) < HEAD_DIM
        hi = jnp.logical_not(lo)
        row = lax.broadcasted_iota(jnp.int32, (tb, tb), 0)
        col = lax.broadcasted_iota(jnp.int32, (tb, tb), 1)

        @pl.when(j == 0)
        def _():
            dq_acc[...] = jnp.zeros(dq_acc.shape, F32)
            rs_acc[...] = jnp.zeros(rs_acc.shape, F32)

        kb = k_ref[0]
        vb = v_ref[0]
        ks = kb * QK_SCALE
        one = jnp.ones_like(ks)
        zero = jnp.zeros_like(ks)
        ks1 = (jnp.where(lo, ks, one), jnp.where(hi, ks, one))
        dk_acc[...] = jnp.zeros(dk_acc.shape, F32)
        dv_acc[...] = jnp.zeros(dv_acc.shape, F32)
        crow = [c_ref[0, j, hh:hh + 1, :] for hh in range(2)]

        def step(i, masked):
            off = pl.multiple_of(i * tb, tb)
            qb = q_ref[0, pl.ds(off, tb), :] * QK_SCALE
            dob = do_ref[0, pl.ds(off, tb), :]
            st = st_ref[0, pl.ds(off, tb), :]
            dqs = []
            for hh, sel in enumerate((lo, hi)):
                lse = _lane_col(st, HEAD_DIM * hh)
                dd = _lane_col(st, HEAD_DIM * hh + 32)
                s = lax.dot_general(jnp.where(sel, qb, zero), kb, NT_DIMS, preferred_element_type=F32) - crow[hh]
                if masked:
                    s = jnp.where(row >= col, s, -jnp.inf)
                p = jnp.exp(s - lse)
                dv_acc[hh] += lax.dot_general(p.astype(BF16), dob, TN_DIMS, preferred_element_type=F32)
                dp = lax.dot_general(jnp.where(sel, dob, zero), vb, NT_DIMS, preferred_element_type=F32)
                dsb = (p * (dp - dd)).astype(BF16)
                dk_acc[hh] += lax.dot_general(dsb, jnp.where(sel, qb, one), TN_DIMS, preferred_element_type=F32)
                dqs.append(jnp.dot(dsb, ks1[hh], preferred_element_type=F32))
            dq_acc[pl.ds(off, tb), :] += jnp.where(lo, dqs[0], dqs[1])
            rs_acc[pl.ds(off, tb), :] += jnp.where(lo, dqs[1], dqs[0])

        step(j, True)

        def loop_body(i, carry):
            step(i, False)
            return carry

        lax.fori_loop(j + 1, nb, loop_body, 0)
        dk_ref[0] = jnp.where(lo, dk_acc[0], dk_acc[1]).astype(BF16)
        cs_ref[0] = jnp.where(lo, dk_acc[1], dk_acc[0])
        dv_ref[0] = jnp.where(lo, dv_acc[0], dv_acc[1]).astype(BF16)

        @pl.when(j == nb - 1)
        def _():
            pltpu.sync_copy(dq_acc, dq_hbm.at[h])
            pltpu.sync_copy(rs_acc, rs_hbm.at[h])

    res = lambda base: pl.BlockSpec((1, T, LANES), lambda h, j: (base + h, 0, 0), pipeline_mode=pl.Buffered(1))
    tile = lambda base: pl.BlockSpec((1, tb, LANES), lambda h, j: (base + h, j, 0))
    hbm = pl.BlockSpec(memory_space=pl.ANY)
    return pl.pallas_call(
        body,
        name=name,
        grid=(N_PAIR, nb),
        in_specs=[res(0), res(0), res(0), tile(N_PAIR), tile(2 * N_PAIR),
                  pl.BlockSpec((1, nb, 2, tb), lambda h, j: (h, 0, 0, 0))],
        out_specs=[hbm, hbm, tile(0), tile(0), tile(0)],
        out_shape=[
            jax.ShapeDtypeStruct((N_PAIR, T, LANES), F32),
            jax.ShapeDtypeStruct((N_PAIR, T, LANES), F32),
            jax.ShapeDtypeStruct((N_PAIR, T, LANES), BF16),
            jax.ShapeDtypeStruct((N_PAIR, T, LANES), BF16),
            jax.ShapeDtypeStruct((N_PAIR, T, LANES), F32),
        ],
        scratch_shapes=[pltpu.VMEM((T, LANES), F32), pltpu.VMEM((T, LANES), F32),
                        pltpu.VMEM((2, tb, LANES), F32), pltpu.VMEM((2, tb, LANES), F32)],
        compiler_params=_params(("arbitrary", "arbitrary"), 56),
    )(qkv3, do3, st3, qkv3, qkv3, c4)


def _attn_out(o3, w, x, name):
    T = x.shape[0]

    def body(o_ref, w_ref, x_ref, out_ref):
        out_ref[...] = x_ref[...] + jnp.dot(_cat_groups(o_ref, N_PAIR), w_ref[...], preferred_element_type=F32)

    return _rowcall(name, body, T, min(512, T), [(o3, "grp"), (w, "full"), (x, "row")],
                    [((T, D_MODEL), F32, "row")], 24)[0]


def _attn_dout(dx, wt, o3, lse3, name):
    T = dx.shape[0]
    tm = min(512, T)

    def body(dx_ref, w_ref, o_ref, lse_ref, do_ref, st_ref):
        do = jnp.dot(dx_ref[...].astype(BF16), w_ref[...], preferred_element_type=F32).astype(BF16)
        lane = lax.broadcasted_iota(jnp.int32, (tm, LANES), 1)
        lo = lane < HEAD_DIM
        is_lse = jnp.bitwise_and(lane, HEAD_DIM - 1) < 32
        for t in range(N_PAIR):
            d = do[:, LANES * t:LANES * (t + 1)]
            do_ref[t] = d
            prod = d.astype(F32) * o_ref[t].astype(F32)
            d0 = jnp.sum(jnp.where(lo, prod, 0.0), axis=1, keepdims=True)
            d1 = jnp.sum(jnp.where(lo, 0.0, prod), axis=1, keepdims=True)
            st_ref[t] = jnp.where(is_lse, lse_ref[t], jnp.where(lo, d0, d1))

    return _rowcall(name, body, T, tm, [(dx, "row"), (wt, "full"), (o3, "grp"), (lse3, "grp")],
                    [((N_PAIR, T, LANES), BF16, "grp"), ((N_PAIR, T, LANES), F32, "grp")], 32)


def _ffn_out(gu, w, x, name):
    T = x.shape[0]
    tm = min(256, T)

    def body(gu_ref, w_ref, x_ref, out_ref, hid_ref):
        acc = x_ref[...]
        for j in range(N_FFN_TILE):
            g = gu_ref[:, 2 * FFN_TILE * j:2 * FFN_TILE * j + FFN_TILE].astype(F32)
            u = gu_ref[:, 2 * FFN_TILE * j + FFN_TILE:2 * FFN_TILE * (j + 1)].astype(F32)
            hj = (g * jax.nn.sigmoid(g) * u).astype(BF16)
            hid_ref[:, FFN_TILE * j:FFN_TILE * (j + 1)] = hj
            acc = acc + jnp.dot(hj, w_ref[FFN_TILE * j:FFN_TILE * (j + 1), :], preferred_element_type=F32)
        out_ref[...] = acc

    return _rowcall(name, body, T, tm, [(gu, "row"), (w, "full"), (x, "row")],
                    [((T, D_MODEL), F32, "row"), ((T, FFN_H), BF16, "row")], 40)


def _ffn_dgu(dx, wt, gu, name):
    T = dx.shape[0]
    tm = min(256, T)

    def body(dx_ref, w_ref, gu_ref, dgu_ref):
        dxb = dx_ref[...].astype(BF16)
        for j in range(N_FFN_TILE):
            dh = jnp.dot(dxb, w_ref[:, FFN_TILE * j:FFN_TILE * (j + 1)], preferred_element_type=F32)
            g = gu_ref[:, 2 * FFN_TILE * j:2 * FFN_TILE * j + FFN_TILE].astype(F32)
            u = gu_ref[:, 2 * FFN_TILE * j + FFN_TILE:2 * FFN_TILE * (j + 1)].astype(F32)
            sg = jax.nn.sigmoid(g)
            dgu_ref[:, 2 * FFN_TILE * j:2 * FFN_TILE * j + FFN_TILE] = (
                dh * u * (sg * (1.0 + g * (1.0 - sg)))).astype(BF16)
            dgu_ref[:, 2 * FFN_TILE * j + FFN_TILE:2 * FFN_TILE * (j + 1)] = (dh * (g * sg)).astype(BF16)

    return _rowcall(name, body, T, tm, [(dx, "row"), (wt, "full"), (gu, "row")],
                    [((T, 2 * FFN_H), BF16, "row")], 40)[0]


def _sgu_core(a, ln_g, ln_b, w_s, bst, w, x, name):
    T = x.shape[0]
    tm = min(256, T)

    def body(a_ref, lg_ref, lb_ref, ws_ref, bs_ref, w_ref, x_ref, out_ref, gated_ref, vn_ref, mixed_ref):
        v = _gelu(a_ref[:, SGU_W:].astype(F32))
        mu = jnp.mean(v, axis=-1, keepdims=True)
        vc = v - mu
        rstd = lax.rsqrt(jnp.mean(vc * vc, axis=-1, keepdims=True) + LN_EPS)
        vn_ref[...] = (vc * rstd * lg_ref[...] + lb_ref[...]).astype(BF16)
        tt = lax.broadcasted_iota(jnp.int32, (CHUNK, CHUNK), 0)
        ss = lax.broadcasted_iota(jnp.int32, (CHUNK, CHUNK), 1)
        for g in range(SGU_G):
            wg = jnp.where(tt >= ss, ws_ref[g], 0.0).astype(BF16)
            bcol = _lane_col(bs_ref[...], g)
            cols = slice(CHUNK * g, CHUNK * (g + 1))
            for c in range(tm // CHUNK):
                rows = slice(CHUNK * c, CHUNK * (c + 1))
                mixed = jnp.dot(wg, vn_ref[rows, cols], preferred_element_type=F32) + bcol
                u = _gelu(a_ref[rows, cols].astype(F32))
                mixed_ref[rows, cols] = mixed.astype(BF16)
                gated_ref[rows, cols] = (u * mixed).astype(BF16)
        out_ref[...] = x_ref[...] + jnp.dot(gated_ref[...], w_ref[...], preferred_element_type=F32)

    ins = [(a, "row"), (ln_g, "full"), (ln_b, "full"), (w_s, "full"), (bst, "full"), (w, "full"), (x, "row")]
    outs = [((T, D_MODEL), F32, "row")] + [((T, SGU_W), BF16, "row")] * 3
    return _rowcall(name, body, T, tm, ins, outs, 40)


def _sgu_core_bwd(dx, wt, a, vn, mixed, ln_g, w_s, name):
    T = dx.shape[0]
    tm = min(256, T)

    def body(dx_ref, wt_ref, a_ref, vn_ref, mx_ref, lg_ref, ws_ref,
             da_ref, dws_ref, dba_ref, dlg_ref, dlb_ref, dg_scr, dvn_scr):
        @pl.when(pl.program_id(0) == 0)
        def _():
            dws_ref[...] = jnp.zeros(dws_ref.shape, F32)
            dba_ref[...] = jnp.zeros(dba_ref.shape, F32)
            dlg_ref[...] = jnp.zeros(dlg_ref.shape, F32)
            dlb_ref[...] = jnp.zeros(dlb_ref.shape, F32)

        dg_scr[...] = jnp.dot(dx_ref[...].astype(BF16), wt_ref[...], preferred_element_type=F32)
        tt = lax.broadcasted_iota(jnp.int32, (CHUNK, CHUNK), 0)
        ss = lax.broadcasted_iota(jnp.int32, (CHUNK, CHUNK), 1)
        tril = tt >= ss
        for g in range(SGU_G):
            wg = jnp.where(tril, ws_ref[g], 0.0).astype(BF16)
            cols = slice(CHUNK * g, CHUNK * (g + 1))
            for c in range(tm // CHUNK):
                rows = slice(CHUNK * c, CHUNK * (c + 1))
                dgb = dg_scr[rows, cols]
                au = a_ref[rows, cols].astype(F32)
                dmx = dgb * _gelu(au)
                da_ref[rows, cols] = (dgb * mx_ref[rows, cols].astype(F32) * _gelu_grad(au)).astype(BF16)
                dmb = dmx.astype(BF16)
                dvn_scr[rows, cols] = lax.dot_general(wg, dmb, TN_DIMS, preferred_element_type=F32)
                dws_ref[g] += jnp.where(
                    tril, lax.dot_general(dmb, vn_ref[rows, cols], NT_DIMS, preferred_element_type=F32), 0.0)
                dba_ref[:, cols] += dmx
        av = a_ref[:, SGU_W:].astype(F32)
        v = _gelu(av)
        mu = jnp.mean(v, axis=-1, keepdims=True)
        vc = v - mu
        rstd = lax.rsqrt(jnp.mean(vc * vc, axis=-1, keepdims=True) + LN_EPS)
        xhat = vc * rstd
        dvn = dvn_scr[...]
        dlg_ref[...] += jnp.sum(dvn * xhat, axis=0, keepdims=True)
        dlb_ref[...] += jnp.sum(dvn, axis=0, keepdims=True)
        dxh = dvn * lg_ref[...]
        dv = rstd * (dxh - jnp.mean(dxh, axis=-1, keepdims=True)
                     - xhat * jnp.mean(dxh * xhat, axis=-1, keepdims=True))
        da_ref[:, SGU_W:] = (dv * _gelu_grad(av)).astype(BF16)

    ins = [(dx, "row"), (wt, "full"), (a, "row"), (vn, "row"), (mixed, "row"), (ln_g, "full"), (w_s, "full")]
    outs = [((T, 2 * SGU_W), BF16, "row"), ((SGU_G, CHUNK, CHUNK), F32, "full"), ((CHUNK, SGU_W), F32, "full"),
            ((1, SGU_W), F32, "full"), ((1, SGU_W), F32, "full")]
    return _rowcall(name, body, T, tm, ins, outs, 40,
                    scratch=[pltpu.VMEM((tm, SGU_W), F32), pltpu.VMEM((tm, SGU_W), F32)])


def _loss_head(x, wf, tgt, name):
    T = x.shape[0]
    tm = min(512, T)

    def body(x_ref, wf_ref, tgt_ref, dx_ref, dwf_ref, loss_ref):
        @pl.when(pl.program_id(0) == 0)
        def _():
            dwf_ref[...] = jnp.zeros(dwf_ref.shape, F32)
            loss_ref[...] = jnp.zeros(loss_ref.shape, F32)

        xv = x_ref[...]
        r = lax.rsqrt(jnp.mean(xv * xv, axis=-1, keepdims=True) + NORM_EPS)
        xn = xv * r
        err = xn * wf_ref[...] - tgt_ref[...]
        loss_ref[...] += 0.5 * jnp.sum(jnp.mean(err * err, axis=-1, keepdims=True), axis=0, keepdims=True)
        dy = err * (1.0 / D_MODEL)
        dwf_ref[...] += jnp.sum(dy * xn, axis=0, keepdims=True)
        dyw = dy * wf_ref[...]
        dx_ref[...] = r * (dyw - xn * jnp.mean(dyw * xn, axis=-1, keepdims=True))

    return _rowcall(name, body, T, tm, [(x, "row"), (wf, "full"), (tgt, "row")],
                    [((T, D_MODEL), F32, "row"), ((1, D_MODEL), F32, "full"), ((8, LANES), F32, "full")], 32)


def _peers():
    x, y, c = lax.axis_index("x"), lax.axis_index("y"), lax.axis_index("c")
    peers = []
    for p in range(1, N_DEV):
        px = 1 - x if p & 4 else x
        py = 1 - y if p & 2 else y
        pc = 1 - c if p & 1 else c
        peers.append((4 * px + 2 * py + pc, (px, py, pc)))
    return 4 * x + 2 * y + c, peers


def _all_gather(arrs, name):
    n = len(arrs)
    hbm = pl.BlockSpec(memory_space=pl.ANY)

    def body(*refs):
        ins, outs = refs[:n], refs[n:2 * n]
        send_sems, recv_sems, local_sems = refs[2 * n:]
        me, peers = _peers()
        sends, recvs, locals_ = [], [], []
        for t in range(n):
            cp = pltpu.make_async_copy(ins[t], outs[t].at[me], local_sems.at[t])
            cp.start()
            locals_.append(cp)
            for k, (pidx, pid) in enumerate(peers):
                s = t * (N_DEV - 1) + k
                send = pltpu.make_async_remote_copy(
                    src_ref=ins[t], dst_ref=outs[t].at[me], send_sem=send_sems.at[s], recv_sem=recv_sems.at[s],
                    device_id=pid, device_id_type=MESH)
                send.start()
                sends.append(send)
                recvs.append(pltpu.make_async_remote_copy(
                    src_ref=ins[t], dst_ref=outs[t].at[pidx], send_sem=send_sems.at[s], recv_sem=recv_sems.at[s],
                    device_id=pid, device_id_type=MESH))
        for r in recvs:
            r.wait_recv()
        for s in sends:
            s.wait_send()
        for cp in locals_:
            cp.wait()

    return pl.pallas_call(
        body,
        name=name,
        in_specs=[hbm] * n,
        out_specs=[hbm] * n,
        out_shape=[jax.ShapeDtypeStruct((N_DEV,) + a.shape, a.dtype) for a in arrs],
        scratch_shapes=[pltpu.SemaphoreType.DMA((n * (N_DEV - 1),)), pltpu.SemaphoreType.DMA((n * (N_DEV - 1),)),
                        pltpu.SemaphoreType.DMA((n,))],
    )(*arrs)


def _exchange(arrs, name):
    n = len(arrs)
    hbm = pl.BlockSpec(memory_space=pl.ANY)

    def body(*refs):
        ins, outs = refs[:n], refs[n:2 * n]
        send_sems, recv_sems, local_sems = refs[2 * n:]
        me, peers = _peers()
        sends, recvs, locals_ = [], [], []
        for t in range(n):
            cp = pltpu.make_async_copy(ins[t].at[me], outs[t].at[me], local_sems.at[t])
            cp.start()
            locals_.append(cp)
            for k, (pidx, pid) in enumerate(peers):
                s = t * (N_DEV - 1) + k
                send = pltpu.make_async_remote_copy(
                    src_ref=ins[t].at[pidx], dst_ref=outs[t].at[me], send_sem=send_sems.at[s],
                    recv_sem=recv_sems.at[s], device_id=pid, device_id_type=MESH)
                send.start()
                sends.append(send)
                recvs.append(pltpu.make_async_remote_copy(
                    src_ref=ins[t].at[pidx], dst_ref=outs[t].at[pidx], send_sem=send_sems.at[s],
                    recv_sem=recv_sems.at[s], device_id=pid, device_id_type=MESH))
        for r in recvs:
            r.wait_recv()
        for s in sends:
            s.wait_send()
        for cp in locals_:
            cp.wait()

    return pl.pallas_call(
        body,
        name=name,
        in_specs=[hbm] * n,
        out_specs=[hbm] * n,
        out_shape=[jax.ShapeDtypeStruct(a.shape, a.dtype) for a in arrs],
        scratch_shapes=[pltpu.SemaphoreType.DMA((n * (N_DEV - 1),)), pltpu.SemaphoreType.DMA((n * (N_DEV - 1),)),
                        pltpu.SemaphoreType.DMA((n,))],
    )(*arrs)


def _row_tile(rows, cap):
    best = None
    for t in range(16, cap + 1, 16):
        if rows % t == 0:
            best = t
    assert best is not None, rows
    return best


def _adam_sum(parts, w, m, v, name):
    R, C = w.shape
    tr = _row_tile(R, 128)

    def body(p_ref, w_ref, m_ref, v_ref, g_ref, d_ref, nm_ref, nv_ref):
        g = p_ref[0].astype(F32)
        for s in range(1, N_DEV):
            g = g + p_ref[s].astype(F32)
        mm = ADAM_B1 * m_ref[...] + (1.0 - ADAM_B1) * g
        vv = ADAM_B2 * v_ref[...] + (1.0 - ADAM_B2) * (g * g)
        m_hat = mm / (1.0 - ADAM_B1 ** ADAM_STEP)
        v_hat = vv / (1.0 - ADAM_B2 ** ADAM_STEP)
        g_ref[...] = g
        d_ref[...] = -ADAM_LR * (m_hat / (jnp.sqrt(v_hat) + ADAM_EPS) + ADAM_WD * w_ref[...])
        nm_ref[...] = mm
        nv_ref[...] = vv

    mat = pl.BlockSpec((tr, C), lambda i: (i, 0))
    return pl.pallas_call(
        body,
        name=name,
        grid=(R // tr,),
        in_specs=[pl.BlockSpec((N_DEV, tr, C), lambda i: (0, i, 0)), mat, mat, mat],
        out_specs=[mat] * 4,
        out_shape=[jax.ShapeDtypeStruct((R, C), F32)] * 4,
        compiler_params=_params(("parallel",), 32),
    )(parts, w, m, v)


def _cols_from_gathered(g):
    _, L, K, n = g.shape
    return jnp.transpose(g, (1, 2, 0, 3)).reshape(L, K, N_DEV * n)


def _rows_from_gathered(g):
    _, L, k, N = g.shape
    return jnp.transpose(g, (1, 0, 2, 3)).reshape(L, N_DEV * k, N)


def _cols_to_blocks(dw):
    L, K, N = dw.shape
    n = N // N_DEV
    return jnp.transpose(dw.reshape(L, K, N_DEV, n), (2, 0, 1, 3)).reshape(N_DEV, L * K, n)


def _rows_to_blocks(dw):
    L, K, N = dw.shape
    k = K // N_DEV
    return jnp.transpose(dw.reshape(L, N_DEV, k, N), (1, 0, 2, 3)).reshape(N_DEV, L * k, N)


def _ffn_interleave(w):
    lead = w.shape[:-1]
    t = w.reshape(lead + (2, N_FFN_TILE, FFN_TILE))
    return jnp.swapaxes(t, -3, -2).reshape(lead + (2 * FFN_H,))


def _ffn_deinterleave(w):
    lead = w.shape[:-1]
    t = w.reshape(lead + (N_FFN_TILE, 2, FFN_TILE))
    return jnp.swapaxes(t, -3, -2).reshape(lead + (2 * FFN_H,))


def _pad_rows(a, rows=8):
    a = a.reshape(-1, a.shape[-1])
    return jnp.pad(a, ((0, rows - a.shape[0]), (0, 0)))


SMALL_ROWS = 6 * 8 + 2 * SGU_G * CHUNK * CHUNK // D_MODEL


def _pack_small(mixer, ffn, final, b_f, extra, w_s, b_s):
    bf_row = jnp.pad(b_f.reshape(1, -1), ((0, 0), (0, D_MODEL - b_f.size)))
    bs_rows = jnp.pad(b_s.reshape(4, -1), ((0, 0), (0, D_MODEL - b_s.size // 4)))
    return jnp.concatenate([
        _pad_rows(mixer), _pad_rows(ffn), _pad_rows(final.reshape(1, -1)), _pad_rows(bf_row),
        _pad_rows(extra), _pad_rows(bs_rows), w_s.reshape(-1, D_MODEL)], axis=0)


def _unpack_small(p):
    mixer, ffn, final = p[0:4], p[8:12], p[16]
    b_f = p[24, :32].reshape(2, 2 * N_PAIR)
    extra = p[32]
    b_s = p[40:44, :2 * SGU_G * CHUNK // 4].reshape(2, SGU_G, CHUNK)
    w_s = p[48:].reshape(2, SGU_G, CHUNK, CHUNK)
    return mixer, ffn, final, b_f, extra, w_s, b_s


def kernel(x, mixer_norm_w, attn_w_in, attn_b_f, attn_w_out, sgu_w_in, sgu_ln_g, sgu_ln_b, sgu_w_s, sgu_b_s, sgu_w_out, ffn_norm_w, ffn_w_in, ffn_w_out, final_norm_w, loss_target, m_mixer_norm_w, m_attn_w_in, m_attn_b_f, m_attn_w_out, m_sgu_w_in, m_sgu_ln_g, m_sgu_ln_b, m_sgu_w_s, m_sgu_b_s, m_sgu_w_out, m_ffn_norm_w, m_ffn_w_in, m_ffn_w_out, m_final_norm_w, v_mixer_norm_w, v_attn_w_in, v_attn_b_f, v_attn_w_out, v_sgu_w_in, v_sgu_ln_g, v_sgu_ln_b, v_sgu_w_s, v_sgu_b_s, v_sgu_w_out, v_ffn_norm_w, v_ffn_w_in, v_ffn_w_out, v_final_norm_w):
    T = x.shape[1]
    tb = min(ATT_BLOCK, T)
    xs, tgt = x[0], loss_target[0]

    shards = [attn_w_in, attn_w_out, sgu_w_in, sgu_w_out, ffn_w_in, ffn_w_out, sgu_ln_g, sgu_ln_b]
    gathered = _all_gather([s.astype(BF16) for s in shards[:6]] + shards[6:], "gather_weights")
    w_attn_in = _cols_from_gathered(gathered[0])
    w_attn_out = _rows_from_gathered(gathered[1])
    w_sgu_in = _cols_from_gathered(gathered[2])
    w_sgu_out = _rows_from_gathered(gathered[3])
    w_ffn_in = _ffn_interleave(_cols_from_gathered(gathered[4]))
    w_ffn_out = _rows_from_gathered(gathered[5])
    ln_g = jnp.transpose(gathered[6], (1, 0, 2)).reshape(2, 1, SGU_W)
    ln_b = jnp.transpose(gathered[7], (1, 0, 2)).reshape(2, 1, SGU_W)
    w_qkv = w_attn_in[:, :, :3 * D_MODEL]
    w_f = jnp.pad(w_attn_in[:, :, 3 * D_MODEL:], ((0, 0), (0, 0), (0, LANES - 2 * N_PAIR)))
    tr = lambda w: jnp.swapaxes(w, -1, -2)
    w_qkv_t, w_f_t, w_attn_out_t = tr(w_qkv), tr(w_f), tr(w_attn_out)
    w_sgu_in_t, w_sgu_out_t, w_ffn_in_t, w_ffn_out_t = tr(w_sgu_in), tr(w_sgu_out), tr(w_ffn_in), tr(w_ffn_out)
    mixer_nw = mixer_norm_w.reshape(4, 1, D_MODEL)
    ffn_nw = ffn_norm_w.reshape(4, 1, D_MODEL)
    b_col = attn_b_f.reshape(2, 2 * N_PAIR, 1)
    bs_t = jnp.swapaxes(sgu_b_s, 1, 2)

    saved = []
    xr = xs
    for i in range(4):
        j = i // 2
        if i % 2 == 0:
            qkv3, h = _norm_matmul(xr, mixer_nw[i], w_qkv[j], BF16, f"attn_qkv_{j}", 512, groups=True)
            fl, _ = _norm_matmul(xr, mixer_nw[i], w_f[j], F32, f"attn_gate_{j}", LANES)
            fl3 = jnp.transpose(fl[:, :2 * N_PAIR].reshape(T // CHUNK, CHUNK, 2 * N_PAIR), (0, 2, 1))
            c_chunks = _fgate_fwd(fl3, b_col[j], f"fgate_fwd_{j}")
            c4 = jnp.transpose(jnp.transpose(c_chunks, (1, 0, 2)).reshape(N_PAIR, 2, T // tb, tb), (0, 2, 1, 3))
            o3, lse3 = _attn_fwd(qkv3, c4, f"attn_fwd_{j}")
            xm = _attn_out(o3, w_attn_out[j], xr, f"attn_out_{j}")
            mix_saved = (xr, h, qkv3, fl3, c4, o3, lse3)
        else:
            a, h = _norm_matmul(xr, mixer_nw[i], w_sgu_in[j], BF16, f"sgu_in_{j}", 512)
            xm, gated, vn, mixed = _sgu_core(a, ln_g[j], ln_b[j], sgu_w_s[j], bs_t[j], w_sgu_out[j], xr,
                                             f"sgu_core_{j}")
            mix_saved = (xr, h, a, gated, vn, mixed)
        gu, h2 = _norm_matmul(xm, ffn_nw[i], w_ffn_in[i], BF16, f"ffn_in_{i}", 512)
        xo, hid = _ffn_out(gu, w_ffn_out[i], xm, f"ffn_out_{i}")
        saved.append((mix_saved, (xm, h2, gu, hid)))
        xr = xo
    dx, d_final, loss_part = _loss_head(xr, final_norm_w.reshape(1, D_MODEL), tgt, "loss_head")

    d_mixer_nw, d_ffn_nw = [None] * 4, [None] * 4
    d_attn_in, d_attn_out, d_bf, d_sgu_in, d_sgu_out = [None] * 2, [None] * 2, [None] * 2, [None] * 2, [None] * 2
    d_ws, d_bs, d_lg, d_lb = [None] * 2, [None] * 2, [None] * 2, [None] * 2
    d_ffn_in, d_ffn_out = [None] * 4, [None] * 4
    for i in reversed(range(4)):
        j = i // 2
        mix_saved, (xm, h2, gu, hid) = saved[i]
        dgu = _ffn_dgu(dx, w_ffn_out_t[i], gu, f"ffn_dgu_{i}")
        d_ffn_out[i] = _matmul_tn(hid, dx, f"ffn_dwout_{i}", FFN_H // 2, D_MODEL)
        d_ffn_in[i] = _matmul_tn(h2, dgu, f"ffn_dwin_{i}", D_MODEL, 512)
        dx, d_ffn_nw[i] = _matmul_rms_bwd([(dgu, "row")], [w_ffn_in_t[i]], xm, ffn_nw[i], dx, f"ffn_dx_{i}", 256, 48)
        if i % 2 == 0:
            xr, h, qkv3, fl3, c4, o3, lse3 = mix_saved
            do3, st3 = _attn_dout(dx, w_attn_out_t[j], o3, lse3, f"attn_dout_{j}")
            d_attn_out[j] = _matmul_tn(o3, dx, f"attn_dwout_{j}", D_MODEL, D_MODEL, a_grp=True)
            dq3, rs3, dk3, dv3, cs3 = _attn_bwd(qkv3, do3, st3, c4, f"attn_bwd_{j}")
            dc_pair = jnp.stack([rs3[:, :, HEAD_DIM] - cs3[:, :, HEAD_DIM], rs3[:, :, 0] - cs3[:, :, 0]], axis=1)
            dc_chunks = jnp.transpose(dc_pair.reshape(2 * N_PAIR, T // CHUNK, CHUNK), (1, 0, 2))
            dfl3, db = _fgate_bwd(dc_chunks, fl3, b_col[j], f"fgate_bwd_{j}")
            d_bf[j] = db[:, 0]
            dfl = jnp.transpose(dfl3, (0, 2, 1)).reshape(T, 2 * N_PAIR)
            dfl = jnp.pad(dfl.astype(BF16), ((0, 0), (0, LANES - 2 * N_PAIR)))
            d_qkv = [_matmul_tn(h, d3, f"attn_dw{nm}_{j}", D_MODEL, D_MODEL, g_grp=True)
                     for nm, d3 in (("q", dq3), ("k", dk3), ("v", dv3))]
            d_f = _matmul_tn(h, dfl, f"attn_dwf_{j}", D_MODEL, LANES)[:, :2 * N_PAIR]
            d_attn_in[j] = jnp.concatenate(d_qkv + [d_f], axis=1)
            wts = [w_qkv_t[j, k * D_MODEL:(k + 1) * D_MODEL] for k in range(3)] + [w_f_t[j]]
            dx, d_mixer_nw[i] = _matmul_rms_bwd(
                [(dq3, "grp"), (dk3, "grp"), (dv3, "grp"), (dfl, "row")], wts, xr, mixer_nw[i], dx,
                f"attn_dx_{j}", 256, 40)
        else:
            xr, h, a, gated, vn, mixed = mix_saved
            da, d_ws[j], dba, d_lg[j], d_lb[j] = _sgu_core_bwd(dx, w_sgu_out_t[j], a, vn, mixed, ln_g[j], sgu_w_s[j],
                                                               f"sgu_core_bwd_{j}")
            d_bs[j] = jnp.sum(dba.reshape(CHUNK, SGU_G, CHUNK), axis=-1).T
            d_sgu_out[j] = _matmul_tn(gated, dx, f"sgu_dwout_{j}", D_MODEL, D_MODEL)
            d_sgu_in[j] = _matmul_tn(h, da, f"sgu_dwin_{j}", D_MODEL, 512)
            dx, d_mixer_nw[i] = _matmul_rms_bwd([(da, "row")], [w_sgu_in_t[j]], xr, mixer_nw[i], dx,
                                                f"sgu_dx_{j}", 256, 40)
    grad_x = dx[None]

    rows4 = lambda parts: jnp.concatenate(parts, axis=1).reshape(4, D_MODEL)
    small_g = _pack_small(rows4(d_mixer_nw), rows4(d_ffn_nw), d_final[0], jnp.stack(d_bf),
                          loss_part[0:1, 0:1] * jnp.ones((1, D_MODEL), F32), jnp.stack(d_ws), jnp.stack(d_bs))
    zero_row = jnp.zeros((1, D_MODEL), F32)
    pack = lambda pre: _pack_small(pre[0], pre[1], pre[2], pre[3], zero_row, pre[4], pre[5])
    small_w = pack((mixer_norm_w, ffn_norm_w, final_norm_w, attn_b_f, sgu_w_s, sgu_b_s))
    small_m = pack((m_mixer_norm_w, m_ffn_norm_w, m_final_norm_w, m_attn_b_f, m_sgu_w_s, m_sgu_b_s))
    small_v = pack((v_mixer_norm_w, v_ffn_norm_w, v_final_norm_w, v_attn_b_f, v_sgu_w_s, v_sgu_b_s))
    small_all = _all_gather([small_g], "gather_small_grads")[0]
    small_out = [_unpack_small(p) for p in _adam_sum(small_all, small_w, small_m, small_v, "adam_small")]
    loss = small_out[0][4][0]

    blocks = [
        _cols_to_blocks(jnp.stack(d_attn_in)), _rows_to_blocks(jnp.stack(d_attn_out)),
        _cols_to_blocks(jnp.stack(d_sgu_in)), _rows_to_blocks(jnp.stack(d_sgu_out)),
        _cols_to_blocks(_ffn_deinterleave(jnp.stack(d_ffn_in))), _rows_to_blocks(jnp.stack(d_ffn_out)),
        jnp.stack(d_lg).reshape(2, N_DEV, 1, SGU_W // N_DEV).transpose(1, 0, 2, 3).reshape(N_DEV, 2, SGU_W // N_DEV),
        jnp.stack(d_lb).reshape(2, N_DEV, 1, SGU_W // N_DEV).transpose(1, 0, 2, 3).reshape(N_DEV, 2, SGU_W // N_DEV),
    ]
    received = _exchange([b.astype(BF16) for b in blocks[:6]] + blocks[6:], "exchange_grads")
    names = ["attn_w_in", "attn_w_out", "sgu_w_in", "sgu_w_out", "ffn_w_in", "ffn_w_out"]
    ws = [attn_w_in, attn_w_out, sgu_w_in, sgu_w_out, ffn_w_in, ffn_w_out]
    ms = [m_attn_w_in, m_attn_w_out, m_sgu_w_in, m_sgu_w_out, m_ffn_w_in, m_ffn_w_out]
    vs = [v_attn_w_in, v_attn_w_out, v_sgu_w_in, v_sgu_w_out, v_ffn_w_in, v_ffn_w_out]
    big_out = {}
    for nm, rec, w, m, v in zip(names, received[:6], ws, ms, vs):
        flat = lambda a: a.reshape(-1, a.shape[-1])
        big_out[nm] = [o.reshape(w.shape) for o in _adam_sum(rec, flat(w), flat(m), flat(v), f"adam_{nm}")]
    pad8 = lambda a: jnp.pad(a, [(0, 0)] * (a.ndim - 2) + [(0, 8 - a.shape[-2]), (0, 0)])
    ln_parts = jnp.concatenate([pad8(received[6]), pad8(received[7])], axis=1)
    ln_pack = lambda g, b: jnp.concatenate([pad8(g), pad8(b)], axis=0)
    ln_out = _adam_sum(ln_parts, ln_pack(sgu_ln_g, sgu_ln_b), ln_pack(m_sgu_ln_g, m_sgu_ln_b),
                       ln_pack(v_sgu_ln_g, v_sgu_ln_b), "adam_sgu_ln")

    def leaf(kind):
        mixer, ffn, final, b_f, _, w_s, b_s = small_out[kind]
        o = lambda nm: big_out[nm][kind]
        return [mixer, o("attn_w_in"), b_f, o("attn_w_out"), o("sgu_w_in"), ln_out[kind][0:2], ln_out[kind][8:10],
                w_s, b_s, o("sgu_w_out"), ffn, o("ffn_w_in"), o("ffn_w_out"), final]

    return (loss, grad_x, *leaf(0), *leaf(1), *leaf(2), *leaf(3))
```

```python
import functools

import jax
import jax.numpy as jnp
from jax import lax
from jax.experimental import pallas as pl
from jax.experimental.pallas import tpu as pltpu

F32 = jnp.float32
BF16 = jnp.bfloat16

D_MODEL = 1024
HEAD_DIM = 64
N_PAIR = 8
LANES = 128
SGU_W = 2048
SGU_G = 16
CHUNK = 128
FFN_H = 2816
FFN_TILE = 256
N_FFN_TILE = FFN_H // FFN_TILE
NORM_EPS = 1e-6
LN_EPS = 1e-5
QK_SCALE = 0.125
ATT_BLOCK = 512
N_DEV = 8
ADAM_LR = 0.001
ADAM_B1 = 0.9
ADAM_B2 = 0.999
ADAM_EPS = 1e-08
ADAM_WD = 0.01
ADAM_STEP = 10
MESH = pl.DeviceIdType.MESH
SQRT_HALF = 0.7071067811865476
INV_SQRT_2PI = 0.3989422804014327

NT_DIMS = (((1,), (1,)), ((), ()))
TN_DIMS = (((0,), (0,)), ((), ()))


def _gelu(x):
    return 0.5 * x * (1.0 + lax.erf(x * SQRT_HALF))


def _gelu_grad(x):
    return 0.5 * (1.0 + lax.erf(x * SQRT_HALF)) + x * jnp.exp(-0.5 * x * x) * INV_SQRT_2PI


def _lane_col(v, lane):
    idx = lax.broadcasted_iota(jnp.int32, v.shape, 1)
    return jnp.sum(jnp.where(idx == lane, v, 0.0), axis=1, keepdims=True)


def _params(sem, vmem_mb):
    return pltpu.CompilerParams(dimension_semantics=sem, vmem_limit_bytes=vmem_mb << 20)


def _cat_groups(ref, n):
    if n == 1:
        return ref[0]
    return jnp.concatenate([ref[t] for t in range(n)], axis=1)


def _rowcall(name, body, T, tm, ins, outs, vmem_mb, scratch=()):
    def spec(shape, kind, resident_once=False):
        shape = tuple(shape)
        if kind == "row":
            return pl.BlockSpec((tm,) + shape[1:], lambda i: (i,) + (0,) * (len(shape) - 1))
        if kind == "grp":
            return pl.BlockSpec((shape[0], tm, shape[2]), lambda i: (0, i, 0))
        if resident_once:
            return pl.BlockSpec(shape, lambda i: (0,) * len(shape), pipeline_mode=pl.Buffered(1))
        return pl.BlockSpec(shape, lambda i: (0,) * len(shape))

    return pl.pallas_call(
        body,
        name=name,
        grid=(T // tm,),
        in_specs=[spec(a.shape, k, True) for a, k in ins],
        out_specs=[spec(s, k) for s, _, k in outs],
        out_shape=[jax.ShapeDtypeStruct(tuple(s), d) for s, d, _ in outs],
        scratch_shapes=list(scratch),
        compiler_params=_params(("arbitrary",), vmem_mb),
    )(*[a for a, _ in ins])


def _norm_matmul(x, nw, w, out_dtype, name, tn, groups=False):
    T, N = x.shape[0], w.shape[1]
    tm = min(1024, T)

    def body(x_ref, nw_ref, w_ref, o_ref, h_ref, h_scr):
        @pl.when(pl.program_id(1) == 0)
        def _():
            xv = x_ref[...]
            r = lax.rsqrt(jnp.mean(xv * xv, axis=-1, keepdims=True) + NORM_EPS)
            hv = (xv * r * nw_ref[...]).astype(BF16)
            h_scr[...] = hv
            h_ref[...] = hv

        acc = jnp.dot(h_scr[...], w_ref[...], preferred_element_type=F32)
        if groups:
            for t in range(tn // LANES):
                o_ref[t] = acc[:, LANES * t:LANES * (t + 1)].astype(out_dtype)
        else:
            o_ref[...] = acc.astype(out_dtype)

    if groups:
        o_shape = (N // LANES, T, LANES)
        o_spec = pl.BlockSpec((tn // LANES, tm, LANES), lambda i, j: (j, i, 0))
    else:
        o_shape = (T, N)
        o_spec = pl.BlockSpec((tm, tn), lambda i, j: (i, j))
    return pl.pallas_call(
        body,
        name=name,
        grid=(T // tm, N // tn),
        in_specs=[
            pl.BlockSpec((tm, D_MODEL), lambda i, j: (i, 0)),
            pl.BlockSpec((1, D_MODEL), lambda i, j: (0, 0)),
            pl.BlockSpec((D_MODEL, tn), lambda i, j: (0, j)),
        ],
        out_specs=[o_spec, pl.BlockSpec((tm, D_MODEL), lambda i, j: (i, 0))],
        out_shape=[jax.ShapeDtypeStruct(o_shape, out_dtype), jax.ShapeDtypeStruct((T, D_MODEL), BF16)],
        scratch_shapes=[pltpu.VMEM((tm, D_MODEL), BF16)],
        compiler_params=_params(("arbitrary", "arbitrary"), 48),
    )(x, nw, w)


def _matmul_tn(a, g, name, tk, tn, a_grp=False, g_grp=False):
    T = a.shape[1] if a_grp else a.shape[0]
    K = a.shape[0] * LANES if a_grp else a.shape[1]
    N = g.shape[0] * LANES if g_grp else g.shape[1]
    tm = min(1024, T)

    def body(a_ref, g_ref, o_ref):
        @pl.when(pl.program_id(2) == 0)
        def _():
            o_ref[...] = jnp.zeros(o_ref.shape, F32)

        av = _cat_groups(a_ref, tk // LANES) if a_grp else a_ref[...]
        gv = _cat_groups(g_ref, tn // LANES) if g_grp else g_ref[...]
        o_ref[...] += lax.dot_general(av.astype(BF16), gv.astype(BF16), TN_DIMS, preferred_element_type=F32)

    if a_grp:
        a_spec = pl.BlockSpec((tk // LANES, tm, LANES), lambda k, n, m: (k, m, 0))
    else:
        a_spec = pl.BlockSpec((tm, tk), lambda k, n, m: (m, k))
    if g_grp:
        g_spec = pl.BlockSpec((tn // LANES, tm, LANES), lambda k, n, m: (n, m, 0))
    else:
        g_spec = pl.BlockSpec((tm, tn), lambda k, n, m: (m, n))
    return pl.pallas_call(
        body,
        name=name,
        grid=(K // tk, N // tn, T // tm),
        in_specs=[a_spec, g_spec],
        out_specs=pl.BlockSpec((tk, tn), lambda k, n, m: (k, n)),
        out_shape=jax.ShapeDtypeStruct((K, N), F32),
        compiler_params=_params(("parallel", "parallel", "arbitrary"), 48),
    )(a, g)


def _matmul_rms_bwd(a_list, wt_list, x, nw, dres, name, tm, vmem_mb):
    T = x.shape[0]
    n = len(a_list)

    def body(*refs):
        a_refs, w_refs = refs[:n], refs[n:2 * n]
        x_ref, nw_ref, dres_ref, dx_ref, dnw_ref = refs[2 * n:]

        @pl.when(pl.program_id(0) == 0)
        def _():
            dnw_ref[...] = jnp.zeros(dnw_ref.shape, F32)

        dh = None
        for (arr, kind), a_ref, w_ref in zip(a_list, a_refs, w_refs):
            av = _cat_groups(a_ref, arr.shape[0]) if kind == "grp" else a_ref[...]
            part = jnp.dot(av.astype(BF16), w_ref[...], preferred_element_type=F32)
            dh = part if dh is None else dh + part
        xv = x_ref[...]
        r = lax.rsqrt(jnp.mean(xv * xv, axis=-1, keepdims=True) + NORM_EPS)
        xn = xv * r
        dnw_ref[...] += jnp.sum(dh * xn, axis=0, keepdims=True)
        dyw = dh * nw_ref[...]
        dx_ref[...] = dres_ref[...] + r * (dyw - xn * jnp.mean(dyw * xn, axis=-1, keepdims=True))

    ins = list(a_list) + [(w, "full") for w in wt_list] + [(x, "row"), (nw, "full"), (dres, "row")]
    outs = [((T, D_MODEL), F32, "row"), ((1, D_MODEL), F32, "full")]
    return _rowcall(name, body, T, tm, ins, outs, vmem_mb)


def _fgate_fwd(fl3, bcol, name):
    n_chunk = fl3.shape[0]

    def body(fl_ref, b_ref, c_ref):
        r = lax.broadcasted_iota(jnp.int32, (CHUNK, CHUNK), 0)
        t = lax.broadcasted_iota(jnp.int32, (CHUNK, CHUNK), 1)
        tri = jnp.where(r <= t, 1.0, 0.0).astype(BF16)

        def chunk(i, carry):
            z = fl_ref[i] + b_ref[...]
            lf = jnp.minimum(z, 0.0) - jnp.log(1.0 + jnp.exp(-jnp.abs(z)))
            hi = lf.astype(BF16)
            r1 = lf - hi.astype(F32)
            mid = r1.astype(BF16)
            low = (r1 - mid.astype(F32)).astype(BF16)
            cs = (jnp.dot(hi, tri, preferred_element_type=F32) + jnp.dot(mid, tri, preferred_element_type=F32)
                  + jnp.dot(low, tri, preferred_element_type=F32)) + carry
            c_ref[i] = cs
            return _lane_col(cs, CHUNK - 1)

        lax.fori_loop(0, n_chunk, chunk, jnp.zeros((2 * N_PAIR, 1), F32))

    return pl.pallas_call(
        body, name=name, out_shape=jax.ShapeDtypeStruct(fl3.shape, F32),
        compiler_params=pltpu.CompilerParams(vmem_limit_bytes=16 << 20),
    )(fl3, bcol)


def _fgate_bwd(dc3, fl3, bcol, name):
    n_chunk = fl3.shape[0]

    def body(dc_ref, fl_ref, b_ref, dfl_ref, db_ref):
        tt = lax.broadcasted_iota(jnp.int32, (CHUNK, CHUNK), 0)
        rr = lax.broadcasted_iota(jnp.int32, (CHUNK, CHUNK), 1)
        tri = jnp.where(tt >= rr, 1.0, 0.0).astype(BF16)

        def chunk(k, carry):
            tail, acc = carry
            i = n_chunk - 1 - k
            dc = dc_ref[i]
            hi = dc.astype(BF16)
            r1 = dc - hi.astype(F32)
            mid = r1.astype(BF16)
            low = (r1 - mid.astype(F32)).astype(BF16)
            dlf = (jnp.dot(hi, tri, preferred_element_type=F32) + jnp.dot(mid, tri, preferred_element_type=F32)
                   + jnp.dot(low, tri, preferred_element_type=F32)) + tail
            z = fl_ref[i] + b_ref[...]
            dfl = dlf / (1.0 + jnp.exp(z))
            dfl_ref[i] = dfl
            return _lane_col(dlf, 0), acc + dfl

        _, acc = lax.fori_loop(0, n_chunk, chunk,
                               (jnp.zeros((2 * N_PAIR, 1), F32), jnp.zeros((2 * N_PAIR, CHUNK), F32)))
        db_ref[...] = jnp.broadcast_to(jnp.sum(acc, axis=1, keepdims=True), db_ref.shape)

    return pl.pallas_call(
        body, name=name,
        out_shape=[jax.ShapeDtypeStruct(fl3.shape, F32), jax.ShapeDtypeStruct((2 * N_PAIR, LANES), F32)],
        compiler_params=pltpu.CompilerParams(vmem_limit_bytes=16 << 20),
    )(dc3, fl3, bcol)


BIAS_LANES = 3


def _own_lanes(shape, hh, axis=1):
    idx = lax.broadcasted_iota(jnp.int32, shape, axis)
    return idx < HEAD_DIM if hh == 0 else idx >= HEAD_DIM


def _spare(hh):
    return HEAD_DIM * (1 - hh)


def _attn_operands(qkv3, c16, tb):
    T = qkv3.shape[1]
    k, v = qkv3[N_PAIR:2 * N_PAIR], qkv3[2 * N_PAIR:]
    negc = -c16
    hi = negc.astype(BF16)
    r1 = negc - hi.astype(F32)
    mid = r1.astype(BF16)
    low = (r1 - mid.astype(F32)).astype(BF16)
    pieces = jnp.stack([hi, mid, low], axis=-1).reshape(N_PAIR, 2, T, BIAS_LANES)
    zpad = jnp.zeros((N_PAIR, T, HEAD_DIM - BIAS_LANES), BF16)
    k0 = jnp.concatenate([k[:, :, :HEAD_DIM], pieces[:, 0], zpad], axis=-1)
    k1 = jnp.concatenate([pieces[:, 1], zpad, k[:, :, HEAD_DIM:]], axis=-1)
    kaug = jnp.stack([k0, k1], axis=1).reshape(2 * N_PAIR, T, LANES)
    one = jnp.ones((N_PAIR, T, 1), BF16)
    zrow = jnp.zeros((N_PAIR, T, HEAD_DIM - 1), BF16)
    v0 = jnp.concatenate([v[:, :, :HEAD_DIM], one, zrow], axis=-1)
    v1 = jnp.concatenate([one, zrow, v[:, :, HEAD_DIM:]], axis=-1)
    va = jnp.stack([v0, v1], axis=1).reshape(2 * N_PAIR, T // tb, tb, LANES)
    return kaug, jnp.swapaxes(va, 2, 3)


def _attn_fwd(qkv3, kaug, vta, name):
    T = qkv3.shape[1]
    tb = min(ATT_BLOCK, T)
    nb = T // tb

    def body(q_ref, k_ref, v_ref, o_ref, lse_ref):
        i = pl.program_id(1)
        lane = lax.broadcasted_iota(jnp.int32, (tb, LANES), 1)
        key = lax.broadcasted_iota(jnp.int32, (tb, tb), 0)
        qry = lax.broadcasted_iota(jnp.int32, (tb, tb), 1)
        feat = lax.broadcasted_iota(jnp.int32, (LANES, tb), 0)
        q2 = q_ref[0] * QK_SCALE
        qa = []
        for hh in range(2):
            bias = jnp.logical_and(lane >= _spare(hh), lane < _spare(hh) + BIAS_LANES)
            qa.append(jnp.where(_own_lanes((tb, LANES), hh), q2, jnp.where(bias, 1.0, 0.0).astype(BF16)))

        def step(j, carry, masked):
            off = pl.multiple_of(j * tb, tb)
            out = []
            for hh in range(2):
                m_old, acc = carry[2 * hh], carry[2 * hh + 1]
                st = lax.dot_general(k_ref[hh, pl.ds(off, tb), :], qa[hh], NT_DIMS, preferred_element_type=F32)
                if masked:
                    st = jnp.where(key <= qry, st, -jnp.inf)
                m = jnp.maximum(m_old, jnp.max(st, axis=0, keepdims=True))
                p = jnp.exp(st - m)
                acc = jnp.exp(m_old - m) * acc + jnp.dot(v_ref[hh, j], p.astype(BF16), preferred_element_type=F32)
                out += [m, acc]
            return tuple(out)

        ninf = jnp.full((1, tb), -jnp.inf, F32)
        zacc = jnp.zeros((LANES, tb), F32)
        carry = lax.fori_loop(0, i, lambda j, c: step(j, c, False), (ninf, zacc, ninf, zacc))
        carry = step(i, carry, True)
        outs = []
        for hh in range(2):
            m, acc = carry[2 * hh], carry[2 * hh + 1]
            l = jnp.sum(jnp.where(feat == _spare(hh), acc, 0.0), axis=0, keepdims=True)
            lse_ref[0, 0, hh:hh + 1, :] = m + jnp.log(l)
            outs.append(acc * (1.0 / l))
        o_ref[0] = jnp.where(feat < HEAD_DIM, outs[0], outs[1]).astype(BF16)

    once = pl.Buffered(1)
    return pl.pallas_call(
        body,
        name=name,
        grid=(N_PAIR, nb),
        in_specs=[pl.BlockSpec((1, tb, LANES), lambda h, i: (h, i, 0)),
                  pl.BlockSpec((2, T, LANES), lambda h, i: (h, 0, 0), pipeline_mode=once),
                  pl.BlockSpec((2, nb, LANES, tb), lambda h, i: (h, 0, 0, 0), pipeline_mode=once)],
        out_specs=[pl.BlockSpec((1, LANES, tb), lambda h, i: (h, 0, i)),
                   pl.BlockSpec((1, 1, 2, tb), lambda h, i: (h, i, 0, 0))],
        out_shape=[jax.ShapeDtypeStruct((N_PAIR, LANES, T), BF16), jax.ShapeDtypeStruct((N_PAIR, nb, 2, tb), F32)],
        compiler_params=_params(("parallel", "arbitrary"), 48),
    )(qkv3, kaug, vta)


def _attn_bwd(qkv3, kaug, do3, st4, name):
    T = qkv3.shape[1]
    tb = min(ATT_BLOCK, T)
    nb = T // tb

    def body(q_ref, do_ref, st_ref, k_ref, v_ref, dq_hbm, rs_ref, dk_ref, dv_ref, cs_ref, dq_acc, dk_acc, dv_acc):
        h, j = pl.program_id(0), pl.program_id(1)
        lane = lax.broadcasted_iota(jnp.int32, (tb, LANES), 1)
        own = [_own_lanes((tb, LANES), hh) for hh in range(2)]
        bias = [jnp.logical_and(lane >= _spare(hh), lane < _spare(hh) + BIAS_LANES) for hh in range(2)]
        key = lax.broadcasted_iota(jnp.int32, (tb, tb), 0)
        qry = lax.broadcasted_iota(jnp.int32, (tb, tb), 1)

        @pl.when(j == 0)
        def _():
            dq_acc[...] = jnp.zeros(dq_acc.shape, F32)
            rs_ref[...] = jnp.zeros(rs_ref.shape, F32)

        vb = v_ref[0]
        zero = jnp.zeros_like(vb)
        one = jnp.ones_like(vb)
        bias_one = [jnp.where(bias[hh], one, zero) for hh in range(2)]
        kb = [k_ref[hh] for hh in range(2)]
        ks = [jnp.where(own[hh], kb[hh], zero) * QK_SCALE for hh in range(2)]
        vm = [jnp.where(own[hh], vb, zero) for hh in range(2)]
        dk_acc[...] = jnp.zeros(dk_acc.shape, F32)
        dv_acc[...] = jnp.zeros(dv_acc.shape, F32)

        def step(i, masked):
            off = pl.multiple_of(i * tb, tb)
            qb = q_ref[0, pl.ds(off, tb), :] * QK_SCALE
            dob = do_ref[0, pl.ds(off, tb), :]
            dq = None
            for hh in range(2):
                st = lax.dot_general(kb[hh], jnp.where(own[hh], qb, bias_one[hh]), NT_DIMS,
                                     preferred_element_type=F32)
                if masked:
                    st = jnp.where(key <= qry, st, -jnp.inf)
                p = jnp.exp(st - st_ref[0, i, hh:hh + 1, :])
                dp = lax.dot_general(vm[hh], dob, NT_DIMS, preferred_element_type=F32)
                dsb = (p * (dp - st_ref[0, i, 2 + hh:3 + hh, :])).astype(BF16)
                dv_acc[hh] += jnp.dot(p.astype(BF16), dob, preferred_element_type=F32)
                dk_acc[hh] += jnp.dot(dsb, jnp.where(own[hh], qb, one), preferred_element_type=F32)
                rs_ref[0, i, hh:hh + 1, :] += jnp.sum(dsb.astype(F32), axis=0, keepdims=True)
                d = lax.dot_general(dsb, ks[hh], TN_DIMS, preferred_element_type=F32)
                dq = d if dq is None else dq + d
            dq_acc[pl.ds(off, tb), :] += dq

        step(j, True)

        def loop_body(i, carry):
            step(i, False)
            return carry

        lax.fori_loop(j + 1, nb, loop_body, 0)
        dk_ref[0] = jnp.where(own[0], dk_acc[0], dk_acc[1]).astype(BF16)
        cs_ref[0] = jnp.where(own[0], dk_acc[1], dk_acc[0])
        dv_ref[0] = jnp.where(own[0], dv_acc[0], dv_acc[1]).astype(BF16)

        @pl.when(j == nb - 1)
        def _():
            pltpu.sync_copy(dq_acc, dq_hbm.at[h])

    res = pl.BlockSpec((1, T, LANES), lambda h, j: (h, 0, 0), pipeline_mode=pl.Buffered(1))
    tile = lambda base: pl.BlockSpec((1, tb, LANES), lambda h, j: (base + h, j, 0))
    rows = lambda n: pl.BlockSpec((1, nb, n, tb), lambda h, j: (h, 0, 0, 0))
    return pl.pallas_call(
        body,
        name=name,
        grid=(N_PAIR, nb),
        in_specs=[res, res, rows(4), pl.BlockSpec((2, tb, LANES), lambda h, j: (h, j, 0)), tile(2 * N_PAIR)],
        out_specs=[pl.BlockSpec(memory_space=pl.ANY), rows(2), tile(0), tile(0), tile(0)],
        out_shape=[
            jax.ShapeDtypeStruct((N_PAIR, T, LANES), F32),
            jax.ShapeDtypeStruct((N_PAIR, nb, 2, tb), F32),
            jax.ShapeDtypeStruct((N_PAIR, T, LANES), BF16),
            jax.ShapeDtypeStruct((N_PAIR, T, LANES), BF16),
            jax.ShapeDtypeStruct((N_PAIR, T, LANES), F32),
        ],
        scratch_shapes=[pltpu.VMEM((T, LANES), F32), pltpu.VMEM((2, tb, LANES), F32),
                        pltpu.VMEM((2, tb, LANES), F32)],
        compiler_params=_params(("arbitrary", "arbitrary"), 56),
    )(qkv3, do3, st4, kaug, qkv3)


def _attn_out(o3, w, x, name):
    T = x.shape[0]

    def body(o_ref, w_ref, x_ref, out_ref):
        out_ref[...] = x_ref[...] + jnp.dot(_cat_groups(o_ref, N_PAIR), w_ref[...], preferred_element_type=F32)

    return _rowcall(name, body, T, min(512, T), [(o3, "grp"), (w, "full"), (x, "row")],
                    [((T, D_MODEL), F32, "row")], 24)[0]


def _attn_dout(dx, wt, o3, name):
    T = dx.shape[0]
    tm = min(512, T)

    def body(dx_ref, w_ref, o_ref, do_ref, dd_ref):
        do = jnp.dot(dx_ref[...].astype(BF16), w_ref[...], preferred_element_type=F32).astype(BF16)
        lo = _own_lanes((tm, LANES), 0)
        for t in range(N_PAIR):
            d = do[:, LANES * t:LANES * (t + 1)]
            do_ref[t] = d
            prod = d.astype(F32) * o_ref[t].astype(F32)
            d0 = jnp.sum(jnp.where(lo, prod, 0.0), axis=1, keepdims=True)
            d1 = jnp.sum(jnp.where(lo, 0.0, prod), axis=1, keepdims=True)
            dd_ref[t] = jnp.where(lo, d0, d1)

    return _rowcall(name, body, T, tm, [(dx, "row"), (wt, "full"), (o3, "grp")],
                    [((N_PAIR, T, LANES), BF16, "grp"), ((N_PAIR, T, LANES), F32, "grp")], 32)


def _ffn_out(gu, w, x, name):
    T = x.shape[0]
    tm = min(512, T)

    def body(gu_ref, w_ref, x_ref, out_ref, hid_ref):
        acc = x_ref[...]
        for j in range(N_FFN_TILE):
            g = gu_ref[:, 2 * FFN_TILE * j:2 * FFN_TILE * j + FFN_TILE].astype(F32)
            u = gu_ref[:, 2 * FFN_TILE * j + FFN_TILE:2 * FFN_TILE * (j + 1)].astype(F32)
            hj = (g * jax.nn.sigmoid(g) * u).astype(BF16)
            hid_ref[:, FFN_TILE * j:FFN_TILE * (j + 1)] = hj
            acc = acc + jnp.dot(hj, w_ref[FFN_TILE * j:FFN_TILE * (j + 1), :], preferred_element_type=F32)
        out_ref[...] = acc

    return _rowcall(name, body, T, tm, [(gu, "row"), (w, "full"), (x, "row")],
                    [((T, D_MODEL), F32, "row"), ((T, FFN_H), BF16, "row")], 48)


def _ffn_dgu(dx, wt, gu, name):
    T = dx.shape[0]
    tm = min(512, T)

    def body(dx_ref, w_ref, gu_ref, dgu_ref):
        dxb = dx_ref[...].astype(BF16)
        for j in range(N_FFN_TILE):
            dh = jnp.dot(dxb, w_ref[:, FFN_TILE * j:FFN_TILE * (j + 1)], preferred_element_type=F32)
            g = gu_ref[:, 2 * FFN_TILE * j:2 * FFN_TILE * j + FFN_TILE].astype(F32)
            u = gu_ref[:, 2 * FFN_TILE * j + FFN_TILE:2 * FFN_TILE * (j + 1)].astype(F32)
            sg = jax.nn.sigmoid(g)
            dgu_ref[:, 2 * FFN_TILE * j:2 * FFN_TILE * j + FFN_TILE] = (
                dh * u * (sg * (1.0 + g * (1.0 - sg)))).astype(BF16)
            dgu_ref[:, 2 * FFN_TILE * j + FFN_TILE:2 * FFN_TILE * (j + 1)] = (dh * (g * sg)).astype(BF16)

    return _rowcall(name, body, T, tm, [(dx, "row"), (wt, "full"), (gu, "row")],
                    [((T, 2 * FFN_H), BF16, "row")], 48)[0]


def _sgu_core(a, ln_g, ln_b, w_s, bst, w, x, name):
    T = x.shape[0]
    tm = min(256, T)

    def body(a_ref, lg_ref, lb_ref, ws_ref, bs_ref, w_ref, x_ref, out_ref, gated_ref, vn_ref, mixed_ref):
        v = _gelu(a_ref[:, SGU_W:].astype(F32))
        mu = jnp.mean(v, axis=-1, keepdims=True)
        vc = v - mu
        rstd = lax.rsqrt(jnp.mean(vc * vc, axis=-1, keepdims=True) + LN_EPS)
        vn_ref[...] = (vc * rstd * lg_ref[...] + lb_ref[...]).astype(BF16)
        tt = lax.broadcasted_iota(jnp.int32, (CHUNK, CHUNK), 0)
        ss = lax.broadcasted_iota(jnp.int32, (CHUNK, CHUNK), 1)
        for g in range(SGU_G):
            wg = jnp.where(tt >= ss, ws_ref[g], 0.0).astype(BF16)
            bcol = _lane_col(bs_ref[...], g)
            cols = slice(CHUNK * g, CHUNK * (g + 1))
            for c in range(tm // CHUNK):
                rows = slice(CHUNK * c, CHUNK * (c + 1))
                mixed = jnp.dot(wg, vn_ref[rows, cols], preferred_element_type=F32) + bcol
                u = _gelu(a_ref[rows, cols].astype(F32))
                mixed_ref[rows, cols] = mixed.astype(BF16)
                gated_ref[rows, cols] = (u * mixed).astype(BF16)
        out_ref[...] = x_ref[...] + jnp.dot(gated_ref[...], w_ref[...], preferred_element_type=F32)

    ins = [(a, "row"), (ln_g, "full"), (ln_b, "full"), (w_s, "full"), (bst, "full"), (w, "full"), (x, "row")]
    outs = [((T, D_MODEL), F32, "row")] + [((T, SGU_W), BF16, "row")] * 3
    return _rowcall(name, body, T, tm, ins, outs, 40)


def _sgu_core_bwd(dx, wt, a, vn, mixed, ln_g, w_s, name):
    T = dx.shape[0]
    tm = min(256, T)

    def body(dx_ref, wt_ref, a_ref, vn_ref, mx_ref, lg_ref, ws_ref,
             da_ref, dws_ref, dba_ref, dlg_ref, dlb_ref, dg_scr, dvn_scr):
        @pl.when(pl.program_id(0) == 0)
        def _():
            dws_ref[...] = jnp.zeros(dws_ref.shape, F32)
            dba_ref[...] = jnp.zeros(dba_ref.shape, F32)
            dlg_ref[...] = jnp.zeros(dlg_ref.shape, F32)
            dlb_ref[...] = jnp.zeros(dlb_ref.shape, F32)

        dg_scr[...] = jnp.dot(dx_ref[...].astype(BF16), wt_ref[...], preferred_element_type=F32)
        tt = lax.broadcasted_iota(jnp.int32, (CHUNK, CHUNK), 0)
        ss = lax.broadcasted_iota(jnp.int32, (CHUNK, CHUNK), 1)
        tril = tt >= ss
        for g in range(SGU_G):
            wg = jnp.where(tril, ws_ref[g], 0.0).astype(BF16)
            cols = slice(CHUNK * g, CHUNK * (g + 1))
            for c in range(tm // CHUNK):
                rows = slice(CHUNK * c, CHUNK * (c + 1))
                dgb = dg_scr[rows, cols]
                au = a_ref[rows, cols].astype(F32)
                dmx = dgb * _gelu(au)
                da_ref[rows, cols] = (dgb * mx_ref[rows, cols].astype(F32) * _gelu_grad(au)).astype(BF16)
                dmb = dmx.astype(BF16)
                dvn_scr[rows, cols] = lax.dot_general(wg, dmb, TN_DIMS, preferred_element_type=F32)
                dws_ref[g] += jnp.where(
                    tril, lax.dot_general(dmb, vn_ref[rows, cols], NT_DIMS, preferred_element_type=F32), 0.0)
                dba_ref[:, cols] += dmx
        av = a_ref[:, SGU_W:].astype(F32)
        v = _gelu(av)
        mu = jnp.mean(v, axis=-1, keepdims=True)
        vc = v - mu
        rstd = lax.rsqrt(jnp.mean(vc * vc, axis=-1, keepdims=True) + LN_EPS)
        xhat = vc * rstd
        dvn = dvn_scr[...]
        dlg_ref[...] += jnp.sum(dvn * xhat, axis=0, keepdims=True)
        dlb_ref[...] += jnp.sum(dvn, axis=0, keepdims=True)
        dxh = dvn * lg_ref[...]
        dv = rstd * (dxh - jnp.mean(dxh, axis=-1, keepdims=True)
                     - xhat * jnp.mean(dxh * xhat, axis=-1, keepdims=True))
        da_ref[:, SGU_W:] = (dv * _gelu_grad(av)).astype(BF16)

    ins = [(dx, "row"), (wt, "full"), (a, "row"), (vn, "row"), (mixed, "row"), (ln_g, "full"), (w_s, "full")]
    outs = [((T, 2 * SGU_W), BF16, "row"), ((SGU_G, CHUNK, CHUNK), F32, "full"), ((CHUNK, SGU_W), F32, "full"),
            ((1, SGU_W), F32, "full"), ((1, SGU_W), F32, "full")]
    return _rowcall(name, body, T, tm, ins, outs, 40,
                    scratch=[pltpu.VMEM((tm, SGU_W), F32), pltpu.VMEM((tm, SGU_W), F32)])


def _loss_head(x, wf, tgt, name):
    T = x.shape[0]
    tm = min(512, T)

    def body(x_ref, wf_ref, tgt_ref, dx_ref, dwf_ref, loss_ref):
        @pl.when(pl.program_id(0) == 0)
        def _():
            dwf_ref[...] = jnp.zeros(dwf_ref.shape, F32)
            loss_ref[...] = jnp.zeros(loss_ref.shape, F32)

        xv = x_ref[...]
        r = lax.rsqrt(jnp.mean(xv * xv, axis=-1, keepdims=True) + NORM_EPS)
        xn = xv * r
        err = xn * wf_ref[...] - tgt_ref[...]
        loss_ref[...] += 0.5 * jnp.sum(jnp.mean(err * err, axis=-1, keepdims=True), axis=0, keepdims=True)
        dy = err * (1.0 / D_MODEL)
        dwf_ref[...] += jnp.sum(dy * xn, axis=0, keepdims=True)
        dyw = dy * wf_ref[...]
        dx_ref[...] = r * (dyw - xn * jnp.mean(dyw * xn, axis=-1, keepdims=True))

    return _rowcall(name, body, T, tm, [(x, "row"), (wf, "full"), (tgt, "row")],
                    [((T, D_MODEL), F32, "row"), ((1, D_MODEL), F32, "full"), ((8, LANES), F32, "full")], 32)


def _peers():
    x, y, c = lax.axis_index("x"), lax.axis_index("y"), lax.axis_index("c")
    peers = []
    for p in range(1, N_DEV):
        px = 1 - x if p & 4 else x
        py = 1 - y if p & 2 else y
        pc = 1 - c if p & 1 else c
        peers.append((4 * px + 2 * py + pc, (px, py, pc)))
    return 4 * x + 2 * y + c, peers


def _all_gather(arrs, name):
    n = len(arrs)
    hbm = pl.BlockSpec(memory_space=pl.ANY)

    def body(*refs):
        ins, outs = refs[:n], refs[n:2 * n]
        send_sems, recv_sems, local_sems = refs[2 * n:]
        me, peers = _peers()
        sends, recvs, locals_ = [], [], []
        for t in range(n):
            cp = pltpu.make_async_copy(ins[t], outs[t].at[me], local_sems.at[t])
            cp.start()
            locals_.append(cp)
            for k, (pidx, pid) in enumerate(peers):
                s = t * (N_DEV - 1) + k
                send = pltpu.make_async_remote_copy(
                    src_ref=ins[t], dst_ref=outs[t].at[me], send_sem=send_sems.at[s], recv_sem=recv_sems.at[s],
                    device_id=pid, device_id_type=MESH)
                send.start()
                sends.append(send)
                recvs.append(pltpu.make_async_remote_copy(
                    src_ref=ins[t], dst_ref=outs[t].at[pidx], send_sem=send_sems.at[s], recv_sem=recv_sems.at[s],
                    device_id=pid, device_id_type=MESH))
        for r in recvs:
            r.wait_recv()
        for s in sends:
            s.wait_send()
        for cp in locals_:
            cp.wait()

    return pl.pallas_call(
        body,
        name=name,
        in_specs=[hbm] * n,
        out_specs=[hbm] * n,
        out_shape=[jax.ShapeDtypeStruct((N_DEV,) + a.shape, a.dtype) for a in arrs],
        scratch_shapes=[pltpu.SemaphoreType.DMA((n * (N_DEV - 1),)), pltpu.SemaphoreType.DMA((n * (N_DEV - 1),)),
                        pltpu.SemaphoreType.DMA((n,))],
    )(*arrs)


def _exchange(arrs, name):
    n = len(arrs)
    hbm = pl.BlockSpec(memory_space=pl.ANY)

    def body(*refs):
        ins, outs = refs[:n], refs[n:2 * n]
        send_sems, recv_sems, local_sems = refs[2 * n:]
        me, peers = _peers()
        sends, recvs, locals_ = [], [], []
        for t in range(n):
            cp = pltpu.make_async_copy(ins[t].at[me], outs[t].at[me], local_sems.at[t])
            cp.start()
            locals_.append(cp)
            for k, (pidx, pid) in enumerate(peers):
                s = t * (N_DEV - 1) + k
                send = pltpu.make_async_remote_copy(
                    src_ref=ins[t].at[pidx], dst_ref=outs[t].at[me], send_sem=send_sems.at[s],
                    recv_sem=recv_sems.at[s], device_id=pid, device_id_type=MESH)
                send.start()
                sends.append(send)
                recvs.append(pltpu.make_async_remote_copy(
                    src_ref=ins[t].at[pidx], dst_ref=outs[t].at[pidx], send_sem=send_sems.at[s],
                    recv_sem=recv_sems.at[s], device_id=pid, device_id_type=MESH))
        for r in recvs:
            r.wait_recv()
        for s in sends:
            s.wait_send()
        for cp in locals_:
            cp.wait()

    return pl.pallas_call(
        body,
        name=name,
        in_specs=[hbm] * n,
        out_specs=[hbm] * n,
        out_shape=[jax.ShapeDtypeStruct(a.shape, a.dtype) for a in arrs],
        scratch_shapes=[pltpu.SemaphoreType.DMA((n * (N_DEV - 1),)), pltpu.SemaphoreType.DMA((n * (N_DEV - 1),)),
                        pltpu.SemaphoreType.DMA((n,))],
    )(*arrs)


def _row_tile(rows, cap):
    best = None
    for t in range(16, cap + 1, 16):
        if rows % t == 0:
            best = t
    assert best is not None, rows
    return best


def _adam_sum(parts, w, m, v, name):
    R, C = w.shape
    tr = _row_tile(R, 128)

    def body(p_ref, w_ref, m_ref, v_ref, g_ref, d_ref, nm_ref, nv_ref):
        g = p_ref[0].astype(F32)
        for s in range(1, N_DEV):
            g = g + p_ref[s].astype(F32)
        mm = ADAM_B1 * m_ref[...] + (1.0 - ADAM_B1) * g
        vv = ADAM_B2 * v_ref[...] + (1.0 - ADAM_B2) * (g * g)
        m_hat = mm / (1.0 - ADAM_B1 ** ADAM_STEP)
        v_hat = vv / (1.0 - ADAM_B2 ** ADAM_STEP)
        g_ref[...] = g
        d_ref[...] = -ADAM_LR * (m_hat / (jnp.sqrt(v_hat) + ADAM_EPS) + ADAM_WD * w_ref[...])
        nm_ref[...] = mm
        nv_ref[...] = vv

    mat = pl.BlockSpec((tr, C), lambda i: (i, 0))
    return pl.pallas_call(
        body,
        name=name,
        grid=(R // tr,),
        in_specs=[pl.BlockSpec((N_DEV, tr, C), lambda i: (0, i, 0)), mat, mat, mat],
        out_specs=[mat] * 4,
        out_shape=[jax.ShapeDtypeStruct((R, C), F32)] * 4,
        compiler_params=_params(("parallel",), 32),
    )(parts, w, m, v)


def _cols_from_gathered(g):
    _, L, K, n = g.shape
    return jnp.transpose(g, (1, 2, 0, 3)).reshape(L, K, N_DEV * n)


def _rows_from_gathered(g):
    _, L, k, N = g.shape
    return jnp.transpose(g, (1, 0, 2, 3)).reshape(L, N_DEV * k, N)


def _cols_to_blocks(dw):
    L, K, N = dw.shape
    n = N // N_DEV
    return jnp.transpose(dw.reshape(L, K, N_DEV, n), (2, 0, 1, 3)).reshape(N_DEV, L * K, n)


def _rows_to_blocks(dw):
    L, K, N = dw.shape
    k = K // N_DEV
    return jnp.transpose(dw.reshape(L, N_DEV, k, N), (1, 0, 2, 3)).reshape(N_DEV, L * k, N)


def _ffn_interleave(w):
    lead = w.shape[:-1]
    t = w.reshape(lead + (2, N_FFN_TILE, FFN_TILE))
    return jnp.swapaxes(t, -3, -2).reshape(lead + (2 * FFN_H,))


def _ffn_deinterleave(w):
    lead = w.shape[:-1]
    t = w.reshape(lead + (N_FFN_TILE, 2, FFN_TILE))
    return jnp.swapaxes(t, -3, -2).reshape(lead + (2 * FFN_H,))


def _pad_rows(a, rows=8):
    a = a.reshape(-1, a.shape[-1])
    return jnp.pad(a, ((0, rows - a.shape[0]), (0, 0)))


SMALL_ROWS = 6 * 8 + 2 * SGU_G * CHUNK * CHUNK // D_MODEL


def _pack_small(mixer, ffn, final, b_f, extra, w_s, b_s):
    bf_row = jnp.pad(b_f.reshape(1, -1), ((0, 0), (0, D_MODEL - b_f.size)))
    bs_rows = jnp.pad(b_s.reshape(4, -1), ((0, 0), (0, D_MODEL - b_s.size // 4)))
    return jnp.concatenate([
        _pad_rows(mixer), _pad_rows(ffn), _pad_rows(final.reshape(1, -1)), _pad_rows(bf_row),
        _pad_rows(extra), _pad_rows(bs_rows), w_s.reshape(-1, D_MODEL)], axis=0)


def _unpack_small(p):
    mixer, ffn, final = p[0:4], p[8:12], p[16]
    b_f = p[24, :32].reshape(2, 2 * N_PAIR)
    extra = p[32]
    b_s = p[40:44, :2 * SGU_G * CHUNK // 4].reshape(2, SGU_G, CHUNK)
    w_s = p[48:].reshape(2, SGU_G, CHUNK, CHUNK)
    return mixer, ffn, final, b_f, extra, w_s, b_s


def kernel(x, mixer_norm_w, attn_w_in, attn_b_f, attn_w_out, sgu_w_in, sgu_ln_g, sgu_ln_b, sgu_w_s, sgu_b_s, sgu_w_out, ffn_norm_w, ffn_w_in, ffn_w_out, final_norm_w, loss_target, m_mixer_norm_w, m_attn_w_in, m_attn_b_f, m_attn_w_out, m_sgu_w_in, m_sgu_ln_g, m_sgu_ln_b, m_sgu_w_s, m_sgu_b_s, m_sgu_w_out, m_ffn_norm_w, m_ffn_w_in, m_ffn_w_out, m_final_norm_w, v_mixer_norm_w, v_attn_w_in, v_attn_b_f, v_attn_w_out, v_sgu_w_in, v_sgu_ln_g, v_sgu_ln_b, v_sgu_w_s, v_sgu_b_s, v_sgu_w_out, v_ffn_norm_w, v_ffn_w_in, v_ffn_w_out, v_final_norm_w):
    T = x.shape[1]
    tb = min(ATT_BLOCK, T)
    xs, tgt = x[0], loss_target[0]

    shards = [attn_w_in, attn_w_out, sgu_w_in, sgu_w_out, ffn_w_in, ffn_w_out, sgu_ln_g, sgu_ln_b]
    gathered = _all_gather([s.astype(BF16) for s in shards[:6]] + shards[6:], "gather_weights")
    w_attn_in = _cols_from_gathered(gathered[0])
    w_attn_out = _rows_from_gathered(gathered[1])
    w_sgu_in = _cols_from_gathered(gathered[2])
    w_sgu_out = _rows_from_gathered(gathered[3])
    w_ffn_in = _ffn_interleave(_cols_from_gathered(gathered[4]))
    w_ffn_out = _rows_from_gathered(gathered[5])
    ln_g = jnp.transpose(gathered[6], (1, 0, 2)).reshape(2, 1, SGU_W)
    ln_b = jnp.transpose(gathered[7], (1, 0, 2)).reshape(2, 1, SGU_W)
    w_qkv = w_attn_in[:, :, :3 * D_MODEL]
    w_f = jnp.pad(w_attn_in[:, :, 3 * D_MODEL:], ((0, 0), (0, 0), (0, LANES - 2 * N_PAIR)))
    tr = lambda w: jnp.swapaxes(w, -1, -2)
    w_qkv_t, w_f_t, w_attn_out_t = tr(w_qkv), tr(w_f), tr(w_attn_out)
    w_sgu_in_t, w_sgu_out_t, w_ffn_in_t, w_ffn_out_t = tr(w_sgu_in), tr(w_sgu_out), tr(w_ffn_in), tr(w_ffn_out)
    mixer_nw = mixer_norm_w.reshape(4, 1, D_MODEL)
    ffn_nw = ffn_norm_w.reshape(4, 1, D_MODEL)
    b_col = attn_b_f.reshape(2, 2 * N_PAIR, 1)
    bs_t = jnp.swapaxes(sgu_b_s, 1, 2)

    saved = []
    xr = xs
    for i in range(4):
        j = i // 2
        if i % 2 == 0:
            qkv3, h = _norm_matmul(xr, mixer_nw[i], w_qkv[j], BF16, f"attn_qkv_{j}", 1024, groups=True)
            fl, _ = _norm_matmul(xr, mixer_nw[i], w_f[j], F32, f"attn_gate_{j}", LANES)
            fl3 = jnp.transpose(fl[:, :2 * N_PAIR].reshape(T // CHUNK, CHUNK, 2 * N_PAIR), (0, 2, 1))
            c_chunks = _fgate_fwd(fl3, b_col[j], f"fgate_fwd_{j}")
            kaug, vta = _attn_operands(qkv3, jnp.transpose(c_chunks, (1, 0, 2)).reshape(2 * N_PAIR, T), tb)
            ot3, lse4 = _attn_fwd(qkv3, kaug, vta, f"attn_fwd_{j}")
            o3 = jnp.swapaxes(ot3, 1, 2)
            xm = _attn_out(o3, w_attn_out[j], xr, f"attn_out_{j}")
            mix_saved = (xr, h, qkv3, fl3, kaug, o3, lse4)
        else:
            a, h = _norm_matmul(xr, mixer_nw[i], w_sgu_in[j], BF16, f"sgu_in_{j}", 1024)
            xm, gated, vn, mixed = _sgu_core(a, ln_g[j], ln_b[j], sgu_w_s[j], bs_t[j], w_sgu_out[j], xr,
                                             f"sgu_core_{j}")
            mix_saved = (xr, h, a, gated, vn, mixed)
        gu, h2 = _norm_matmul(xm, ffn_nw[i], w_ffn_in[i], BF16, f"ffn_in_{i}", FFN_H // 2)
        xo, hid = _ffn_out(gu, w_ffn_out[i], xm, f"ffn_out_{i}")
        saved.append((mix_saved, (xm, h2, gu, hid)))
        xr = xo
    dx, d_final, loss_part = _loss_head(xr, final_norm_w.reshape(1, D_MODEL), tgt, "loss_head")

    d_mixer_nw, d_ffn_nw = [None] * 4, [None] * 4
    d_attn_in, d_attn_out, d_bf, d_sgu_in, d_sgu_out = [None] * 2, [None] * 2, [None] * 2, [None] * 2, [None] * 2
    d_ws, d_bs, d_lg, d_lb = [None] * 2, [None] * 2, [None] * 2, [None] * 2
    d_ffn_in, d_ffn_out = [None] * 4, [None] * 4
    for i in reversed(range(4)):
        j = i // 2
        mix_saved, (xm, h2, gu, hid) = saved[i]
        dgu = _ffn_dgu(dx, w_ffn_out_t[i], gu, f"ffn_dgu_{i}")
        d_ffn_out[i] = _matmul_tn(hid, dx, f"ffn_dwout_{i}", FFN_H // 2, D_MODEL)
        d_ffn_in[i] = _matmul_tn(h2, dgu, f"ffn_dwin_{i}", D_MODEL, FFN_H // 2)
        dx, d_ffn_nw[i] = _matmul_rms_bwd([(dgu, "row")], [w_ffn_in_t[i]], xm, ffn_nw[i], dx, f"ffn_dx_{i}", 512, 48)
        if i % 2 == 0:
            xr, h, qkv3, fl3, kaug, o3, lse4 = mix_saved
            do3, dd3 = _attn_dout(dx, w_attn_out_t[j], o3, f"attn_dout_{j}")
            d_attn_out[j] = _matmul_tn(o3, dx, f"attn_dwout_{j}", D_MODEL, D_MODEL, a_grp=True)
            to_blocks = lambda a: jnp.swapaxes(a.reshape(N_PAIR, 2, T // tb, tb), 1, 2)
            from_blocks = lambda a: jnp.swapaxes(a, 1, 2).reshape(N_PAIR, 2, T)
            dd4 = to_blocks(jnp.stack([dd3[:, :, 0], dd3[:, :, HEAD_DIM]], axis=1))
            st4 = jnp.concatenate([lse4, dd4], axis=2)
            dq3, rs4, dk3, dv3, cs3 = _attn_bwd(qkv3, kaug, do3, st4, f"attn_bwd_{j}")
            dc_pair = from_blocks(rs4) - jnp.stack([cs3[:, :, HEAD_DIM], cs3[:, :, 0]], axis=1)
            dc_chunks = jnp.transpose(dc_pair.reshape(2 * N_PAIR, T // CHUNK, CHUNK), (1, 0, 2))
            dfl3, db = _fgate_bwd(dc_chunks, fl3, b_col[j], f"fgate_bwd_{j}")
            d_bf[j] = db[:, 0]
            dfl = jnp.transpose(dfl3, (0, 2, 1)).reshape(T, 2 * N_PAIR)
            dfl = jnp.pad(dfl.astype(BF16), ((0, 0), (0, LANES - 2 * N_PAIR)))
            d_qkv = [_matmul_tn(h, d3, f"attn_dw{nm}_{j}", D_MODEL, D_MODEL, g_grp=True)
                     for nm, d3 in (("q", dq3), ("k", dk3), ("v", dv3))]
            d_f = _matmul_tn(h, dfl, f"attn_dwf_{j}", D_MODEL, LANES)[:, :2 * N_PAIR]
            d_attn_in[j] = jnp.concatenate(d_qkv + [d_f], axis=1)
            wts = [w_qkv_t[j, k * D_MODEL:(k + 1) * D_MODEL] for k in range(3)] + [w_f_t[j]]
            dx, d_mixer_nw[i] = _matmul_rms_bwd(
                [(dq3, "grp"), (dk3, "grp"), (dv3, "grp"), (dfl, "row")], wts, xr, mixer_nw[i], dx,
                f"attn_dx_{j}", 256, 40)
        else:
            xr, h, a, gated, vn, mixed = mix_saved
            da, d_ws[j], dba, d_lg[j], d_lb[j] = _sgu_core_bwd(dx, w_sgu_out_t[j], a, vn, mixed, ln_g[j], sgu_w_s[j],
                                                               f"sgu_core_bwd_{j}")
            d_bs[j] = jnp.sum(dba.reshape(CHUNK, SGU_G, CHUNK), axis=-1).T
            d_sgu_out[j] = _matmul_tn(gated, dx, f"sgu_dwout_{j}", D_MODEL, D_MODEL)
            d_sgu_in[j] = _matmul_tn(h, da, f"sgu_dwin_{j}", D_MODEL, 1024)
            dx, d_mixer_nw[i] = _matmul_rms_bwd([(da, "row")], [w_sgu_in_t[j]], xr, mixer_nw[i], dx,
                                                f"sgu_dx_{j}", 256, 40)
    grad_x = dx[None]

    rows4 = lambda parts: jnp.concatenate(parts, axis=1).reshape(4, D_MODEL)
    small_g = _pack_small(rows4(d_mixer_nw), rows4(d_ffn_nw), d_final[0], jnp.stack(d_bf),
                          loss_part[0:1, 0:1] * jnp.ones((1, D_MODEL), F32), jnp.stack(d_ws), jnp.stack(d_bs))
    zero_row = jnp.zeros((1, D_MODEL), F32)
    pack = lambda pre: _pack_small(pre[0], pre[1], pre[2], pre[3], zero_row, pre[4], pre[5])
    small_w = pack((mixer_norm_w, ffn_norm_w, final_norm_w, attn_b_f, sgu_w_s, sgu_b_s))
    small_m = pack((m_mixer_norm_w, m_ffn_norm_w, m_final_norm_w, m_attn_b_f, m_sgu_w_s, m_sgu_b_s))
    small_v = pack((v_mixer_norm_w, v_ffn_norm_w, v_final_norm_w, v_attn_b_f, v_sgu_w_s, v_sgu_b_s))
    small_all = _all_gather([small_g], "gather_small_grads")[0]
    small_out = [_unpack_small(p) for p in _adam_sum(small_all, small_w, small_m, small_v, "adam_small")]
    loss = small_out[0][4][0]

    blocks = [
        _cols_to_blocks(jnp.stack(d_attn_in)), _rows_to_blocks(jnp.stack(d_attn_out)),
        _cols_to_blocks(jnp.stack(d_sgu_in)), _rows_to_blocks(jnp.stack(d_sgu_out)),
        _cols_to_blocks(_ffn_deinterleave(jnp.stack(d_ffn_in))), _rows_to_blocks(jnp.stack(d_ffn_out)),
        jnp.stack(d_lg).reshape(2, N_DEV, 1, SGU_W // N_DEV).transpose(1, 0, 2, 3).reshape(N_DEV, 2, SGU_W // N_DEV),
        jnp.stack(d_lb).reshape(2, N_DEV, 1, SGU_W // N_DEV).transpose(1, 0, 2, 3).reshape(N_DEV, 2, SGU_W // N_DEV),
    ]
    received = _exchange([b.astype(BF16) for b in blocks[:6]] + blocks[6:], "exchange_grads")
    names = ["attn_w_in", "attn_w_out", "sgu_w_in", "sgu_w_out", "ffn_w_in", "ffn_w_out"]
    ws = [attn_w_in, attn_w_out, sgu_w_in, sgu_w_out, ffn_w_in, ffn_w_out]
    ms = [m_attn_w_in, m_attn_w_out, m_sgu_w_in, m_sgu_w_out, m_ffn_w_in, m_ffn_w_out]
    vs = [v_attn_w_in, v_attn_w_out, v_sgu_w_in, v_sgu_w_out, v_ffn_w_in, v_ffn_w_out]
    big_out = {}
    for nm, rec, w, m, v in zip(names, received[:6], ws, ms, vs):
        flat = lambda a: a.reshape(-1, a.shape[-1])
        big_out[nm] = [o.reshape(w.shape) for o in _adam_sum(rec, flat(w), flat(m), flat(v), f"adam_{nm}")]
    pad8 = lambda a: jnp.pad(a, [(0, 0)] * (a.ndim - 2) + [(0, 8 - a.shape[-2]), (0, 0)])
    ln_parts = jnp.concatenate([pad8(received[6]), pad8(received[7])], axis=1)
    ln_pack = lambda g, b: jnp.concatenate([pad8(g), pad8(b)], axis=0)
    ln_out = _adam_sum(ln_parts, ln_pack(sgu_ln_g, sgu_ln_b), ln_pack(m_sgu_ln_g, m_sgu_ln_b),
                       ln_pack(v_sgu_ln_g, v_sgu_ln_b), "adam_sgu_ln")

    def leaf(kind):
        mixer, ffn, final, b_f, _, w_s, b_s = small_out[kind]
        o = lambda nm: big_out[nm][kind]
        return [mixer, o("attn_w_in"), b_f, o("attn_w_out"), o("sgu_w_in"), ln_out[kind][0:2], ln_out[kind][8:10],
                w_s, b_s, o("sgu_w_out"), ffn, o("ffn_w_in"), o("ffn_w_out"), final]

    return (loss, grad_x, *leaf(0), *leaf(1), *leaf(2), *leaf(3))
```

```python
import functools

import jax
import jax.numpy as jnp
from jax import lax
from jax.experimental import pallas as pl
from jax.experimental.pallas import tpu as pltpu

F32 = jnp.float32
BF16 = jnp.bfloat16

D_MODEL = 1024
HEAD_DIM = 64
N_PAIR = 8
LANES = 128
SGU_W = 2048
SGU_G = 16
CHUNK = 128
FFN_H = 2816
FFN_TILE = 256
N_FFN_TILE = FFN_H // FFN_TILE
NORM_EPS = 1e-6
LN_EPS = 1e-5
QK_SCALE = 0.125
ATT_BLOCK = 512
N_DEV = 8
ADAM_LR = 0.001
ADAM_B1 = 0.9
ADAM_B2 = 0.999
ADAM_EPS = 1e-08
ADAM_WD = 0.01
ADAM_STEP = 10
MESH = pl.DeviceIdType.MESH
SQRT_HALF = 0.7071067811865476
INV_SQRT_2PI = 0.3989422804014327

NT_DIMS = (((1,), (1,)), ((), ()))
TN_DIMS = (((0,), (0,)), ((), ()))


def _gelu(x):
    return 0.5 * x * (1.0 + lax.erf(x * SQRT_HALF))


def _gelu_grad(x):
    return 0.5 * (1.0 + lax.erf(x * SQRT_HALF)) + x * jnp.exp(-0.5 * x * x) * INV_SQRT_2PI


def _lane_col(v, lane):
    idx = lax.broadcasted_iota(jnp.int32, v.shape, 1)
    return jnp.sum(jnp.where(idx == lane, v, 0.0), axis=1, keepdims=True)


def _params(sem, vmem_mb):
    return pltpu.CompilerParams(dimension_semantics=sem, vmem_limit_bytes=vmem_mb << 20)


def _cat_groups(ref, n):
    if n == 1:
        return ref[0]
    return jnp.concatenate([ref[t] for t in range(n)], axis=1)


def _wkind(w):
    return (w[0], ("layer", w[1])) if isinstance(w, tuple) else (w, "full")


def _rowcall(name, body, T, tm, ins, outs, vmem_mb, scratch=()):
    def spec(shape, kind, resident_once=False):
        shape = tuple(shape)
        if isinstance(kind, tuple):
            layer = kind[1]
            return pl.BlockSpec((None,) + shape[1:], lambda i: (layer,) + (0,) * (len(shape) - 1),
                                pipeline_mode=pl.Buffered(1))
        if kind == "row":
            return pl.BlockSpec((tm,) + shape[1:], lambda i: (i,) + (0,) * (len(shape) - 1))
        if kind == "grp":
            return pl.BlockSpec((shape[0], tm, shape[2]), lambda i: (0, i, 0))
        if resident_once:
            return pl.BlockSpec(shape, lambda i: (0,) * len(shape), pipeline_mode=pl.Buffered(1))
        return pl.BlockSpec(shape, lambda i: (0,) * len(shape))

    return pl.pallas_call(
        body,
        name=name,
        grid=(T // tm,),
        in_specs=[spec(a.shape, k, True) for a, k in ins],
        out_specs=[spec(s, k) for s, _, k in outs],
        out_shape=[jax.ShapeDtypeStruct(tuple(s), d) for s, d, _ in outs],
        scratch_shapes=list(scratch),
        compiler_params=_params(("arbitrary",), vmem_mb),
    )(*[a for a, _ in ins])


def _norm_matmul(x, nw, w, out_dtype, name, tn, groups=False):
    w, layer = w
    T, N = x.shape[0], w.shape[2]
    tm = min(1024, T)

    def body(x_ref, nw_ref, w_ref, o_ref, h_ref, h_scr):
        @pl.when(pl.program_id(1) == 0)
        def _():
            xv = x_ref[...]
            r = lax.rsqrt(jnp.mean(xv * xv, axis=-1, keepdims=True) + NORM_EPS)
            hv = (xv * r * nw_ref[...]).astype(BF16)
            h_scr[...] = hv
            h_ref[...] = hv

        acc = jnp.dot(h_scr[...], w_ref[...], preferred_element_type=F32)
        if groups:
            for t in range(tn // LANES):
                o_ref[t] = acc[:, LANES * t:LANES * (t + 1)].astype(out_dtype)
        else:
            o_ref[...] = acc.astype(out_dtype)

    if groups:
        o_shape = (N // LANES, T, LANES)
        o_spec = pl.BlockSpec((tn // LANES, tm, LANES), lambda i, j: (j, i, 0))
    else:
        o_shape = (T, N)
        o_spec = pl.BlockSpec((tm, tn), lambda i, j: (i, j))
    return pl.pallas_call(
        body,
        name=name,
        grid=(T // tm, N // tn),
        in_specs=[
            pl.BlockSpec((tm, D_MODEL), lambda i, j: (i, 0)),
            pl.BlockSpec((1, D_MODEL), lambda i, j: (0, 0)),
            pl.BlockSpec((None, D_MODEL, tn), lambda i, j: (layer, 0, j)),
        ],
        out_specs=[o_spec, pl.BlockSpec((tm, D_MODEL), lambda i, j: (i, 0))],
        out_shape=[jax.ShapeDtypeStruct(o_shape, out_dtype), jax.ShapeDtypeStruct((T, D_MODEL), BF16)],
        scratch_shapes=[pltpu.VMEM((tm, D_MODEL), BF16)],
        compiler_params=_params(("arbitrary", "arbitrary"), 48),
    )(x, nw, w)


def _matmul_tn(a, g, name, tk, tn, a_grp=False, g_grp=False):
    T = a.shape[1] if a_grp else a.shape[0]
    K = a.shape[0] * LANES if a_grp else a.shape[1]
    N = g.shape[0] * LANES if g_grp else g.shape[1]
    tm = min(1024, T)

    def body(a_ref, g_ref, o_ref):
        @pl.when(pl.program_id(2) == 0)
        def _():
            o_ref[...] = jnp.zeros(o_ref.shape, F32)

        av = _cat_groups(a_ref, tk // LANES) if a_grp else a_ref[...]
        gv = _cat_groups(g_ref, tn // LANES) if g_grp else g_ref[...]
        o_ref[...] += lax.dot_general(av.astype(BF16), gv.astype(BF16), TN_DIMS, preferred_element_type=F32)

    if a_grp:
        a_spec = pl.BlockSpec((tk // LANES, tm, LANES), lambda k, n, m: (k, m, 0))
    else:
        a_spec = pl.BlockSpec((tm, tk), lambda k, n, m: (m, k))
    if g_grp:
        g_spec = pl.BlockSpec((tn // LANES, tm, LANES), lambda k, n, m: (n, m, 0))
    else:
        g_spec = pl.BlockSpec((tm, tn), lambda k, n, m: (m, n))
    return pl.pallas_call(
        body,
        name=name,
        grid=(K // tk, N // tn, T // tm),
        in_specs=[a_spec, g_spec],
        out_specs=pl.BlockSpec((tk, tn), lambda k, n, m: (k, n)),
        out_shape=jax.ShapeDtypeStruct((K, N), F32),
        compiler_params=_params(("parallel", "parallel", "arbitrary"), 48),
    )(a, g)


def _matmul_rms_bwd(a_list, wt_list, x, nw, dres, name, tm, vmem_mb):
    T = x.shape[0]
    n = len(a_list)

    def body(*refs):
        a_refs, w_refs = refs[:n], refs[n:2 * n]
        x_ref, nw_ref, dres_ref, dx_ref, dnw_ref = refs[2 * n:]

        @pl.when(pl.program_id(0) == 0)
        def _():
            dnw_ref[...] = jnp.zeros(dnw_ref.shape, F32)

        dh = None
        for (arr, kind), a_ref, w_ref in zip(a_list, a_refs, w_refs):
            av = _cat_groups(a_ref, arr.shape[0]) if kind == "grp" else a_ref[...]
            part = jnp.dot(av.astype(BF16), w_ref[...], preferred_element_type=F32)
            dh = part if dh is None else dh + part
        xv = x_ref[...]
        r = lax.rsqrt(jnp.mean(xv * xv, axis=-1, keepdims=True) + NORM_EPS)
        xn = xv * r
        dnw_ref[...] += jnp.sum(dh * xn, axis=0, keepdims=True)
        dyw = dh * nw_ref[...]
        dx_ref[...] = dres_ref[...] + r * (dyw - xn * jnp.mean(dyw * xn, axis=-1, keepdims=True))

    ins = list(a_list) + [_wkind(w) for w in wt_list] + [(x, "row"), (nw, "full"), (dres, "row")]
    outs = [((T, D_MODEL), F32, "row"), ((1, D_MODEL), F32, "full")]
    return _rowcall(name, body, T, tm, ins, outs, vmem_mb)


def _fgate_fwd(fl3, bcol, name):
    n_chunk = fl3.shape[0]

    def body(fl_ref, b_ref, c_ref):
        r = lax.broadcasted_iota(jnp.int32, (CHUNK, CHUNK), 0)
        t = lax.broadcasted_iota(jnp.int32, (CHUNK, CHUNK), 1)
        tri = jnp.where(r <= t, 1.0, 0.0).astype(BF16)

        def chunk(i, carry):
            z = fl_ref[i] + b_ref[...]
            lf = jnp.minimum(z, 0.0) - jnp.log(1.0 + jnp.exp(-jnp.abs(z)))
            hi = lf.astype(BF16)
            r1 = lf - hi.astype(F32)
            mid = r1.astype(BF16)
            low = (r1 - mid.astype(F32)).astype(BF16)
            cs = (jnp.dot(hi, tri, preferred_element_type=F32) + jnp.dot(mid, tri, preferred_element_type=F32)
                  + jnp.dot(low, tri, preferred_element_type=F32)) + carry
            c_ref[i] = cs
            return _lane_col(cs, CHUNK - 1)

        lax.fori_loop(0, n_chunk, chunk, jnp.zeros((2 * N_PAIR, 1), F32))

    return pl.pallas_call(
        body, name=name, out_shape=jax.ShapeDtypeStruct(fl3.shape, F32),
        compiler_params=pltpu.CompilerParams(vmem_limit_bytes=16 << 20),
    )(fl3, bcol)


def _fgate_bwd(dc3, fl3, bcol, name):
    n_chunk = fl3.shape[0]

    def body(dc_ref, fl_ref, b_ref, dfl_ref, db_ref):
        tt = lax.broadcasted_iota(jnp.int32, (CHUNK, CHUNK), 0)
        rr = lax.broadcasted_iota(jnp.int32, (CHUNK, CHUNK), 1)
        tri = jnp.where(tt >= rr, 1.0, 0.0).astype(BF16)

        def chunk(k, carry):
            tail, acc = carry
            i = n_chunk - 1 - k
            dc = dc_ref[i]
            hi = dc.astype(BF16)
            r1 = dc - hi.astype(F32)
            mid = r1.astype(BF16)
            low = (r1 - mid.astype(F32)).astype(BF16)
            dlf = (jnp.dot(hi, tri, preferred_element_type=F32) + jnp.dot(mid, tri, preferred_element_type=F32)
                   + jnp.dot(low, tri, preferred_element_type=F32)) + tail
            z = fl_ref[i] + b_ref[...]
            dfl = dlf / (1.0 + jnp.exp(z))
            dfl_ref[i] = dfl
            return _lane_col(dlf, 0), acc + dfl

        _, acc = lax.fori_loop(0, n_chunk, chunk,
                               (jnp.zeros((2 * N_PAIR, 1), F32), jnp.zeros((2 * N_PAIR, CHUNK), F32)))
        db_ref[...] = jnp.broadcast_to(jnp.sum(acc, axis=1, keepdims=True), db_ref.shape)

    return pl.pallas_call(
        body, name=name,
        out_shape=[jax.ShapeDtypeStruct(fl3.shape, F32), jax.ShapeDtypeStruct((2 * N_PAIR, LANES), F32)],
        compiler_params=pltpu.CompilerParams(vmem_limit_bytes=16 << 20),
    )(dc3, fl3, bcol)


BIAS_LANES = 3
ATT_Q_BLOCK = 1024


def _own_lanes(shape, hh, axis=1):
    idx = lax.broadcasted_iota(jnp.int32, shape, axis)
    return idx < HEAD_DIM if hh == 0 else idx >= HEAD_DIM


def _spare(hh):
    return HEAD_DIM * (1 - hh)


def _bias_pieces(c16):
    T = c16.shape[1]
    negc = -c16
    hi = negc.astype(BF16)
    r1 = negc - hi.astype(F32)
    mid = r1.astype(BF16)
    low = (r1 - mid.astype(F32)).astype(BF16)
    pieces = jnp.stack([hi, mid, low], axis=-1).reshape(N_PAIR, 2, T, BIAS_LANES)
    zpad = jnp.zeros((N_PAIR, T, HEAD_DIM - BIAS_LANES), BF16)
    return jnp.concatenate([pieces[:, 1], zpad, pieces[:, 0], zpad], axis=-1)


def _attn_fwd(qkv3, cp3, name):
    T = qkv3.shape[1]
    tb = min(ATT_BLOCK, T)
    tq = min(ATT_Q_BLOCK, T)
    nb = T // tb
    per_q = tq // tb

    def body(q_ref, k_ref, v_ref, cp_ref, o_ref, lse_ref):
        i = pl.program_id(1)
        lane = lax.broadcasted_iota(jnp.int32, (tq, LANES), 1)
        lane_k = lax.broadcasted_iota(jnp.int32, (tb, LANES), 1)
        key = lax.broadcasted_iota(jnp.int32, (tb, tq), 0)
        qry = lax.broadcasted_iota(jnp.int32, (tb, tq), 1)
        feat = lax.broadcasted_iota(jnp.int32, (LANES, tq), 0)
        q2 = q_ref[0] * QK_SCALE
        qa, own_k, one_k = [], [], []
        for hh in range(2):
            bias = jnp.logical_and(lane >= _spare(hh), lane < _spare(hh) + BIAS_LANES)
            qa.append(jnp.where(_own_lanes((tq, LANES), hh), q2, jnp.where(bias, 1.0, 0.0).astype(BF16)))
            own_k.append(_own_lanes((tb, LANES), hh))
            one_k.append(jnp.where(lane_k == _spare(hh), 1.0, 0.0).astype(BF16))

        def step(j, carry, key_shift):
            off = pl.multiple_of(j * tb, tb)
            kb, vb, cb = k_ref[0, pl.ds(off, tb), :], v_ref[0, pl.ds(off, tb), :], cp_ref[0, pl.ds(off, tb), :]
            out = []
            for hh in range(2):
                m_old, acc = carry[2 * hh], carry[2 * hh + 1]
                st = lax.dot_general(jnp.where(own_k[hh], kb, cb), qa[hh], NT_DIMS, preferred_element_type=F32)
                if key_shift is not None:
                    st = jnp.where(key + key_shift <= qry, st, -jnp.inf)
                m = jnp.maximum(m_old, jnp.max(st, axis=0, keepdims=True))
                p = jnp.exp(st - m)
                acc = jnp.exp(m_old - m) * acc + lax.dot_general(
                    jnp.where(own_k[hh], vb, one_k[hh]), p.astype(BF16), TN_DIMS, preferred_element_type=F32)
                out += [m, acc]
            return tuple(out)

        ninf = jnp.full((1, tq), -jnp.inf, F32)
        zacc = jnp.zeros((LANES, tq), F32)
        carry = lax.fori_loop(0, i * per_q, lambda j, c: step(j, c, None), (ninf, zacc, ninf, zacc))
        for t in range(per_q):
            carry = step(i * per_q + t, carry, t * tb)
        outs = []
        for hh in range(2):
            m, acc = carry[2 * hh], carry[2 * hh + 1]
            l = jnp.sum(jnp.where(feat == _spare(hh), acc, 0.0), axis=0, keepdims=True)
            lse_ref[0, 0, hh:hh + 1, :] = m + jnp.log(l)
            outs.append(acc * (1.0 / l))
        o_ref[0] = jnp.where(feat < HEAD_DIM, outs[0], outs[1]).astype(BF16)

    res = lambda base: pl.BlockSpec((1, T, LANES), lambda h, i: (base + h, 0, 0), pipeline_mode=pl.Buffered(1))
    return pl.pallas_call(
        body,
        name=name,
        grid=(N_PAIR, T // tq),
        in_specs=[pl.BlockSpec((1, tq, LANES), lambda h, i: (h, i, 0)), res(N_PAIR), res(2 * N_PAIR), res(0)],
        out_specs=[pl.BlockSpec((1, LANES, tq), lambda h, i: (h, 0, i)),
                   pl.BlockSpec((1, 1, 2, tq), lambda h, i: (h, i, 0, 0))],
        out_shape=[jax.ShapeDtypeStruct((N_PAIR, LANES, T), BF16),
                   jax.ShapeDtypeStruct((N_PAIR, T // tq, 2, tq), F32)],
        compiler_params=_params(("parallel", "arbitrary"), 56),
    )(qkv3, qkv3, qkv3, cp3)


def _attn_bwd(qkv3, cp3, do3, st4, name):
    T = qkv3.shape[1]
    tb = min(ATT_BLOCK, T)
    nb = T // tb

    def body(q_ref, do_ref, st_ref, k_ref, v_ref, cp_ref, dq_hbm, rs_ref, dk_ref, dv_ref, cs_ref,
             dq_acc, dk_acc, dv_acc):
        h, j = pl.program_id(0), pl.program_id(1)
        lane = lax.broadcasted_iota(jnp.int32, (tb, LANES), 1)
        own = [_own_lanes((tb, LANES), hh) for hh in range(2)]
        bias = [jnp.logical_and(lane >= _spare(hh), lane < _spare(hh) + BIAS_LANES) for hh in range(2)]
        key = lax.broadcasted_iota(jnp.int32, (tb, tb), 0)
        qry = lax.broadcasted_iota(jnp.int32, (tb, tb), 1)

        @pl.when(j == 0)
        def _():
            dq_acc[...] = jnp.zeros(dq_acc.shape, F32)
            rs_ref[...] = jnp.zeros(rs_ref.shape, F32)

        vb = v_ref[0]
        zero = jnp.zeros_like(vb)
        one = jnp.ones_like(vb)
        bias_one = [jnp.where(bias[hh], one, zero) for hh in range(2)]
        kb = [jnp.where(own[hh], k_ref[0], cp_ref[0]) for hh in range(2)]
        ks = [jnp.where(own[hh], k_ref[0], zero) * QK_SCALE for hh in range(2)]
        vm = [jnp.where(own[hh], vb, zero) for hh in range(2)]
        dk_acc[...] = jnp.zeros(dk_acc.shape, F32)
        dv_acc[...] = jnp.zeros(dv_acc.shape, F32)

        def step(i, masked):
            off = pl.multiple_of(i * tb, tb)
            qb = q_ref[0, pl.ds(off, tb), :] * QK_SCALE
            dob = do_ref[0, pl.ds(off, tb), :]
            dq = None
            for hh in range(2):
                st = lax.dot_general(kb[hh], jnp.where(own[hh], qb, bias_one[hh]), NT_DIMS,
                                     preferred_element_type=F32)
                if masked:
                    st = jnp.where(key <= qry, st, -jnp.inf)
                p = jnp.exp(st - st_ref[0, i, hh:hh + 1, :])
                dp = lax.dot_general(vm[hh], dob, NT_DIMS, preferred_element_type=F32)
                dsb = (p * (dp - st_ref[0, i, 2 + hh:3 + hh, :])).astype(BF16)
                dv_acc[hh] += jnp.dot(p.astype(BF16), dob, preferred_element_type=F32)
                dk_acc[hh] += jnp.dot(dsb, jnp.where(own[hh], qb, one), preferred_element_type=F32)
                rs_ref[0, i, hh:hh + 1, :] += jnp.sum(dsb.astype(F32), axis=0, keepdims=True)
                d = lax.dot_general(dsb, ks[hh], TN_DIMS, preferred_element_type=F32)
                dq = d if dq is None else dq + d
            dq_acc[pl.ds(off, tb), :] += dq

        step(j, True)

        def loop_body(i, carry):
            step(i, False)
            return carry

        lax.fori_loop(j + 1, nb, loop_body, 0)
        dk_ref[0] = jnp.where(own[0], dk_acc[0], dk_acc[1]).astype(BF16)
        dv_ref[0] = jnp.where(own[0], dv_acc[0], dv_acc[1]).astype(BF16)
        lane8 = lax.broadcasted_iota(jnp.int32, (8, LANES), 1)
        for hh in range(2):
            pick = jnp.where(lane8 == _spare(hh), 1.0, 0.0).astype(BF16)
            x = dk_acc[hh]
            hi = x.astype(BF16)
            r1 = x - hi.astype(F32)
            mid = r1.astype(BF16)
            low = (r1 - mid.astype(F32)).astype(BF16)
            cs_ref[0, 0, 8 * hh:8 * hh + 8, :] = (
                lax.dot_general(pick, hi, NT_DIMS, preferred_element_type=F32)
                + lax.dot_general(pick, mid, NT_DIMS, preferred_element_type=F32)
                + lax.dot_general(pick, low, NT_DIMS, preferred_element_type=F32))

        @pl.when(j == nb - 1)
        def _():
            pltpu.sync_copy(dq_acc, dq_hbm.at[h])

    res = pl.BlockSpec((1, T, LANES), lambda h, j: (h, 0, 0), pipeline_mode=pl.Buffered(1))
    tile = lambda base: pl.BlockSpec((1, tb, LANES), lambda h, j: (base + h, j, 0))
    rows = lambda n: pl.BlockSpec((1, nb, n, tb), lambda h, j: (h, 0, 0, 0))
    return pl.pallas_call(
        body,
        name=name,
        grid=(N_PAIR, nb),
        in_specs=[res, res, rows(4), tile(N_PAIR), tile(2 * N_PAIR), tile(0)],
        out_specs=[pl.BlockSpec(memory_space=pl.ANY), rows(2), tile(0), tile(0),
                   pl.BlockSpec((1, 1, 16, tb), lambda h, j: (h, j, 0, 0))],
        out_shape=[
            jax.ShapeDtypeStruct((N_PAIR, T, LANES), F32),
            jax.ShapeDtypeStruct((N_PAIR, nb, 2, tb), F32),
            jax.ShapeDtypeStruct((N_PAIR, T, LANES), BF16),
            jax.ShapeDtypeStruct((N_PAIR, T, LANES), BF16),
            jax.ShapeDtypeStruct((N_PAIR, nb, 16, tb), F32),
        ],
        scratch_shapes=[pltpu.VMEM((T, LANES), F32), pltpu.VMEM((2, tb, LANES), F32),
                        pltpu.VMEM((2, tb, LANES), F32)],
        compiler_params=_params(("arbitrary", "arbitrary"), 56),
    )(qkv3, do3, st4, qkv3, qkv3, cp3)


def _attn_out(o3, w, x, name):
    T = x.shape[0]

    def body(o_ref, w_ref, x_ref, out_ref):
        out_ref[...] = x_ref[...] + jnp.dot(_cat_groups(o_ref, N_PAIR), w_ref[...], preferred_element_type=F32)

    return _rowcall(name, body, T, min(512, T), [(o3, "grp"), _wkind(w), (x, "row")],
                    [((T, D_MODEL), F32, "row")], 24)[0]


def _attn_dout(dx, wt, o3, name):
    T = dx.shape[0]
    tm = min(512, T)

    def body(dx_ref, w_ref, o_ref, do_ref, dd_ref):
        do = jnp.dot(dx_ref[...].astype(BF16), w_ref[...], preferred_element_type=F32).astype(BF16)
        lo = _own_lanes((tm, LANES), 0)
        head = lax.broadcasted_iota(jnp.int32, (tm, 2 * N_PAIR), 1)
        dd = jnp.zeros((tm, 2 * N_PAIR), F32)
        for t in range(N_PAIR):
            d = do[:, LANES * t:LANES * (t + 1)]
            do_ref[t] = d
            prod = d.astype(F32) * o_ref[t].astype(F32)
            d0 = jnp.sum(jnp.where(lo, prod, 0.0), axis=1, keepdims=True)
            d1 = jnp.sum(jnp.where(lo, 0.0, prod), axis=1, keepdims=True)
            dd = jnp.where(head == 2 * t, d0, jnp.where(head == 2 * t + 1, d1, dd))
        dd_ref[...] = dd

    return _rowcall(name, body, T, tm, [(dx, "row"), _wkind(wt), (o3, "grp")],
                    [((N_PAIR, T, LANES), BF16, "grp"), ((T, 2 * N_PAIR), F32, "row")], 32)


def _ffn_out(gu, w, x, name):
    T = x.shape[0]
    tm = min(512, T)

    def body(gu_ref, w_ref, x_ref, out_ref, hid_ref):
        acc = x_ref[...]
        for j in range(N_FFN_TILE):
            g = gu_ref[:, 2 * FFN_TILE * j:2 * FFN_TILE * j + FFN_TILE].astype(F32)
            u = gu_ref[:, 2 * FFN_TILE * j + FFN_TILE:2 * FFN_TILE * (j + 1)].astype(F32)
            hj = (g * jax.nn.sigmoid(g) * u).astype(BF16)
            hid_ref[:, FFN_TILE * j:FFN_TILE * (j + 1)] = hj
            acc = acc + jnp.dot(hj, w_ref[FFN_TILE * j:FFN_TILE * (j + 1), :], preferred_element_type=F32)
        out_ref[...] = acc

    return _rowcall(name, body, T, tm, [(gu, "row"), _wkind(w), (x, "row")],
                    [((T, D_MODEL), F32, "row"), ((T, FFN_H), BF16, "row")], 48)


def _ffn_dgu(dx, wt, gu, name):
    T = dx.shape[0]
    tm = min(512, T)

    def body(dx_ref, w_ref, gu_ref, dgu_ref):
        dxb = dx_ref[...].astype(BF16)
        for j in range(N_FFN_TILE):
            dh = jnp.dot(dxb, w_ref[:, FFN_TILE * j:FFN_TILE * (j + 1)], preferred_element_type=F32)
            g = gu_ref[:, 2 * FFN_TILE * j:2 * FFN_TILE * j + FFN_TILE].astype(F32)
            u = gu_ref[:, 2 * FFN_TILE * j + FFN_TILE:2 * FFN_TILE * (j + 1)].astype(F32)
            sg = jax.nn.sigmoid(g)
            dgu_ref[:, 2 * FFN_TILE * j:2 * FFN_TILE * j + FFN_TILE] = (
                dh * u * (sg * (1.0 + g * (1.0 - sg)))).astype(BF16)
            dgu_ref[:, 2 * FFN_TILE * j + FFN_TILE:2 * FFN_TILE * (j + 1)] = (dh * (g * sg)).astype(BF16)

    return _rowcall(name, body, T, tm, [(dx, "row"), _wkind(wt), (gu, "row")],
                    [((T, 2 * FFN_H), BF16, "row")], 48)[0]


def _sgu_core(a, ln_g, ln_b, w_s, bst, w, x, name):
    T = x.shape[0]
    tm = min(256, T)

    def body(a_ref, lg_ref, lb_ref, ws_ref, bs_ref, w_ref, x_ref, out_ref, gated_ref, vn_ref, mixed_ref):
        v = _gelu(a_ref[:, SGU_W:].astype(F32))
        mu = jnp.mean(v, axis=-1, keepdims=True)
        vc = v - mu
        rstd = lax.rsqrt(jnp.mean(vc * vc, axis=-1, keepdims=True) + LN_EPS)
        vn_ref[...] = (vc * rstd * lg_ref[...] + lb_ref[...]).astype(BF16)
        tt = lax.broadcasted_iota(jnp.int32, (CHUNK, CHUNK), 0)
        ss = lax.broadcasted_iota(jnp.int32, (CHUNK, CHUNK), 1)
        for g in range(SGU_G):
            wg = jnp.where(tt >= ss, ws_ref[g], 0.0).astype(BF16)
            bcol = _lane_col(bs_ref[...], g)
            cols = slice(CHUNK * g, CHUNK * (g + 1))
            for c in range(tm // CHUNK):
                rows = slice(CHUNK * c, CHUNK * (c + 1))
                mixed = jnp.dot(wg, vn_ref[rows, cols], preferred_element_type=F32) + bcol
                u = _gelu(a_ref[rows, cols].astype(F32))
                mixed_ref[rows, cols] = mixed.astype(BF16)
                gated_ref[rows, cols] = (u * mixed).astype(BF16)
        out_ref[...] = x_ref[...] + jnp.dot(gated_ref[...], w_ref[...], preferred_element_type=F32)

    ins = [(a, "row"), (ln_g, "full"), (ln_b, "full"), (w_s, "full"), (bst, "full"), _wkind(w), (x, "row")]
    outs = [((T, D_MODEL), F32, "row")] + [((T, SGU_W), BF16, "row")] * 3
    return _rowcall(name, body, T, tm, ins, outs, 40)


def _sgu_core_bwd(dx, wt, a, vn, mixed, ln_g, w_s, name):
    T = dx.shape[0]
    tm = min(256, T)

    def body(dx_ref, wt_ref, a_ref, vn_ref, mx_ref, lg_ref, ws_ref,
             da_ref, dws_ref, dba_ref, dlg_ref, dlb_ref, dg_scr, dvn_scr):
        @pl.when(pl.program_id(0) == 0)
        def _():
            dws_ref[...] = jnp.zeros(dws_ref.shape, F32)
            dba_ref[...] = jnp.zeros(dba_ref.shape, F32)
            dlg_ref[...] = jnp.zeros(dlg_ref.shape, F32)
            dlb_ref[...] = jnp.zeros(dlb_ref.shape, F32)

        dg_scr[...] = jnp.dot(dx_ref[...].astype(BF16), wt_ref[...], preferred_element_type=F32)
        tt = lax.broadcasted_iota(jnp.int32, (CHUNK, CHUNK), 0)
        ss = lax.broadcasted_iota(jnp.int32, (CHUNK, CHUNK), 1)
        tril = tt >= ss
        for g in range(SGU_G):
            wg = jnp.where(tril, ws_ref[g], 0.0).astype(BF16)
            cols = slice(CHUNK * g, CHUNK * (g + 1))
            for c in range(tm // CHUNK):
                rows = slice(CHUNK * c, CHUNK * (c + 1))
                dgb = dg_scr[rows, cols]
                au = a_ref[rows, cols].astype(F32)
                dmx = dgb * _gelu(au)
                da_ref[rows, cols] = (dgb * mx_ref[rows, cols].astype(F32) * _gelu_grad(au)).astype(BF16)
                dmb = dmx.astype(BF16)
                dvn_scr[rows, cols] = lax.dot_general(wg, dmb, TN_DIMS, preferred_element_type=F32)
                dws_ref[g] += jnp.where(
                    tril, lax.dot_general(dmb, vn_ref[rows, cols], NT_DIMS, preferred_element_type=F32), 0.0)
                dba_ref[:, cols] += dmx
        av = a_ref[:, SGU_W:].astype(F32)
        v = _gelu(av)
        mu = jnp.mean(v, axis=-1, keepdims=True)
        vc = v - mu
        rstd = lax.rsqrt(jnp.mean(vc * vc, axis=-1, keepdims=True) + LN_EPS)
        xhat = vc * rstd
        dvn = dvn_scr[...]
        dlg_ref[...] += jnp.sum(dvn * xhat, axis=0, keepdims=True)
        dlb_ref[...] += jnp.sum(dvn, axis=0, keepdims=True)
        dxh = dvn * lg_ref[...]
        dv = rstd * (dxh - jnp.mean(dxh, axis=-1, keepdims=True)
                     - xhat * jnp.mean(dxh * xhat, axis=-1, keepdims=True))
        da_ref[:, SGU_W:] = (dv * _gelu_grad(av)).astype(BF16)

    ins = [(dx, "row"), _wkind(wt), (a, "row"), (vn, "row"), (mixed, "row"), (ln_g, "full"), (w_s, "full")]
    outs = [((T, 2 * SGU_W), BF16, "row"), ((SGU_G, CHUNK, CHUNK), F32, "full"), ((CHUNK, SGU_W), F32, "full"),
            ((1, SGU_W), F32, "full"), ((1, SGU_W), F32, "full")]
    return _rowcall(name, body, T, tm, ins, outs, 40,
                    scratch=[pltpu.VMEM((tm, SGU_W), F32), pltpu.VMEM((tm, SGU_W), F32)])


def _loss_head(x, wf, tgt, name):
    T = x.shape[0]
    tm = min(512, T)

    def body(x_ref, wf_ref, tgt_ref, dx_ref, dwf_ref, loss_ref):
        @pl.when(pl.program_id(0) == 0)
        def _():
            dwf_ref[...] = jnp.zeros(dwf_ref.shape, F32)
            loss_ref[...] = jnp.zeros(loss_ref.shape, F32)

        xv = x_ref[...]
        r = lax.rsqrt(jnp.mean(xv * xv, axis=-1, keepdims=True) + NORM_EPS)
        xn = xv * r
        err = xn * wf_ref[...] - tgt_ref[...]
        loss_ref[...] += 0.5 * jnp.sum(jnp.mean(err * err, axis=-1, keepdims=True), axis=0, keepdims=True)
        dy = err * (1.0 / D_MODEL)
        dwf_ref[...] += jnp.sum(dy * xn, axis=0, keepdims=True)
        dyw = dy * wf_ref[...]
        dx_ref[...] = r * (dyw - xn * jnp.mean(dyw * xn, axis=-1, keepdims=True))

    return _rowcall(name, body, T, tm, [(x, "row"), (wf, "full"), (tgt, "row")],
                    [((T, D_MODEL), F32, "row"), ((1, D_MODEL), F32, "full"), ((8, LANES), F32, "full")], 32)


def _peers():
    x, y, c = lax.axis_index("x"), lax.axis_index("y"), lax.axis_index("c")
    peers = []
    for p in range(1, N_DEV):
        px = 1 - x if p & 4 else x
        py = 1 - y if p & 2 else y
        pc = 1 - c if p & 1 else c
        peers.append((4 * px + 2 * py + pc, (px, py, pc)))
    return 4 * x + 2 * y + c, peers


def _all_gather(arrs, name):
    n = len(arrs)
    hbm = pl.BlockSpec(memory_space=pl.ANY)

    def body(*refs):
        ins, outs = refs[:n], refs[n:2 * n]
        send_sems, recv_sems, local_sems = refs[2 * n:]
        me, peers = _peers()
        sends, recvs, locals_ = [], [], []
        for t in range(n):
            cp = pltpu.make_async_copy(ins[t], outs[t].at[me], local_sems.at[t])
            cp.start()
            locals_.append(cp)
            for k, (pidx, pid) in enumerate(peers):
                s = t * (N_DEV - 1) + k
                send = pltpu.make_async_remote_copy(
                    src_ref=ins[t], dst_ref=outs[t].at[me], send_sem=send_sems.at[s], recv_sem=recv_sems.at[s],
                    device_id=pid, device_id_type=MESH)
                send.start()
                sends.append(send)
                recvs.append(pltpu.make_async_remote_copy(
                    src_ref=ins[t], dst_ref=outs[t].at[pidx], send_sem=send_sems.at[s], recv_sem=recv_sems.at[s],
                    device_id=pid, device_id_type=MESH))
        for r in recvs:
            r.wait_recv()
        for s in sends:
            s.wait_send()
        for cp in locals_:
            cp.wait()

    return pl.pallas_call(
        body,
        name=name,
        in_specs=[hbm] * n,
        out_specs=[hbm] * n,
        out_shape=[jax.ShapeDtypeStruct((N_DEV,) + a.shape, a.dtype) for a in arrs],
        scratch_shapes=[pltpu.SemaphoreType.DMA((n * (N_DEV - 1),)), pltpu.SemaphoreType.DMA((n * (N_DEV - 1),)),
                        pltpu.SemaphoreType.DMA((n,))],
    )(*arrs)


def _exchange(arrs, name):
    n = len(arrs)
    hbm = pl.BlockSpec(memory_space=pl.ANY)

    def body(*refs):
        ins, outs = refs[:n], refs[n:2 * n]
        send_sems, recv_sems, local_sems = refs[2 * n:]
        me, peers = _peers()
        sends, recvs, locals_ = [], [], []
        for t in range(n):
            cp = pltpu.make_async_copy(ins[t].at[me], outs[t].at[me], local_sems.at[t])
            cp.start()
            locals_.append(cp)
            for k, (pidx, pid) in enumerate(peers):
                s = t * (N_DEV - 1) + k
                send = pltpu.make_async_remote_copy(
                    src_ref=ins[t].at[pidx], dst_ref=outs[t].at[me], send_sem=send_sems.at[s],
                    recv_sem=recv_sems.at[s], device_id=pid, device_id_type=MESH)
                send.start()
                sends.append(send)
                recvs.append(pltpu.make_async_remote_copy(
                    src_ref=ins[t].at[pidx], dst_ref=outs[t].at[pidx], send_sem=send_sems.at[s],
                    recv_sem=recv_sems.at[s], device_id=pid, device_id_type=MESH))
        for r in recvs:
            r.wait_recv()
        for s in sends:
            s.wait_send()
        for cp in locals_:
            cp.wait()

    return pl.pallas_call(
        body,
        name=name,
        in_specs=[hbm] * n,
        out_specs=[hbm] * n,
        out_shape=[jax.ShapeDtypeStruct(a.shape, a.dtype) for a in arrs],
        scratch_shapes=[pltpu.SemaphoreType.DMA((n * (N_DEV - 1),)), pltpu.SemaphoreType.DMA((n * (N_DEV - 1),)),
                        pltpu.SemaphoreType.DMA((n,))],
    )(*arrs)


def _row_tile(rows, cap):
    best = None
    for t in range(16, cap + 1, 16):
        if rows % t == 0:
            best = t
    assert best is not None, rows
    return best


def _adam_sum(parts, w, m, v, name):
    R, C = w.shape
    tr = _row_tile(R, 128)

    def body(p_ref, w_ref, m_ref, v_ref, g_ref, d_ref, nm_ref, nv_ref):
        g = p_ref[0].astype(F32)
        for s in range(1, N_DEV):
            g = g + p_ref[s].astype(F32)
        mm = ADAM_B1 * m_ref[...] + (1.0 - ADAM_B1) * g
        vv = ADAM_B2 * v_ref[...] + (1.0 - ADAM_B2) * (g * g)
        m_hat = mm / (1.0 - ADAM_B1 ** ADAM_STEP)
        v_hat = vv / (1.0 - ADAM_B2 ** ADAM_STEP)
        g_ref[...] = g
        d_ref[...] = -ADAM_LR * (m_hat / (jnp.sqrt(v_hat) + ADAM_EPS) + ADAM_WD * w_ref[...])
        nm_ref[...] = mm
        nv_ref[...] = vv

    mat = pl.BlockSpec((tr, C), lambda i: (i, 0))
    return pl.pallas_call(
        body,
        name=name,
        grid=(R // tr,),
        in_specs=[pl.BlockSpec((N_DEV, tr, C), lambda i: (0, i, 0)), mat, mat, mat],
        out_specs=[mat] * 4,
        out_shape=[jax.ShapeDtypeStruct((R, C), F32)] * 4,
        compiler_params=_params(("parallel",), 32),
    )(parts, w, m, v)


def _cols_from_gathered(g):
    _, L, K, n = g.shape
    return jnp.transpose(g, (1, 2, 0, 3)).reshape(L, K, N_DEV * n)


def _rows_from_gathered(g):
    _, L, k, N = g.shape
    return jnp.transpose(g, (1, 0, 2, 3)).reshape(L, N_DEV * k, N)


def _cols_to_blocks(dw):
    L, K, N = dw.shape
    n = N // N_DEV
    return jnp.transpose(dw.reshape(L, K, N_DEV, n), (2, 0, 1, 3)).reshape(N_DEV, L * K, n)


def _rows_to_blocks(dw):
    L, K, N = dw.shape
    k = K // N_DEV
    return jnp.transpose(dw.reshape(L, N_DEV, k, N), (1, 0, 2, 3)).reshape(N_DEV, L * k, N)


def _ffn_interleave(w):
    lead = w.shape[:-1]
    t = w.reshape(lead + (2, N_FFN_TILE, FFN_TILE))
    return jnp.swapaxes(t, -3, -2).reshape(lead + (2 * FFN_H,))


def _ffn_deinterleave(w):
    lead = w.shape[:-1]
    t = w.reshape(lead + (N_FFN_TILE, 2, FFN_TILE))
    return jnp.swapaxes(t, -3, -2).reshape(lead + (2 * FFN_H,))


def _pad_rows(a, rows=8):
    a = a.reshape(-1, a.shape[-1])
    return jnp.pad(a, ((0, rows - a.shape[0]), (0, 0)))


SMALL_ROWS = 6 * 8 + 2 * SGU_G * CHUNK * CHUNK // D_MODEL


def _pack_small(mixer, ffn, final, b_f, extra, w_s, b_s):
    bf_row = jnp.pad(b_f.reshape(1, -1), ((0, 0), (0, D_MODEL - b_f.size)))
    bs_rows = jnp.pad(b_s.reshape(4, -1), ((0, 0), (0, D_MODEL - b_s.size // 4)))
    return jnp.concatenate([
        _pad_rows(mixer), _pad_rows(ffn), _pad_rows(final.reshape(1, -1)), _pad_rows(bf_row),
        _pad_rows(extra), _pad_rows(bs_rows), w_s.reshape(-1, D_MODEL)], axis=0)


def _unpack_small(p):
    mixer, ffn, final = p[0:4], p[8:12], p[16]
    b_f = p[24, :32].reshape(2, 2 * N_PAIR)
    extra = p[32]
    b_s = p[40:44, :2 * SGU_G * CHUNK // 4].reshape(2, SGU_G, CHUNK)
    w_s = p[48:].reshape(2, SGU_G, CHUNK, CHUNK)
    return mixer, ffn, final, b_f, extra, w_s, b_s


def kernel(x, mixer_norm_w, attn_w_in, attn_b_f, attn_w_out, sgu_w_in, sgu_ln_g, sgu_ln_b, sgu_w_s, sgu_b_s, sgu_w_out, ffn_norm_w, ffn_w_in, ffn_w_out, final_norm_w, loss_target, m_mixer_norm_w, m_attn_w_in, m_attn_b_f, m_attn_w_out, m_sgu_w_in, m_sgu_ln_g, m_sgu_ln_b, m_sgu_w_s, m_sgu_b_s, m_sgu_w_out, m_ffn_norm_w, m_ffn_w_in, m_ffn_w_out, m_final_norm_w, v_mixer_norm_w, v_attn_w_in, v_attn_b_f, v_attn_w_out, v_sgu_w_in, v_sgu_ln_g, v_sgu_ln_b, v_sgu_w_s, v_sgu_b_s, v_sgu_w_out, v_ffn_norm_w, v_ffn_w_in, v_ffn_w_out, v_final_norm_w):
    T = x.shape[1]
    tb = min(ATT_BLOCK, T)
    xs, tgt = x[0], loss_target[0]

    shards = [attn_w_in, attn_w_out, sgu_w_in, sgu_w_out, ffn_w_in, ffn_w_out, sgu_ln_g, sgu_ln_b]
    gathered = _all_gather([s.astype(BF16) for s in shards[:6]] + shards[6:], "gather_weights")
    w_attn_in = _cols_from_gathered(gathered[0])
    w_attn_out = _rows_from_gathered(gathered[1])
    w_sgu_in = _cols_from_gathered(gathered[2])
    w_sgu_out = _rows_from_gathered(gathered[3])
    w_ffn_in = _ffn_interleave(_cols_from_gathered(gathered[4]))
    w_ffn_out = _rows_from_gathered(gathered[5])
    ln_g = jnp.transpose(gathered[6], (1, 0, 2)).reshape(2, 1, SGU_W)
    ln_b = jnp.transpose(gathered[7], (1, 0, 2)).reshape(2, 1, SGU_W)
    w_qkv = w_attn_in[:, :, :3 * D_MODEL]
    w_f = jnp.pad(w_attn_in[:, :, 3 * D_MODEL:], ((0, 0), (0, 0), (0, LANES - 2 * N_PAIR)))
    tr = lambda w: jnp.swapaxes(w, -1, -2)
    w_qkv_t6, w_f_t, w_attn_out_t = tr(w_qkv).reshape(6, D_MODEL, D_MODEL), tr(w_f), tr(w_attn_out)
    w_sgu_in_t, w_sgu_out_t, w_ffn_in_t, w_ffn_out_t = tr(w_sgu_in), tr(w_sgu_out), tr(w_ffn_in), tr(w_ffn_out)
    mixer_nw = mixer_norm_w.reshape(4, 1, D_MODEL)
    ffn_nw = ffn_norm_w.reshape(4, 1, D_MODEL)
    b_col = attn_b_f.reshape(2, 2 * N_PAIR, 1)
    bs_t = jnp.swapaxes(sgu_b_s, 1, 2)

    saved = []
    xr = xs
    for i in range(4):
        j = i // 2
        if i % 2 == 0:
            qkv3, h = _norm_matmul(xr, mixer_nw[i], (w_qkv, j), BF16, f"attn_qkv_{j}", 1024, groups=True)
            fl, _ = _norm_matmul(xr, mixer_nw[i], (w_f, j), F32, f"attn_gate_{j}", LANES)
            fl3 = jnp.transpose(fl[:, :2 * N_PAIR].reshape(T // CHUNK, CHUNK, 2 * N_PAIR), (0, 2, 1))
            c_chunks = _fgate_fwd(fl3, b_col[j], f"fgate_fwd_{j}")
            cp3 = _bias_pieces(jnp.transpose(c_chunks, (1, 0, 2)).reshape(2 * N_PAIR, T))
            ot3, lse4 = _attn_fwd(qkv3, cp3, f"attn_fwd_{j}")
            o3 = jnp.swapaxes(ot3, 1, 2)
            xm = _attn_out(o3, (w_attn_out, j), xr, f"attn_out_{j}")
            mix_saved = (xr, h, qkv3, fl3, cp3, o3, lse4)
        else:
            a, h = _norm_matmul(xr, mixer_nw[i], (w_sgu_in, j), BF16, f"sgu_in_{j}", 1024)
            xm, gated, vn, mixed = _sgu_core(a, ln_g[j], ln_b[j], sgu_w_s[j], bs_t[j], (w_sgu_out, j), xr,
                                             f"sgu_core_{j}")
            mix_saved = (xr, h, a, gated, vn, mixed)
        gu, h2 = _norm_matmul(xm, ffn_nw[i], (w_ffn_in, i), BF16, f"ffn_in_{i}", FFN_H // 2)
        xo, hid = _ffn_out(gu, (w_ffn_out, i), xm, f"ffn_out_{i}")
        saved.append((mix_saved, (xm, h2, gu, hid)))
        xr = xo
    dx, d_final, loss_part = _loss_head(xr, final_norm_w.reshape(1, D_MODEL), tgt, "loss_head")

    d_mixer_nw, d_ffn_nw = [None] * 4, [None] * 4
    d_attn_in, d_attn_out, d_bf, d_sgu_in, d_sgu_out = [None] * 2, [None] * 2, [None] * 2, [None] * 2, [None] * 2
    d_ws, d_bs, d_lg, d_lb = [None] * 2, [None] * 2, [None] * 2, [None] * 2
    d_ffn_in, d_ffn_out = [None] * 4, [None] * 4
    for i in reversed(range(4)):
        j = i // 2
        mix_saved, (xm, h2, gu, hid) = saved[i]
        dgu = _ffn_dgu(dx, (w_ffn_out_t, i), gu, f"ffn_dgu_{i}")
        d_ffn_out[i] = _matmul_tn(hid, dx, f"ffn_dwout_{i}", FFN_H // 2, D_MODEL)
        d_ffn_in[i] = _matmul_tn(h2, dgu, f"ffn_dwin_{i}", D_MODEL, FFN_H // 2)
        dx, d_ffn_nw[i] = _matmul_rms_bwd([(dgu, "row")], [(w_ffn_in_t, i)], xm, ffn_nw[i], dx, f"ffn_dx_{i}", 512, 48)
        if i % 2 == 0:
            xr, h, qkv3, fl3, cp3, o3, lse4 = mix_saved
            do3, dd = _attn_dout(dx, (w_attn_out_t, j), o3, f"attn_dout_{j}")
            d_attn_out[j] = _matmul_tn(o3, dx, f"attn_dwout_{j}", D_MODEL, D_MODEL, a_grp=True)
            to_blocks = lambda a: jnp.swapaxes(a.reshape(N_PAIR, 2, T // tb, tb), 1, 2)
            from_blocks = lambda a: jnp.swapaxes(a, 1, 2).reshape(N_PAIR, 2, T)
            dd4 = to_blocks(dd.T.reshape(N_PAIR, 2, T))
            st4 = jnp.concatenate([to_blocks(from_blocks(lse4)), dd4], axis=2)
            dq3, rs4, dk3, dv3, cs4 = _attn_bwd(qkv3, cp3, do3, st4, f"attn_bwd_{j}")
            dc_pair = from_blocks(rs4 - jnp.stack([cs4[:, :, 0], cs4[:, :, 8]], axis=2))
            dc_chunks = jnp.transpose(dc_pair.reshape(2 * N_PAIR, T // CHUNK, CHUNK), (1, 0, 2))
            dfl3, db = _fgate_bwd(dc_chunks, fl3, b_col[j], f"fgate_bwd_{j}")
            d_bf[j] = db[:, 0]
            dfl = jnp.transpose(dfl3, (0, 2, 1)).reshape(T, 2 * N_PAIR)
            dfl = jnp.pad(dfl.astype(BF16), ((0, 0), (0, LANES - 2 * N_PAIR)))
            d_qkv = [_matmul_tn(h, d3, f"attn_dw{nm}_{j}", D_MODEL, D_MODEL, g_grp=True)
                     for nm, d3 in (("q", dq3), ("k", dk3), ("v", dv3))]
            d_f = _matmul_tn(h, dfl, f"attn_dwf_{j}", D_MODEL, LANES)[:, :2 * N_PAIR]
            d_attn_in[j] = jnp.concatenate(d_qkv + [d_f], axis=1)
            wts = [(w_qkv_t6, 3 * j + k) for k in range(3)] + [(w_f_t, j)]
            dx, d_mixer_nw[i] = _matmul_rms_bwd(
                [(dq3, "grp"), (dk3, "grp"), (dv3, "grp"), (dfl, "row")], wts, xr, mixer_nw[i], dx,
                f"attn_dx_{j}", 256, 40)
        else:
            xr, h, a, gated, vn, mixed = mix_saved
            da, d_ws[j], dba, d_lg[j], d_lb[j] = _sgu_core_bwd(dx, (w_sgu_out_t, j), a, vn, mixed, ln_g[j], sgu_w_s[j],
                                                               f"sgu_core_bwd_{j}")
            d_bs[j] = jnp.sum(dba.reshape(CHUNK, SGU_G, CHUNK), axis=-1).T
            d_sgu_out[j] = _matmul_tn(gated, dx, f"sgu_dwout_{j}", D_MODEL, D_MODEL)
            d_sgu_in[j] = _matmul_tn(h, da, f"sgu_dwin_{j}", D_MODEL, 1024)
            dx, d_mixer_nw[i] = _matmul_rms_bwd([(da, "row")], [(w_sgu_in_t, j)], xr, mixer_nw[i], dx,
                                                f"sgu_dx_{j}", 256, 40)
    grad_x = dx[None]

    rows4 = lambda parts: jnp.concatenate(parts, axis=1).reshape(4, D_MODEL)
    small_g = _pack_small(rows4(d_mixer_nw), rows4(d_ffn_nw), d_final[0], jnp.stack(d_bf),
                          loss_part[0:1, 0:1] * jnp.ones((1, D_MODEL), F32), jnp.stack(d_ws), jnp.stack(d_bs))
    zero_row = jnp.zeros((1, D_MODEL), F32)
    pack = lambda pre: _pack_small(pre[0], pre[1], pre[2], pre[3], zero_row, pre[4], pre[5])
    small_w = pack((mixer_norm_w, ffn_norm_w, final_norm_w, attn_b_f, sgu_w_s, sgu_b_s))
    small_m = pack((m_mixer_norm_w, m_ffn_norm_w, m_final_norm_w, m_attn_b_f, m_sgu_w_s, m_sgu_b_s))
    small_v = pack((v_mixer_norm_w, v_ffn_norm_w, v_final_norm_w, v_attn_b_f, v_sgu_w_s, v_sgu_b_s))
    small_all = _all_gather([small_g], "gather_small_grads")[0]
    small_out = [_unpack_small(p) for p in _adam_sum(small_all, small_w, small_m, small_v, "adam_small")]
    loss = small_out[0][4][0]

    blocks = [
        _cols_to_blocks(jnp.stack(d_attn_in)), _rows_to_blocks(jnp.stack(d_attn_out)),
        _cols_to_blocks(jnp.stack(d_sgu_in)), _rows_to_blocks(jnp.stack(d_sgu_out)),
        _cols_to_blocks(_ffn_deinterleave(jnp.stack(d_ffn_in))), _rows_to_blocks(jnp.stack(d_ffn_out)),
        jnp.stack(d_lg).reshape(2, N_DEV, 1, SGU_W // N_DEV).transpose(1, 0, 2, 3).reshape(N_DEV, 2, SGU_W // N_DEV),
        jnp.stack(d_lb).reshape(2, N_DEV, 1, SGU_W // N_DEV).transpose(1, 0, 2, 3).reshape(N_DEV, 2, SGU_W // N_DEV),
    ]
    received = _exchange([b.astype(BF16) for b in blocks[:6]] + blocks[6:], "exchange_grads")
    names = ["attn_w_in", "attn_w_out", "sgu_w_in", "sgu_w_out", "ffn_w_in", "ffn_w_out"]
    ws = [attn_w_in, attn_w_out, sgu_w_in, sgu_w_out, ffn_w_in, ffn_w_out]
    ms = [m_attn_w_in, m_attn_w_out, m_sgu_w_in, m_sgu_w_out, m_ffn_w_in, m_ffn_w_out]
    vs = [v_attn_w_in, v_attn_w_out, v_sgu_w_in, v_sgu_w_out, v_ffn_w_in, v_ffn_w_out]
    big_out = {}
    for nm, rec, w, m, v in zip(names, received[:6], ws, ms, vs):
        flat = lambda a: a.reshape(-1, a.shape[-1])
        big_out[nm] = [o.reshape(w.shape) for o in _adam_sum(rec, flat(w), flat(m), flat(v), f"adam_{nm}")]
    pad8 = lambda a: jnp.pad(a, [(0, 0)] * (a.ndim - 2) + [(0, 8 - a.shape[-2]), (0, 0)])
    ln_parts = jnp.concatenate([pad8(received[6]), pad8(received[7])], axis=1)
    ln_pack = lambda g, b: jnp.concatenate([pad8(g), pad8(b)], axis=0)
    ln_out = _adam_sum(ln_parts, ln_pack(sgu_ln_g, sgu_ln_b), ln_pack(m_sgu_ln_g, m_sgu_ln_b),
                       ln_pack(v_sgu_ln_g, v_sgu_ln_b), "adam_sgu_ln")

    def leaf(kind):
        mixer, ffn, final, b_f, _, w_s, b_s = small_out[kind]
        o = lambda nm: big_out[nm][kind]
        return [mixer, o("attn_w_in"), b_f, o("attn_w_out"), o("sgu_w_in"), ln_out[kind][0:2], ln_out[kind][8:10],
                w_s, b_s, o("sgu_w_out"), ffn, o("ffn_w_in"), o("ffn_w_out"), final]

    return (loss, grad_x, *leaf(0), *leaf(1), *leaf(2), *leaf(3))
```

```python
import functools

import jax
import jax.numpy as jnp
from jax import lax
from jax.experimental import pallas as pl
from jax.experimental.pallas import tpu as pltpu

F32 = jnp.float32
BF16 = jnp.bfloat16

D_MODEL = 1024
HEAD_DIM = 64
N_PAIR = 8
LANES = 128
SGU_W = 2048
SGU_G = 16
CHUNK = 128
FFN_H = 2816
FFN_TILE = 256
N_FFN_TILE = FFN_H // FFN_TILE
NORM_EPS = 1e-6
LN_EPS = 1e-5
QK_SCALE = 0.125
ATT_BLOCK = 512
N_DEV = 8
ADAM_LR = 0.001
ADAM_B1 = 0.9
ADAM_B2 = 0.999
ADAM_EPS = 1e-08
ADAM_WD = 0.01
ADAM_STEP = 10
MESH = pl.DeviceIdType.MESH
SQRT_HALF = 0.7071067811865476
INV_SQRT_2PI = 0.3989422804014327

NT_DIMS = (((1,), (1,)), ((), ()))
TN_DIMS = (((0,), (0,)), ((), ()))


def _gelu(x):
    return 0.5 * x * (1.0 + lax.erf(x * SQRT_HALF))


def _gelu_grad(x):
    return 0.5 * (1.0 + lax.erf(x * SQRT_HALF)) + x * jnp.exp(-0.5 * x * x) * INV_SQRT_2PI


def _lane_col(v, lane):
    idx = lax.broadcasted_iota(jnp.int32, v.shape, 1)
    return jnp.sum(jnp.where(idx == lane, v, 0.0), axis=1, keepdims=True)


def _params(sem, vmem_mb):
    return pltpu.CompilerParams(dimension_semantics=sem, vmem_limit_bytes=vmem_mb << 20)


def _cat_groups(ref, n):
    if n == 1:
        return ref[0]
    return jnp.concatenate([ref[t] for t in range(n)], axis=1)


def _wkind(w):
    return (w[0], ("layer", w[1])) if isinstance(w, tuple) else (w, "full")


def _rowcall(name, body, T, tm, ins, outs, vmem_mb, scratch=()):
    def spec(shape, kind, resident_once=False):
        shape = tuple(shape)
        if isinstance(kind, tuple):
            layer = kind[1]
            return pl.BlockSpec((None,) + shape[1:], lambda i: (layer,) + (0,) * (len(shape) - 1),
                                pipeline_mode=pl.Buffered(1))
        if kind == "row":
            return pl.BlockSpec((tm,) + shape[1:], lambda i: (i,) + (0,) * (len(shape) - 1))
        if kind == "grp":
            return pl.BlockSpec((shape[0], tm, shape[2]), lambda i: (0, i, 0))
        if resident_once:
            return pl.BlockSpec(shape, lambda i: (0,) * len(shape), pipeline_mode=pl.Buffered(1))
        return pl.BlockSpec(shape, lambda i: (0,) * len(shape))

    return pl.pallas_call(
        body,
        name=name,
        grid=(T // tm,),
        in_specs=[spec(a.shape, k, True) for a, k in ins],
        out_specs=[spec(s, k) for s, _, k in outs],
        out_shape=[jax.ShapeDtypeStruct(tuple(s), d) for s, d, _ in outs],
        scratch_shapes=list(scratch),
        compiler_params=_params(("arbitrary",), vmem_mb),
    )(*[a for a, _ in ins])


def _norm_matmul(x, nw, w, out_dtype, name, tn, groups=False):
    w, layer = w
    T, N = x.shape[0], w.shape[2]
    tm = min(1024, T)

    def body(x_ref, nw_ref, w_ref, o_ref, h_ref, h_scr):
        @pl.when(pl.program_id(1) == 0)
        def _():
            xv = x_ref[...]
            r = lax.rsqrt(jnp.mean(xv * xv, axis=-1, keepdims=True) + NORM_EPS)
            hv = (xv * r * nw_ref[...]).astype(BF16)
            h_scr[...] = hv
            h_ref[...] = hv

        acc = jnp.dot(h_scr[...], w_ref[...], preferred_element_type=F32)
        if groups:
            for t in range(tn // LANES):
                o_ref[t] = acc[:, LANES * t:LANES * (t + 1)].astype(out_dtype)
        else:
            o_ref[...] = acc.astype(out_dtype)

    if groups:
        o_shape = (N // LANES, T, LANES)
        o_spec = pl.BlockSpec((tn // LANES, tm, LANES), lambda i, j: (j, i, 0))
    else:
        o_shape = (T, N)
        o_spec = pl.BlockSpec((tm, tn), lambda i, j: (i, j))
    return pl.pallas_call(
        body,
        name=name,
        grid=(T // tm, N // tn),
        in_specs=[
            pl.BlockSpec((tm, D_MODEL), lambda i, j: (i, 0)),
            pl.BlockSpec((1, D_MODEL), lambda i, j: (0, 0)),
            pl.BlockSpec((None, D_MODEL, tn), lambda i, j: (layer, 0, j)),
        ],
        out_specs=[o_spec, pl.BlockSpec((tm, D_MODEL), lambda i, j: (i, 0))],
        out_shape=[jax.ShapeDtypeStruct(o_shape, out_dtype), jax.ShapeDtypeStruct((T, D_MODEL), BF16)],
        scratch_shapes=[pltpu.VMEM((tm, D_MODEL), BF16)],
        compiler_params=_params(("arbitrary", "arbitrary"), 48),
    )(x, nw, w)


def _matmul_tn(a, g, name, tk, tn, a_grp=False, g_grp=False):
    T = a.shape[1] if a_grp else a.shape[0]
    K = a.shape[0] * LANES if a_grp else a.shape[1]
    N = g.shape[0] * LANES if g_grp else g.shape[1]
    tm = min(1024, T)

    def body(a_ref, g_ref, o_ref):
        @pl.when(pl.program_id(2) == 0)
        def _():
            o_ref[...] = jnp.zeros(o_ref.shape, F32)

        av = _cat_groups(a_ref, tk // LANES) if a_grp else a_ref[...]
        gv = _cat_groups(g_ref, tn // LANES) if g_grp else g_ref[...]
        o_ref[...] += lax.dot_general(av.astype(BF16), gv.astype(BF16), TN_DIMS, preferred_element_type=F32)

    if a_grp:
        a_spec = pl.BlockSpec((tk // LANES, tm, LANES), lambda k, n, m: (k, m, 0))
    else:
        a_spec = pl.BlockSpec((tm, tk), lambda k, n, m: (m, k))
    if g_grp:
        g_spec = pl.BlockSpec((tn // LANES, tm, LANES), lambda k, n, m: (n, m, 0))
    else:
        g_spec = pl.BlockSpec((tm, tn), lambda k, n, m: (m, n))
    return pl.pallas_call(
        body,
        name=name,
        grid=(K // tk, N // tn, T // tm),
        in_specs=[a_spec, g_spec],
        out_specs=pl.BlockSpec((tk, tn), lambda k, n, m: (k, n)),
        out_shape=jax.ShapeDtypeStruct((K, N), F32),
        compiler_params=_params(("parallel", "parallel", "arbitrary"), 48),
    )(a, g)


def _matmul_rms_bwd(a_list, wt_list, x, nw, dres, name, tm, vmem_mb):
    T = x.shape[0]
    n = len(a_list)

    def body(*refs):
        a_refs, w_refs = refs[:n], refs[n:2 * n]
        x_ref, nw_ref, dres_ref, dx_ref, dnw_ref = refs[2 * n:]

        @pl.when(pl.program_id(0) == 0)
        def _():
            dnw_ref[...] = jnp.zeros(dnw_ref.shape, F32)

        dh = None
        for (arr, kind), a_ref, w_ref in zip(a_list, a_refs, w_refs):
            av = _cat_groups(a_ref, arr.shape[0]) if kind == "grp" else a_ref[...]
            part = jnp.dot(av.astype(BF16), w_ref[...], preferred_element_type=F32)
            dh = part if dh is None else dh + part
        xv = x_ref[...]
        r = lax.rsqrt(jnp.mean(xv * xv, axis=-1, keepdims=True) + NORM_EPS)
        xn = xv * r
        dnw_ref[...] += jnp.sum(dh * xn, axis=0, keepdims=True)
        dyw = dh * nw_ref[...]
        dx_ref[...] = dres_ref[...] + r * (dyw - xn * jnp.mean(dyw * xn, axis=-1, keepdims=True))

    ins = list(a_list) + [_wkind(w) for w in wt_list] + [(x, "row"), (nw, "full"), (dres, "row")]
    outs = [((T, D_MODEL), F32, "row"), ((1, D_MODEL), F32, "full")]
    return _rowcall(name, body, T, tm, ins, outs, vmem_mb)


def _fgate_fwd(fl3, bcol, name):
    n_chunk = fl3.shape[0]

    def body(fl_ref, b_ref, c_ref):
        r = lax.broadcasted_iota(jnp.int32, (CHUNK, CHUNK), 0)
        t = lax.broadcasted_iota(jnp.int32, (CHUNK, CHUNK), 1)
        tri = jnp.where(r <= t, 1.0, 0.0).astype(BF16)

        def chunk(i, carry):
            z = fl_ref[i] + b_ref[...]
            lf = jnp.minimum(z, 0.0) - jnp.log(1.0 + jnp.exp(-jnp.abs(z)))
            hi = lf.astype(BF16)
            r1 = lf - hi.astype(F32)
            mid = r1.astype(BF16)
            low = (r1 - mid.astype(F32)).astype(BF16)
            cs = (jnp.dot(hi, tri, preferred_element_type=F32) + jnp.dot(mid, tri, preferred_element_type=F32)
                  + jnp.dot(low, tri, preferred_element_type=F32)) + carry
            c_ref[i] = cs
            return _lane_col(cs, CHUNK - 1)

        lax.fori_loop(0, n_chunk, chunk, jnp.zeros((2 * N_PAIR, 1), F32))

    return pl.pallas_call(
        body, name=name, out_shape=jax.ShapeDtypeStruct(fl3.shape, F32),
        compiler_params=pltpu.CompilerParams(vmem_limit_bytes=16 << 20),
    )(fl3, bcol)


def _fgate_bwd(dc3, fl3, bcol, name):
    n_chunk = fl3.shape[0]

    def body(dc_ref, fl_ref, b_ref, dfl_ref, db_ref):
        tt = lax.broadcasted_iota(jnp.int32, (CHUNK, CHUNK), 0)
        rr = lax.broadcasted_iota(jnp.int32, (CHUNK, CHUNK), 1)
        tri = jnp.where(tt >= rr, 1.0, 0.0).astype(BF16)

        def chunk(k, carry):
            tail, acc = carry
            i = n_chunk - 1 - k
            dc = dc_ref[i]
            hi = dc.astype(BF16)
            r1 = dc - hi.astype(F32)
            mid = r1.astype(BF16)
            low = (r1 - mid.astype(F32)).astype(BF16)
            dlf = (jnp.dot(hi, tri, preferred_element_type=F32) + jnp.dot(mid, tri, preferred_element_type=F32)
                   + jnp.dot(low, tri, preferred_element_type=F32)) + tail
            z = fl_ref[i] + b_ref[...]
            dfl = dlf / (1.0 + jnp.exp(z))
            dfl_ref[i] = dfl
            return _lane_col(dlf, 0), acc + dfl

        _, acc = lax.fori_loop(0, n_chunk, chunk,
                               (jnp.zeros((2 * N_PAIR, 1), F32), jnp.zeros((2 * N_PAIR, CHUNK), F32)))
        db_ref[...] = jnp.broadcast_to(jnp.sum(acc, axis=1, keepdims=True), db_ref.shape)

    return pl.pallas_call(
        body, name=name,
        out_shape=[jax.ShapeDtypeStruct(fl3.shape, F32), jax.ShapeDtypeStruct((2 * N_PAIR, LANES), F32)],
        compiler_params=pltpu.CompilerParams(vmem_limit_bytes=16 << 20),
    )(dc3, fl3, bcol)


BIAS_LANES = 3
ATT_Q_BLOCK = 2048


def _own_lanes(shape, hh, axis=1):
    idx = lax.broadcasted_iota(jnp.int32, shape, axis)
    return idx < HEAD_DIM if hh == 0 else idx >= HEAD_DIM


def _spare(hh):
    return HEAD_DIM * (1 - hh)


def _bias_pieces(c16):
    T = c16.shape[1]
    negc = -c16
    hi = negc.astype(BF16)
    r1 = negc - hi.astype(F32)
    mid = r1.astype(BF16)
    low = (r1 - mid.astype(F32)).astype(BF16)
    pieces = jnp.stack([hi, mid, low], axis=-1).reshape(N_PAIR, 2, T, BIAS_LANES)
    zpad = jnp.zeros((N_PAIR, T, HEAD_DIM - BIAS_LANES), BF16)
    return jnp.concatenate([pieces[:, 1], zpad, pieces[:, 0], zpad], axis=-1)


def _attn_fwd(qkv3, cp3, name):
    T = qkv3.shape[1]
    tb = min(ATT_BLOCK, T)
    tq = min(ATT_Q_BLOCK, T)
    nb = T // tb
    per_q = tq // tb

    def body(q_ref, k_ref, v_ref, cp_ref, o_ref, lse_ref):
        i = pl.program_id(1)
        lane = lax.broadcasted_iota(jnp.int32, (tq, LANES), 1)
        lane_k = lax.broadcasted_iota(jnp.int32, (tb, LANES), 1)
        feat = lax.broadcasted_iota(jnp.int32, (LANES, tq), 0)
        q2 = q_ref[0] * QK_SCALE
        qa, own_k, one_k = [], [], []
        for hh in range(2):
            bias = jnp.logical_and(lane >= _spare(hh), lane < _spare(hh) + BIAS_LANES)
            qa.append(jnp.where(_own_lanes((tq, LANES), hh), q2, jnp.where(bias, 1.0, 0.0).astype(BF16)))
            own_k.append(_own_lanes((tb, LANES), hh))
            one_k.append(jnp.where(lane_k == _spare(hh), 1.0, 0.0).astype(BF16))

        def step(j, carry, first):
            off = pl.multiple_of(j * tb, tb)
            kb, vb, cb = k_ref[0, pl.ds(off, tb), :], v_ref[0, pl.ds(off, tb), :], cp_ref[0, pl.ds(off, tb), :]
            lo = 0 if first is None else first
            out = []
            for hh in range(2):
                m_all, acc_all = carry[2 * hh], carry[2 * hh + 1]
                m_old, acc = m_all[:, lo:], acc_all[:, lo:]
                st = lax.dot_general(jnp.where(own_k[hh], kb, cb), qa[hh][lo:], NT_DIMS, preferred_element_type=F32)
                if first is not None:
                    key = lax.broadcasted_iota(jnp.int32, (tb, tq - lo), 0)
                    qry = lax.broadcasted_iota(jnp.int32, (tb, tq - lo), 1)
                    st = jnp.where(key <= qry, st, -jnp.inf)
                m = jnp.maximum(m_old, jnp.max(st, axis=0, keepdims=True))
                p = jnp.exp(st - m)
                acc = jnp.exp(m_old - m) * acc + lax.dot_general(
                    jnp.where(own_k[hh], vb, one_k[hh]), p.astype(BF16), TN_DIMS, preferred_element_type=F32)
                if lo:
                    m = jnp.concatenate([m_all[:, :lo], m], axis=1)
                    acc = jnp.concatenate([acc_all[:, :lo], acc], axis=1)
                out += [m, acc]
            return tuple(out)

        ninf = jnp.full((1, tq), -jnp.inf, F32)
        zacc = jnp.zeros((LANES, tq), F32)
        carry = lax.fori_loop(0, i * per_q, lambda j, c: step(j, c, None), (ninf, zacc, ninf, zacc))
        for t in range(per_q):
            carry = step(i * per_q + t, carry, t * tb)
        outs = []
        for hh in range(2):
            m, acc = carry[2 * hh], carry[2 * hh + 1]
            l = jnp.sum(jnp.where(feat == _spare(hh), acc, 0.0), axis=0, keepdims=True)
            lse_ref[0, 0, hh:hh + 1, :] = m + jnp.log(l)
            outs.append(acc * (1.0 / l))
        o_ref[0] = jnp.where(feat < HEAD_DIM, outs[0], outs[1]).astype(BF16)

    res = lambda base: pl.BlockSpec((1, T, LANES), lambda h, i: (base + h, 0, 0), pipeline_mode=pl.Buffered(1))
    return pl.pallas_call(
        body,
        name=name,
        grid=(N_PAIR, T // tq),
        in_specs=[pl.BlockSpec((1, tq, LANES), lambda h, i: (h, i, 0)), res(N_PAIR), res(2 * N_PAIR), res(0)],
        out_specs=[pl.BlockSpec((1, LANES, tq), lambda h, i: (h, 0, i)),
                   pl.BlockSpec((1, 1, 2, tq), lambda h, i: (h, i, 0, 0))],
        out_shape=[jax.ShapeDtypeStruct((N_PAIR, LANES, T), BF16),
                   jax.ShapeDtypeStruct((N_PAIR, T // tq, 2, tq), F32)],
        compiler_params=_params(("parallel", "arbitrary"), 56),
    )(qkv3, qkv3, qkv3, cp3)


def _attn_bwd(qkv3, cp3, kst4, do3, st4, name):
    T = qkv3.shape[1]
    tb = min(ATT_BLOCK, T)
    nb = T // tb

    def body(q_ref, do_ref, st_ref, k_ref, v_ref, cp_ref, kst_ref, dq_hbm, rs_ref, dk_ref, dv_ref, cs_ref,
             dq_acc, dk_acc, dv_acc):
        h, j = pl.program_id(0), pl.program_id(1)
        lane = lax.broadcasted_iota(jnp.int32, (tb, LANES), 1)
        own = [_own_lanes((tb, LANES), hh) for hh in range(2)]
        bias = [jnp.logical_and(lane >= _spare(hh), lane < _spare(hh) + BIAS_LANES) for hh in range(2)]
        key = lax.broadcasted_iota(jnp.int32, (tb, tb), 0)
        qry = lax.broadcasted_iota(jnp.int32, (tb, tb), 1)

        @pl.when(j == 0)
        def _():
            dq_acc[...] = jnp.zeros(dq_acc.shape, F32)
            rs_ref[...] = jnp.zeros(rs_ref.shape, F32)

        vb = v_ref[0]
        zero = jnp.zeros_like(vb)
        one = jnp.ones_like(vb)
        bias_one = [jnp.where(bias[hh], one, zero) for hh in range(2)]
        kb = [jnp.where(own[hh], k_ref[0], cp_ref[0]) for hh in range(2)]
        kst = kst_ref[0, 0]
        kst = [jnp.where(_own_lanes((LANES, tb), hh, axis=0), kst, jnp.zeros_like(kst)) for hh in range(2)]
        vm = [jnp.where(own[hh], vb, zero) for hh in range(2)]
        dk_acc[...] = jnp.zeros(dk_acc.shape, F32)
        dv_acc[...] = jnp.zeros(dv_acc.shape, F32)

        def step(i, masked):
            off = pl.multiple_of(i * tb, tb)
            qb = q_ref[0, pl.ds(off, tb), :] * QK_SCALE
            dob = do_ref[0, pl.ds(off, tb), :]
            dq = None
            for hh in range(2):
                st = lax.dot_general(kb[hh], jnp.where(own[hh], qb, bias_one[hh]), NT_DIMS,
                                     preferred_element_type=F32)
                if masked:
                    st = jnp.where(key <= qry, st, -jnp.inf)
                p = jnp.exp(st - st_ref[0, i, hh:hh + 1, :])
                dp = lax.dot_general(vm[hh], dob, NT_DIMS, preferred_element_type=F32)
                dsb = (p * (dp - st_ref[0, i, 2 + hh:3 + hh, :])).astype(BF16)
                dv_acc[hh] += jnp.dot(p.astype(BF16), dob, preferred_element_type=F32)
                dk_acc[hh] += jnp.dot(dsb, jnp.where(own[hh], qb, one), preferred_element_type=F32)
                rs_ref[0, i, hh:hh + 1, :] += jnp.sum(dsb.astype(F32), axis=0, keepdims=True)
                d = jnp.dot(kst[hh], dsb, preferred_element_type=F32)
                dq = d if dq is None else dq + d
            dq_acc[i] += dq

        step(j, True)

        def loop_body(i, carry):
            step(i, False)
            return carry

        lax.fori_loop(j + 1, nb, loop_body, 0)
        dk_ref[0] = jnp.where(own[0], dk_acc[0], dk_acc[1]).astype(BF16)
        dv_ref[0] = jnp.where(own[0], dv_acc[0], dv_acc[1]).astype(BF16)
        lane8 = lax.broadcasted_iota(jnp.int32, (8, LANES), 1)
        for hh in range(2):
            pick = jnp.where(lane8 == _spare(hh), 1.0, 0.0).astype(BF16)
            x = dk_acc[hh]
            hi = x.astype(BF16)
            r1 = x - hi.astype(F32)
            mid = r1.astype(BF16)
            low = (r1 - mid.astype(F32)).astype(BF16)
            cs_ref[0, 0, 8 * hh:8 * hh + 8, :] = (
                lax.dot_general(pick, hi, NT_DIMS, preferred_element_type=F32)
                + lax.dot_general(pick, mid, NT_DIMS, preferred_element_type=F32)
                + lax.dot_general(pick, low, NT_DIMS, preferred_element_type=F32))

        @pl.when(j == nb - 1)
        def _():
            pltpu.sync_copy(dq_acc, dq_hbm.at[h])

    res = pl.BlockSpec((1, T, LANES), lambda h, j: (h, 0, 0), pipeline_mode=pl.Buffered(1))
    tile = lambda base: pl.BlockSpec((1, tb, LANES), lambda h, j: (base + h, j, 0))
    rows = lambda n: pl.BlockSpec((1, nb, n, tb), lambda h, j: (h, 0, 0, 0))
    return pl.pallas_call(
        body,
        name=name,
        grid=(N_PAIR, nb),
        in_specs=[res, res, rows(4), tile(N_PAIR), tile(2 * N_PAIR), tile(0),
                  pl.BlockSpec((1, 1, LANES, tb), lambda h, j: (h, j, 0, 0))],
        out_specs=[pl.BlockSpec(memory_space=pl.ANY), rows(2), tile(0), tile(0),
                   pl.BlockSpec((1, 1, 16, tb), lambda h, j: (h, j, 0, 0))],
        out_shape=[
            jax.ShapeDtypeStruct((N_PAIR, nb, LANES, tb), F32),
            jax.ShapeDtypeStruct((N_PAIR, nb, 2, tb), F32),
            jax.ShapeDtypeStruct((N_PAIR, T, LANES), BF16),
            jax.ShapeDtypeStruct((N_PAIR, T, LANES), BF16),
            jax.ShapeDtypeStruct((N_PAIR, nb, 16, tb), F32),
        ],
        scratch_shapes=[pltpu.VMEM((nb, LANES, tb), F32), pltpu.VMEM((2, tb, LANES), F32),
                        pltpu.VMEM((2, tb, LANES), F32)],
        compiler_params=_params(("arbitrary", "arbitrary"), 56),
    )(qkv3, do3, st4, qkv3, qkv3, cp3, kst4)


def _attn_out(o3, w, x, name):
    T = x.shape[0]

    def body(o_ref, w_ref, x_ref, out_ref):
        out_ref[...] = x_ref[...] + jnp.dot(_cat_groups(o_ref, N_PAIR), w_ref[...], preferred_element_type=F32)

    return _rowcall(name, body, T, min(512, T), [(o3, "grp"), _wkind(w), (x, "row")],
                    [((T, D_MODEL), F32, "row")], 24)[0]


def _attn_dout(dx, wt, o3, name):
    T = dx.shape[0]
    tm = min(512, T)

    def body(dx_ref, w_ref, o_ref, do_ref, dd_ref):
        do = jnp.dot(dx_ref[...].astype(BF16), w_ref[...], preferred_element_type=F32).astype(BF16)
        lo = _own_lanes((tm, LANES), 0)
        head = lax.broadcasted_iota(jnp.int32, (tm, 2 * N_PAIR), 1)
        dd = jnp.zeros((tm, 2 * N_PAIR), F32)
        for t in range(N_PAIR):
            d = do[:, LANES * t:LANES * (t + 1)]
            do_ref[t] = d
            prod = d.astype(F32) * o_ref[t].astype(F32)
            d0 = jnp.sum(jnp.where(lo, prod, 0.0), axis=1, keepdims=True)
            d1 = jnp.sum(jnp.where(lo, 0.0, prod), axis=1, keepdims=True)
            dd = jnp.where(head == 2 * t, d0, jnp.where(head == 2 * t + 1, d1, dd))
        dd_ref[...] = dd

    return _rowcall(name, body, T, tm, [(dx, "row"), _wkind(wt), (o3, "grp")],
                    [((N_PAIR, T, LANES), BF16, "grp"), ((T, 2 * N_PAIR), F32, "row")], 32)


def _ffn_out(gu, w, x, name):
    T = x.shape[0]
    tm = min(512, T)

    def body(gu_ref, w_ref, x_ref, out_ref, hid_ref):
        acc = x_ref[...]
        for j in range(N_FFN_TILE):
            g = gu_ref[:, 2 * FFN_TILE * j:2 * FFN_TILE * j + FFN_TILE].astype(F32)
            u = gu_ref[:, 2 * FFN_TILE * j + FFN_TILE:2 * FFN_TILE * (j + 1)].astype(F32)
            hj = (g * jax.nn.sigmoid(g) * u).astype(BF16)
            hid_ref[:, FFN_TILE * j:FFN_TILE * (j + 1)] = hj
            acc = acc + jnp.dot(hj, w_ref[FFN_TILE * j:FFN_TILE * (j + 1), :], preferred_element_type=F32)
        out_ref[...] = acc

    return _rowcall(name, body, T, tm, [(gu, "row"), _wkind(w), (x, "row")],
                    [((T, D_MODEL), F32, "row"), ((T, FFN_H), BF16, "row")], 48)


def _ffn_dgu(dx, wt, gu, name):
    T = dx.shape[0]
    tm = min(512, T)

    def body(dx_ref, w_ref, gu_ref, dgu_ref):
        dxb = dx_ref[...].astype(BF16)
        for j in range(N_FFN_TILE):
            dh = jnp.dot(dxb, w_ref[:, FFN_TILE * j:FFN_TILE * (j + 1)], preferred_element_type=F32)
            g = gu_ref[:, 2 * FFN_TILE * j:2 * FFN_TILE * j + FFN_TILE].astype(F32)
            u = gu_ref[:, 2 * FFN_TILE * j + FFN_TILE:2 * FFN_TILE * (j + 1)].astype(F32)
            sg = jax.nn.sigmoid(g)
            dgu_ref[:, 2 * FFN_TILE * j:2 * FFN_TILE * j + FFN_TILE] = (
                dh * u * (sg * (1.0 + g * (1.0 - sg)))).astype(BF16)
            dgu_ref[:, 2 * FFN_TILE * j + FFN_TILE:2 * FFN_TILE * (j + 1)] = (dh * (g * sg)).astype(BF16)

    return _rowcall(name, body, T, tm, [(dx, "row"), _wkind(wt), (gu, "row")],
                    [((T, 2 * FFN_H), BF16, "row")], 48)[0]


def _sgu_core(a, ln_g, ln_b, w_s, bst, w, x, name):
    T = x.shape[0]
    tm = min(256, T)

    def body(a_ref, lg_ref, lb_ref, ws_ref, bs_ref, w_ref, x_ref, out_ref, gated_ref, vn_ref, mixed_ref):
        v = _gelu(a_ref[:, SGU_W:].astype(F32))
        mu = jnp.mean(v, axis=-1, keepdims=True)
        vc = v - mu
        rstd = lax.rsqrt(jnp.mean(vc * vc, axis=-1, keepdims=True) + LN_EPS)
        vn_ref[...] = (vc * rstd * lg_ref[...] + lb_ref[...]).astype(BF16)
        tt = lax.broadcasted_iota(jnp.int32, (CHUNK, CHUNK), 0)
        ss = lax.broadcasted_iota(jnp.int32, (CHUNK, CHUNK), 1)
        for g in range(SGU_G):
            wg = jnp.where(tt >= ss, ws_ref[g], 0.0).astype(BF16)
            bcol = _lane_col(bs_ref[...], g)
            cols = slice(CHUNK * g, CHUNK * (g + 1))
            for c in range(tm // CHUNK):
                rows = slice(CHUNK * c, CHUNK * (c + 1))
                mixed = jnp.dot(wg, vn_ref[rows, cols], preferred_element_type=F32) + bcol
                u = _gelu(a_ref[rows, cols].astype(F32))
                mixed_ref[rows, cols] = mixed.astype(BF16)
                gated_ref[rows, cols] = (u * mixed).astype(BF16)
        out_ref[...] = x_ref[...] + jnp.dot(gated_ref[...], w_ref[...], preferred_element_type=F32)

    ins = [(a, "row"), (ln_g, "full"), (ln_b, "full"), (w_s, "full"), (bst, "full"), _wkind(w), (x, "row")]
    outs = [((T, D_MODEL), F32, "row")] + [((T, SGU_W), BF16, "row")] * 3
    return _rowcall(name, body, T, tm, ins, outs, 40)


def _sgu_core_bwd(dx, wt, a, vn, mixed, ln_g, w_s, name):
    T = dx.shape[0]
    tm = min(256, T)

    def body(dx_ref, wt_ref, a_ref, vn_ref, mx_ref, lg_ref, ws_ref,
             da_ref, dws_ref, dba_ref, dlg_ref, dlb_ref, dg_scr, dvn_scr):
        @pl.when(pl.program_id(0) == 0)
        def _():
            dws_ref[...] = jnp.zeros(dws_ref.shape, F32)
            dba_ref[...] = jnp.zeros(dba_ref.shape, F32)
            dlg_ref[...] = jnp.zeros(dlg_ref.shape, F32)
            dlb_ref[...] = jnp.zeros(dlb_ref.shape, F32)

        dg_scr[...] = jnp.dot(dx_ref[...].astype(BF16), wt_ref[...], preferred_element_type=F32)
        tt = lax.broadcasted_iota(jnp.int32, (CHUNK, CHUNK), 0)
        ss = lax.broadcasted_iota(jnp.int32, (CHUNK, CHUNK), 1)
        tril = tt >= ss
        for g in range(SGU_G):
            wg = jnp.where(tril, ws_ref[g], 0.0).astype(BF16)
            cols = slice(CHUNK * g, CHUNK * (g + 1))
            for c in range(tm // CHUNK):
                rows = slice(CHUNK * c, CHUNK * (c + 1))
                dgb = dg_scr[rows, cols]
                au = a_ref[rows, cols].astype(F32)
                dmx = dgb * _gelu(au)
                da_ref[rows, cols] = (dgb * mx_ref[rows, cols].astype(F32) * _gelu_grad(au)).astype(BF16)
                dmb = dmx.astype(BF16)
                dvn_scr[rows, cols] = lax.dot_general(wg, dmb, TN_DIMS, preferred_element_type=F32)
                dws_ref[g] += jnp.where(
                    tril, lax.dot_general(dmb, vn_ref[rows, cols], NT_DIMS, preferred_element_type=F32), 0.0)
                dba_ref[:, cols] += dmx
        av = a_ref[:, SGU_W:].astype(F32)
        v = _gelu(av)
        mu = jnp.mean(v, axis=-1, keepdims=True)
        vc = v - mu
        rstd = lax.rsqrt(jnp.mean(vc * vc, axis=-1, keepdims=True) + LN_EPS)
        xhat = vc * rstd
        dvn = dvn_scr[...]
        dlg_ref[...] += jnp.sum(dvn * xhat, axis=0, keepdims=True)
        dlb_ref[...] += jnp.sum(dvn, axis=0, keepdims=True)
        dxh = dvn * lg_ref[...]
        dv = rstd * (dxh - jnp.mean(dxh, axis=-1, keepdims=True)
                     - xhat * jnp.mean(dxh * xhat, axis=-1, keepdims=True))
        da_ref[:, SGU_W:] = (dv * _gelu_grad(av)).astype(BF16)

    ins = [(dx, "row"), _wkind(wt), (a, "row"), (vn, "row"), (mixed, "row"), (ln_g, "full"), (w_s, "full")]
    outs = [((T, 2 * SGU_W), BF16, "row"), ((SGU_G, CHUNK, CHUNK), F32, "full"), ((CHUNK, SGU_W), F32, "full"),
            ((1, SGU_W), F32, "full"), ((1, SGU_W), F32, "full")]
    return _rowcall(name, body, T, tm, ins, outs, 40,
                    scratch=[pltpu.VMEM((tm, SGU_W), F32), pltpu.VMEM((tm, SGU_W), F32)])


def _loss_head(x, wf, tgt, name):
    T = x.shape[0]
    tm = min(512, T)

    def body(x_ref, wf_ref, tgt_ref, dx_ref, dwf_ref, loss_ref):
        @pl.when(pl.program_id(0) == 0)
        def _():
            dwf_ref[...] = jnp.zeros(dwf_ref.shape, F32)
            loss_ref[...] = jnp.zeros(loss_ref.shape, F32)

        xv = x_ref[...]
        r = lax.rsqrt(jnp.mean(xv * xv, axis=-1, keepdims=True) + NORM_EPS)
        xn = xv * r
        err = xn * wf_ref[...] - tgt_ref[...]
        loss_ref[...] += 0.5 * jnp.sum(jnp.mean(err * err, axis=-1, keepdims=True), axis=0, keepdims=True)
        dy = err * (1.0 / D_MODEL)
        dwf_ref[...] += jnp.sum(dy * xn, axis=0, keepdims=True)
        dyw = dy * wf_ref[...]
        dx_ref[...] = r * (dyw - xn * jnp.mean(dyw * xn, axis=-1, keepdims=True))

    return _rowcall(name, body, T, tm, [(x, "row"), (wf, "full"), (tgt, "row")],
                    [((T, D_MODEL), F32, "row"), ((1, D_MODEL), F32, "full"), ((8, LANES), F32, "full")], 32)


def _peers():
    x, y, c = lax.axis_index("x"), lax.axis_index("y"), lax.axis_index("c")
    peers = []
    for p in range(1, N_DEV):
        px = 1 - x if p & 4 else x
        py = 1 - y if p & 2 else y
        pc = 1 - c if p & 1 else c
        peers.append((4 * px + 2 * py + pc, (px, py, pc)))
    return 4 * x + 2 * y + c, peers


def _all_gather(arrs, name):
    n = len(arrs)
    hbm = pl.BlockSpec(memory_space=pl.ANY)

    def body(*refs):
        ins, outs = refs[:n], refs[n:2 * n]
        send_sems, recv_sems, local_sems = refs[2 * n:]
        me, peers = _peers()
        sends, recvs, locals_ = [], [], []
        for t in range(n):
            cp = pltpu.make_async_copy(ins[t], outs[t].at[me], local_sems.at[t])
            cp.start()
            locals_.append(cp)
            for k, (pidx, pid) in enumerate(peers):
                s = t * (N_DEV - 1) + k
                send = pltpu.make_async_remote_copy(
                    src_ref=ins[t], dst_ref=outs[t].at[me], send_sem=send_sems.at[s], recv_sem=recv_sems.at[s],
                    device_id=pid, device_id_type=MESH)
                send.start()
                sends.append(send)
                recvs.append(pltpu.make_async_remote_copy(
                    src_ref=ins[t], dst_ref=outs[t].at[pidx], send_sem=send_sems.at[s], recv_sem=recv_sems.at[s],
                    device_id=pid, device_id_type=MESH))
        for r in recvs:
            r.wait_recv()
        for s in sends:
            s.wait_send()
        for cp in locals_:
            cp.wait()

    return pl.pallas_call(
        body,
        name=name,
        in_specs=[hbm] * n,
        out_specs=[hbm] * n,
        out_shape=[jax.ShapeDtypeStruct((N_DEV,) + a.shape, a.dtype) for a in arrs],
        scratch_shapes=[pltpu.SemaphoreType.DMA((n * (N_DEV - 1),)), pltpu.SemaphoreType.DMA((n * (N_DEV - 1),)),
                        pltpu.SemaphoreType.DMA((n,))],
    )(*arrs)


def _exchange(arrs, name):
    n = len(arrs)
    hbm = pl.BlockSpec(memory_space=pl.ANY)

    def body(*refs):
        ins, outs = refs[:n], refs[n:2 * n]
        send_sems, recv_sems, local_sems = refs[2 * n:]
        me, peers = _peers()
        sends, recvs, locals_ = [], [], []
        for t in range(n):
            cp = pltpu.make_async_copy(ins[t].at[me], outs[t].at[me], local_sems.at[t])
            cp.start()
            locals_.append(cp)
            for k, (pidx, pid) in enumerate(peers):
                s = t * (N_DEV - 1) + k
                send = pltpu.make_async_remote_copy(
                    src_ref=ins[t].at[pidx], dst_ref=outs[t].at[me], send_sem=send_sems.at[s],
                    recv_sem=recv_sems.at[s], device_id=pid, device_id_type=MESH)
                send.start()
                sends.append(send)
                recvs.append(pltpu.make_async_remote_copy(
                    src_ref=ins[t].at[pidx], dst_ref=outs[t].at[pidx], send_sem=send_sems.at[s],
                    recv_sem=recv_sems.at[s], device_id=pid, device_id_type=MESH))
        for r in recvs:
            r.wait_recv()
        for s in sends:
            s.wait_send()
        for cp in locals_:
            cp.wait()

    return pl.pallas_call(
        body,
        name=name,
        in_specs=[hbm] * n,
        out_specs=[hbm] * n,
        out_shape=[jax.ShapeDtypeStruct(a.shape, a.dtype) for a in arrs],
        scratch_shapes=[pltpu.SemaphoreType.DMA((n * (N_DEV - 1),)), pltpu.SemaphoreType.DMA((n * (N_DEV - 1),)),
                        pltpu.SemaphoreType.DMA((n,))],
    )(*arrs)


def _row_tile(rows, cap):
    best = None
    for t in range(16, cap + 1, 16):
        if rows % t == 0:
            best = t
    assert best is not None, rows
    return best


def _adam_sum(parts, w, m, v, name):
    R, C = w.shape
    tr = _row_tile(R, 128)

    def body(p_ref, w_ref, m_ref, v_ref, g_ref, d_ref, nm_ref, nv_ref):
        g = p_ref[0].astype(F32)
        for s in range(1, N_DEV):
            g = g + p_ref[s].astype(F32)
        mm = ADAM_B1 * m_ref[...] + (1.0 - ADAM_B1) * g
        vv = ADAM_B2 * v_ref[...] + (1.0 - ADAM_B2) * (g * g)
        m_hat = mm / (1.0 - ADAM_B1 ** ADAM_STEP)
        v_hat = vv / (1.0 - ADAM_B2 ** ADAM_STEP)
        g_ref[...] = g
        d_ref[...] = -ADAM_LR * (m_hat / (jnp.sqrt(v_hat) + ADAM_EPS) + ADAM_WD * w_ref[...])
        nm_ref[...] = mm
        nv_ref[...] = vv

    mat = pl.BlockSpec((tr, C), lambda i: (i, 0))
    return pl.pallas_call(
        body,
        name=name,
        grid=(R // tr,),
        in_specs=[pl.BlockSpec((N_DEV, tr, C), lambda i: (0, i, 0)), mat, mat, mat],
        out_specs=[mat] * 4,
        out_shape=[jax.ShapeDtypeStruct((R, C), F32)] * 4,
        compiler_params=_params(("parallel",), 32),
    )(parts, w, m, v)


def _cols_from_gathered(g):
    _, L, K, n = g.shape
    return jnp.transpose(g, (1, 2, 0, 3)).reshape(L, K, N_DEV * n)


def _rows_from_gathered(g):
    _, L, k, N = g.shape
    return jnp.transpose(g, (1, 0, 2, 3)).reshape(L, N_DEV * k, N)


def _cols_to_blocks(dw):
    L, K, N = dw.shape
    n = N // N_DEV
    return jnp.transpose(dw.reshape(L, K, N_DEV, n), (2, 0, 1, 3)).reshape(N_DEV, L * K, n)


def _rows_to_blocks(dw):
    L, K, N = dw.shape
    k = K // N_DEV
    return jnp.transpose(dw.reshape(L, N_DEV, k, N), (1, 0, 2, 3)).reshape(N_DEV, L * k, N)


def _ffn_interleave(w):
    lead = w.shape[:-1]
    t = w.reshape(lead + (2, N_FFN_TILE, FFN_TILE))
    return jnp.swapaxes(t, -3, -2).reshape(lead + (2 * FFN_H,))


def _ffn_deinterleave(w):
    lead = w.shape[:-1]
    t = w.reshape(lead + (N_FFN_TILE, 2, FFN_TILE))
    return jnp.swapaxes(t, -3, -2).reshape(lead + (2 * FFN_H,))


def _pad_rows(a, rows=8):
    a = a.reshape(-1, a.shape[-1])
    return jnp.pad(a, ((0, rows - a.shape[0]), (0, 0)))


SMALL_ROWS = 6 * 8 + 2 * SGU_G * CHUNK * CHUNK // D_MODEL


def _pack_small(mixer, ffn, final, b_f, extra, w_s, b_s):
    bf_row = jnp.pad(b_f.reshape(1, -1), ((0, 0), (0, D_MODEL - b_f.size)))
    bs_rows = jnp.pad(b_s.reshape(4, -1), ((0, 0), (0, D_MODEL - b_s.size // 4)))
    return jnp.concatenate([
        _pad_rows(mixer), _pad_rows(ffn), _pad_rows(final.reshape(1, -1)), _pad_rows(bf_row),
        _pad_rows(extra), _pad_rows(bs_rows), w_s.reshape(-1, D_MODEL)], axis=0)


def _unpack_small(p):
    mixer, ffn, final = p[0:4], p[8:12], p[16]
    b_f = p[24, :32].reshape(2, 2 * N_PAIR)
    extra = p[32]
    b_s = p[40:44, :2 * SGU_G * CHUNK // 4].reshape(2, SGU_G, CHUNK)
    w_s = p[48:].reshape(2, SGU_G, CHUNK, CHUNK)
    return mixer, ffn, final, b_f, extra, w_s, b_s


def kernel(x, mixer_norm_w, attn_w_in, attn_b_f, attn_w_out, sgu_w_in, sgu_ln_g, sgu_ln_b, sgu_w_s, sgu_b_s, sgu_w_out, ffn_norm_w, ffn_w_in, ffn_w_out, final_norm_w, loss_target, m_mixer_norm_w, m_attn_w_in, m_attn_b_f, m_attn_w_out, m_sgu_w_in, m_sgu_ln_g, m_sgu_ln_b, m_sgu_w_s, m_sgu_b_s, m_sgu_w_out, m_ffn_norm_w, m_ffn_w_in, m_ffn_w_out, m_final_norm_w, v_mixer_norm_w, v_attn_w_in, v_attn_b_f, v_attn_w_out, v_sgu_w_in, v_sgu_ln_g, v_sgu_ln_b, v_sgu_w_s, v_sgu_b_s, v_sgu_w_out, v_ffn_norm_w, v_ffn_w_in, v_ffn_w_out, v_final_norm_w):
    T = x.shape[1]
    tb = min(ATT_BLOCK, T)
    xs, tgt = x[0], loss_target[0]

    shards = [attn_w_in, attn_w_out, sgu_w_in, sgu_w_out, ffn_w_in, ffn_w_out, sgu_ln_g, sgu_ln_b]
    gathered = _all_gather([s.astype(BF16) for s in shards[:6]] + shards[6:], "gather_weights")
    w_attn_in = _cols_from_gathered(gathered[0])
    w_attn_out = _rows_from_gathered(gathered[1])
    w_sgu_in = _cols_from_gathered(gathered[2])
    w_sgu_out = _rows_from_gathered(gathered[3])
    w_ffn_in = _ffn_interleave(_cols_from_gathered(gathered[4]))
    w_ffn_out = _rows_from_gathered(gathered[5])
    ln_g = jnp.transpose(gathered[6], (1, 0, 2)).reshape(2, 1, SGU_W)
    ln_b = jnp.transpose(gathered[7], (1, 0, 2)).reshape(2, 1, SGU_W)
    w_qkv = w_attn_in[:, :, :3 * D_MODEL]
    w_f = jnp.pad(w_attn_in[:, :, 3 * D_MODEL:], ((0, 0), (0, 0), (0, LANES - 2 * N_PAIR)))
    tr = lambda w: jnp.swapaxes(w, -1, -2)
    w_qkv_t6, w_f_t, w_attn_out_t = tr(w_qkv).reshape(6, D_MODEL, D_MODEL), tr(w_f), tr(w_attn_out)
    w_sgu_in_t, w_sgu_out_t, w_ffn_in_t, w_ffn_out_t = tr(w_sgu_in), tr(w_sgu_out), tr(w_ffn_in), tr(w_ffn_out)
    mixer_nw = mixer_norm_w.reshape(4, 1, D_MODEL)
    ffn_nw = ffn_norm_w.reshape(4, 1, D_MODEL)
    b_col = attn_b_f.reshape(2, 2 * N_PAIR, 1)
    bs_t = jnp.swapaxes(sgu_b_s, 1, 2)

    saved = []
    xr = xs
    for i in range(4):
        j = i // 2
        if i % 2 == 0:
            qkv3, h = _norm_matmul(xr, mixer_nw[i], (w_qkv, j), BF16, f"attn_qkv_{j}", 1024, groups=True)
            fl, _ = _norm_matmul(xr, mixer_nw[i], (w_f, j), F32, f"attn_gate_{j}", LANES)
            fl3 = jnp.transpose(fl[:, :2 * N_PAIR].reshape(T // CHUNK, CHUNK, 2 * N_PAIR), (0, 2, 1))
            c_chunks = _fgate_fwd(fl3, b_col[j], f"fgate_fwd_{j}")
            cp3 = _bias_pieces(jnp.transpose(c_chunks, (1, 0, 2)).reshape(2 * N_PAIR, T))
            ot3, lse4 = _attn_fwd(qkv3, cp3, f"attn_fwd_{j}")
            o3 = jnp.swapaxes(ot3, 1, 2)
            xm = _attn_out(o3, (w_attn_out, j), xr, f"attn_out_{j}")
            mix_saved = (xr, h, qkv3, fl3, cp3, o3, lse4)
        else:
            a, h = _norm_matmul(xr, mixer_nw[i], (w_sgu_in, j), BF16, f"sgu_in_{j}", 1024)
            xm, gated, vn, mixed = _sgu_core(a, ln_g[j], ln_b[j], sgu_w_s[j], bs_t[j], (w_sgu_out, j), xr,
                                             f"sgu_core_{j}")
            mix_saved = (xr, h, a, gated, vn, mixed)
        gu, h2 = _norm_matmul(xm, ffn_nw[i], (w_ffn_in, i), BF16, f"ffn_in_{i}", FFN_H // 2)
        xo, hid = _ffn_out(gu, (w_ffn_out, i), xm, f"ffn_out_{i}")
        saved.append((mix_saved, (xm, h2, gu, hid)))
        xr = xo
    dx, d_final, loss_part = _loss_head(xr, final_norm_w.reshape(1, D_MODEL), tgt, "loss_head")

    d_mixer_nw, d_ffn_nw = [None] * 4, [None] * 4
    d_attn_in, d_attn_out, d_bf, d_sgu_in, d_sgu_out = [None] * 2, [None] * 2, [None] * 2, [None] * 2, [None] * 2
    d_ws, d_bs, d_lg, d_lb = [None] * 2, [None] * 2, [None] * 2, [None] * 2
    d_ffn_in, d_ffn_out = [None] * 4, [None] * 4
    for i in reversed(range(4)):
        j = i // 2
        mix_saved, (xm, h2, gu, hid) = saved[i]
        dgu = _ffn_dgu(dx, (w_ffn_out_t, i), gu, f"ffn_dgu_{i}")
        d_ffn_out[i] = _matmul_tn(hid, dx, f"ffn_dwout_{i}", FFN_H // 2, D_MODEL)
        d_ffn_in[i] = _matmul_tn(h2, dgu, f"ffn_dwin_{i}", D_MODEL, FFN_H // 2)
        dx, d_ffn_nw[i] = _matmul_rms_bwd([(dgu, "row")], [(w_ffn_in_t, i)], xm, ffn_nw[i], dx, f"ffn_dx_{i}", 512, 48)
        if i % 2 == 0:
            xr, h, qkv3, fl3, cp3, o3, lse4 = mix_saved
            do3, dd = _attn_dout(dx, (w_attn_out_t, j), o3, f"attn_dout_{j}")
            d_attn_out[j] = _matmul_tn(o3, dx, f"attn_dwout_{j}", D_MODEL, D_MODEL, a_grp=True)
            to_blocks = lambda a: jnp.swapaxes(a.reshape(N_PAIR, 2, T // tb, tb), 1, 2)
            from_blocks = lambda a: jnp.swapaxes(a, 1, 2).reshape(N_PAIR, 2, T)
            dd4 = to_blocks(dd.T.reshape(N_PAIR, 2, T))
            st4 = jnp.concatenate([to_blocks(from_blocks(lse4)), dd4], axis=2)
            kst4 = jnp.swapaxes((qkv3[N_PAIR:2 * N_PAIR] * QK_SCALE).reshape(N_PAIR, T // tb, tb, LANES), 2, 3)
            dqt4, rs4, dk3, dv3, cs4 = _attn_bwd(qkv3, cp3, kst4, do3, st4, f"attn_bwd_{j}")
            dq3 = jnp.swapaxes(dqt4, 2, 3).reshape(N_PAIR, T, LANES).astype(BF16)
            dc_pair = from_blocks(rs4 - jnp.stack([cs4[:, :, 0], cs4[:, :, 8]], axis=2))
            dc_chunks = jnp.transpose(dc_pair.reshape(2 * N_PAIR, T // CHUNK, CHUNK), (1, 0, 2))
            dfl3, db = _fgate_bwd(dc_chunks, fl3, b_col[j], f"fgate_bwd_{j}")
            d_bf[j] = db[:, 0]
            dfl = jnp.transpose(dfl3, (0, 2, 1)).reshape(T, 2 * N_PAIR)
            dfl = jnp.pad(dfl.astype(BF16), ((0, 0), (0, LANES - 2 * N_PAIR)))
            d_qkv = [_matmul_tn(h, d3, f"attn_dw{nm}_{j}", D_MODEL, D_MODEL, g_grp=True)
                     for nm, d3 in (("q", dq3), ("k", dk3), ("v", dv3))]
            d_f = _matmul_tn(h, dfl, f"attn_dwf_{j}", D_MODEL, LANES)[:, :2 * N_PAIR]
            d_attn_in[j] = jnp.concatenate(d_qkv + [d_f], axis=1)
            wts = [(w_qkv_t6, 3 * j + k) for k in range(3)] + [(w_f_t, j)]
            dx, d_mixer_nw[i] = _matmul_rms_bwd(
                [(dq3, "grp"), (dk3, "grp"), (dv3, "grp"), (dfl, "row")], wts, xr, mixer_nw[i], dx,
                f"attn_dx_{j}", 256, 40)
        else:
            xr, h, a, gated, vn, mixed = mix_saved
            da, d_ws[j], dba, d_lg[j], d_lb[j] = _sgu_core_bwd(dx, (w_sgu_out_t, j), a, vn, mixed, ln_g[j], sgu_w_s[j],
                                                               f"sgu_core_bwd_{j}")
            d_bs[j] = jnp.sum(dba.reshape(CHUNK, SGU_G, CHUNK), axis=-1).T
            d_sgu_out[j] = _matmul_tn(gated, dx, f"sgu_dwout_{j}", D_MODEL, D_MODEL)
            d_sgu_in[j] = _matmul_tn(h, da, f"sgu_dwin_{j}", D_MODEL, 1024)
            dx, d_mixer_nw[i] = _matmul_rms_bwd([(da, "row")], [(w_sgu_in_t, j)], xr, mixer_nw[i], dx,
                                                f"sgu_dx_{j}", 256, 40)
    grad_x = dx[None]

    rows4 = lambda parts: jnp.concatenate(parts, axis=1).reshape(4, D_MODEL)
    small_g = _pack_small(rows4(d_mixer_nw), rows4(d_ffn_nw), d_final[0], jnp.stack(d_bf),
                          loss_part[0:1, 0:1] * jnp.ones((1, D_MODEL), F32), jnp.stack(d_ws), jnp.stack(d_bs))
    zero_row = jnp.zeros((1, D_MODEL), F32)
    pack = lambda pre: _pack_small(pre[0], pre[1], pre[2], pre[3], zero_row, pre[4], pre[5])
    small_w = pack((mixer_norm_w, ffn_norm_w, final_norm_w, attn_b_f, sgu_w_s, sgu_b_s))
    small_m = pack((m_mixer_norm_w, m_ffn_norm_w, m_final_norm_w, m_attn_b_f, m_sgu_w_s, m_sgu_b_s))
    small_v = pack((v_mixer_norm_w, v_ffn_norm_w, v_final_norm_w, v_attn_b_f, v_sgu_w_s, v_sgu_b_s))
    small_all = _all_gather([small_g], "gather_small_grads")[0]
    small_out = [_unpack_small(p) for p in _adam_sum(small_all, small_w, small_m, small_v, "adam_small")]
    loss = small_out[0][4][0]

    blocks = [
        _cols_to_blocks(jnp.stack(d_attn_in)), _rows_to_blocks(jnp.stack(d_attn_out)),
        _cols_to_blocks(jnp.stack(d_sgu_in)), _rows_to_blocks(jnp.stack(d_sgu_out)),
        _cols_to_blocks(_ffn_deinterleave(jnp.stack(d_ffn_in))), _rows_to_blocks(jnp.stack(d_ffn_out)),
        jnp.stack(d_lg).reshape(2, N_DEV, 1, SGU_W // N_DEV).transpose(1, 0, 2, 3).reshape(N_DEV, 2, SGU_W // N_DEV),
        jnp.stack(d_lb).reshape(2, N_DEV, 1, SGU_W // N_DEV).transpose(1, 0, 2, 3).reshape(N_DEV, 2, SGU_W // N_DEV),
    ]
    received = _exchange([b.astype(BF16) for b in blocks[:6]] + blocks[6:], "exchange_grads")
    names = ["attn_w_in", "attn_w_out", "sgu_w_in", "sgu_w_out", "ffn_w_in", "ffn_w_out"]
    ws = [attn_w_in, attn_w_out, sgu_w_in, sgu_w_out, ffn_w_in, ffn_w_out]
    ms = [m_attn_w_in, m_attn_w_out, m_sgu_w_in, m_sgu_w_out, m_ffn_w_in, m_ffn_w_out]
    vs = [v_attn_w_in, v_attn_w_out, v_sgu_w_in, v_sgu_w_out, v_ffn_w_in, v_ffn_w_out]
    big_out = {}
    for nm, rec, w, m, v in zip(names, received[:6], ws, ms, vs):
        flat = lambda a: a.reshape(-1, a.shape[-1])
        big_out[nm] = [o.reshape(w.shape) for o in _adam_sum(rec, flat(w), flat(m), flat(v), f"adam_{nm}")]
    pad8 = lambda a: jnp.pad(a, [(0, 0)] * (a.ndim - 2) + [(0, 8 - a.shape[-2]), (0, 0)])
    ln_parts = jnp.concatenate([pad8(received[6]), pad8(received[7])], axis=1)
    ln_pack = lambda g, b: jnp.concatenate([pad8(g), pad8(b)], axis=0)
    ln_out = _adam_sum(ln_parts, ln_pack(sgu_ln_g, sgu_ln_b), ln_pack(m_sgu_ln_g, m_sgu_ln_b),
                       ln_pack(v_sgu_ln_g, v_sgu_ln_b), "adam_sgu_ln")

    def leaf(kind):
        mixer, ffn, final, b_f, _, w_s, b_s = small_out[kind]
        o = lambda nm: big_out[nm][kind]
        return [mixer, o("attn_w_in"), b_f, o("attn_w_out"), o("sgu_w_in"), ln_out[kind][0:2], ln_out[kind][8:10],
                w_s, b_s, o("sgu_w_out"), ffn, o("ffn_w_in"), o("ffn_w_out"), final]

    return (loss, grad_x, *leaf(0), *leaf(1), *leaf(2), *leaf(3))
```

```python
import functools

import jax
import jax.numpy as jnp
from jax import lax
from jax.experimental import pallas as pl
from jax.experimental.pallas import tpu as pltpu

F32 = jnp.float32
BF16 = jnp.bfloat16

D_MODEL = 1024
HEAD_DIM = 64
N_PAIR = 8
LANES = 128
SGU_W = 2048
SGU_G = 16
CHUNK = 128
FFN_H = 2816
FFN_TILE = 256
N_FFN_TILE = FFN_H // FFN_TILE
NORM_EPS = 1e-6
LN_EPS = 1e-5
QK_SCALE = 0.125
ATT_BLOCK = 512
N_DEV = 8
ADAM_LR = 0.001
ADAM_B1 = 0.9
ADAM_B2 = 0.999
ADAM_EPS = 1e-08
ADAM_WD = 0.01
ADAM_STEP = 10
MESH = pl.DeviceIdType.MESH
SQRT_HALF = 0.7071067811865476
INV_SQRT_2PI = 0.3989422804014327

NT_DIMS = (((1,), (1,)), ((), ()))
TN_DIMS = (((0,), (0,)), ((), ()))


def _gelu(x):
    return 0.5 * x * (1.0 + lax.erf(x * SQRT_HALF))


def _gelu_grad(x):
    return 0.5 * (1.0 + lax.erf(x * SQRT_HALF)) + x * jnp.exp(-0.5 * x * x) * INV_SQRT_2PI


def _lane_col(v, lane):
    idx = lax.broadcasted_iota(jnp.int32, v.shape, 1)
    return jnp.sum(jnp.where(idx == lane, v, 0.0), axis=1, keepdims=True)


def _params(sem, vmem_mb):
    return pltpu.CompilerParams(dimension_semantics=sem, vmem_limit_bytes=vmem_mb << 20)


def _cat_groups(ref, n):
    if n == 1:
        return ref[0]
    return jnp.concatenate([ref[t] for t in range(n)], axis=1)


def _wkind(w):
    return (w[0], ("layer", w[1])) if isinstance(w, tuple) else (w, "full")


def _rowcall(name, body, T, tm, ins, outs, vmem_mb, scratch=()):
    def spec(shape, kind, resident_once=False):
        shape = tuple(shape)
        if isinstance(kind, tuple):
            layer = kind[1]
            return pl.BlockSpec((None,) + shape[1:], lambda i: (layer,) + (0,) * (len(shape) - 1),
                                pipeline_mode=pl.Buffered(1))
        if kind == "row":
            return pl.BlockSpec((tm,) + shape[1:], lambda i: (i,) + (0,) * (len(shape) - 1))
        if kind == "grp":
            return pl.BlockSpec((shape[0], tm, shape[2]), lambda i: (0, i, 0))
        if resident_once:
            return pl.BlockSpec(shape, lambda i: (0,) * len(shape), pipeline_mode=pl.Buffered(1))
        return pl.BlockSpec(shape, lambda i: (0,) * len(shape))

    return pl.pallas_call(
        body,
        name=name,
        grid=(T // tm,),
        in_specs=[spec(a.shape, k, True) for a, k in ins],
        out_specs=[spec(s, k) for s, _, k in outs],
        out_shape=[jax.ShapeDtypeStruct(tuple(s), d) for s, d, _ in outs],
        scratch_shapes=list(scratch),
        compiler_params=_params(("arbitrary",), vmem_mb),
    )(*[a for a, _ in ins])


def _norm_matmul(x, nw, w, out_dtype, name, tn, groups=False):
    w, layer = w
    T, N = x.shape[0], w.shape[2]
    tm = min(1024, T)

    def body(x_ref, nw_ref, w_ref, o_ref, h_ref, h_scr):
        @pl.when(pl.program_id(1) == 0)
        def _():
            xv = x_ref[...]
            r = lax.rsqrt(jnp.mean(xv * xv, axis=-1, keepdims=True) + NORM_EPS)
            hv = (xv * r * nw_ref[...]).astype(BF16)
            h_scr[...] = hv
            h_ref[...] = hv

        acc = jnp.dot(h_scr[...], w_ref[...], preferred_element_type=F32)
        if groups:
            for t in range(tn // LANES):
                o_ref[t] = acc[:, LANES * t:LANES * (t + 1)].astype(out_dtype)
        else:
            o_ref[...] = acc.astype(out_dtype)

    if groups:
        o_shape = (N // LANES, T, LANES)
        o_spec = pl.BlockSpec((tn // LANES, tm, LANES), lambda i, j: (j, i, 0))
    else:
        o_shape = (T, N)
        o_spec = pl.BlockSpec((tm, tn), lambda i, j: (i, j))
    return pl.pallas_call(
        body,
        name=name,
        grid=(T // tm, N // tn),
        in_specs=[
            pl.BlockSpec((tm, D_MODEL), lambda i, j: (i, 0)),
            pl.BlockSpec((1, D_MODEL), lambda i, j: (0, 0)),
            pl.BlockSpec((None, D_MODEL, tn), lambda i, j: (layer, 0, j)),
        ],
        out_specs=[o_spec, pl.BlockSpec((tm, D_MODEL), lambda i, j: (i, 0))],
        out_shape=[jax.ShapeDtypeStruct(o_shape, out_dtype), jax.ShapeDtypeStruct((T, D_MODEL), BF16)],
        scratch_shapes=[pltpu.VMEM((tm, D_MODEL), BF16)],
        compiler_params=_params(("arbitrary", "arbitrary"), 48),
    )(x, nw, w)


def _matmul_tn(a, g, name, tk, tn, a_grp=False, g_grp=False):
    T = a.shape[1] if a_grp else a.shape[0]
    K = a.shape[0] * LANES if a_grp else a.shape[1]
    N = g.shape[0] * LANES if g_grp else g.shape[1]
    tm = min(1024, T)

    def body(a_ref, g_ref, o_ref):
        @pl.when(pl.program_id(2) == 0)
        def _():
            o_ref[...] = jnp.zeros(o_ref.shape, F32)

        av = _cat_groups(a_ref, tk // LANES) if a_grp else a_ref[...]
        gv = _cat_groups(g_ref, tn // LANES) if g_grp else g_ref[...]
        o_ref[...] += lax.dot_general(av.astype(BF16), gv.astype(BF16), TN_DIMS, preferred_element_type=F32)

    if a_grp:
        a_spec = pl.BlockSpec((tk // LANES, tm, LANES), lambda k, n, m: (k, m, 0))
    else:
        a_spec = pl.BlockSpec((tm, tk), lambda k, n, m: (m, k))
    if g_grp:
        g_spec = pl.BlockSpec((tn // LANES, tm, LANES), lambda k, n, m: (n, m, 0))
    else:
        g_spec = pl.BlockSpec((tm, tn), lambda k, n, m: (m, n))
    return pl.pallas_call(
        body,
        name=name,
        grid=(K // tk, N // tn, T // tm),
        in_specs=[a_spec, g_spec],
        out_specs=pl.BlockSpec((tk, tn), lambda k, n, m: (k, n)),
        out_shape=jax.ShapeDtypeStruct((K, N), F32),
        compiler_params=_params(("parallel", "parallel", "arbitrary"), 48),
    )(a, g)


def _matmul_rms_bwd(a_list, wt_list, x, nw, dres, name, tm, vmem_mb):
    T = x.shape[0]
    n = len(a_list)

    def body(*refs):
        a_refs, w_refs = refs[:n], refs[n:2 * n]
        x_ref, nw_ref, dres_ref, dx_ref, dnw_ref = refs[2 * n:]

        @pl.when(pl.program_id(0) == 0)
        def _():
            dnw_ref[...] = jnp.zeros(dnw_ref.shape, F32)

        dh = None
        for (arr, kind), a_ref, w_ref in zip(a_list, a_refs, w_refs):
            av = _cat_groups(a_ref, arr.shape[0]) if kind == "grp" else a_ref[...]
            part = jnp.dot(av.astype(BF16), w_ref[...], preferred_element_type=F32)
            dh = part if dh is None else dh + part
        xv = x_ref[...]
        r = lax.rsqrt(jnp.mean(xv * xv, axis=-1, keepdims=True) + NORM_EPS)
        xn = xv * r
        dnw_ref[...] += jnp.sum(dh * xn, axis=0, keepdims=True)
        dyw = dh * nw_ref[...]
        dx_ref[...] = dres_ref[...] + r * (dyw - xn * jnp.mean(dyw * xn, axis=-1, keepdims=True))

    ins = list(a_list) + [_wkind(w) for w in wt_list] + [(x, "row"), (nw, "full"), (dres, "row")]
    outs = [((T, D_MODEL), F32, "row"), ((1, D_MODEL), F32, "full")]
    return _rowcall(name, body, T, tm, ins, outs, vmem_mb)


def _fgate_fwd(fl3, bcol, name):
    n_chunk = fl3.shape[0]

    def body(fl_ref, b_ref, c_ref):
        r = lax.broadcasted_iota(jnp.int32, (CHUNK, CHUNK), 0)
        t = lax.broadcasted_iota(jnp.int32, (CHUNK, CHUNK), 1)
        tri = jnp.where(r <= t, 1.0, 0.0).astype(BF16)

        def chunk(i, carry):
            z = fl_ref[i] + b_ref[...]
            lf = jnp.minimum(z, 0.0) - jnp.log(1.0 + jnp.exp(-jnp.abs(z)))
            hi = lf.astype(BF16)
            r1 = lf - hi.astype(F32)
            mid = r1.astype(BF16)
            low = (r1 - mid.astype(F32)).astype(BF16)
            cs = (jnp.dot(hi, tri, preferred_element_type=F32) + jnp.dot(mid, tri, preferred_element_type=F32)
                  + jnp.dot(low, tri, preferred_element_type=F32)) + carry
            c_ref[i] = cs
            return _lane_col(cs, CHUNK - 1)

        lax.fori_loop(0, n_chunk, chunk, jnp.zeros((2 * N_PAIR, 1), F32))

    return pl.pallas_call(
        body, name=name, out_shape=jax.ShapeDtypeStruct(fl3.shape, F32),
        compiler_params=pltpu.CompilerParams(vmem_limit_bytes=16 << 20),
    )(fl3, bcol)


def _fgate_bwd(dc3, fl3, bcol, name):
    n_chunk = fl3.shape[0]

    def body(dc_ref, fl_ref, b_ref, dfl_ref, db_ref):
        tt = lax.broadcasted_iota(jnp.int32, (CHUNK, CHUNK), 0)
        rr = lax.broadcasted_iota(jnp.int32, (CHUNK, CHUNK), 1)
        tri = jnp.where(tt >= rr, 1.0, 0.0).astype(BF16)

        def chunk(k, carry):
            tail, acc = carry
            i = n_chunk - 1 - k
            dc = dc_ref[i]
            hi = dc.astype(BF16)
            r1 = dc - hi.astype(F32)
            mid = r1.astype(BF16)
            low = (r1 - mid.astype(F32)).astype(BF16)
            dlf = (jnp.dot(hi, tri, preferred_element_type=F32) + jnp.dot(mid, tri, preferred_element_type=F32)
                   + jnp.dot(low, tri, preferred_element_type=F32)) + tail
            z = fl_ref[i] + b_ref[...]
            dfl = dlf / (1.0 + jnp.exp(z))
            dfl_ref[i] = dfl
            return _lane_col(dlf, 0), acc + dfl

        _, acc = lax.fori_loop(0, n_chunk, chunk,
                               (jnp.zeros((2 * N_PAIR, 1), F32), jnp.zeros((2 * N_PAIR, CHUNK), F32)))
        db_ref[...] = jnp.broadcast_to(jnp.sum(acc, axis=1, keepdims=True), db_ref.shape)

    return pl.pallas_call(
        body, name=name,
        out_shape=[jax.ShapeDtypeStruct(fl3.shape, F32), jax.ShapeDtypeStruct((2 * N_PAIR, LANES), F32)],
        compiler_params=pltpu.CompilerParams(vmem_limit_bytes=16 << 20),
    )(dc3, fl3, bcol)


BIAS_LANES = 3
ATT_Q_BLOCK = 2048


def _own_lanes(shape, hh, axis=1):
    idx = lax.broadcasted_iota(jnp.int32, shape, axis)
    return idx < HEAD_DIM if hh == 0 else idx >= HEAD_DIM


def _spare(hh):
    return HEAD_DIM * (1 - hh)


def _bias_pieces(c16):
    T = c16.shape[1]
    negc = -c16
    hi = negc.astype(BF16)
    r1 = negc - hi.astype(F32)
    mid = r1.astype(BF16)
    low = (r1 - mid.astype(F32)).astype(BF16)
    pieces = jnp.stack([hi, mid, low], axis=-1).reshape(N_PAIR, 2, T, BIAS_LANES)
    zpad = jnp.zeros((N_PAIR, T, HEAD_DIM - BIAS_LANES), BF16)
    return jnp.concatenate([pieces[:, 1], zpad, pieces[:, 0], zpad], axis=-1)


def _attn_fwd(qkv3, cp3, name):
    T = qkv3.shape[1]
    tb = min(ATT_BLOCK, T)
    tq = min(ATT_Q_BLOCK, T)
    nb = T // tb
    per_q = tq // tb

    assert per_q % 2 == 0 or T == tq, (T, tq, tb)

    def body(q_ref, k_ref, v_ref, cp_ref, o_ref, lse_ref, st_a, st_b):
        i = pl.program_id(1)
        lane = lax.broadcasted_iota(jnp.int32, (tq, LANES), 1)
        lane_k = lax.broadcasted_iota(jnp.int32, (tb, LANES), 1)
        feat = lax.broadcasted_iota(jnp.int32, (LANES, tq), 0)
        q2 = q_ref[0] * QK_SCALE
        qa, own_k, one_k = [], [], []
        for hh in range(2):
            bias = jnp.logical_and(lane >= _spare(hh), lane < _spare(hh) + BIAS_LANES)
            qa.append(jnp.where(_own_lanes((tq, LANES), hh), q2, jnp.where(bias, 1.0, 0.0).astype(BF16)))
            own_k.append(_own_lanes((tb, LANES), hh))
            one_k.append(jnp.where(lane_k == _spare(hh), 1.0, 0.0).astype(BF16))

        def scores(j, scr):
            off = pl.multiple_of(j * tb, tb)
            kb, cb = k_ref[0, pl.ds(off, tb), :], cp_ref[0, pl.ds(off, tb), :]
            for hh in range(2):
                scr[hh] = lax.dot_general(jnp.where(own_k[hh], kb, cb), qa[hh], NT_DIMS, preferred_element_type=F32)

        def step(j, carry, first, scr=None):
            off = pl.multiple_of(j * tb, tb)
            kb, vb, cb = k_ref[0, pl.ds(off, tb), :], v_ref[0, pl.ds(off, tb), :], cp_ref[0, pl.ds(off, tb), :]
            lo = 0 if first is None else first
            out = []
            for hh in range(2):
                m_all, acc_all = carry[2 * hh], carry[2 * hh + 1]
                m_old, acc = m_all[:, lo:], acc_all[:, lo:]
                if scr is None:
                    st = lax.dot_general(jnp.where(own_k[hh], kb, cb), qa[hh][lo:], NT_DIMS,
                                         preferred_element_type=F32)
                else:
                    st = scr[hh]
                if first is not None:
                    key = lax.broadcasted_iota(jnp.int32, (tb, tq - lo), 0)
                    qry = lax.broadcasted_iota(jnp.int32, (tb, tq - lo), 1)
                    st = jnp.where(key <= qry, st, -jnp.inf)
                m = jnp.maximum(m_old, jnp.max(st, axis=0, keepdims=True))
                p = jnp.exp(st - m)
                acc = jnp.exp(m_old - m) * acc + lax.dot_general(
                    jnp.where(own_k[hh], vb, one_k[hh]), p.astype(BF16), TN_DIMS, preferred_element_type=F32)
                if lo:
                    m = jnp.concatenate([m_all[:, :lo], m], axis=1)
                    acc = jnp.concatenate([acc_all[:, :lo], acc], axis=1)
                out += [m, acc]
            return tuple(out)

        ninf = jnp.full((1, tq), -jnp.inf, F32)
        zacc = jnp.zeros((LANES, tq), F32)
        def pair(jj, c):
            j = 2 * jj
            scores(j + 1, st_b)
            c = step(j, c, None, st_a)
            scores(j + 2, st_a)
            return step(j + 1, c, None, st_b)

        scores(0, st_a)
        carry = lax.fori_loop(0, (i * per_q) // 2, pair, (ninf, zacc, ninf, zacc))
        carry = step(i * per_q, carry, 0, st_a)
        for t in range(1, per_q):
            carry = step(i * per_q + t, carry, t * tb)
        outs = []
        for hh in range(2):
            m, acc = carry[2 * hh], carry[2 * hh + 1]
            l = jnp.sum(jnp.where(feat == _spare(hh), acc, 0.0), axis=0, keepdims=True)
            lse_ref[0, 0, hh:hh + 1, :] = m + jnp.log(l)
            outs.append(acc * (1.0 / l))
        o_ref[0] = jnp.where(feat < HEAD_DIM, outs[0], outs[1]).astype(BF16)

    res = lambda base: pl.BlockSpec((1, T, LANES), lambda h, i: (base + h, 0, 0), pipeline_mode=pl.Buffered(1))
    return pl.pallas_call(
        body,
        name=name,
        grid=(N_PAIR, T // tq),
        in_specs=[pl.BlockSpec((1, tq, LANES), lambda h, i: (h, i, 0)), res(N_PAIR), res(2 * N_PAIR), res(0)],
        out_specs=[pl.BlockSpec((1, LANES, tq), lambda h, i: (h, 0, i)),
                   pl.BlockSpec((1, 1, 2, tq), lambda h, i: (h, i, 0, 0))],
        out_shape=[jax.ShapeDtypeStruct((N_PAIR, LANES, T), BF16),
                   jax.ShapeDtypeStruct((N_PAIR, T // tq, 2, tq), F32)],
        scratch_shapes=[pltpu.VMEM((2, tb, tq), F32), pltpu.VMEM((2, tb, tq), F32)],
        compiler_params=_params(("parallel", "arbitrary"), 56),
    )(qkv3, qkv3, qkv3, cp3)


def _attn_bwd(qkv3, cp3, qst4, kst4, dot4, st4, name):
    T = qkv3.shape[1]
    tb = min(ATT_BLOCK, T)
    nb = T // tb

    def body(qst_ref, dot_ref, st_ref, k_ref, v_ref, cp_ref, kst_ref, dq_hbm, rs_ref, dk_ref, dv_ref, cs_ref,
             dq_acc, dk_acc, dv_acc):
        h, j = pl.program_id(0), pl.program_id(1)
        own = [_own_lanes((tb, LANES), hh) for hh in range(2)]
        own_r = [_own_lanes((LANES, tb), hh, axis=0) for hh in range(2)]
        feat = lax.broadcasted_iota(jnp.int32, (LANES, tb), 0)
        key = lax.broadcasted_iota(jnp.int32, (tb, tb), 0)
        qry = lax.broadcasted_iota(jnp.int32, (tb, tb), 1)

        @pl.when(j == 0)
        def _():
            dq_acc[...] = jnp.zeros(dq_acc.shape, F32)
            rs_ref[...] = jnp.zeros(rs_ref.shape, F32)

        vb = v_ref[0]
        kb = [jnp.where(own[hh], k_ref[0], cp_ref[0]) for hh in range(2)]
        vm = [jnp.where(own[hh], vb, jnp.zeros_like(vb)) for hh in range(2)]
        kst = kst_ref[0, 0]
        zero_r = jnp.zeros_like(kst)
        one_r = jnp.ones_like(kst)
        kst = [jnp.where(own_r[hh], kst, zero_r) for hh in range(2)]
        bias_r = [jnp.where(jnp.logical_and(feat >= _spare(hh), feat < _spare(hh) + BIAS_LANES), one_r, zero_r)
                  for hh in range(2)]
        dk_acc[...] = jnp.zeros(dk_acc.shape, F32)
        dv_acc[...] = jnp.zeros(dv_acc.shape, F32)

        def step(i, masked):
            qt = qst_ref[0, i]
            dot = dot_ref[0, i]
            dq = None
            for hh in range(2):
                st = jnp.dot(kb[hh], jnp.where(own_r[hh], qt, bias_r[hh]), preferred_element_type=F32)
                if masked:
                    st = jnp.where(key <= qry, st, -jnp.inf)
                p = jnp.exp(st - st_ref[0, i, hh:hh + 1, :])
                dp = jnp.dot(vm[hh], dot, preferred_element_type=F32)
                dsb = (p * (dp - st_ref[0, i, 2 + hh:3 + hh, :])).astype(BF16)
                dv_acc[hh] += lax.dot_general(jnp.where(own_r[hh], dot, zero_r), p.astype(BF16), NT_DIMS,
                                              preferred_element_type=F32)
                dk_acc[hh] += lax.dot_general(jnp.where(own_r[hh], qt, one_r), dsb, NT_DIMS,
                                              preferred_element_type=F32)
                rs_ref[0, i, hh:hh + 1, :] += jnp.sum(dsb.astype(F32), axis=0, keepdims=True)
                d = jnp.dot(kst[hh], dsb, preferred_element_type=F32)
                dq = d if dq is None else dq + d
            dq_acc[i] += dq

        step(j, True)

        def loop_body(i, carry):
            step(i, False)
            return carry

        lax.fori_loop(j + 1, nb, loop_body, 0)
        dk_ref[0, 0] = jnp.where(own_r[0], dk_acc[0], dk_acc[1]).astype(BF16)
        dv_ref[0, 0] = (dv_acc[0] + dv_acc[1]).astype(BF16)
        cs_ref[0, 0, 0:8, :] = dk_acc[0, HEAD_DIM:HEAD_DIM + 8, :]
        cs_ref[0, 0, 8:16, :] = dk_acc[1, 0:8, :]

        @pl.when(j == nb - 1)
        def _():
            pltpu.sync_copy(dq_acc, dq_hbm.at[h])

    res = pl.BlockSpec((1, nb, LANES, tb), lambda h, j: (h, 0, 0, 0), pipeline_mode=pl.Buffered(1))
    tile = lambda base: pl.BlockSpec((1, tb, LANES), lambda h, j: (base + h, j, 0))
    ttile = pl.BlockSpec((1, 1, LANES, tb), lambda h, j: (h, j, 0, 0))
    rows = lambda n: pl.BlockSpec((1, nb, n, tb), lambda h, j: (h, 0, 0, 0))
    return pl.pallas_call(
        body,
        name=name,
        grid=(N_PAIR, nb),
        in_specs=[res, res, rows(4), tile(N_PAIR), tile(2 * N_PAIR), tile(0), ttile],
        out_specs=[pl.BlockSpec(memory_space=pl.ANY), rows(2), ttile, ttile,
                   pl.BlockSpec((1, 1, 16, tb), lambda h, j: (h, j, 0, 0))],
        out_shape=[
            jax.ShapeDtypeStruct((N_PAIR, nb, LANES, tb), F32),
            jax.ShapeDtypeStruct((N_PAIR, nb, 2, tb), F32),
            jax.ShapeDtypeStruct((N_PAIR, nb, LANES, tb), BF16),
            jax.ShapeDtypeStruct((N_PAIR, nb, LANES, tb), BF16),
            jax.ShapeDtypeStruct((N_PAIR, nb, 16, tb), F32),
        ],
        scratch_shapes=[pltpu.VMEM((nb, LANES, tb), F32), pltpu.VMEM((2, LANES, tb), F32),
                        pltpu.VMEM((2, LANES, tb), F32)],
        compiler_params=_params(("arbitrary", "arbitrary"), 56),
    )(qst4, dot4, st4, qkv3, qkv3, cp3, kst4)


def _attn_out(o3, w, x, name):
    T = x.shape[0]

    def body(o_ref, w_ref, x_ref, out_ref):
        out_ref[...] = x_ref[...] + jnp.dot(_cat_groups(o_ref, N_PAIR), w_ref[...], preferred_element_type=F32)

    return _rowcall(name, body, T, min(512, T), [(o3, "grp"), _wkind(w), (x, "row")],
                    [((T, D_MODEL), F32, "row")], 24)[0]


def _attn_dout(dx, wt, o3, name):
    T = dx.shape[0]
    tm = min(512, T)

    def body(dx_ref, w_ref, o_ref, do_ref, dd_ref):
        do = jnp.dot(dx_ref[...].astype(BF16), w_ref[...], preferred_element_type=F32).astype(BF16)
        lo = _own_lanes((tm, LANES), 0)
        head = lax.broadcasted_iota(jnp.int32, (tm, 2 * N_PAIR), 1)
        dd = jnp.zeros((tm, 2 * N_PAIR), F32)
        for t in range(N_PAIR):
            d = do[:, LANES * t:LANES * (t + 1)]
            do_ref[t] = d
            prod = d.astype(F32) * o_ref[t].astype(F32)
            d0 = jnp.sum(jnp.where(lo, prod, 0.0), axis=1, keepdims=True)
            d1 = jnp.sum(jnp.where(lo, 0.0, prod), axis=1, keepdims=True)
            dd = jnp.where(head == 2 * t, d0, jnp.where(head == 2 * t + 1, d1, dd))
        dd_ref[...] = dd

    return _rowcall(name, body, T, tm, [(dx, "row"), _wkind(wt), (o3, "grp")],
                    [((N_PAIR, T, LANES), BF16, "grp"), ((T, 2 * N_PAIR), F32, "row")], 32)


def _ffn_out(gu, w, x, name):
    T = x.shape[0]
    tm = min(512, T)

    def body(gu_ref, w_ref, x_ref, out_ref, hid_ref):
        acc = x_ref[...]
        for j in range(N_FFN_TILE):
            g = gu_ref[:, 2 * FFN_TILE * j:2 * FFN_TILE * j + FFN_TILE].astype(F32)
            u = gu_ref[:, 2 * FFN_TILE * j + FFN_TILE:2 * FFN_TILE * (j + 1)].astype(F32)
            hj = (g * jax.nn.sigmoid(g) * u).astype(BF16)
            hid_ref[:, FFN_TILE * j:FFN_TILE * (j + 1)] = hj
            acc = acc + jnp.dot(hj, w_ref[FFN_TILE * j:FFN_TILE * (j + 1), :], preferred_element_type=F32)
        out_ref[...] = acc

    return _rowcall(name, body, T, tm, [(gu, "row"), _wkind(w), (x, "row")],
                    [((T, D_MODEL), F32, "row"), ((T, FFN_H), BF16, "row")], 48)


def _ffn_dgu(dx, wt, gu, name):
    T = dx.shape[0]
    tm = min(512, T)

    def body(dx_ref, w_ref, gu_ref, dgu_ref):
        dxb = dx_ref[...].astype(BF16)
        for j in range(N_FFN_TILE):
            dh = jnp.dot(dxb, w_ref[:, FFN_TILE * j:FFN_TILE * (j + 1)], preferred_element_type=F32)
            g = gu_ref[:, 2 * FFN_TILE * j:2 * FFN_TILE * j + FFN_TILE].astype(F32)
            u = gu_ref[:, 2 * FFN_TILE * j + FFN_TILE:2 * FFN_TILE * (j + 1)].astype(F32)
            sg = jax.nn.sigmoid(g)
            dgu_ref[:, 2 * FFN_TILE * j:2 * FFN_TILE * j + FFN_TILE] = (
                dh * u * (sg * (1.0 + g * (1.0 - sg)))).astype(BF16)
            dgu_ref[:, 2 * FFN_TILE * j + FFN_TILE:2 * FFN_TILE * (j + 1)] = (dh * (g * sg)).astype(BF16)

    return _rowcall(name, body, T, tm, [(dx, "row"), _wkind(wt), (gu, "row")],
                    [((T, 2 * FFN_H), BF16, "row")], 48)[0]


def _sgu_core(a, ln_g, ln_b, w_s, bst, w, x, name):
    T = x.shape[0]
    tm = min(256, T)

    def body(a_ref, lg_ref, lb_ref, ws_ref, bs_ref, w_ref, x_ref, out_ref, gated_ref, vn_ref, mixed_ref):
        v = _gelu(a_ref[:, SGU_W:].astype(F32))
        mu = jnp.mean(v, axis=-1, keepdims=True)
        vc = v - mu
        rstd = lax.rsqrt(jnp.mean(vc * vc, axis=-1, keepdims=True) + LN_EPS)
        vn_ref[...] = (vc * rstd * lg_ref[...] + lb_ref[...]).astype(BF16)
        tt = lax.broadcasted_iota(jnp.int32, (CHUNK, CHUNK), 0)
        ss = lax.broadcasted_iota(jnp.int32, (CHUNK, CHUNK), 1)
        for g in range(SGU_G):
            wg = jnp.where(tt >= ss, ws_ref[g], 0.0).astype(BF16)
            bcol = _lane_col(bs_ref[...], g)
            cols = slice(CHUNK * g, CHUNK * (g + 1))
            for c in range(tm // CHUNK):
                rows = slice(CHUNK * c, CHUNK * (c + 1))
                mixed = jnp.dot(wg, vn_ref[rows, cols], preferred_element_type=F32) + bcol
                u = _gelu(a_ref[rows, cols].astype(F32))
                mixed_ref[rows, cols] = mixed.astype(BF16)
                gated_ref[rows, cols] = (u * mixed).astype(BF16)
        out_ref[...] = x_ref[...] + jnp.dot(gated_ref[...], w_ref[...], preferred_element_type=F32)

    ins = [(a, "row"), (ln_g, "full"), (ln_b, "full"), (w_s, "full"), (bst, "full"), _wkind(w), (x, "row")]
    outs = [((T, D_MODEL), F32, "row")] + [((T, SGU_W), BF16, "row")] * 3
    return _rowcall(name, body, T, tm, ins, outs, 40)


def _sgu_core_bwd(dx, wt, a, vn, mixed, ln_g, w_s, name):
    T = dx.shape[0]
    tm = min(256, T)

    def body(dx_ref, wt_ref, a_ref, vn_ref, mx_ref, lg_ref, ws_ref,
             da_ref, dws_ref, dba_ref, dlg_ref, dlb_ref, dg_scr, dvn_scr):
        @pl.when(pl.program_id(0) == 0)
        def _():
            dws_ref[...] = jnp.zeros(dws_ref.shape, F32)
            dba_ref[...] = jnp.zeros(dba_ref.shape, F32)
            dlg_ref[...] = jnp.zeros(dlg_ref.shape, F32)
            dlb_ref[...] = jnp.zeros(dlb_ref.shape, F32)

        dg_scr[...] = jnp.dot(dx_ref[...].astype(BF16), wt_ref[...], preferred_element_type=F32)
        tt = lax.broadcasted_iota(jnp.int32, (CHUNK, CHUNK), 0)
        ss = lax.broadcasted_iota(jnp.int32, (CHUNK, CHUNK), 1)
        tril = tt >= ss
        for g in range(SGU_G):
            wg = jnp.where(tril, ws_ref[g], 0.0).astype(BF16)
            cols = slice(CHUNK * g, CHUNK * (g + 1))
            for c in range(tm // CHUNK):
                rows = slice(CHUNK * c, CHUNK * (c + 1))
                dgb = dg_scr[rows, cols]
                au = a_ref[rows, cols].astype(F32)
                dmx = dgb * _gelu(au)
                da_ref[rows, cols] = (dgb * mx_ref[rows, cols].astype(F32) * _gelu_grad(au)).astype(BF16)
                dmb = dmx.astype(BF16)
                dvn_scr[rows, cols] = lax.dot_general(wg, dmb, TN_DIMS, preferred_element_type=F32)
                dws_ref[g] += jnp.where(
                    tril, lax.dot_general(dmb, vn_ref[rows, cols], NT_DIMS, preferred_element_type=F32), 0.0)
                dba_ref[:, cols] += dmx
        av = a_ref[:, SGU_W:].astype(F32)
        v = _gelu(av)
        mu = jnp.mean(v, axis=-1, keepdims=True)
        vc = v - mu
        rstd = lax.rsqrt(jnp.mean(vc * vc, axis=-1, keepdims=True) + LN_EPS)
        xhat = vc * rstd
        dvn = dvn_scr[...]
        dlg_ref[...] += jnp.sum(dvn * xhat, axis=0, keepdims=True)
        dlb_ref[...] += jnp.sum(dvn, axis=0, keepdims=True)
        dxh = dvn * lg_ref[...]
        dv = rstd * (dxh - jnp.mean(dxh, axis=-1, keepdims=True)
                     - xhat * jnp.mean(dxh * xhat, axis=-1, keepdims=True))
        da_ref[:, SGU_W:] = (dv * _gelu_grad(av)).astype(BF16)

    ins = [(dx, "row"), _wkind(wt), (a, "row"), (vn, "row"), (mixed, "row"), (ln_g, "full"), (w_s, "full")]
    outs = [((T, 2 * SGU_W), BF16, "row"), ((SGU_G, CHUNK, CHUNK), F32, "full"), ((CHUNK, SGU_W), F32, "full"),
            ((1, SGU_W), F32, "full"), ((1, SGU_W), F32, "full")]
    return _rowcall(name, body, T, tm, ins, outs, 40,
                    scratch=[pltpu.VMEM((tm, SGU_W), F32), pltpu.VMEM((tm, SGU_W), F32)])


def _loss_head(x, wf, tgt, name):
    T = x.shape[0]
    tm = min(512, T)

    def body(x_ref, wf_ref, tgt_ref, dx_ref, dwf_ref, loss_ref):
        @pl.when(pl.program_id(0) == 0)
        def _():
            dwf_ref[...] = jnp.zeros(dwf_ref.shape, F32)
            loss_ref[...] = jnp.zeros(loss_ref.shape, F32)

        xv = x_ref[...]
        r = lax.rsqrt(jnp.mean(xv * xv, axis=-1, keepdims=True) + NORM_EPS)
        xn = xv * r
        err = xn * wf_ref[...] - tgt_ref[...]
        loss_ref[...] += 0.5 * jnp.sum(jnp.mean(err * err, axis=-1, keepdims=True), axis=0, keepdims=True)
        dy = err * (1.0 / D_MODEL)
        dwf_ref[...] += jnp.sum(dy * xn, axis=0, keepdims=True)
        dyw = dy * wf_ref[...]
        dx_ref[...] = r * (dyw - xn * jnp.mean(dyw * xn, axis=-1, keepdims=True))

    return _rowcall(name, body, T, tm, [(x, "row"), (wf, "full"), (tgt, "row")],
                    [((T, D_MODEL), F32, "row"), ((1, D_MODEL), F32, "full"), ((8, LANES), F32, "full")], 32)


def _peers():
    x, y, c = lax.axis_index("x"), lax.axis_index("y"), lax.axis_index("c")
    peers = []
    for p in range(1, N_DEV):
        px = 1 - x if p & 4 else x
        py = 1 - y if p & 2 else y
        pc = 1 - c if p & 1 else c
        peers.append((4 * px + 2 * py + pc, (px, py, pc)))
    return 4 * x + 2 * y + c, peers


def _all_gather(arrs, name):
    n = len(arrs)
    hbm = pl.BlockSpec(memory_space=pl.ANY)

    def body(*refs):
        ins, outs = refs[:n], refs[n:2 * n]
        send_sems, recv_sems, local_sems = refs[2 * n:]
        me, peers = _peers()
        sends, recvs, locals_ = [], [], []
        for t in range(n):
            cp = pltpu.make_async_copy(ins[t], outs[t].at[me], local_sems.at[t])
            cp.start()
            locals_.append(cp)
            for k, (pidx, pid) in enumerate(peers):
                s = t * (N_DEV - 1) + k
                send = pltpu.make_async_remote_copy(
                    src_ref=ins[t], dst_ref=outs[t].at[me], send_sem=send_sems.at[s], recv_sem=recv_sems.at[s],
                    device_id=pid, device_id_type=MESH)
                send.start()
                sends.append(send)
                recvs.append(pltpu.make_async_remote_copy(
                    src_ref=ins[t], dst_ref=outs[t].at[pidx], send_sem=send_sems.at[s], recv_sem=recv_sems.at[s],
                    device_id=pid, device_id_type=MESH))
        for r in recvs:
            r.wait_recv()
        for s in sends:
            s.wait_send()
        for cp in locals_:
            cp.wait()

    return pl.pallas_call(
        body,
        name=name,
        in_specs=[hbm] * n,
        out_specs=[hbm] * n,
        out_shape=[jax.ShapeDtypeStruct((N_DEV,) + a.shape, a.dtype) for a in arrs],
        scratch_shapes=[pltpu.SemaphoreType.DMA((n * (N_DEV - 1),)), pltpu.SemaphoreType.DMA((n * (N_DEV - 1),)),
                        pltpu.SemaphoreType.DMA((n,))],
    )(*arrs)


def _exchange(arrs, name):
    n = len(arrs)
    hbm = pl.BlockSpec(memory_space=pl.ANY)

    def body(*refs):
        ins, outs = refs[:n], refs[n:2 * n]
        send_sems, recv_sems, local_sems = refs[2 * n:]
        me, peers = _peers()
        sends, recvs, locals_ = [], [], []
        for t in range(n):
            cp = pltpu.make_async_copy(ins[t].at[me], outs[t].at[me], local_sems.at[t])
            cp.start()
            locals_.append(cp)
            for k, (pidx, pid) in enumerate(peers):
                s = t * (N_DEV - 1) + k
                send = pltpu.make_async_remote_copy(
                    src_ref=ins[t].at[pidx], dst_ref=outs[t].at[me], send_sem=send_sems.at[s],
                    recv_sem=recv_sems.at[s], device_id=pid, device_id_type=MESH)
                send.start()
                sends.append(send)
                recvs.append(pltpu.make_async_remote_copy(
                    src_ref=ins[t].at[pidx], dst_ref=outs[t].at[pidx], send_sem=send_sems.at[s],
                    recv_sem=recv_sems.at[s], device_id=pid, device_id_type=MESH))
        for r in recvs:
            r.wait_recv()
        for s in sends:
            s.wait_send()
        for cp in locals_:
            cp.wait()

    return pl.pallas_call(
        body,
        name=name,
        in_specs=[hbm] * n,
        out_specs=[hbm] * n,
        out_shape=[jax.ShapeDtypeStruct(a.shape, a.dtype) for a in arrs],
        scratch_shapes=[pltpu.SemaphoreType.DMA((n * (N_DEV - 1),)), pltpu.SemaphoreType.DMA((n * (N_DEV - 1),)),
                        pltpu.SemaphoreType.DMA((n,))],
    )(*arrs)


def _row_tile(rows, cap):
    best = None
    for t in range(16, cap + 1, 16):
        if rows % t == 0:
            best = t
    assert best is not None, rows
    return best


def _adam_sum(parts, w, m, v, name):
    R, C = w.shape
    tr = _row_tile(R, 128)

    def body(p_ref, w_ref, m_ref, v_ref, g_ref, d_ref, nm_ref, nv_ref):
        g = p_ref[0].astype(F32)
        for s in range(1, N_DEV):
            g = g + p_ref[s].astype(F32)
        mm = ADAM_B1 * m_ref[...] + (1.0 - ADAM_B1) * g
        vv = ADAM_B2 * v_ref[...] + (1.0 - ADAM_B2) * (g * g)
        m_hat = mm / (1.0 - ADAM_B1 ** ADAM_STEP)
        v_hat = vv / (1.0 - ADAM_B2 ** ADAM_STEP)
        g_ref[...] = g
        d_ref[...] = -ADAM_LR * (m_hat / (jnp.sqrt(v_hat) + ADAM_EPS) + ADAM_WD * w_ref[...])
        nm_ref[...] = mm
        nv_ref[...] = vv

    mat = pl.BlockSpec((tr, C), lambda i: (i, 0))
    return pl.pallas_call(
        body,
        name=name,
        grid=(R // tr,),
        in_specs=[pl.BlockSpec((N_DEV, tr, C), lambda i: (0, i, 0)), mat, mat, mat],
        out_specs=[mat] * 4,
        out_shape=[jax.ShapeDtypeStruct((R, C), F32)] * 4,
        compiler_params=_params(("parallel",), 32),
    )(parts, w, m, v)


def _cols_from_gathered(g):
    _, L, K, n = g.shape
    return jnp.transpose(g, (1, 2, 0, 3)).reshape(L, K, N_DEV * n)


def _rows_from_gathered(g):
    _, L, k, N = g.shape
    return jnp.transpose(g, (1, 0, 2, 3)).reshape(L, N_DEV * k, N)


def _cols_to_blocks(dw):
    L, K, N = dw.shape
    n = N // N_DEV
    return jnp.transpose(dw.reshape(L, K, N_DEV, n), (2, 0, 1, 3)).reshape(N_DEV, L * K, n)


def _rows_to_blocks(dw):
    L, K, N = dw.shape
    k = K // N_DEV
    return jnp.transpose(dw.reshape(L, N_DEV, k, N), (1, 0, 2, 3)).reshape(N_DEV, L * k, N)


def _ffn_interleave(w):
    lead = w.shape[:-1]
    t = w.reshape(lead + (2, N_FFN_TILE, FFN_TILE))
    return jnp.swapaxes(t, -3, -2).reshape(lead + (2 * FFN_H,))


def _ffn_deinterleave(w):
    lead = w.shape[:-1]
    t = w.reshape(lead + (N_FFN_TILE, 2, FFN_TILE))
    return jnp.swapaxes(t, -3, -2).reshape(lead + (2 * FFN_H,))


def _pad_rows(a, rows=8):
    a = a.reshape(-1, a.shape[-1])
    return jnp.pad(a, ((0, rows - a.shape[0]), (0, 0)))


SMALL_ROWS = 6 * 8 + 2 * SGU_G * CHUNK * CHUNK // D_MODEL


def _pack_small(mixer, ffn, final, b_f, extra, w_s, b_s):
    bf_row = jnp.pad(b_f.reshape(1, -1), ((0, 0), (0, D_MODEL - b_f.size)))
    bs_rows = jnp.pad(b_s.reshape(4, -1), ((0, 0), (0, D_MODEL - b_s.size // 4)))
    return jnp.concatenate([
        _pad_rows(mixer), _pad_rows(ffn), _pad_rows(final.reshape(1, -1)), _pad_rows(bf_row),
        _pad_rows(extra), _pad_rows(bs_rows), w_s.reshape(-1, D_MODEL)], axis=0)


def _unpack_small(p):
    mixer, ffn, final = p[0:4], p[8:12], p[16]
    b_f = p[24, :32].reshape(2, 2 * N_PAIR)
    extra = p[32]
    b_s = p[40:44, :2 * SGU_G * CHUNK // 4].reshape(2, SGU_G, CHUNK)
    w_s = p[48:].reshape(2, SGU_G, CHUNK, CHUNK)
    return mixer, ffn, final, b_f, extra, w_s, b_s


def kernel(x, mixer_norm_w, attn_w_in, attn_b_f, attn_w_out, sgu_w_in, sgu_ln_g, sgu_ln_b, sgu_w_s, sgu_b_s, sgu_w_out, ffn_norm_w, ffn_w_in, ffn_w_out, final_norm_w, loss_target, m_mixer_norm_w, m_attn_w_in, m_attn_b_f, m_attn_w_out, m_sgu_w_in, m_sgu_ln_g, m_sgu_ln_b, m_sgu_w_s, m_sgu_b_s, m_sgu_w_out, m_ffn_norm_w, m_ffn_w_in, m_ffn_w_out, m_final_norm_w, v_mixer_norm_w, v_attn_w_in, v_attn_b_f, v_attn_w_out, v_sgu_w_in, v_sgu_ln_g, v_sgu_ln_b, v_sgu_w_s, v_sgu_b_s, v_sgu_w_out, v_ffn_norm_w, v_ffn_w_in, v_ffn_w_out, v_final_norm_w):
    T = x.shape[1]
    tb = min(ATT_BLOCK, T)
    xs, tgt = x[0], loss_target[0]

    shards = [attn_w_in, attn_w_out, sgu_w_in, sgu_w_out, ffn_w_in, ffn_w_out, sgu_ln_g, sgu_ln_b]
    gathered = _all_gather([s.astype(BF16) for s in shards[:6]] + shards[6:], "gather_weights")
    w_attn_in = _cols_from_gathered(gathered[0])
    w_attn_out = _rows_from_gathered(gathered[1])
    w_sgu_in = _cols_from_gathered(gathered[2])
    w_sgu_out = _rows_from_gathered(gathered[3])
    w_ffn_in = _ffn_interleave(_cols_from_gathered(gathered[4]))
    w_ffn_out = _rows_from_gathered(gathered[5])
    ln_g = jnp.transpose(gathered[6], (1, 0, 2)).reshape(2, 1, SGU_W)
    ln_b = jnp.transpose(gathered[7], (1, 0, 2)).reshape(2, 1, SGU_W)
    w_qkv = w_attn_in[:, :, :3 * D_MODEL]
    w_f = jnp.pad(w_attn_in[:, :, 3 * D_MODEL:], ((0, 0), (0, 0), (0, LANES - 2 * N_PAIR)))
    tr = lambda w: jnp.swapaxes(w, -1, -2)
    w_qkv_t6, w_f_t, w_attn_out_t = tr(w_qkv).reshape(6, D_MODEL, D_MODEL), tr(w_f), tr(w_attn_out)
    w_sgu_in_t, w_sgu_out_t, w_ffn_in_t, w_ffn_out_t = tr(w_sgu_in), tr(w_sgu_out), tr(w_ffn_in), tr(w_ffn_out)
    mixer_nw = mixer_norm_w.reshape(4, 1, D_MODEL)
    ffn_nw = ffn_norm_w.reshape(4, 1, D_MODEL)
    b_col = attn_b_f.reshape(2, 2 * N_PAIR, 1)
    bs_t = jnp.swapaxes(sgu_b_s, 1, 2)

    saved = []
    xr = xs
    for i in range(4):
        j = i // 2
        if i % 2 == 0:
            qkv3, h = _norm_matmul(xr, mixer_nw[i], (w_qkv, j), BF16, f"attn_qkv_{j}", 1024, groups=True)
            fl, _ = _norm_matmul(xr, mixer_nw[i], (w_f, j), F32, f"attn_gate_{j}", LANES)
            fl3 = jnp.transpose(fl[:, :2 * N_PAIR].reshape(T // CHUNK, CHUNK, 2 * N_PAIR), (0, 2, 1))
            c_chunks = _fgate_fwd(fl3, b_col[j], f"fgate_fwd_{j}")
            cp3 = _bias_pieces(jnp.transpose(c_chunks, (1, 0, 2)).reshape(2 * N_PAIR, T))
            ot3, lse4 = _attn_fwd(qkv3, cp3, f"attn_fwd_{j}")
            o3 = jnp.swapaxes(ot3, 1, 2)
            xm = _attn_out(o3, (w_attn_out, j), xr, f"attn_out_{j}")
            mix_saved = (xr, h, qkv3, fl3, cp3, o3, lse4)
        else:
            a, h = _norm_matmul(xr, mixer_nw[i], (w_sgu_in, j), BF16, f"sgu_in_{j}", 1024)
            xm, gated, vn, mixed = _sgu_core(a, ln_g[j], ln_b[j], sgu_w_s[j], bs_t[j], (w_sgu_out, j), xr,
                                             f"sgu_core_{j}")
            mix_saved = (xr, h, a, gated, vn, mixed)
        gu, h2 = _norm_matmul(xm, ffn_nw[i], (w_ffn_in, i), BF16, f"ffn_in_{i}", FFN_H // 2)
        xo, hid = _ffn_out(gu, (w_ffn_out, i), xm, f"ffn_out_{i}")
        saved.append((mix_saved, (xm, h2, gu, hid)))
        xr = xo
    dx, d_final, loss_part = _loss_head(xr, final_norm_w.reshape(1, D_MODEL), tgt, "loss_head")

    d_mixer_nw, d_ffn_nw = [None] * 4, [None] * 4
    d_attn_in, d_attn_out, d_bf, d_sgu_in, d_sgu_out = [None] * 2, [None] * 2, [None] * 2, [None] * 2, [None] * 2
    d_ws, d_bs, d_lg, d_lb = [None] * 2, [None] * 2, [None] * 2, [None] * 2
    d_ffn_in, d_ffn_out = [None] * 4, [None] * 4
    for i in reversed(range(4)):
        j = i // 2
        mix_saved, (xm, h2, gu, hid) = saved[i]
        dgu = _ffn_dgu(dx, (w_ffn_out_t, i), gu, f"ffn_dgu_{i}")
        d_ffn_out[i] = _matmul_tn(hid, dx, f"ffn_dwout_{i}", FFN_H // 2, D_MODEL)
        d_ffn_in[i] = _matmul_tn(h2, dgu, f"ffn_dwin_{i}", D_MODEL, FFN_H // 2)
        dx, d_ffn_nw[i] = _matmul_rms_bwd([(dgu, "row")], [(w_ffn_in_t, i)], xm, ffn_nw[i], dx, f"ffn_dx_{i}", 512, 48)
        if i % 2 == 0:
            xr, h, qkv3, fl3, cp3, o3, lse4 = mix_saved
            do3, dd = _attn_dout(dx, (w_attn_out_t, j), o3, f"attn_dout_{j}")
            d_attn_out[j] = _matmul_tn(o3, dx, f"attn_dwout_{j}", D_MODEL, D_MODEL, a_grp=True)
            to_blocks = lambda a: jnp.swapaxes(a.reshape(N_PAIR, 2, T // tb, tb), 1, 2)
            from_blocks = lambda a: jnp.swapaxes(a, 1, 2).reshape(N_PAIR, 2, T)
            dd4 = to_blocks(dd.T.reshape(N_PAIR, 2, T))
            st4 = jnp.concatenate([to_blocks(from_blocks(lse4)), dd4], axis=2)
            to_t = lambda a: jnp.swapaxes(a.reshape(N_PAIR, T // tb, tb, LANES), 2, 3)
            from_t = lambda a: jnp.swapaxes(a, 2, 3).reshape(N_PAIR, T, LANES).astype(BF16)
            qst4, kst4 = to_t(qkv3[:N_PAIR] * QK_SCALE), to_t(qkv3[N_PAIR:2 * N_PAIR] * QK_SCALE)
            dqt4, rs4, dkt4, dvt4, cs4 = _attn_bwd(qkv3, cp3, qst4, kst4, to_t(do3), st4, f"attn_bwd_{j}")
            dq3, dk3, dv3 = from_t(dqt4), from_t(dkt4), from_t(dvt4)
            dc_pair = from_blocks(rs4 - jnp.stack([cs4[:, :, 0], cs4[:, :, 8]], axis=2))
            dc_chunks = jnp.transpose(dc_pair.reshape(2 * N_PAIR, T // CHUNK, CHUNK), (1, 0, 2))
            dfl3, db = _fgate_bwd(dc_chunks, fl3, b_col[j], f"fgate_bwd_{j}")
            d_bf[j] = db[:, 0]
            dfl = jnp.transpose(dfl3, (0, 2, 1)).reshape(T, 2 * N_PAIR)
            dfl = jnp.pad(dfl.astype(BF16), ((0, 0), (0, LANES - 2 * N_PAIR)))
            d_qkv = [_matmul_tn(h, d3, f"attn_dw{nm}_{j}", D_MODEL, D_MODEL, g_grp=True)
                     for nm, d3 in (("q", dq3), ("k", dk3), ("v", dv3))]
            d_f = _matmul_tn(h, dfl, f"attn_dwf_{j}", D_MODEL, LANES)[:, :2 * N_PAIR]
            d_attn_in[j] = jnp.concatenate(d_qkv + [d_f], axis=1)
            wts = [(w_qkv_t6, 3 * j + k) for k in range(3)] + [(w_f_t, j)]
            dx, d_mixer_nw[i] = _matmul_rms_bwd(
                [(dq3, "grp"), (dk3, "grp"), (dv3, "grp"), (dfl, "row")], wts, xr, mixer_nw[i], dx,
                f"attn_dx_{j}", 256, 40)
        else:
            xr, h, a, gated, vn, mixed = mix_saved
            da, d_ws[j], dba, d_lg[j], d_lb[j] = _sgu_core_bwd(dx, (w_sgu_out_t, j), a, vn, mixed, ln_g[j], sgu_w_s[j],
                                                               f"sgu_core_bwd_{j}")
            d_bs[j] = jnp.sum(dba.reshape(CHUNK, SGU_G, CHUNK), axis=-1).T
            d_sgu_out[j] = _matmul_tn(gated, dx, f"sgu_dwout_{j}", D_MODEL, D_MODEL)
            d_sgu_in[j] = _matmul_tn(h, da, f"sgu_dwin_{j}", D_MODEL, 1024)
            dx, d_mixer_nw[i] = _matmul_rms_bwd([(da, "row")], [(w_sgu_in_t, j)], xr, mixer_nw[i], dx,
                                                f"sgu_dx_{j}", 256, 40)
    grad_x = dx[None]

    rows4 = lambda parts: jnp.concatenate(parts, axis=1).reshape(4, D_MODEL)
    small_g = _pack_small(rows4(d_mixer_nw), rows4(d_ffn_nw), d_final[0], jnp.stack(d_bf),
                          loss_part[0:1, 0:1] * jnp.ones((1, D_MODEL), F32), jnp.stack(d_ws), jnp.stack(d_bs))
    zero_row = jnp.zeros((1, D_MODEL), F32)
    pack = lambda pre: _pack_small(pre[0], pre[1], pre[2], pre[3], zero_row, pre[4], pre[5])
    small_w = pack((mixer_norm_w, ffn_norm_w, final_norm_w, attn_b_f, sgu_w_s, sgu_b_s))
    small_m = pack((m_mixer_norm_w, m_ffn_norm_w, m_final_norm_w, m_attn_b_f, m_sgu_w_s, m_sgu_b_s))
    small_v = pack((v_mixer_norm_w, v_ffn_norm_w, v_final_norm_w, v_attn_b_f, v_sgu_w_s, v_sgu_b_s))
    small_all = _all_gather([small_g], "gather_small_grads")[0]
    small_out = [_unpack_small(p) for p in _adam_sum(small_all, small_w, small_m, small_v, "adam_small")]
    loss = small_out[0][4][0]

    blocks = [
        _cols_to_blocks(jnp.stack(d_attn_in)), _rows_to_blocks(jnp.stack(d_attn_out)),
        _cols_to_blocks(jnp.stack(d_sgu_in)), _rows_to_blocks(jnp.stack(d_sgu_out)),
        _cols_to_blocks(_ffn_deinterleave(jnp.stack(d_ffn_in))), _rows_to_blocks(jnp.stack(d_ffn_out)),
        jnp.stack(d_lg).reshape(2, N_DEV, 1, SGU_W // N_DEV).transpose(1, 0, 2, 3).reshape(N_DEV, 2, SGU_W // N_DEV),
        jnp.stack(d_lb).reshape(2, N_DEV, 1, SGU_W // N_DEV).transpose(1, 0, 2, 3).reshape(N_DEV, 2, SGU_W // N_DEV),
    ]
    received = _exchange([b.astype(BF16) for b in blocks[:6]] + blocks[6:], "exchange_grads")
    names = ["attn_w_in", "attn_w_out", "sgu_w_in", "sgu_w_out", "ffn_w_in", "ffn_w_out"]
    ws = [attn_w_in, attn_w_out, sgu_w_in, sgu_w_out, ffn_w_in, ffn_w_out]
    ms = [m_attn_w_in, m_attn_w_out, m_sgu_w_in, m_sgu_w_out, m_ffn_w_in, m_ffn_w_out]
    vs = [v_attn_w_in, v_attn_w_out, v_sgu_w_in, v_sgu_w_out, v_ffn_w_in, v_ffn_w_out]
    big_out = {}
    for nm, rec, w, m, v in zip(names, received[:6], ws, ms, vs):
        flat = lambda a: a.reshape(-1, a.shape[-1])
        big_out[nm] = [o.reshape(w.shape) for o in _adam_sum(rec, flat(w), flat(m), flat(v), f"adam_{nm}")]
    pad8 = lambda a: jnp.pad(a, [(0, 0)] * (a.ndim - 2) + [(0, 8 - a.shape[-2]), (0, 0)])
    ln_parts = jnp.concatenate([pad8(received[6]), pad8(received[7])], axis=1)
    ln_pack = lambda g, b: jnp.concatenate([pad8(g), pad8(b)], axis=0)
    ln_out = _adam_sum(ln_parts, ln_pack(sgu_ln_g, sgu_ln_b), ln_pack(m_sgu_ln_g, m_sgu_ln_b),
                       ln_pack(v_sgu_ln_g, v_sgu_ln_b), "adam_sgu_ln")

    def leaf(kind):
        mixer, ffn, final, b_f, _, w_s, b_s = small_out[kind]
        o = lambda nm: big_out[nm][kind]
        return [mixer, o("attn_w_in"), b_f, o("attn_w_out"), o("sgu_w_in"), ln_out[kind][0:2], ln_out[kind][8:10],
                w_s, b_s, o("sgu_w_out"), ffn, o("ffn_w_in"), o("ffn_w_out"), final]

    return (loss, grad_x, *leaf(0), *leaf(1), *leaf(2), *leaf(3))
```

```python
import functools

import jax
import jax.numpy as jnp
from jax import lax
from jax.experimental import pallas as pl
from jax.experimental.pallas import tpu as pltpu

F32 = jnp.float32
BF16 = jnp.bfloat16

D_MODEL = 1024
HEAD_DIM = 64
N_PAIR = 8
LANES = 128
SGU_W = 2048
SGU_G = 16
CHUNK = 128
FFN_H = 2816
FFN_TILE = 256
N_FFN_TILE = FFN_H // FFN_TILE
NORM_EPS = 1e-6
LN_EPS = 1e-5
QK_SCALE = 0.125
ATT_BLOCK = 512
N_DEV = 8
ADAM_LR = 0.001
ADAM_B1 = 0.9
ADAM_B2 = 0.999
ADAM_EPS = 1e-08
ADAM_WD = 0.01
ADAM_STEP = 10
MESH = pl.DeviceIdType.MESH
SQRT_HALF = 0.7071067811865476
INV_SQRT_2PI = 0.3989422804014327

NT_DIMS = (((1,), (1,)), ((), ()))
TN_DIMS = (((0,), (0,)), ((), ()))


def _gelu(x):
    return 0.5 * x * (1.0 + lax.erf(x * SQRT_HALF))


def _gelu_grad(x):
    return 0.5 * (1.0 + lax.erf(x * SQRT_HALF)) + x * jnp.exp(-0.5 * x * x) * INV_SQRT_2PI


def _lane_col(v, lane):
    idx = lax.broadcasted_iota(jnp.int32, v.shape, 1)
    return jnp.sum(jnp.where(idx == lane, v, 0.0), axis=1, keepdims=True)


def _params(sem, vmem_mb):
    return pltpu.CompilerParams(dimension_semantics=sem, vmem_limit_bytes=vmem_mb << 20)


def _cat_groups(ref, n):
    if n == 1:
        return ref[0]
    return jnp.concatenate([ref[t] for t in range(n)], axis=1)


def _wkind(w):
    return (w[0], ("layer", w[1])) if isinstance(w, tuple) else (w, "full")


def _rowcall(name, body, T, tm, ins, outs, vmem_mb, scratch=()):
    def spec(shape, kind, resident_once=False):
        shape = tuple(shape)
        if isinstance(kind, tuple):
            layer = kind[1]
            return pl.BlockSpec((None,) + shape[1:], lambda i: (layer,) + (0,) * (len(shape) - 1),
                                pipeline_mode=pl.Buffered(1))
        if kind == "row":
            return pl.BlockSpec((tm,) + shape[1:], lambda i: (i,) + (0,) * (len(shape) - 1))
        if kind == "grp":
            return pl.BlockSpec((shape[0], tm, shape[2]), lambda i: (0, i, 0))
        if resident_once:
            return pl.BlockSpec(shape, lambda i: (0,) * len(shape), pipeline_mode=pl.Buffered(1))
        return pl.BlockSpec(shape, lambda i: (0,) * len(shape))

    return pl.pallas_call(
        body,
        name=name,
        grid=(T // tm,),
        in_specs=[spec(a.shape, k, True) for a, k in ins],
        out_specs=[spec(s, k) for s, _, k in outs],
        out_shape=[jax.ShapeDtypeStruct(tuple(s), d) for s, d, _ in outs],
        scratch_shapes=list(scratch),
        compiler_params=_params(("arbitrary",), vmem_mb),
    )(*[a for a, _ in ins])


def _norm_matmul(x, nw, w, out_dtype, name, tn, groups=False):
    w, layer = w
    T, N = x.shape[0], w.shape[2]
    tm = min(1024, T)

    def body(x_ref, nw_ref, w_ref, o_ref, h_ref, h_scr):
        @pl.when(pl.program_id(1) == 0)
        def _():
            xv = x_ref[...]
            r = lax.rsqrt(jnp.mean(xv * xv, axis=-1, keepdims=True) + NORM_EPS)
            hv = (xv * r * nw_ref[...]).astype(BF16)
            h_scr[...] = hv
            h_ref[...] = hv

        acc = jnp.dot(h_scr[...], w_ref[...], preferred_element_type=F32)
        if groups:
            for t in range(tn // LANES):
                o_ref[t] = acc[:, LANES * t:LANES * (t + 1)].astype(out_dtype)
        else:
            o_ref[...] = acc.astype(out_dtype)

    if groups:
        o_shape = (N // LANES, T, LANES)
        o_spec = pl.BlockSpec((tn // LANES, tm, LANES), lambda i, j: (j, i, 0))
    else:
        o_shape = (T, N)
        o_spec = pl.BlockSpec((tm, tn), lambda i, j: (i, j))
    return pl.pallas_call(
        body,
        name=name,
        grid=(T // tm, N // tn),
        in_specs=[
            pl.BlockSpec((tm, D_MODEL), lambda i, j: (i, 0)),
            pl.BlockSpec((1, D_MODEL), lambda i, j: (0, 0)),
            pl.BlockSpec((None, D_MODEL, tn), lambda i, j: (layer, 0, j)),
        ],
        out_specs=[o_spec, pl.BlockSpec((tm, D_MODEL), lambda i, j: (i, 0))],
        out_shape=[jax.ShapeDtypeStruct(o_shape, out_dtype), jax.ShapeDtypeStruct((T, D_MODEL), BF16)],
        scratch_shapes=[pltpu.VMEM((tm, D_MODEL), BF16)],
        compiler_params=_params(("arbitrary", "arbitrary"), 48),
    )(x, nw, w)


def _matmul_tn(a, g, name, tk, tn, a_grp=False, g_grp=False):
    T = a.shape[1] if a_grp else a.shape[0]
    K = a.shape[0] * LANES if a_grp else a.shape[1]
    N = g.shape[0] * LANES if g_grp else g.shape[1]
    tm = min(1024, T)

    def body(a_ref, g_ref, o_ref):
        @pl.when(pl.program_id(2) == 0)
        def _():
            o_ref[...] = jnp.zeros(o_ref.shape, F32)

        av = _cat_groups(a_ref, tk // LANES) if a_grp else a_ref[...]
        gv = _cat_groups(g_ref, tn // LANES) if g_grp else g_ref[...]
        o_ref[...] += lax.dot_general(av.astype(BF16), gv.astype(BF16), TN_DIMS, preferred_element_type=F32)

    if a_grp:
        a_spec = pl.BlockSpec((tk // LANES, tm, LANES), lambda k, n, m: (k, m, 0))
    else:
        a_spec = pl.BlockSpec((tm, tk), lambda k, n, m: (m, k))
    if g_grp:
        g_spec = pl.BlockSpec((tn // LANES, tm, LANES), lambda k, n, m: (n, m, 0))
    else:
        g_spec = pl.BlockSpec((tm, tn), lambda k, n, m: (m, n))
    return pl.pallas_call(
        body,
        name=name,
        grid=(K // tk, N // tn, T // tm),
        in_specs=[a_spec, g_spec],
        out_specs=pl.BlockSpec((tk, tn), lambda k, n, m: (k, n)),
        out_shape=jax.ShapeDtypeStruct((K, N), F32),
        compiler_params=_params(("parallel", "parallel", "arbitrary"), 48),
    )(a, g)


def _matmul_rms_bwd(a_list, wt_list, x, nw, dres, name, tm, vmem_mb):
    T = x.shape[0]
    n = len(a_list)

    def body(*refs):
        a_refs, w_refs = refs[:n], refs[n:2 * n]
        x_ref, nw_ref, dres_ref, dx_ref, dnw_ref = refs[2 * n:]

        @pl.when(pl.program_id(0) == 0)
        def _():
            dnw_ref[...] = jnp.zeros(dnw_ref.shape, F32)

        dh = None
        for (arr, kind), a_ref, w_ref in zip(a_list, a_refs, w_refs):
            av = _cat_groups(a_ref, arr.shape[0]) if kind == "grp" else a_ref[...]
            part = jnp.dot(av.astype(BF16), w_ref[...], preferred_element_type=F32)
            dh = part if dh is None else dh + part
        xv = x_ref[...]
        r = lax.rsqrt(jnp.mean(xv * xv, axis=-1, keepdims=True) + NORM_EPS)
        xn = xv * r
        dnw_ref[...] += jnp.sum(dh * xn, axis=0, keepdims=True)
        dyw = dh * nw_ref[...]
        dx_ref[...] = dres_ref[...] + r * (dyw - xn * jnp.mean(dyw * xn, axis=-1, keepdims=True))

    ins = list(a_list) + [_wkind(w) for w in wt_list] + [(x, "row"), (nw, "full"), (dres, "row")]
    outs = [((T, D_MODEL), F32, "row"), ((1, D_MODEL), F32, "full")]
    return _rowcall(name, body, T, tm, ins, outs, vmem_mb)


def _fgate_fwd(fl3, bcol, name):
    n_chunk = fl3.shape[0]

    def body(fl_ref, b_ref, c_ref):
        r = lax.broadcasted_iota(jnp.int32, (CHUNK, CHUNK), 0)
        t = lax.broadcasted_iota(jnp.int32, (CHUNK, CHUNK), 1)
        tri = jnp.where(r <= t, 1.0, 0.0).astype(BF16)

        def chunk(i, carry):
            z = fl_ref[i] + b_ref[...]
            lf = jnp.minimum(z, 0.0) - jnp.log(1.0 + jnp.exp(-jnp.abs(z)))
            hi = lf.astype(BF16)
            r1 = lf - hi.astype(F32)
            mid = r1.astype(BF16)
            low = (r1 - mid.astype(F32)).astype(BF16)
            cs = (jnp.dot(hi, tri, preferred_element_type=F32) + jnp.dot(mid, tri, preferred_element_type=F32)
                  + jnp.dot(low, tri, preferred_element_type=F32)) + carry
            c_ref[i] = cs
            return _lane_col(cs, CHUNK - 1)

        lax.fori_loop(0, n_chunk, chunk, jnp.zeros((2 * N_PAIR, 1), F32))

    return pl.pallas_call(
        body, name=name, out_shape=jax.ShapeDtypeStruct(fl3.shape, F32),
        compiler_params=pltpu.CompilerParams(vmem_limit_bytes=16 << 20),
    )(fl3, bcol)


def _fgate_bwd(dc3, fl3, bcol, name):
    n_chunk = fl3.shape[0]

    def body(dc_ref, fl_ref, b_ref, dfl_ref, db_ref):
        tt = lax.broadcasted_iota(jnp.int32, (CHUNK, CHUNK), 0)
        rr = lax.broadcasted_iota(jnp.int32, (CHUNK, CHUNK), 1)
        tri = jnp.where(tt >= rr, 1.0, 0.0).astype(BF16)

        def chunk(k, carry):
            tail, acc = carry
            i = n_chunk - 1 - k
            dc = dc_ref[i]
            hi = dc.astype(BF16)
            r1 = dc - hi.astype(F32)
            mid = r1.astype(BF16)
            low = (r1 - mid.astype(F32)).astype(BF16)
            dlf = (jnp.dot(hi, tri, preferred_element_type=F32) + jnp.dot(mid, tri, preferred_element_type=F32)
                   + jnp.dot(low, tri, preferred_element_type=F32)) + tail
            z = fl_ref[i] + b_ref[...]
            dfl = dlf / (1.0 + jnp.exp(z))
            dfl_ref[i] = dfl
            return _lane_col(dlf, 0), acc + dfl

        _, acc = lax.fori_loop(0, n_chunk, chunk,
                               (jnp.zeros((2 * N_PAIR, 1), F32), jnp.zeros((2 * N_PAIR, CHUNK), F32)))
        db_ref[...] = jnp.broadcast_to(jnp.sum(acc, axis=1, keepdims=True), db_ref.shape)

    return pl.pallas_call(
        body, name=name,
        out_shape=[jax.ShapeDtypeStruct(fl3.shape, F32), jax.ShapeDtypeStruct((2 * N_PAIR, LANES), F32)],
        compiler_params=pltpu.CompilerParams(vmem_limit_bytes=16 << 20),
    )(dc3, fl3, bcol)


BIAS_LANES = 3
ATT_Q_BLOCK = 2048


def _own_lanes(shape, hh, axis=1):
    idx = lax.broadcasted_iota(jnp.int32, shape, axis)
    return idx < HEAD_DIM if hh == 0 else idx >= HEAD_DIM


def _spare(hh):
    return HEAD_DIM * (1 - hh)


def _bias_pieces(c16):
    T = c16.shape[1]
    negc = -c16
    hi = negc.astype(BF16)
    r1 = negc - hi.astype(F32)
    mid = r1.astype(BF16)
    low = (r1 - mid.astype(F32)).astype(BF16)
    pieces = jnp.stack([hi, mid, low], axis=-1).reshape(N_PAIR, 2, T, BIAS_LANES)
    zpad = jnp.zeros((N_PAIR, T, HEAD_DIM - BIAS_LANES), BF16)
    return jnp.concatenate([pieces[:, 1], zpad, pieces[:, 0], zpad], axis=-1)


UNDERFLOW_BOUND = -110.0


def _attn_reach(qkv3, c16, tb, tq):
    T = qkv3.shape[1]
    nb = T // tb

    def block_norm(a):
        sq = jnp.sum(jnp.square(a.astype(F32)).reshape(N_PAIR, nb, tb, 2, HEAD_DIM), axis=-1)
        return jnp.transpose(jnp.sqrt(jnp.max(sq, axis=2)), (0, 2, 1))

    qn, kn = block_norm(qkv3[:N_PAIR]), block_norm(qkv3[N_PAIR:2 * N_PAIR])
    cb = c16.reshape(N_PAIR, 2, nb, tb)
    c_max, negc_max = jnp.max(cb, axis=-1), jnp.max(-cb, axis=-1)
    bound = (qn[:, :, :, None] * (kn[:, :, None, :] + kn[:, :, :, None]) * QK_SCALE
             + negc_max[:, :, None, :] + c_max[:, :, :, None])
    qi = lax.broadcasted_iota(jnp.int32, (nb, nb), 0)
    kj = lax.broadcasted_iota(jnp.int32, (nb, nb), 1)
    active = jnp.logical_or(bound > UNDERFLOW_BOUND, kj >= qi)
    last_q = jnp.max(jnp.where(active, qi, -1), axis=(1, 2))
    first_blk = jnp.min(jnp.where(active, kj, nb), axis=(1, 3))
    first_kv = jnp.min(first_blk.reshape(N_PAIR, T // tq, tq // tb), axis=-1)
    return first_kv.astype(F32), last_q.astype(F32)


def _attn_fwd(qkv3, cp3, first_kv, name):
    T = qkv3.shape[1]
    tb = min(ATT_BLOCK, T)
    tq = min(ATT_Q_BLOCK, T)
    nb = T // tb
    per_q = tq // tb

    assert per_q % 2 == 0 or T == tq, (T, tq, tb)

    def body(q_ref, k_ref, v_ref, cp_ref, first_ref, o_ref, lse_ref, st_a, st_b):
        h, i = pl.program_id(0), pl.program_id(1)
        lane = lax.broadcasted_iota(jnp.int32, (tq, LANES), 1)
        lane_k = lax.broadcasted_iota(jnp.int32, (tb, LANES), 1)
        feat = lax.broadcasted_iota(jnp.int32, (LANES, tq), 0)
        q2 = q_ref[0] * QK_SCALE
        qa, own_k, one_k = [], [], []
        for hh in range(2):
            bias = jnp.logical_and(lane >= _spare(hh), lane < _spare(hh) + BIAS_LANES)
            qa.append(jnp.where(_own_lanes((tq, LANES), hh), q2, jnp.where(bias, 1.0, 0.0).astype(BF16)))
            own_k.append(_own_lanes((tb, LANES), hh))
            one_k.append(jnp.where(lane_k == _spare(hh), 1.0, 0.0).astype(BF16))

        def scores(j, scr):
            off = pl.multiple_of(j * tb, tb)
            kb, cb = k_ref[0, pl.ds(off, tb), :], cp_ref[0, pl.ds(off, tb), :]
            for hh in range(2):
                scr[hh] = lax.dot_general(jnp.where(own_k[hh], kb, cb), qa[hh], NT_DIMS, preferred_element_type=F32)

        def step(j, carry, first, scr=None):
            off = pl.multiple_of(j * tb, tb)
            kb, vb, cb = k_ref[0, pl.ds(off, tb), :], v_ref[0, pl.ds(off, tb), :], cp_ref[0, pl.ds(off, tb), :]
            lo = 0 if first is None else first
            out = []
            for hh in range(2):
                m_all, acc_all = carry[2 * hh], carry[2 * hh + 1]
                m_old, acc = m_all[:, lo:], acc_all[:, lo:]
                if scr is None:
                    st = lax.dot_general(jnp.where(own_k[hh], kb, cb), qa[hh][lo:], NT_DIMS,
                                         preferred_element_type=F32)
                else:
                    st = scr[hh]
                if first is not None:
                    key = lax.broadcasted_iota(jnp.int32, (tb, tq - lo), 0)
                    qry = lax.broadcasted_iota(jnp.int32, (tb, tq - lo), 1)
                    st = jnp.where(key <= qry, st, -jnp.inf)
                m = jnp.maximum(m_old, jnp.max(st, axis=0, keepdims=True))
                p = jnp.exp(st - m)
                acc = jnp.exp(m_old - m) * acc + lax.dot_general(
                    jnp.where(own_k[hh], vb, one_k[hh]), p.astype(BF16), TN_DIMS, preferred_element_type=F32)
                if lo:
                    m = jnp.concatenate([m_all[:, :lo], m], axis=1)
                    acc = jnp.concatenate([acc_all[:, :lo], acc], axis=1)
                out += [m, acc]
            return tuple(out)

        ninf = jnp.full((1, tq), -jnp.inf, F32)
        zacc = jnp.zeros((LANES, tq), F32)
        def pair(jj, c):
            j = 2 * jj
            scores(j + 1, st_b)
            c = step(j, c, None, st_a)
            scores(j + 2, st_a)
            return step(j + 1, c, None, st_b)

        first_pair = jnp.clip(first_ref[h, i].astype(jnp.int32), 0, i * per_q) // 2
        scores(2 * first_pair, st_a)
        carry = lax.fori_loop(first_pair, (i * per_q) // 2, pair, (ninf, zacc, ninf, zacc))
        carry = step(i * per_q, carry, 0, st_a)
        for t in range(1, per_q):
            carry = step(i * per_q + t, carry, t * tb)
        outs = []
        for hh in range(2):
            m, acc = carry[2 * hh], carry[2 * hh + 1]
            l = jnp.sum(jnp.where(feat == _spare(hh), acc, 0.0), axis=0, keepdims=True)
            lse_ref[0, 0, hh:hh + 1, :] = m + jnp.log(l)
            outs.append(acc * (1.0 / l))
        o_ref[0] = jnp.where(feat < HEAD_DIM, outs[0], outs[1]).astype(BF16)

    res = lambda base: pl.BlockSpec((1, T, LANES), lambda h, i: (base + h, 0, 0), pipeline_mode=pl.Buffered(1))
    return pl.pallas_call(
        body,
        name=name,
        grid=(N_PAIR, T // tq),
        in_specs=[pl.BlockSpec((1, tq, LANES), lambda h, i: (h, i, 0)), res(N_PAIR), res(2 * N_PAIR), res(0),
                  pl.BlockSpec(memory_space=pltpu.SMEM)],
        out_specs=[pl.BlockSpec((1, LANES, tq), lambda h, i: (h, 0, i)),
                   pl.BlockSpec((1, 1, 2, tq), lambda h, i: (h, i, 0, 0))],
        out_shape=[jax.ShapeDtypeStruct((N_PAIR, LANES, T), BF16),
                   jax.ShapeDtypeStruct((N_PAIR, T // tq, 2, tq), F32)],
        scratch_shapes=[pltpu.VMEM((2, tb, tq), F32), pltpu.VMEM((2, tb, tq), F32)],
        compiler_params=_params(("parallel", "arbitrary"), 56),
    )(qkv3, qkv3, qkv3, cp3, first_kv)


def _attn_bwd(qkv3, cp3, kst4, do3, st4, last_q, name):
    T = qkv3.shape[1]
    tb = min(ATT_BLOCK, T)
    nb = T // tb

    def body(q_ref, do_ref, st_ref, k_ref, v_ref, cp_ref, kst_ref, last_ref, dq_hbm, rs_ref, dk_ref, dv_ref, cs_ref,
             dq_acc, dk_acc, dv_acc):
        h, j = pl.program_id(0), pl.program_id(1)
        lane = lax.broadcasted_iota(jnp.int32, (tb, LANES), 1)
        own = [_own_lanes((tb, LANES), hh) for hh in range(2)]
        bias = [jnp.logical_and(lane >= _spare(hh), lane < _spare(hh) + BIAS_LANES) for hh in range(2)]
        key = lax.broadcasted_iota(jnp.int32, (tb, tb), 0)
        qry = lax.broadcasted_iota(jnp.int32, (tb, tb), 1)

        @pl.when(j == 0)
        def _():
            dq_acc[...] = jnp.zeros(dq_acc.shape, F32)
            rs_ref[...] = jnp.zeros(rs_ref.shape, F32)

        vb = v_ref[0]
        zero = jnp.zeros_like(vb)
        one = jnp.ones_like(vb)
        bias_one = [jnp.where(bias[hh], one, zero) for hh in range(2)]
        kb = [jnp.where(own[hh], k_ref[0], cp_ref[0]) for hh in range(2)]
        kst = kst_ref[0, 0]
        kst = [jnp.where(_own_lanes((LANES, tb), hh, axis=0), kst, jnp.zeros_like(kst)) for hh in range(2)]
        vm = [jnp.where(own[hh], vb, zero) for hh in range(2)]
        dk_acc[...] = jnp.zeros(dk_acc.shape, F32)
        dv_acc[...] = jnp.zeros(dv_acc.shape, F32)

        def step(i, masked):
            off = pl.multiple_of(i * tb, tb)
            qb = q_ref[0, pl.ds(off, tb), :] * QK_SCALE
            dob = do_ref[0, pl.ds(off, tb), :]
            dq = None
            for hh in range(2):
                st = lax.dot_general(kb[hh], jnp.where(own[hh], qb, bias_one[hh]), NT_DIMS,
                                     preferred_element_type=F32)
                if masked:
                    st = jnp.where(key <= qry, st, -jnp.inf)
                p = jnp.exp(st - st_ref[0, i, hh:hh + 1, :])
                dp = lax.dot_general(vm[hh], dob, NT_DIMS, preferred_element_type=F32)
                dsb = (p * (dp - st_ref[0, i, 2 + hh:3 + hh, :])).astype(BF16)
                dv_acc[hh] += jnp.dot(p.astype(BF16), dob, preferred_element_type=F32)
                dk_acc[hh] += jnp.dot(dsb, jnp.where(own[hh], qb, one), preferred_element_type=F32)
                rs_ref[0, i, hh:hh + 1, :] += jnp.sum(dsb.astype(F32), axis=0, keepdims=True)
                d = jnp.dot(kst[hh], dsb, preferred_element_type=F32)
                dq = d if dq is None else dq + d
            dq_acc[i] += dq

        step(j, True)

        def loop_body(i, carry):
            step(i, False)
            return carry

        last = jnp.clip(last_ref[h, j].astype(jnp.int32), j, nb - 1)
        lax.fori_loop(j + 1, last + 1, loop_body, 0)
        dk_ref[0] = jnp.where(own[0], dk_acc[0], dk_acc[1]).astype(BF16)
        dv_ref[0] = jnp.where(own[0], dv_acc[0], dv_acc[1]).astype(BF16)
        lane8 = lax.broadcasted_iota(jnp.int32, (8, LANES), 1)
        for hh in range(2):
            pick = jnp.where(lane8 == _spare(hh), 1.0, 0.0).astype(BF16)
            x = dk_acc[hh]
            hi = x.astype(BF16)
            r1 = x - hi.astype(F32)
            mid = r1.astype(BF16)
            low = (r1 - mid.astype(F32)).astype(BF16)
            cs_ref[0, 0, 8 * hh:8 * hh + 8, :] = (
                lax.dot_general(pick, hi, NT_DIMS, preferred_element_type=F32)
                + lax.dot_general(pick, mid, NT_DIMS, preferred_element_type=F32)
                + lax.dot_general(pick, low, NT_DIMS, preferred_element_type=F32))

        @pl.when(j == nb - 1)
        def _():
            pltpu.sync_copy(dq_acc, dq_hbm.at[h])

    res = pl.BlockSpec((1, T, LANES), lambda h, j: (h, 0, 0), pipeline_mode=pl.Buffered(1))
    tile = lambda base: pl.BlockSpec((1, tb, LANES), lambda h, j: (base + h, j, 0))
    rows = lambda n: pl.BlockSpec((1, nb, n, tb), lambda h, j: (h, 0, 0, 0))
    return pl.pallas_call(
        body,
        name=name,
        grid=(N_PAIR, nb),
        in_specs=[res, res, rows(4), tile(N_PAIR), tile(2 * N_PAIR), tile(0),
                  pl.BlockSpec((1, 1, LANES, tb), lambda h, j: (h, j, 0, 0)),
                  pl.BlockSpec(memory_space=pltpu.SMEM)],
        out_specs=[pl.BlockSpec(memory_space=pl.ANY), rows(2), tile(0), tile(0),
                   pl.BlockSpec((1, 1, 16, tb), lambda h, j: (h, j, 0, 0))],
        out_shape=[
            jax.ShapeDtypeStruct((N_PAIR, nb, LANES, tb), F32),
            jax.ShapeDtypeStruct((N_PAIR, nb, 2, tb), F32),
            jax.ShapeDtypeStruct((N_PAIR, T, LANES), BF16),
            jax.ShapeDtypeStruct((N_PAIR, T, LANES), BF16),
            jax.ShapeDtypeStruct((N_PAIR, nb, 16, tb), F32),
        ],
        scratch_shapes=[pltpu.VMEM((nb, LANES, tb), F32), pltpu.VMEM((2, tb, LANES), F32),
                        pltpu.VMEM((2, tb, LANES), F32)],
        compiler_params=_params(("arbitrary", "arbitrary"), 56),
    )(qkv3, do3, st4, qkv3, qkv3, cp3, kst4, last_q)


def _attn_out(o3, w, x, name):
    T = x.shape[0]

    def body(o_ref, w_ref, x_ref, out_ref):
        out_ref[...] = x_ref[...] + jnp.dot(_cat_groups(o_ref, N_PAIR), w_ref[...], preferred_element_type=F32)

    return _rowcall(name, body, T, min(512, T), [(o3, "grp"), _wkind(w), (x, "row")],
                    [((T, D_MODEL), F32, "row")], 24)[0]


def _attn_dout(dx, wt, o3, name):
    T = dx.shape[0]
    tm = min(512, T)

    def body(dx_ref, w_ref, o_ref, do_ref, dd_ref):
        do = jnp.dot(dx_ref[...].astype(BF16), w_ref[...], preferred_element_type=F32).astype(BF16)
        lo = _own_lanes((tm, LANES), 0)
        head = lax.broadcasted_iota(jnp.int32, (tm, 2 * N_PAIR), 1)
        dd = jnp.zeros((tm, 2 * N_PAIR), F32)
        for t in range(N_PAIR):
            d = do[:, LANES * t:LANES * (t + 1)]
            do_ref[t] = d
            prod = d.astype(F32) * o_ref[t].astype(F32)
            d0 = jnp.sum(jnp.where(lo, prod, 0.0), axis=1, keepdims=True)
            d1 = jnp.sum(jnp.where(lo, 0.0, prod), axis=1, keepdims=True)
            dd = jnp.where(head == 2 * t, d0, jnp.where(head == 2 * t + 1, d1, dd))
        dd_ref[...] = dd

    return _rowcall(name, body, T, tm, [(dx, "row"), _wkind(wt), (o3, "grp")],
                    [((N_PAIR, T, LANES), BF16, "grp"), ((T, 2 * N_PAIR), F32, "row")], 32)


def _ffn_out(gu, w, x, name):
    T = x.shape[0]
    tm = min(512, T)

    def body(gu_ref, w_ref, x_ref, out_ref, hid_ref):
        acc = x_ref[...]
        for j in range(N_FFN_TILE):
            g = gu_ref[:, 2 * FFN_TILE * j:2 * FFN_TILE * j + FFN_TILE].astype(F32)
            u = gu_ref[:, 2 * FFN_TILE * j + FFN_TILE:2 * FFN_TILE * (j + 1)].astype(F32)
            hj = (g * jax.nn.sigmoid(g) * u).astype(BF16)
            hid_ref[:, FFN_TILE * j:FFN_TILE * (j + 1)] = hj
            acc = acc + jnp.dot(hj, w_ref[FFN_TILE * j:FFN_TILE * (j + 1), :], preferred_element_type=F32)
        out_ref[...] = acc

    return _rowcall(name, body, T, tm, [(gu, "row"), _wkind(w), (x, "row")],
                    [((T, D_MODEL), F32, "row"), ((T, FFN_H), BF16, "row")], 48)


def _ffn_dgu(dx, wt, gu, name):
    T = dx.shape[0]
    tm = min(512, T)

    def body(dx_ref, w_ref, gu_ref, dgu_ref):
        dxb = dx_ref[...].astype(BF16)
        for j in range(N_FFN_TILE):
            dh = jnp.dot(dxb, w_ref[:, FFN_TILE * j:FFN_TILE * (j + 1)], preferred_element_type=F32)
            g = gu_ref[:, 2 * FFN_TILE * j:2 * FFN_TILE * j + FFN_TILE].astype(F32)
            u = gu_ref[:, 2 * FFN_TILE * j + FFN_TILE:2 * FFN_TILE * (j + 1)].astype(F32)
            sg = jax.nn.sigmoid(g)
            dgu_ref[:, 2 * FFN_TILE * j:2 * FFN_TILE * j + FFN_TILE] = (
                dh * u * (sg * (1.0 + g * (1.0 - sg)))).astype(BF16)
            dgu_ref[:, 2 * FFN_TILE * j + FFN_TILE:2 * FFN_TILE * (j + 1)] = (dh * (g * sg)).astype(BF16)

    return _rowcall(name, body, T, tm, [(dx, "row"), _wkind(wt), (gu, "row")],
                    [((T, 2 * FFN_H), BF16, "row")], 48)[0]


def _sgu_core(a, ln_g, ln_b, w_s, bst, w, x, name):
    T = x.shape[0]
    tm = min(256, T)

    def body(a_ref, lg_ref, lb_ref, ws_ref, bs_ref, w_ref, x_ref, out_ref, gated_ref, vn_ref, mixed_ref):
        v = _gelu(a_ref[:, SGU_W:].astype(F32))
        mu = jnp.mean(v, axis=-1, keepdims=True)
        vc = v - mu
        rstd = lax.rsqrt(jnp.mean(vc * vc, axis=-1, keepdims=True) + LN_EPS)
        vn_ref[...] = (vc * rstd * lg_ref[...] + lb_ref[...]).astype(BF16)
        tt = lax.broadcasted_iota(jnp.int32, (CHUNK, CHUNK), 0)
        ss = lax.broadcasted_iota(jnp.int32, (CHUNK, CHUNK), 1)
        for g in range(SGU_G):
            wg = jnp.where(tt >= ss, ws_ref[g], 0.0).astype(BF16)
            bcol = _lane_col(bs_ref[...], g)
            cols = slice(CHUNK * g, CHUNK * (g + 1))
            for c in range(tm // CHUNK):
                rows = slice(CHUNK * c, CHUNK * (c + 1))
                mixed = jnp.dot(wg, vn_ref[rows, cols], preferred_element_type=F32) + bcol
                u = _gelu(a_ref[rows, cols].astype(F32))
                mixed_ref[rows, cols] = mixed.astype(BF16)
                gated_ref[rows, cols] = (u * mixed).astype(BF16)
        out_ref[...] = x_ref[...] + jnp.dot(gated_ref[...], w_ref[...], preferred_element_type=F32)

    ins = [(a, "row"), (ln_g, "full"), (ln_b, "full"), (w_s, "full"), (bst, "full"), _wkind(w), (x, "row")]
    outs = [((T, D_MODEL), F32, "row")] + [((T, SGU_W), BF16, "row")] * 3
    return _rowcall(name, body, T, tm, ins, outs, 40)


def _sgu_core_bwd(dx, wt, a, vn, mixed, ln_g, w_s, name):
    T = dx.shape[0]
    tm = min(256, T)

    def body(dx_ref, wt_ref, a_ref, vn_ref, mx_ref, lg_ref, ws_ref,
             da_ref, dws_ref, dba_ref, dlg_ref, dlb_ref, dg_scr, dvn_scr):
        @pl.when(pl.program_id(0) == 0)
        def _():
            dws_ref[...] = jnp.zeros(dws_ref.shape, F32)
            dba_ref[...] = jnp.zeros(dba_ref.shape, F32)
            dlg_ref[...] = jnp.zeros(dlg_ref.shape, F32)
            dlb_ref[...] = jnp.zeros(dlb_ref.shape, F32)

        dg_scr[...] = jnp.dot(dx_ref[...].astype(BF16), wt_ref[...], preferred_element_type=F32)
        tt = lax.broadcasted_iota(jnp.int32, (CHUNK, CHUNK), 0)
        ss = lax.broadcasted_iota(jnp.int32, (CHUNK, CHUNK), 1)
        tril = tt >= ss
        for g in range(SGU_G):
            wg = jnp.where(tril, ws_ref[g], 0.0).astype(BF16)
            cols = slice(CHUNK * g, CHUNK * (g + 1))
            for c in range(tm // CHUNK):
                rows = slice(CHUNK * c, CHUNK * (c + 1))
                dgb = dg_scr[rows, cols]
                au = a_ref[rows, cols].astype(F32)
                dmx = dgb * _gelu(au)
                da_ref[rows, cols] = (dgb * mx_ref[rows, cols].astype(F32) * _gelu_grad(au)).astype(BF16)
                dmb = dmx.astype(BF16)
                dvn_scr[rows, cols] = lax.dot_general(wg, dmb, TN_DIMS, preferred_element_type=F32)
                dws_ref[g] += jnp.where(
                    tril, lax.dot_general(dmb, vn_ref[rows, cols], NT_DIMS, preferred_element_type=F32), 0.0)
                dba_ref[:, cols] += dmx
        av = a_ref[:, SGU_W:].astype(F32)
        v = _gelu(av)
        mu = jnp.mean(v, axis=-1, keepdims=True)
        vc = v - mu
        rstd = lax.rsqrt(jnp.mean(vc * vc, axis=-1, keepdims=True) + LN_EPS)
        xhat = vc * rstd
        dvn = dvn_scr[...]
        dlg_ref[...] += jnp.sum(dvn * xhat, axis=0, keepdims=True)
        dlb_ref[...] += jnp.sum(dvn, axis=0, keepdims=True)
        dxh = dvn * lg_ref[...]
        dv = rstd * (dxh - jnp.mean(dxh, axis=-1, keepdims=True)
                     - xhat * jnp.mean(dxh * xhat, axis=-1, keepdims=True))
        da_ref[:, SGU_W:] = (dv * _gelu_grad(av)).astype(BF16)

    ins = [(dx, "row"), _wkind(wt), (a, "row"), (vn, "row"), (mixed, "row"), (ln_g, "full"), (w_s, "full")]
    outs = [((T, 2 * SGU_W), BF16, "row"), ((SGU_G, CHUNK, CHUNK), F32, "full"), ((CHUNK, SGU_W), F32, "full"),
            ((1, SGU_W), F32, "full"), ((1, SGU_W), F32, "full")]
    return _rowcall(name, body, T, tm, ins, outs, 40,
                    scratch=[pltpu.VMEM((tm, SGU_W), F32), pltpu.VMEM((tm, SGU_W), F32)])


def _loss_head(x, wf, tgt, name):
    T = x.shape[0]
    tm = min(512, T)

    def body(x_ref, wf_ref, tgt_ref, dx_ref, dwf_ref, loss_ref):
        @pl.when(pl.program_id(0) == 0)
        def _():
            dwf_ref[...] = jnp.zeros(dwf_ref.shape, F32)
            loss_ref[...] = jnp.zeros(loss_ref.shape, F32)

        xv = x_ref[...]
        r = lax.rsqrt(jnp.mean(xv * xv, axis=-1, keepdims=True) + NORM_EPS)
        xn = xv * r
        err = xn * wf_ref[...] - tgt_ref[...]
        loss_ref[...] += 0.5 * jnp.sum(jnp.mean(err * err, axis=-1, keepdims=True), axis=0, keepdims=True)
        dy = err * (1.0 / D_MODEL)
        dwf_ref[...] += jnp.sum(dy * xn, axis=0, keepdims=True)
        dyw = dy * wf_ref[...]
        dx_ref[...] = r * (dyw - xn * jnp.mean(dyw * xn, axis=-1, keepdims=True))

    return _rowcall(name, body, T, tm, [(x, "row"), (wf, "full"), (tgt, "row")],
                    [((T, D_MODEL), F32, "row"), ((1, D_MODEL), F32, "full"), ((8, LANES), F32, "full")], 32)


def _peers():
    x, y, c = lax.axis_index("x"), lax.axis_index("y"), lax.axis_index("c")
    peers = []
    for p in range(1, N_DEV):
        px = 1 - x if p & 4 else x
        py = 1 - y if p & 2 else y
        pc = 1 - c if p & 1 else c
        peers.append((4 * px + 2 * py + pc, (px, py, pc)))
    return 4 * x + 2 * y + c, peers


def _all_gather(arrs, name):
    n = len(arrs)
    hbm = pl.BlockSpec(memory_space=pl.ANY)

    def body(*refs):
        ins, outs = refs[:n], refs[n:2 * n]
        send_sems, recv_sems, local_sems = refs[2 * n:]
        me, peers = _peers()
        sends, recvs, locals_ = [], [], []
        for t in range(n):
            cp = pltpu.make_async_copy(ins[t], outs[t].at[me], local_sems.at[t])
            cp.start()
            locals_.append(cp)
            for k, (pidx, pid) in enumerate(peers):
                s = t * (N_DEV - 1) + k
                send = pltpu.make_async_remote_copy(
                    src_ref=ins[t], dst_ref=outs[t].at[me], send_sem=send_sems.at[s], recv_sem=recv_sems.at[s],
                    device_id=pid, device_id_type=MESH)
                send.start()
                sends.append(send)
                recvs.append(pltpu.make_async_remote_copy(
                    src_ref=ins[t], dst_ref=outs[t].at[pidx], send_sem=send_sems.at[s], recv_sem=recv_sems.at[s],
                    device_id=pid, device_id_type=MESH))
        for r in recvs:
            r.wait_recv()
        for s in sends:
            s.wait_send()
        for cp in locals_:
            cp.wait()

    return pl.pallas_call(
        body,
        name=name,
        in_specs=[hbm] * n,
        out_specs=[hbm] * n,
        out_shape=[jax.ShapeDtypeStruct((N_DEV,) + a.shape, a.dtype) for a in arrs],
        scratch_shapes=[pltpu.SemaphoreType.DMA((n * (N_DEV - 1),)), pltpu.SemaphoreType.DMA((n * (N_DEV - 1),)),
                        pltpu.SemaphoreType.DMA((n,))],
    )(*arrs)


def _exchange(arrs, name):
    n = len(arrs)
    hbm = pl.BlockSpec(memory_space=pl.ANY)

    def body(*refs):
        ins, outs = refs[:n], refs[n:2 * n]
        send_sems, recv_sems, local_sems = refs[2 * n:]
        me, peers = _peers()
        sends, recvs, locals_ = [], [], []
        for t in range(n):
            cp = pltpu.make_async_copy(ins[t].at[me], outs[t].at[me], local_sems.at[t])
            cp.start()
            locals_.append(cp)
            for k, (pidx, pid) in enumerate(peers):
                s = t * (N_DEV - 1) + k
                send = pltpu.make_async_remote_copy(
                    src_ref=ins[t].at[pidx], dst_ref=outs[t].at[me], send_sem=send_sems.at[s],
                    recv_sem=recv_sems.at[s], device_id=pid, device_id_type=MESH)
                send.start()
                sends.append(send)
                recvs.append(pltpu.make_async_remote_copy(
                    src_ref=ins[t].at[pidx], dst_ref=outs[t].at[pidx], send_sem=send_sems.at[s],
                    recv_sem=recv_sems.at[s], device_id=pid, device_id_type=MESH))
        for r in recvs:
            r.wait_recv()
        for s in sends:
            s.wait_send()
        for cp in locals_:
            cp.wait()

    return pl.pallas_call(
        body,
        name=name,
        in_specs=[hbm] * n,
        out_specs=[hbm] * n,
        out_shape=[jax.ShapeDtypeStruct(a.shape, a.dtype) for a in arrs],
        scratch_shapes=[pltpu.SemaphoreType.DMA((n * (N_DEV - 1),)), pltpu.SemaphoreType.DMA((n * (N_DEV - 1),)),
                        pltpu.SemaphoreType.DMA((n,))],
    )(*arrs)


def _row_tile(rows, cap):
    best = None
    for t in range(16, cap + 1, 16):
        if rows % t == 0:
            best = t
    assert best is not None, rows
    return best


def _adam_sum(parts, w, m, v, name):
    R, C = w.shape
    tr = _row_tile(R, 128)

    def body(p_ref, w_ref, m_ref, v_ref, g_ref, d_ref, nm_ref, nv_ref):
        g = p_ref[0].astype(F32)
        for s in range(1, N_DEV):
            g = g + p_ref[s].astype(F32)
        mm = ADAM_B1 * m_ref[...] + (1.0 - ADAM_B1) * g
        vv = ADAM_B2 * v_ref[...] + (1.0 - ADAM_B2) * (g * g)
        m_hat = mm / (1.0 - ADAM_B1 ** ADAM_STEP)
        v_hat = vv / (1.0 - ADAM_B2 ** ADAM_STEP)
        g_ref[...] = g
        d_ref[...] = -ADAM_LR * (m_hat / (jnp.sqrt(v_hat) + ADAM_EPS) + ADAM_WD * w_ref[...])
        nm_ref[...] = mm
        nv_ref[...] = vv

    mat = pl.BlockSpec((tr, C), lambda i: (i, 0))
    return pl.pallas_call(
        body,
        name=name,
        grid=(R // tr,),
        in_specs=[pl.BlockSpec((N_DEV, tr, C), lambda i: (0, i, 0)), mat, mat, mat],
        out_specs=[mat] * 4,
        out_shape=[jax.ShapeDtypeStruct((R, C), F32)] * 4,
        compiler_params=_params(("parallel",), 32),
    )(parts, w, m, v)


def _cols_from_gathered(g):
    _, L, K, n = g.shape
    return jnp.transpose(g, (1, 2, 0, 3)).reshape(L, K, N_DEV * n)


def _rows_from_gathered(g):
    _, L, k, N = g.shape
    return jnp.transpose(g, (1, 0, 2, 3)).reshape(L, N_DEV * k, N)


def _cols_to_blocks(dw):
    L, K, N = dw.shape
    n = N // N_DEV
    return jnp.transpose(dw.reshape(L, K, N_DEV, n), (2, 0, 1, 3)).reshape(N_DEV, L * K, n)


def _rows_to_blocks(dw):
    L, K, N = dw.shape
    k = K // N_DEV
    return jnp.transpose(dw.reshape(L, N_DEV, k, N), (1, 0, 2, 3)).reshape(N_DEV, L * k, N)


def _ffn_interleave(w):
    lead = w.shape[:-1]
    t = w.reshape(lead + (2, N_FFN_TILE, FFN_TILE))
    return jnp.swapaxes(t, -3, -2).reshape(lead + (2 * FFN_H,))


def _ffn_deinterleave(w):
    lead = w.shape[:-1]
    t = w.reshape(lead + (N_FFN_TILE, 2, FFN_TILE))
    return jnp.swapaxes(t, -3, -2).reshape(lead + (2 * FFN_H,))


def _pad_rows(a, rows=8):
    a = a.reshape(-1, a.shape[-1])
    return jnp.pad(a, ((0, rows - a.shape[0]), (0, 0)))


SMALL_ROWS = 6 * 8 + 2 * SGU_G * CHUNK * CHUNK // D_MODEL


def _pack_small(mixer, ffn, final, b_f, extra, w_s, b_s):
    bf_row = jnp.pad(b_f.reshape(1, -1), ((0, 0), (0, D_MODEL - b_f.size)))
    bs_rows = jnp.pad(b_s.reshape(4, -1), ((0, 0), (0, D_MODEL - b_s.size // 4)))
    return jnp.concatenate([
        _pad_rows(mixer), _pad_rows(ffn), _pad_rows(final.reshape(1, -1)), _pad_rows(bf_row),
        _pad_rows(extra), _pad_rows(bs_rows), w_s.reshape(-1, D_MODEL)], axis=0)


def _unpack_small(p):
    mixer, ffn, final = p[0:4], p[8:12], p[16]
    b_f = p[24, :32].reshape(2, 2 * N_PAIR)
    extra = p[32]
    b_s = p[40:44, :2 * SGU_G * CHUNK // 4].reshape(2, SGU_G, CHUNK)
    w_s = p[48:].reshape(2, SGU_G, CHUNK, CHUNK)
    return mixer, ffn, final, b_f, extra, w_s, b_s


def kernel(x, mixer_norm_w, attn_w_in, attn_b_f, attn_w_out, sgu_w_in, sgu_ln_g, sgu_ln_b, sgu_w_s, sgu_b_s, sgu_w_out, ffn_norm_w, ffn_w_in, ffn_w_out, final_norm_w, loss_target, m_mixer_norm_w, m_attn_w_in, m_attn_b_f, m_attn_w_out, m_sgu_w_in, m_sgu_ln_g, m_sgu_ln_b, m_sgu_w_s, m_sgu_b_s, m_sgu_w_out, m_ffn_norm_w, m_ffn_w_in, m_ffn_w_out, m_final_norm_w, v_mixer_norm_w, v_attn_w_in, v_attn_b_f, v_attn_w_out, v_sgu_w_in, v_sgu_ln_g, v_sgu_ln_b, v_sgu_w_s, v_sgu_b_s, v_sgu_w_out, v_ffn_norm_w, v_ffn_w_in, v_ffn_w_out, v_final_norm_w):
    T = x.shape[1]
    tb = min(ATT_BLOCK, T)
    xs, tgt = x[0], loss_target[0]

    shards = [attn_w_in, attn_w_out, sgu_w_in, sgu_w_out, ffn_w_in, ffn_w_out, sgu_ln_g, sgu_ln_b]
    gathered = _all_gather([s.astype(BF16) for s in shards[:6]] + shards[6:], "gather_weights")
    w_attn_in = _cols_from_gathered(gathered[0])
    w_attn_out = _rows_from_gathered(gathered[1])
    w_sgu_in = _cols_from_gathered(gathered[2])
    w_sgu_out = _rows_from_gathered(gathered[3])
    w_ffn_in = _ffn_interleave(_cols_from_gathered(gathered[4]))
    w_ffn_out = _rows_from_gathered(gathered[5])
    ln_g = jnp.transpose(gathered[6], (1, 0, 2)).reshape(2, 1, SGU_W)
    ln_b = jnp.transpose(gathered[7], (1, 0, 2)).reshape(2, 1, SGU_W)
    w_qkv = w_attn_in[:, :, :3 * D_MODEL]
    w_f = jnp.pad(w_attn_in[:, :, 3 * D_MODEL:], ((0, 0), (0, 0), (0, LANES - 2 * N_PAIR)))
    tr = lambda w: jnp.swapaxes(w, -1, -2)
    w_qkv_t6, w_f_t, w_attn_out_t = tr(w_qkv).reshape(6, D_MODEL, D_MODEL), tr(w_f), tr(w_attn_out)
    w_sgu_in_t, w_sgu_out_t, w_ffn_in_t, w_ffn_out_t = tr(w_sgu_in), tr(w_sgu_out), tr(w_ffn_in), tr(w_ffn_out)
    mixer_nw = mixer_norm_w.reshape(4, 1, D_MODEL)
    ffn_nw = ffn_norm_w.reshape(4, 1, D_MODEL)
    b_col = attn_b_f.reshape(2, 2 * N_PAIR, 1)
    bs_t = jnp.swapaxes(sgu_b_s, 1, 2)

    saved = []
    xr = xs
    for i in range(4):
        j = i // 2
        if i % 2 == 0:
            qkv3, h = _norm_matmul(xr, mixer_nw[i], (w_qkv, j), BF16, f"attn_qkv_{j}", 1024, groups=True)
            fl, _ = _norm_matmul(xr, mixer_nw[i], (w_f, j), F32, f"attn_gate_{j}", LANES)
            fl3 = jnp.transpose(fl[:, :2 * N_PAIR].reshape(T // CHUNK, CHUNK, 2 * N_PAIR), (0, 2, 1))
            c_chunks = _fgate_fwd(fl3, b_col[j], f"fgate_fwd_{j}")
            c16 = jnp.transpose(c_chunks, (1, 0, 2)).reshape(2 * N_PAIR, T)
            cp3 = _bias_pieces(c16)
            first_kv, last_q = _attn_reach(qkv3, c16, tb, min(ATT_Q_BLOCK, T))
            ot3, lse4 = _attn_fwd(qkv3, cp3, first_kv, f"attn_fwd_{j}")
            o3 = jnp.swapaxes(ot3, 1, 2)
            xm = _attn_out(o3, (w_attn_out, j), xr, f"attn_out_{j}")
            mix_saved = (xr, h, qkv3, fl3, cp3, o3, lse4, last_q)
        else:
            a, h = _norm_matmul(xr, mixer_nw[i], (w_sgu_in, j), BF16, f"sgu_in_{j}", 1024)
            xm, gated, vn, mixed = _sgu_core(a, ln_g[j], ln_b[j], sgu_w_s[j], bs_t[j], (w_sgu_out, j), xr,
                                             f"sgu_core_{j}")
            mix_saved = (xr, h, a, gated, vn, mixed)
        gu, h2 = _norm_matmul(xm, ffn_nw[i], (w_ffn_in, i), BF16, f"ffn_in_{i}", FFN_H // 2)
        xo, hid = _ffn_out(gu, (w_ffn_out, i), xm, f"ffn_out_{i}")
        saved.append((mix_saved, (xm, h2, gu, hid)))
        xr = xo
    dx, d_final, loss_part = _loss_head(xr, final_norm_w.reshape(1, D_MODEL), tgt, "loss_head")

    d_mixer_nw, d_ffn_nw = [None] * 4, [None] * 4
    d_attn_in, d_attn_out, d_bf, d_sgu_in, d_sgu_out = [None] * 2, [None] * 2, [None] * 2, [None] * 2, [None] * 2
    d_ws, d_bs, d_lg, d_lb = [None] * 2, [None] * 2, [None] * 2, [None] * 2
    d_ffn_in, d_ffn_out = [None] * 4, [None] * 4
    for i in reversed(range(4)):
        j = i // 2
        mix_saved, (xm, h2, gu, hid) = saved[i]
        dgu = _ffn_dgu(dx, (w_ffn_out_t, i), gu, f"ffn_dgu_{i}")
        d_ffn_out[i] = _matmul_tn(hid, dx, f"ffn_dwout_{i}", FFN_H // 2, D_MODEL)
        d_ffn_in[i] = _matmul_tn(h2, dgu, f"ffn_dwin_{i}", D_MODEL, FFN_H // 2)
        dx, d_ffn_nw[i] = _matmul_rms_bwd([(dgu, "row")], [(w_ffn_in_t, i)], xm, ffn_nw[i], dx, f"ffn_dx_{i}", 512, 48)
        if i % 2 == 0:
            xr, h, qkv3, fl3, cp3, o3, lse4, last_q = mix_saved
            do3, dd = _attn_dout(dx, (w_attn_out_t, j), o3, f"attn_dout_{j}")
            d_attn_out[j] = _matmul_tn(o3, dx, f"attn_dwout_{j}", D_MODEL, D_MODEL, a_grp=True)
            to_blocks = lambda a: jnp.swapaxes(a.reshape(N_PAIR, 2, T // tb, tb), 1, 2)
            from_blocks = lambda a: jnp.swapaxes(a, 1, 2).reshape(N_PAIR, 2, T)
            dd4 = to_blocks(dd.T.reshape(N_PAIR, 2, T))
            st4 = jnp.concatenate([to_blocks(from_blocks(lse4)), dd4], axis=2)
            kst4 = jnp.swapaxes((qkv3[N_PAIR:2 * N_PAIR] * QK_SCALE).reshape(N_PAIR, T // tb, tb, LANES), 2, 3)
            dqt4, rs4, dk3, dv3, cs4 = _attn_bwd(qkv3, cp3, kst4, do3, st4, last_q, f"attn_bwd_{j}")
            dq3 = jnp.swapaxes(dqt4, 2, 3).reshape(N_PAIR, T, LANES).astype(BF16)
            dc_pair = from_blocks(rs4 - jnp.stack([cs4[:, :, 0], cs4[:, :, 8]], axis=2))
            dc_chunks = jnp.transpose(dc_pair.reshape(2 * N_PAIR, T // CHUNK, CHUNK), (1, 0, 2))
            dfl3, db = _fgate_bwd(dc_chunks, fl3, b_col[j], f"fgate_bwd_{j}")
            d_bf[j] = db[:, 0]
            dfl = jnp.transpose(dfl3, (0, 2, 1)).reshape(T, 2 * N_PAIR)
            dfl = jnp.pad(dfl.astype(BF16), ((0, 0), (0, LANES - 2 * N_PAIR)))
            d_qkv = [_matmul_tn(h, d3, f"attn_dw{nm}_{j}", D_MODEL, D_MODEL, g_grp=True)
                     for nm, d3 in (("q", dq3), ("k", dk3), ("v", dv3))]
            d_f = _matmul_tn(h, dfl, f"attn_dwf_{j}", D_MODEL, LANES)[:, :2 * N_PAIR]
            d_attn_in[j] = jnp.concatenate(d_qkv + [d_f], axis=1)
            wts = [(w_qkv_t6, 3 * j + k) for k in range(3)] + [(w_f_t, j)]
            dx, d_mixer_nw[i] = _matmul_rms_bwd(
                [(dq3, "grp"), (dk3, "grp"), (dv3, "grp"), (dfl, "row")], wts, xr, mixer_nw[i], dx,
                f"attn_dx_{j}", 256, 40)
        else:
            xr, h, a, gated, vn, mixed = mix_saved
            da, d_ws[j], dba, d_lg[j], d_lb[j] = _sgu_core_bwd(dx, (w_sgu_out_t, j), a, vn, mixed, ln_g[j], sgu_w_s[j],
                                                               f"sgu_core_bwd_{j}")
            d_bs[j] = jnp.sum(dba.reshape(CHUNK, SGU_G, CHUNK), axis=-1).T
            d_sgu_out[j] = _matmul_tn(gated, dx, f"sgu_dwout_{j}", D_MODEL, D_MODEL)
            d_sgu_in[j] = _matmul_tn(h, da, f"sgu_dwin_{j}", D_MODEL, 1024)
            dx, d_mixer_nw[i] = _matmul_rms_bwd([(da, "row")], [(w_sgu_in_t, j)], xr, mixer_nw[i], dx,
                                                f"sgu_dx_{j}", 256, 40)
    grad_x = dx[None]

    rows4 = lambda parts: jnp.concatenate(parts, axis=1).reshape(4, D_MODEL)
    small_g = _pack_small(rows4(d_mixer_nw), rows4(d_ffn_nw), d_final[0], jnp.stack(d_bf),
                          loss_part[0:1, 0:1] * jnp.ones((1, D_MODEL), F32), jnp.stack(d_ws), jnp.stack(d_bs))
    zero_row = jnp.zeros((1, D_MODEL), F32)
    pack = lambda pre: _pack_small(pre[0], pre[1], pre[2], pre[3], zero_row, pre[4], pre[5])
    small_w = pack((mixer_norm_w, ffn_norm_w, final_norm_w, attn_b_f, sgu_w_s, sgu_b_s))
    small_m = pack((m_mixer_norm_w, m_ffn_norm_w, m_final_norm_w, m_attn_b_f, m_sgu_w_s, m_sgu_b_s))
    small_v = pack((v_mixer_norm_w, v_ffn_norm_w, v_final_norm_w, v_attn_b_f, v_sgu_w_s, v_sgu_b_s))
    small_all = _all_gather([small_g], "gather_small_grads")[0]
    small_out = [_unpack_small(p) for p in _adam_sum(small_all, small_w, small_m, small_v, "adam_small")]
    loss = small_out[0][4][0]

    blocks = [
        _cols_to_blocks(jnp.stack(d_attn_in)), _rows_to_blocks(jnp.stack(d_attn_out)),
        _cols_to_blocks(jnp.stack(d_sgu_in)), _rows_to_blocks(jnp.stack(d_sgu_out)),
        _cols_to_blocks(_ffn_deinterleave(jnp.stack(d_ffn_in))), _rows_to_blocks(jnp.stack(d_ffn_out)),
        jnp.stack(d_lg).reshape(2, N_DEV, 1, SGU_W // N_DEV).transpose(1, 0, 2, 3).reshape(N_DEV, 2, SGU_W // N_DEV),
        jnp.stack(d_lb).reshape(2, N_DEV, 1, SGU_W // N_DEV).transpose(1, 0, 2, 3).reshape(N_DEV, 2, SGU_W // N_DEV),
    ]
    received = _exchange([b.astype(BF16) for b in blocks[:6]] + blocks[6:], "exchange_grads")
    names = ["attn_w_in", "attn_w_out", "sgu_w_in", "sgu_w_out", "ffn_w_in", "ffn_w_out"]
    ws = [attn_w_in, attn_w_out, sgu_w_in, sgu_w_out, ffn_w_in, ffn_w_out]
    ms = [m_attn_w_in, m_attn_w_out, m_sgu_w_in, m_sgu_w_out, m_ffn_w_in, m_ffn_w_out]
    vs = [v_attn_w_in, v_attn_w_out, v_sgu_w_in, v_sgu_w_out, v_ffn_w_in, v_ffn_w_out]
    big_out = {}
    for nm, rec, w, m, v in zip(names, received[:6], ws, ms, vs):
        flat = lambda a: a.reshape(-1, a.shape[-1])
        big_out[nm] = [o.reshape(w.shape) for o in _adam_sum(rec, flat(w), flat(m), flat(v), f"adam_{nm}")]
    pad8 = lambda a: jnp.pad(a, [(0, 0)] * (a.ndim - 2) + [(0, 8 - a.shape[-2]), (0, 0)])
    ln_parts = jnp.concatenate([pad8(received[6]), pad8(received[7])], axis=1)
    ln_pack = lambda g, b: jnp.concatenate([pad8(g), pad8(b)], axis=0)
    ln_out = _adam_sum(ln_parts, ln_pack(sgu_ln_g, sgu_ln_b), ln_pack(m_sgu_ln_g, m_sgu_ln_b),
                       ln_pack(v_sgu_ln_g, v_sgu_ln_b), "adam_sgu_ln")

    def leaf(kind):
        mixer, ffn, final, b_f, _, w_s, b_s = small_out[kind]
        o = lambda nm: big_out[nm][kind]
        return [mixer, o("attn_w_in"), b_f, o("attn_w_out"), o("sgu_w_in"), ln_out[kind][0:2], ln_out[kind][8:10],
                w_s, b_s, o("sgu_w_out"), ffn, o("ffn_w_in"), o("ffn_w_out"), final]

    return (loss, grad_x, *leaf(0), *leaf(1), *leaf(2), *leaf(3))
```

```python
import functools

import jax
import jax.numpy as jnp
from jax import lax
from jax.experimental import pallas as pl
from jax.experimental.pallas import tpu as pltpu

F32 = jnp.float32
BF16 = jnp.bfloat16

D_MODEL = 1024
HEAD_DIM = 64
N_PAIR = 8
LANES = 128
SGU_W = 2048
SGU_G = 16
CHUNK = 128
FFN_H = 2816
FFN_TILE = 256
N_FFN_TILE = FFN_H // FFN_TILE
NORM_EPS = 1e-6
LN_EPS = 1e-5
QK_SCALE = 0.125
ATT_BLOCK = 512
N_DEV = 8
ADAM_LR = 0.001
ADAM_B1 = 0.9
ADAM_B2 = 0.999
ADAM_EPS = 1e-08
ADAM_WD = 0.01
ADAM_STEP = 10
MESH = pl.DeviceIdType.MESH
SQRT_HALF = 0.7071067811865476
INV_SQRT_2PI = 0.3989422804014327

NT_DIMS = (((1,), (1,)), ((), ()))
TN_DIMS = (((0,), (0,)), ((), ()))


def _gelu(x):
    return 0.5 * x * (1.0 + lax.erf(x * SQRT_HALF))


def _gelu_grad(x):
    return 0.5 * (1.0 + lax.erf(x * SQRT_HALF)) + x * jnp.exp(-0.5 * x * x) * INV_SQRT_2PI


def _lane_col(v, lane):
    idx = lax.broadcasted_iota(jnp.int32, v.shape, 1)
    return jnp.sum(jnp.where(idx == lane, v, 0.0), axis=1, keepdims=True)


def _params(sem, vmem_mb):
    return pltpu.CompilerParams(dimension_semantics=sem, vmem_limit_bytes=vmem_mb << 20)


def _cat_groups(ref, n):
    if n == 1:
        return ref[0]
    return jnp.concatenate([ref[t] for t in range(n)], axis=1)


def _wkind(w):
    return (w[0], ("layer", w[1])) if isinstance(w, tuple) else (w, "full")


def _rowcall(name, body, T, tm, ins, outs, vmem_mb, scratch=()):
    def spec(shape, kind, resident_once=False):
        shape = tuple(shape)
        if isinstance(kind, tuple):
            layer = kind[1]
            return pl.BlockSpec((None,) + shape[1:], lambda i: (layer,) + (0,) * (len(shape) - 1),
                                pipeline_mode=pl.Buffered(1))
        if kind == "row":
            return pl.BlockSpec((tm,) + shape[1:], lambda i: (i,) + (0,) * (len(shape) - 1))
        if kind == "grp":
            return pl.BlockSpec((shape[0], tm, shape[2]), lambda i: (0, i, 0))
        if resident_once:
            return pl.BlockSpec(shape, lambda i: (0,) * len(shape), pipeline_mode=pl.Buffered(1))
        return pl.BlockSpec(shape, lambda i: (0,) * len(shape))

    return pl.pallas_call(
        body,
        name=name,
        grid=(T // tm,),
        in_specs=[spec(a.shape, k, True) for a, k in ins],
        out_specs=[spec(s, k) for s, _, k in outs],
        out_shape=[jax.ShapeDtypeStruct(tuple(s), d) for s, d, _ in outs],
        scratch_shapes=list(scratch),
        compiler_params=_params(("arbitrary",), vmem_mb),
    )(*[a for a, _ in ins])


def _norm_matmul(x, nw, w, out_dtype, name, tn, groups=False):
    w, layer = w
    T, N = x.shape[0], w.shape[2]
    tm = min(1024, T)

    def body(x_ref, nw_ref, w_ref, o_ref, h_ref, h_scr):
        @pl.when(pl.program_id(1) == 0)
        def _():
            xv = x_ref[...]
            r = lax.rsqrt(jnp.mean(xv * xv, axis=-1, keepdims=True) + NORM_EPS)
            hv = (xv * r * nw_ref[...]).astype(BF16)
            h_scr[...] = hv
            h_ref[...] = hv

        acc = jnp.dot(h_scr[...], w_ref[...], preferred_element_type=F32)
        if groups:
            for t in range(tn // LANES):
                o_ref[t] = acc[:, LANES * t:LANES * (t + 1)].astype(out_dtype)
        else:
            o_ref[...] = acc.astype(out_dtype)

    if groups:
        o_shape = (N // LANES, T, LANES)
        o_spec = pl.BlockSpec((tn // LANES, tm, LANES), lambda i, j: (j, i, 0))
    else:
        o_shape = (T, N)
        o_spec = pl.BlockSpec((tm, tn), lambda i, j: (i, j))
    return pl.pallas_call(
        body,
        name=name,
        grid=(T // tm, N // tn),
        in_specs=[
            pl.BlockSpec((tm, D_MODEL), lambda i, j: (i, 0)),
            pl.BlockSpec((1, D_MODEL), lambda i, j: (0, 0)),
            pl.BlockSpec((None, D_MODEL, tn), lambda i, j: (layer, 0, j)),
        ],
        out_specs=[o_spec, pl.BlockSpec((tm, D_MODEL), lambda i, j: (i, 0))],
        out_shape=[jax.ShapeDtypeStruct(o_shape, out_dtype), jax.ShapeDtypeStruct((T, D_MODEL), BF16)],
        scratch_shapes=[pltpu.VMEM((tm, D_MODEL), BF16)],
        compiler_params=_params(("arbitrary", "arbitrary"), 48),
    )(x, nw, w)


def _matmul_tn(a, g, name, tk, tn, a_grp=False, g_grp=False):
    T = a.shape[1] if a_grp else a.shape[0]
    K = a.shape[0] * LANES if a_grp else a.shape[1]
    N = g.shape[0] * LANES if g_grp else g.shape[1]
    tm = min(1024, T)

    def body(a_ref, g_ref, o_ref):
        @pl.when(pl.program_id(2) == 0)
        def _():
            o_ref[...] = jnp.zeros(o_ref.shape, F32)

        av = _cat_groups(a_ref, tk // LANES) if a_grp else a_ref[...]
        gv = _cat_groups(g_ref, tn // LANES) if g_grp else g_ref[...]
        o_ref[...] += lax.dot_general(av.astype(BF16), gv.astype(BF16), TN_DIMS, preferred_element_type=F32)

    if a_grp:
        a_spec = pl.BlockSpec((tk // LANES, tm, LANES), lambda k, n, m: (k, m, 0))
    else:
        a_spec = pl.BlockSpec((tm, tk), lambda k, n, m: (m, k))
    if g_grp:
        g_spec = pl.BlockSpec((tn // LANES, tm, LANES), lambda k, n, m: (n, m, 0))
    else:
        g_spec = pl.BlockSpec((tm, tn), lambda k, n, m: (m, n))
    return pl.pallas_call(
        body,
        name=name,
        grid=(K // tk, N // tn, T // tm),
        in_specs=[a_spec, g_spec],
        out_specs=pl.BlockSpec((tk, tn), lambda k, n, m: (k, n)),
        out_shape=jax.ShapeDtypeStruct((K, N), F32),
        compiler_params=_params(("parallel", "parallel", "arbitrary"), 48),
    )(a, g)


def _matmul_rms_bwd(a_list, wt_list, x, nw, dres, name, tm, vmem_mb):
    T = x.shape[0]
    n = len(a_list)

    def body(*refs):
        a_refs, w_refs = refs[:n], refs[n:2 * n]
        x_ref, nw_ref, dres_ref, dx_ref, dnw_ref = refs[2 * n:]

        @pl.when(pl.program_id(0) == 0)
        def _():
            dnw_ref[...] = jnp.zeros(dnw_ref.shape, F32)

        dh = None
        for (arr, kind), a_ref, w_ref in zip(a_list, a_refs, w_refs):
            av = _cat_groups(a_ref, arr.shape[0]) if kind == "grp" else a_ref[...]
            part = jnp.dot(av.astype(BF16), w_ref[...], preferred_element_type=F32)
            dh = part if dh is None else dh + part
        xv = x_ref[...]
        r = lax.rsqrt(jnp.mean(xv * xv, axis=-1, keepdims=True) + NORM_EPS)
        xn = xv * r
        dnw_ref[...] += jnp.sum(dh * xn, axis=0, keepdims=True)
        dyw = dh * nw_ref[...]
        dx_ref[...] = dres_ref[...] + r * (dyw - xn * jnp.mean(dyw * xn, axis=-1, keepdims=True))

    ins = list(a_list) + [_wkind(w) for w in wt_list] + [(x, "row"), (nw, "full"), (dres, "row")]
    outs = [((T, D_MODEL), F32, "row"), ((1, D_MODEL), F32, "full")]
    return _rowcall(name, body, T, tm, ins, outs, vmem_mb)


def _fgate_fwd(fl3, bcol, name):
    n_chunk = fl3.shape[0]

    def body(fl_ref, b_ref, c_ref):
        r = lax.broadcasted_iota(jnp.int32, (CHUNK, CHUNK), 0)
        t = lax.broadcasted_iota(jnp.int32, (CHUNK, CHUNK), 1)
        tri = jnp.where(r <= t, 1.0, 0.0).astype(BF16)

        def chunk(i, carry):
            z = fl_ref[i] + b_ref[...]
            lf = jnp.minimum(z, 0.0) - jnp.log(1.0 + jnp.exp(-jnp.abs(z)))
            hi = lf.astype(BF16)
            r1 = lf - hi.astype(F32)
            mid = r1.astype(BF16)
            low = (r1 - mid.astype(F32)).astype(BF16)
            cs = (jnp.dot(hi, tri, preferred_element_type=F32) + jnp.dot(mid, tri, preferred_element_type=F32)
                  + jnp.dot(low, tri, preferred_element_type=F32)) + carry
            c_ref[i] = cs
            return _lane_col(cs, CHUNK - 1)

        lax.fori_loop(0, n_chunk, chunk, jnp.zeros((2 * N_PAIR, 1), F32))

    return pl.pallas_call(
        body, name=name, out_shape=jax.ShapeDtypeStruct(fl3.shape, F32),
        compiler_params=pltpu.CompilerParams(vmem_limit_bytes=16 << 20),
    )(fl3, bcol)


def _fgate_bwd(dc3, fl3, bcol, name):
    n_chunk = fl3.shape[0]

    def body(dc_ref, fl_ref, b_ref, dfl_ref, db_ref):
        tt = lax.broadcasted_iota(jnp.int32, (CHUNK, CHUNK), 0)
        rr = lax.broadcasted_iota(jnp.int32, (CHUNK, CHUNK), 1)
        tri = jnp.where(tt >= rr, 1.0, 0.0).astype(BF16)

        def chunk(k, carry):
            tail, acc = carry
            i = n_chunk - 1 - k
            dc = dc_ref[i]
            hi = dc.astype(BF16)
            r1 = dc - hi.astype(F32)
            mid = r1.astype(BF16)
            low = (r1 - mid.astype(F32)).astype(BF16)
            dlf = (jnp.dot(hi, tri, preferred_element_type=F32) + jnp.dot(mid, tri, preferred_element_type=F32)
                   + jnp.dot(low, tri, preferred_element_type=F32)) + tail
            z = fl_ref[i] + b_ref[...]
            dfl = dlf / (1.0 + jnp.exp(z))
            dfl_ref[i] = dfl
            return _lane_col(dlf, 0), acc + dfl

        _, acc = lax.fori_loop(0, n_chunk, chunk,
                               (jnp.zeros((2 * N_PAIR, 1), F32), jnp.zeros((2 * N_PAIR, CHUNK), F32)))
        db_ref[...] = jnp.broadcast_to(jnp.sum(acc, axis=1, keepdims=True), db_ref.shape)

    return pl.pallas_call(
        body, name=name,
        out_shape=[jax.ShapeDtypeStruct(fl3.shape, F32), jax.ShapeDtypeStruct((2 * N_PAIR, LANES), F32)],
        compiler_params=pltpu.CompilerParams(vmem_limit_bytes=16 << 20),
    )(dc3, fl3, bcol)


BIAS_LANES = 3
ATT_Q_BLOCK = 2048


def _own_lanes(shape, hh, axis=1):
    idx = lax.broadcasted_iota(jnp.int32, shape, axis)
    return idx < HEAD_DIM if hh == 0 else idx >= HEAD_DIM


def _spare(hh):
    return HEAD_DIM * (1 - hh)


def _bias_pieces(c16):
    T = c16.shape[1]
    negc = -c16
    hi = negc.astype(BF16)
    r1 = negc - hi.astype(F32)
    mid = r1.astype(BF16)
    low = (r1 - mid.astype(F32)).astype(BF16)
    pieces = jnp.stack([hi, mid, low], axis=-1).reshape(N_PAIR, 2, T, BIAS_LANES)
    zpad = jnp.zeros((N_PAIR, T, HEAD_DIM - BIAS_LANES), BF16)
    return jnp.concatenate([pieces[:, 1], zpad, pieces[:, 0], zpad], axis=-1)


UNDERFLOW_BOUND = -110.0


def _attn_reach(qkv3, c16, tb, tq):
    T = qkv3.shape[1]
    nb = T // tb

    def block_norm(a):
        sq = jnp.sum(jnp.square(a.astype(F32)).reshape(N_PAIR, nb, tb, 2, HEAD_DIM), axis=-1)
        return jnp.transpose(jnp.sqrt(jnp.max(sq, axis=2)), (0, 2, 1))

    qn, kn = block_norm(qkv3[:N_PAIR]), block_norm(qkv3[N_PAIR:2 * N_PAIR])
    cb = c16.reshape(N_PAIR, 2, nb, tb)
    c_max, negc_max = jnp.max(cb, axis=-1), jnp.max(-cb, axis=-1)
    bound = (qn[:, :, :, None] * (kn[:, :, None, :] + kn[:, :, :, None]) * QK_SCALE
             + negc_max[:, :, None, :] + c_max[:, :, :, None])
    qi = lax.broadcasted_iota(jnp.int32, (nb, nb), 0)
    kj = lax.broadcasted_iota(jnp.int32, (nb, nb), 1)
    active = jnp.logical_or(bound > UNDERFLOW_BOUND, kj >= qi)
    last_q = jnp.max(jnp.where(active, qi, -1), axis=2)
    first_blk = jnp.min(jnp.where(active, kj, nb), axis=3)
    first_kv = jnp.min(first_blk.reshape(N_PAIR, 2, T // tq, tq // tb), axis=-1)
    return first_kv.astype(F32), last_q.astype(F32)


def _attn_fwd(qkv3, cp3, first_kv, name):
    T = qkv3.shape[1]
    tb = min(ATT_BLOCK, T)
    tq = min(ATT_Q_BLOCK, T)
    nb = T // tb
    per_q = tq // tb

    assert per_q % 2 == 0 or T == tq, (T, tq, tb)

    def body(q_ref, k_ref, v_ref, cp_ref, first_ref, o_ref, lse_ref, st_a, st_b):
        h, i = pl.program_id(0), pl.program_id(1)
        lane = lax.broadcasted_iota(jnp.int32, (tq, LANES), 1)
        lane_k = lax.broadcasted_iota(jnp.int32, (tb, LANES), 1)
        feat = lax.broadcasted_iota(jnp.int32, (LANES, tq), 0)
        q2 = q_ref[0] * QK_SCALE
        qa, own_k, one_k = [], [], []
        for hh in range(2):
            bias = jnp.logical_and(lane >= _spare(hh), lane < _spare(hh) + BIAS_LANES)
            qa.append(jnp.where(_own_lanes((tq, LANES), hh), q2, jnp.where(bias, 1.0, 0.0).astype(BF16)))
            own_k.append(_own_lanes((tb, LANES), hh))
            one_k.append(jnp.where(lane_k == _spare(hh), 1.0, 0.0).astype(BF16))

        def scores(j, scr, heads=(0, 1)):
            off = pl.multiple_of(j * tb, tb)
            kb, cb = k_ref[0, pl.ds(off, tb), :], cp_ref[0, pl.ds(off, tb), :]
            for hh in heads:
                scr[hh] = lax.dot_general(jnp.where(own_k[hh], kb, cb), qa[hh], NT_DIMS, preferred_element_type=F32)

        def step(j, carry, first, scr=None, heads=(0, 1)):
            off = pl.multiple_of(j * tb, tb)
            kb, vb, cb = k_ref[0, pl.ds(off, tb), :], v_ref[0, pl.ds(off, tb), :], cp_ref[0, pl.ds(off, tb), :]
            lo = 0 if first is None else first
            out = list(carry)
            for hh in heads:
                m_all, acc_all = carry[2 * hh], carry[2 * hh + 1]
                m_old, acc = m_all[:, lo:], acc_all[:, lo:]
                if scr is None:
                    st = lax.dot_general(jnp.where(own_k[hh], kb, cb), qa[hh][lo:], NT_DIMS,
                                         preferred_element_type=F32)
                else:
                    st = scr[hh]
                if first is not None:
                    key = lax.broadcasted_iota(jnp.int32, (tb, tq - lo), 0)
                    qry = lax.broadcasted_iota(jnp.int32, (tb, tq - lo), 1)
                    st = jnp.where(key <= qry, st, -jnp.inf)
                m = jnp.maximum(m_old, jnp.max(st, axis=0, keepdims=True))
                p = jnp.exp(st - m)
                acc = jnp.exp(m_old - m) * acc + lax.dot_general(
                    jnp.where(own_k[hh], vb, one_k[hh]), p.astype(BF16), TN_DIMS, preferred_element_type=F32)
                if lo:
                    m = jnp.concatenate([m_all[:, :lo], m], axis=1)
                    acc = jnp.concatenate([acc_all[:, :lo], acc], axis=1)
                out[2 * hh], out[2 * hh + 1] = m, acc
            return tuple(out)

        ninf = jnp.full((1, tq), -jnp.inf, F32)
        zacc = jnp.zeros((LANES, tq), F32)

        def make_pair(heads):
            def pair(jj, c):
                j = 2 * jj
                scores(j + 1, st_b, heads)
                c = step(j, c, None, st_a, heads)
                scores(j + 2, st_a, heads)
                return step(j + 1, c, None, st_b, heads)
            return pair

        end_pair = (i * per_q) // 2
        first_pair = [jnp.clip(first_ref[h, hh, i].astype(jnp.int32), 0, i * per_q) // 2 for hh in range(2)]
        both_pair = jnp.maximum(first_pair[0], first_pair[1])
        carry = (ninf, zacc, ninf, zacc)
        for hh in range(2):
            scores(2 * first_pair[hh], st_a, (hh,))
            carry = lax.fori_loop(first_pair[hh], both_pair, make_pair((hh,)), carry)
        scores(2 * both_pair, st_a)
        carry = lax.fori_loop(both_pair, end_pair, make_pair((0, 1)), carry)
        carry = step(i * per_q, carry, 0, st_a)
        for t in range(1, per_q):
            carry = step(i * per_q + t, carry, t * tb)
        outs = []
        for hh in range(2):
            m, acc = carry[2 * hh], carry[2 * hh + 1]
            l = jnp.sum(jnp.where(feat == _spare(hh), acc, 0.0), axis=0, keepdims=True)
            lse_ref[0, 0, hh:hh + 1, :] = m + jnp.log(l)
            outs.append(acc * (1.0 / l))
        o_ref[0] = jnp.where(feat < HEAD_DIM, outs[0], outs[1]).astype(BF16)

    res = lambda base: pl.BlockSpec((1, T, LANES), lambda h, i: (base + h, 0, 0), pipeline_mode=pl.Buffered(1))
    return pl.pallas_call(
        body,
        name=name,
        grid=(N_PAIR, T // tq),
        in_specs=[pl.BlockSpec((1, tq, LANES), lambda h, i: (h, i, 0)), res(N_PAIR), res(2 * N_PAIR), res(0),
                  pl.BlockSpec(memory_space=pltpu.SMEM)],
        out_specs=[pl.BlockSpec((1, LANES, tq), lambda h, i: (h, 0, i)),
                   pl.BlockSpec((1, 1, 2, tq), lambda h, i: (h, i, 0, 0))],
        out_shape=[jax.ShapeDtypeStruct((N_PAIR, LANES, T), BF16),
                   jax.ShapeDtypeStruct((N_PAIR, T // tq, 2, tq), F32)],
        scratch_shapes=[pltpu.VMEM((2, tb, tq), F32), pltpu.VMEM((2, tb, tq), F32)],
        compiler_params=_params(("parallel", "arbitrary"), 56),
    )(qkv3, qkv3, qkv3, cp3, first_kv)


def _attn_bwd(qkv3, cp3, kst4, do3, st4, last_q, name):
    T = qkv3.shape[1]
    tb = min(ATT_BLOCK, T)
    nb = T // tb

    def body(q_ref, do_ref, st_ref, k_ref, v_ref, cp_ref, kst_ref, last_ref, dq_hbm, rs_ref, dk_ref, dv_ref, cs_ref,
             dq_acc, dk_acc, dv_acc):
        h, j = pl.program_id(0), pl.program_id(1)
        lane = lax.broadcasted_iota(jnp.int32, (tb, LANES), 1)
        own = [_own_lanes((tb, LANES), hh) for hh in range(2)]
        bias = [jnp.logical_and(lane >= _spare(hh), lane < _spare(hh) + BIAS_LANES) for hh in range(2)]
        key = lax.broadcasted_iota(jnp.int32, (tb, tb), 0)
        qry = lax.broadcasted_iota(jnp.int32, (tb, tb), 1)

        @pl.when(j == 0)
        def _():
            dq_acc[...] = jnp.zeros(dq_acc.shape, F32)
            rs_ref[...] = jnp.zeros(rs_ref.shape, F32)

        vb = v_ref[0]
        zero = jnp.zeros_like(vb)
        one = jnp.ones_like(vb)
        bias_one = [jnp.where(bias[hh], one, zero) for hh in range(2)]
        kb = [jnp.where(own[hh], k_ref[0], cp_ref[0]) for hh in range(2)]
        kst = kst_ref[0, 0]
        kst = [jnp.where(_own_lanes((LANES, tb), hh, axis=0), kst, jnp.zeros_like(kst)) for hh in range(2)]
        vm = [jnp.where(own[hh], vb, zero) for hh in range(2)]
        dk_acc[...] = jnp.zeros(dk_acc.shape, F32)
        dv_acc[...] = jnp.zeros(dv_acc.shape, F32)

        def step(i, masked, heads=(0, 1)):
            off = pl.multiple_of(i * tb, tb)
            qb = q_ref[0, pl.ds(off, tb), :] * QK_SCALE
            dob = do_ref[0, pl.ds(off, tb), :]
            dq = None
            for hh in heads:
                st = lax.dot_general(kb[hh], jnp.where(own[hh], qb, bias_one[hh]), NT_DIMS,
                                     preferred_element_type=F32)
                if masked:
                    st = jnp.where(key <= qry, st, -jnp.inf)
                p = jnp.exp(st - st_ref[0, i, hh:hh + 1, :])
                dp = lax.dot_general(vm[hh], dob, NT_DIMS, preferred_element_type=F32)
                dsb = (p * (dp - st_ref[0, i, 2 + hh:3 + hh, :])).astype(BF16)
                dv_acc[hh] += jnp.dot(p.astype(BF16), dob, preferred_element_type=F32)
                dk_acc[hh] += jnp.dot(dsb, jnp.where(own[hh], qb, one), preferred_element_type=F32)
                rs_ref[0, i, hh:hh + 1, :] += jnp.sum(dsb.astype(F32), axis=0, keepdims=True)
                d = jnp.dot(kst[hh], dsb, preferred_element_type=F32)
                dq = d if dq is None else dq + d
            dq_acc[i] += dq

        step(j, True)

        def make_body(heads):
            def loop_body(i, carry):
                step(i, False, heads)
                return carry
            return loop_body

        last = [jnp.clip(last_ref[h, hh, j].astype(jnp.int32), j, nb - 1) for hh in range(2)]
        both = jnp.minimum(last[0], last[1])
        lax.fori_loop(j + 1, both + 1, make_body((0, 1)), 0)
        for hh in range(2):
            lax.fori_loop(both + 1, last[hh] + 1, make_body((hh,)), 0)
        dk_ref[0] = jnp.where(own[0], dk_acc[0], dk_acc[1]).astype(BF16)
        dv_ref[0] = jnp.where(own[0], dv_acc[0], dv_acc[1]).astype(BF16)
        lane8 = lax.broadcasted_iota(jnp.int32, (8, LANES), 1)
        for hh in range(2):
            pick = jnp.where(lane8 == _spare(hh), 1.0, 0.0).astype(BF16)
            x = dk_acc[hh]
            hi = x.astype(BF16)
            r1 = x - hi.astype(F32)
            mid = r1.astype(BF16)
            low = (r1 - mid.astype(F32)).astype(BF16)
            cs_ref[0, 0, 8 * hh:8 * hh + 8, :] = (
                lax.dot_general(pick, hi, NT_DIMS, preferred_element_type=F32)
                + lax.dot_general(pick, mid, NT_DIMS, preferred_element_type=F32)
                + lax.dot_general(pick, low, NT_DIMS, preferred_element_type=F32))

        @pl.when(j == nb - 1)
        def _():
            pltpu.sync_copy(dq_acc, dq_hbm.at[h])

    res = pl.BlockSpec((1, T, LANES), lambda h, j: (h, 0, 0), pipeline_mode=pl.Buffered(1))
    tile = lambda base: pl.BlockSpec((1, tb, LANES), lambda h, j: (base + h, j, 0))
    rows = lambda n: pl.BlockSpec((1, nb, n, tb), lambda h, j: (h, 0, 0, 0))
    return pl.pallas_call(
        body,
        name=name,
        grid=(N_PAIR, nb),
        in_specs=[res, res, rows(4), tile(N_PAIR), tile(2 * N_PAIR), tile(0),
                  pl.BlockSpec((1, 1, LANES, tb), lambda h, j: (h, j, 0, 0)),
                  pl.BlockSpec(memory_space=pltpu.SMEM)],
        out_specs=[pl.BlockSpec(memory_space=pl.ANY), rows(2), tile(0), tile(0),
                   pl.BlockSpec((1, 1, 16, tb), lambda h, j: (h, j, 0, 0))],
        out_shape=[
            jax.ShapeDtypeStruct((N_PAIR, nb, LANES, tb), F32),
            jax.ShapeDtypeStruct((N_PAIR, nb, 2, tb), F32),
            jax.ShapeDtypeStruct((N_PAIR, T, LANES), BF16),
            jax.ShapeDtypeStruct((N_PAIR, T, LANES), BF16),
            jax.ShapeDtypeStruct((N_PAIR, nb, 16, tb), F32),
        ],
        scratch_shapes=[pltpu.VMEM((nb, LANES, tb), F32), pltpu.VMEM((2, tb, LANES), F32),
                        pltpu.VMEM((2, tb, LANES), F32)],
        compiler_params=_params(("arbitrary", "arbitrary"), 56),
    )(qkv3, do3, st4, qkv3, qkv3, cp3, kst4, last_q)


def _attn_out(o3, w, x, name):
    T = x.shape[0]

    def body(o_ref, w_ref, x_ref, out_ref):
        out_ref[...] = x_ref[...] + jnp.dot(_cat_groups(o_ref, N_PAIR), w_ref[...], preferred_element_type=F32)

    return _rowcall(name, body, T, min(512, T), [(o3, "grp"), _wkind(w), (x, "row")],
                    [((T, D_MODEL), F32, "row")], 24)[0]


def _attn_dout(dx, wt, o3, name):
    T = dx.shape[0]
    tm = min(512, T)

    def body(dx_ref, w_ref, o_ref, do_ref, dd_ref):
        do = jnp.dot(dx_ref[...].astype(BF16), w_ref[...], preferred_element_type=F32).astype(BF16)
        lo = _own_lanes((tm, LANES), 0)
        head = lax.broadcasted_iota(jnp.int32, (tm, 2 * N_PAIR), 1)
        dd = jnp.zeros((tm, 2 * N_PAIR), F32)
        for t in range(N_PAIR):
            d = do[:, LANES * t:LANES * (t + 1)]
            do_ref[t] = d
            prod = d.astype(F32) * o_ref[t].astype(F32)
            d0 = jnp.sum(jnp.where(lo, prod, 0.0), axis=1, keepdims=True)
            d1 = jnp.sum(jnp.where(lo, 0.0, prod), axis=1, keepdims=True)
            dd = jnp.where(head == 2 * t, d0, jnp.where(head == 2 * t + 1, d1, dd))
        dd_ref[...] = dd

    return _rowcall(name, body, T, tm, [(dx, "row"), _wkind(wt), (o3, "grp")],
                    [((N_PAIR, T, LANES), BF16, "grp"), ((T, 2 * N_PAIR), F32, "row")], 32)


def _ffn_out(gu, w, x, name):
    T = x.shape[0]
    tm = min(512, T)

    def body(gu_ref, w_ref, x_ref, out_ref, hid_ref):
        acc = x_ref[...]
        for j in range(N_FFN_TILE):
            g = gu_ref[:, 2 * FFN_TILE * j:2 * FFN_TILE * j + FFN_TILE].astype(F32)
            u = gu_ref[:, 2 * FFN_TILE * j + FFN_TILE:2 * FFN_TILE * (j + 1)].astype(F32)
            hj = (g * jax.nn.sigmoid(g) * u).astype(BF16)
            hid_ref[:, FFN_TILE * j:FFN_TILE * (j + 1)] = hj
            acc = acc + jnp.dot(hj, w_ref[FFN_TILE * j:FFN_TILE * (j + 1), :], preferred_element_type=F32)
        out_ref[...] = acc

    return _rowcall(name, body, T, tm, [(gu, "row"), _wkind(w), (x, "row")],
                    [((T, D_MODEL), F32, "row"), ((T, FFN_H), BF16, "row")], 48)


def _ffn_dgu(dx, wt, gu, name):
    T = dx.shape[0]
    tm = min(512, T)

    def body(dx_ref, w_ref, gu_ref, dgu_ref):
        dxb = dx_ref[...].astype(BF16)
        for j in range(N_FFN_TILE):
            dh = jnp.dot(dxb, w_ref[:, FFN_TILE * j:FFN_TILE * (j + 1)], preferred_element_type=F32)
            g = gu_ref[:, 2 * FFN_TILE * j:2 * FFN_TILE * j + FFN_TILE].astype(F32)
            u = gu_ref[:, 2 * FFN_TILE * j + FFN_TILE:2 * FFN_TILE * (j + 1)].astype(F32)
            sg = jax.nn.sigmoid(g)
            dgu_ref[:, 2 * FFN_TILE * j:2 * FFN_TILE * j + FFN_TILE] = (
                dh * u * (sg * (1.0 + g * (1.0 - sg)))).astype(BF16)
            dgu_ref[:, 2 * FFN_TILE * j + FFN_TILE:2 * FFN_TILE * (j + 1)] = (dh * (g * sg)).astype(BF16)

    return _rowcall(name, body, T, tm, [(dx, "row"), _wkind(wt), (gu, "row")],
                    [((T, 2 * FFN_H), BF16, "row")], 48)[0]


def _sgu_core(a, ln_g, ln_b, w_s, bst, w, x, name):
    T = x.shape[0]
    tm = min(256, T)

    def body(a_ref, lg_ref, lb_ref, ws_ref, bs_ref, w_ref, x_ref, out_ref, gated_ref, vn_ref, mixed_ref):
        v = _gelu(a_ref[:, SGU_W:].astype(F32))
        mu = jnp.mean(v, axis=-1, keepdims=True)
        vc = v - mu
        rstd = lax.rsqrt(jnp.mean(vc * vc, axis=-1, keepdims=True) + LN_EPS)
        vn_ref[...] = (vc * rstd * lg_ref[...] + lb_ref[...]).astype(BF16)
        tt = lax.broadcasted_iota(jnp.int32, (CHUNK, CHUNK), 0)
        ss = lax.broadcasted_iota(jnp.int32, (CHUNK, CHUNK), 1)
        for g in range(SGU_G):
            wg = jnp.where(tt >= ss, ws_ref[g], 0.0).astype(BF16)
            bcol = _lane_col(bs_ref[...], g)
            cols = slice(CHUNK * g, CHUNK * (g + 1))
            for c in range(tm // CHUNK):
                rows = slice(CHUNK * c, CHUNK * (c + 1))
                mixed = jnp.dot(wg, vn_ref[rows, cols], preferred_element_type=F32) + bcol
                u = _gelu(a_ref[rows, cols].astype(F32))
                mixed_ref[rows, cols] = mixed.astype(BF16)
                gated_ref[rows, cols] = (u * mixed).astype(BF16)
        out_ref[...] = x_ref[...] + jnp.dot(gated_ref[...], w_ref[...], preferred_element_type=F32)

    ins = [(a, "row"), (ln_g, "full"), (ln_b, "full"), (w_s, "full"), (bst, "full"), _wkind(w), (x, "row")]
    outs = [((T, D_MODEL), F32, "row")] + [((T, SGU_W), BF16, "row")] * 3
    return _rowcall(name, body, T, tm, ins, outs, 40)


def _sgu_core_bwd(dx, wt, a, vn, mixed, ln_g, w_s, name):
    T = dx.shape[0]
    tm = min(256, T)

    def body(dx_ref, wt_ref, a_ref, vn_ref, mx_ref, lg_ref, ws_ref,
             da_ref, dws_ref, dba_ref, dlg_ref, dlb_ref, dg_scr, dvn_scr):
        @pl.when(pl.program_id(0) == 0)
        def _():
            dws_ref[...] = jnp.zeros(dws_ref.shape, F32)
            dba_ref[...] = jnp.zeros(dba_ref.shape, F32)
            dlg_ref[...] = jnp.zeros(dlg_ref.shape, F32)
            dlb_ref[...] = jnp.zeros(dlb_ref.shape, F32)

        dg_scr[...] = jnp.dot(dx_ref[...].astype(BF16), wt_ref[...], preferred_element_type=F32)
        tt = lax.broadcasted_iota(jnp.int32, (CHUNK, CHUNK), 0)
        ss = lax.broadcasted_iota(jnp.int32, (CHUNK, CHUNK), 1)
        tril = tt >= ss
        for g in range(SGU_G):
            wg = jnp.where(tril, ws_ref[g], 0.0).astype(BF16)
            cols = slice(CHUNK * g, CHUNK * (g + 1))
            for c in range(tm // CHUNK):
                rows = slice(CHUNK * c, CHUNK * (c + 1))
                dgb = dg_scr[rows, cols]
                au = a_ref[rows, cols].astype(F32)
                dmx = dgb * _gelu(au)
                da_ref[rows, cols] = (dgb * mx_ref[rows, cols].astype(F32) * _gelu_grad(au)).astype(BF16)
                dmb = dmx.astype(BF16)
                dvn_scr[rows, cols] = lax.dot_general(wg, dmb, TN_DIMS, preferred_element_type=F32)
                dws_ref[g] += jnp.where(
                    tril, lax.dot_general(dmb, vn_ref[rows, cols], NT_DIMS, preferred_element_type=F32), 0.0)
                dba_ref[:, cols] += dmx
        av = a_ref[:, SGU_W:].astype(F32)
        v = _gelu(av)
        mu = jnp.mean(v, axis=-1, keepdims=True)
        vc = v - mu
        rstd = lax.rsqrt(jnp.mean(vc * vc, axis=-1, keepdims=True) + LN_EPS)
        xhat = vc * rstd
        dvn = dvn_scr[...]
        dlg_ref[...] += jnp.sum(dvn * xhat, axis=0, keepdims=True)
        dlb_ref[...] += jnp.sum(dvn, axis=0, keepdims=True)
        dxh = dvn * lg_ref[...]
        dv = rstd * (dxh - jnp.mean(dxh, axis=-1, keepdims=True)
                     - xhat * jnp.mean(dxh * xhat, axis=-1, keepdims=True))
        da_ref[:, SGU_W:] = (dv * _gelu_grad(av)).astype(BF16)

    ins = [(dx, "row"), _wkind(wt), (a, "row"), (vn, "row"), (mixed, "row"), (ln_g, "full"), (w_s, "full")]
    outs = [((T, 2 * SGU_W), BF16, "row"), ((SGU_G, CHUNK, CHUNK), F32, "full"), ((CHUNK, SGU_W), F32, "full"),
            ((1, SGU_W), F32, "full"), ((1, SGU_W), F32, "full")]
    return _rowcall(name, body, T, tm, ins, outs, 40,
                    scratch=[pltpu.VMEM((tm, SGU_W), F32), pltpu.VMEM((tm, SGU_W), F32)])


def _loss_head(x, wf, tgt, name):
    T = x.shape[0]
    tm = min(512, T)

    def body(x_ref, wf_ref, tgt_ref, dx_ref, dwf_ref, loss_ref):
        @pl.when(pl.program_id(0) == 0)
        def _():
            dwf_ref[...] = jnp.zeros(dwf_ref.shape, F32)
            loss_ref[...] = jnp.zeros(loss_ref.shape, F32)

        xv = x_ref[...]
        r = lax.rsqrt(jnp.mean(xv * xv, axis=-1, keepdims=True) + NORM_EPS)
        xn = xv * r
        err = xn * wf_ref[...] - tgt_ref[...]
        loss_ref[...] += 0.5 * jnp.sum(jnp.mean(err * err, axis=-1, keepdims=True), axis=0, keepdims=True)
        dy = err * (1.0 / D_MODEL)
        dwf_ref[...] += jnp.sum(dy * xn, axis=0, keepdims=True)
        dyw = dy * wf_ref[...]
        dx_ref[...] = r * (dyw - xn * jnp.mean(dyw * xn, axis=-1, keepdims=True))

    return _rowcall(name, body, T, tm, [(x, "row"), (wf, "full"), (tgt, "row")],
                    [((T, D_MODEL), F32, "row"), ((1, D_MODEL), F32, "full"), ((8, LANES), F32, "full")], 32)


def _peers():
    x, y, c = lax.axis_index("x"), lax.axis_index("y"), lax.axis_index("c")
    peers = []
    for p in range(1, N_DEV):
        px = 1 - x if p & 4 else x
        py = 1 - y if p & 2 else y
        pc = 1 - c if p & 1 else c
        peers.append((4 * px + 2 * py + pc, (px, py, pc)))
    return 4 * x + 2 * y + c, peers


def _all_gather(arrs, name):
    n = len(arrs)
    hbm = pl.BlockSpec(memory_space=pl.ANY)

    def body(*refs):
        ins, outs = refs[:n], refs[n:2 * n]
        send_sems, recv_sems, local_sems = refs[2 * n:]
        me, peers = _peers()
        sends, recvs, locals_ = [], [], []
        for t in range(n):
            cp = pltpu.make_async_copy(ins[t], outs[t].at[me], local_sems.at[t])
            cp.start()
            locals_.append(cp)
            for k, (pidx, pid) in enumerate(peers):
                s = t * (N_DEV - 1) + k
                send = pltpu.make_async_remote_copy(
                    src_ref=ins[t], dst_ref=outs[t].at[me], send_sem=send_sems.at[s], recv_sem=recv_sems.at[s],
                    device_id=pid, device_id_type=MESH)
                send.start()
                sends.append(send)
                recvs.append(pltpu.make_async_remote_copy(
                    src_ref=ins[t], dst_ref=outs[t].at[pidx], send_sem=send_sems.at[s], recv_sem=recv_sems.at[s],
                    device_id=pid, device_id_type=MESH))
        for r in recvs:
            r.wait_recv()
        for s in sends:
            s.wait_send()
        for cp in locals_:
            cp.wait()

    return pl.pallas_call(
        body,
        name=name,
        in_specs=[hbm] * n,
        out_specs=[hbm] * n,
        out_shape=[jax.ShapeDtypeStruct((N_DEV,) + a.shape, a.dtype) for a in arrs],
        scratch_shapes=[pltpu.SemaphoreType.DMA((n * (N_DEV - 1),)), pltpu.SemaphoreType.DMA((n * (N_DEV - 1),)),
                        pltpu.SemaphoreType.DMA((n,))],
    )(*arrs)


def _exchange(arrs, name):
    n = len(arrs)
    hbm = pl.BlockSpec(memory_space=pl.ANY)

    def body(*refs):
        ins, outs = refs[:n], refs[n:2 * n]
        send_sems, recv_sems, local_sems = refs[2 * n:]
        me, peers = _peers()
        sends, recvs, locals_ = [], [], []
        for t in range(n):
            cp = pltpu.make_async_copy(ins[t].at[me], outs[t].at[me], local_sems.at[t])
            cp.start()
            locals_.append(cp)
            for k, (pidx, pid) in enumerate(peers):
                s = t * (N_DEV - 1) + k
                send = pltpu.make_async_remote_copy(
                    src_ref=ins[t].at[pidx], dst_ref=outs[t].at[me], send_sem=send_sems.at[s],
                    recv_sem=recv_sems.at[s], device_id=pid, device_id_type=MESH)
                send.start()
                sends.append(send)
                recvs.append(pltpu.make_async_remote_copy(
                    src_ref=ins[t].at[pidx], dst_ref=outs[t].at[pidx], send_sem=send_sems.at[s],
                    recv_sem=recv_sems.at[s], device_id=pid, device_id_type=MESH))
        for r in recvs:
            r.wait_recv()
        for s in sends:
            s.wait_send()
        for cp in locals_:
            cp.wait()

    return pl.pallas_call(
        body,
        name=name,
        in_specs=[hbm] * n,
        out_specs=[hbm] * n,
        out_shape=[jax.ShapeDtypeStruct(a.shape, a.dtype) for a in arrs],
        scratch_shapes=[pltpu.SemaphoreType.DMA((n * (N_DEV - 1),)), pltpu.SemaphoreType.DMA((n * (N_DEV - 1),)),
                        pltpu.SemaphoreType.DMA((n,))],
    )(*arrs)


def _row_tile(rows, cap):
    best = None
    for t in range(16, cap + 1, 16):
        if rows % t == 0:
            best = t
    assert best is not None, rows
    return best


def _adam_sum(parts, w, m, v, name):
    R, C = w.shape
    tr = _row_tile(R, 128)

    def body(p_ref, w_ref, m_ref, v_ref, g_ref, d_ref, nm_ref, nv_ref):
        g = p_ref[0].astype(F32)
        for s in range(1, N_DEV):
            g = g + p_ref[s].astype(F32)
        mm = ADAM_B1 * m_ref[...] + (1.0 - ADAM_B1) * g
        vv = ADAM_B2 * v_ref[...] + (1.0 - ADAM_B2) * (g * g)
        m_hat = mm / (1.0 - ADAM_B1 ** ADAM_STEP)
        v_hat = vv / (1.0 - ADAM_B2 ** ADAM_STEP)
        g_ref[...] = g
        d_ref[...] = -ADAM_LR * (m_hat / (jnp.sqrt(v_hat) + ADAM_EPS) + ADAM_WD * w_ref[...])
        nm_ref[...] = mm
        nv_ref[...] = vv

    mat = pl.BlockSpec((tr, C), lambda i: (i, 0))
    return pl.pallas_call(
        body,
        name=name,
        grid=(R // tr,),
        in_specs=[pl.BlockSpec((N_DEV, tr, C), lambda i: (0, i, 0)), mat, mat, mat],
        out_specs=[mat] * 4,
        out_shape=[jax.ShapeDtypeStruct((R, C), F32)] * 4,
        compiler_params=_params(("parallel",), 32),
    )(parts, w, m, v)


def _cols_from_gathered(g):
    _, L, K, n = g.shape
    return jnp.transpose(g, (1, 2, 0, 3)).reshape(L, K, N_DEV * n)


def _rows_from_gathered(g):
    _, L, k, N = g.shape
    return jnp.transpose(g, (1, 0, 2, 3)).reshape(L, N_DEV * k, N)


def _cols_to_blocks(dw):
    L, K, N = dw.shape
    n = N // N_DEV
    return jnp.transpose(dw.reshape(L, K, N_DEV, n), (2, 0, 1, 3)).reshape(N_DEV, L * K, n)


def _rows_to_blocks(dw):
    L, K, N = dw.shape
    k = K // N_DEV
    return jnp.transpose(dw.reshape(L, N_DEV, k, N), (1, 0, 2, 3)).reshape(N_DEV, L * k, N)


def _ffn_interleave(w):
    lead = w.shape[:-1]
    t = w.reshape(lead + (2, N_FFN_TILE, FFN_TILE))
    return jnp.swapaxes(t, -3, -2).reshape(lead + (2 * FFN_H,))


def _ffn_deinterleave(w):
    lead = w.shape[:-1]
    t = w.reshape(lead + (N_FFN_TILE, 2, FFN_TILE))
    return jnp.swapaxes(t, -3, -2).reshape(lead + (2 * FFN_H,))


def _pad_rows(a, rows=8):
    a = a.reshape(-1, a.shape[-1])
    return jnp.pad(a, ((0, rows - a.shape[0]), (0, 0)))


SMALL_ROWS = 6 * 8 + 2 * SGU_G * CHUNK * CHUNK // D_MODEL


def _pack_small(mixer, ffn, final, b_f, extra, w_s, b_s):
    bf_row = jnp.pad(b_f.reshape(1, -1), ((0, 0), (0, D_MODEL - b_f.size)))
    bs_rows = jnp.pad(b_s.reshape(4, -1), ((0, 0), (0, D_MODEL - b_s.size // 4)))
    return jnp.concatenate([
        _pad_rows(mixer), _pad_rows(ffn), _pad_rows(final.reshape(1, -1)), _pad_rows(bf_row),
        _pad_rows(extra), _pad_rows(bs_rows), w_s.reshape(-1, D_MODEL)], axis=0)


def _unpack_small(p):
    mixer, ffn, final = p[0:4], p[8:12], p[16]
    b_f = p[24, :32].reshape(2, 2 * N_PAIR)
    extra = p[32]
    b_s = p[40:44, :2 * SGU_G * CHUNK // 4].reshape(2, SGU_G, CHUNK)
    w_s = p[48:].reshape(2, SGU_G, CHUNK, CHUNK)
    return mixer, ffn, final, b_f, extra, w_s, b_s


def kernel(x, mixer_norm_w, attn_w_in, attn_b_f, attn_w_out, sgu_w_in, sgu_ln_g, sgu_ln_b, sgu_w_s, sgu_b_s, sgu_w_out, ffn_norm_w, ffn_w_in, ffn_w_out, final_norm_w, loss_target, m_mixer_norm_w, m_attn_w_in, m_attn_b_f, m_attn_w_out, m_sgu_w_in, m_sgu_ln_g, m_sgu_ln_b, m_sgu_w_s, m_sgu_b_s, m_sgu_w_out, m_ffn_norm_w, m_ffn_w_in, m_ffn_w_out, m_final_norm_w, v_mixer_norm_w, v_attn_w_in, v_attn_b_f, v_attn_w_out, v_sgu_w_in, v_sgu_ln_g, v_sgu_ln_b, v_sgu_w_s, v_sgu_b_s, v_sgu_w_out, v_ffn_norm_w, v_ffn_w_in, v_ffn_w_out, v_final_norm_w):
    T = x.shape[1]
    tb = min(ATT_BLOCK, T)
    xs, tgt = x[0], loss_target[0]

    shards = [attn_w_in, attn_w_out, sgu_w_in, sgu_w_out, ffn_w_in, ffn_w_out, sgu_ln_g, sgu_ln_b]
    gathered = _all_gather([s.astype(BF16) for s in shards[:6]] + shards[6:], "gather_weights")
    w_attn_in = _cols_from_gathered(gathered[0])
    w_attn_out = _rows_from_gathered(gathered[1])
    w_sgu_in = _cols_from_gathered(gathered[2])
    w_sgu_out = _rows_from_gathered(gathered[3])
    w_ffn_in = _ffn_interleave(_cols_from_gathered(gathered[4]))
    w_ffn_out = _rows_from_gathered(gathered[5])
    ln_g = jnp.transpose(gathered[6], (1, 0, 2)).reshape(2, 1, SGU_W)
    ln_b = jnp.transpose(gathered[7], (1, 0, 2)).reshape(2, 1, SGU_W)
    w_qkv = w_attn_in[:, :, :3 * D_MODEL]
    w_f = jnp.pad(w_attn_in[:, :, 3 * D_MODEL:], ((0, 0), (0, 0), (0, LANES - 2 * N_PAIR)))
    tr = lambda w: jnp.swapaxes(w, -1, -2)
    w_qkv_t6, w_f_t, w_attn_out_t = tr(w_qkv).reshape(6, D_MODEL, D_MODEL), tr(w_f), tr(w_attn_out)
    w_sgu_in_t, w_sgu_out_t, w_ffn_in_t, w_ffn_out_t = tr(w_sgu_in), tr(w_sgu_out), tr(w_ffn_in), tr(w_ffn_out)
    mixer_nw = mixer_norm_w.reshape(4, 1, D_MODEL)
    ffn_nw = ffn_norm_w.reshape(4, 1, D_MODEL)
    b_col = attn_b_f.reshape(2, 2 * N_PAIR, 1)
    bs_t = jnp.swapaxes(sgu_b_s, 1, 2)

    saved = []
    xr = xs
    for i in range(4):
        j = i // 2
        if i % 2 == 0:
            qkv3, h = _norm_matmul(xr, mixer_nw[i], (w_qkv, j), BF16, f"attn_qkv_{j}", 1024, groups=True)
            fl, _ = _norm_matmul(xr, mixer_nw[i], (w_f, j), F32, f"attn_gate_{j}", LANES)
            fl3 = jnp.transpose(fl[:, :2 * N_PAIR].reshape(T // CHUNK, CHUNK, 2 * N_PAIR), (0, 2, 1))
            c_chunks = _fgate_fwd(fl3, b_col[j], f"fgate_fwd_{j}")
            c16 = jnp.transpose(c_chunks, (1, 0, 2)).reshape(2 * N_PAIR, T)
            cp3 = _bias_pieces(c16)
            first_kv, last_q = _attn_reach(qkv3, c16, tb, min(ATT_Q_BLOCK, T))
            ot3, lse4 = _attn_fwd(qkv3, cp3, first_kv, f"attn_fwd_{j}")
            o3 = jnp.swapaxes(ot3, 1, 2)
            xm = _attn_out(o3, (w_attn_out, j), xr, f"attn_out_{j}")
            mix_saved = (xr, h, qkv3, fl3, cp3, o3, lse4, last_q)
        else:
            a, h = _norm_matmul(xr, mixer_nw[i], (w_sgu_in, j), BF16, f"sgu_in_{j}", 1024)
            xm, gated, vn, mixed = _sgu_core(a, ln_g[j], ln_b[j], sgu_w_s[j], bs_t[j], (w_sgu_out, j), xr,
                                             f"sgu_core_{j}")
            mix_saved = (xr, h, a, gated, vn, mixed)
        gu, h2 = _norm_matmul(xm, ffn_nw[i], (w_ffn_in, i), BF16, f"ffn_in_{i}", FFN_H // 2)
        xo, hid = _ffn_out(gu, (w_ffn_out, i), xm, f"ffn_out_{i}")
        saved.append((mix_saved, (xm, h2, gu, hid)))
        xr = xo
    dx, d_final, loss_part = _loss_head(xr, final_norm_w.reshape(1, D_MODEL), tgt, "loss_head")

    d_mixer_nw, d_ffn_nw = [None] * 4, [None] * 4
    d_attn_in, d_attn_out, d_bf, d_sgu_in, d_sgu_out = [None] * 2, [None] * 2, [None] * 2, [None] * 2, [None] * 2
    d_ws, d_bs, d_lg, d_lb = [None] * 2, [None] * 2, [None] * 2, [None] * 2
    d_ffn_in, d_ffn_out = [None] * 4, [None] * 4
    for i in reversed(range(4)):
        j = i // 2
        mix_saved, (xm, h2, gu, hid) = saved[i]
        dgu = _ffn_dgu(dx, (w_ffn_out_t, i), gu, f"ffn_dgu_{i}")
        d_ffn_out[i] = _matmul_tn(hid, dx, f"ffn_dwout_{i}", FFN_H // 2, D_MODEL)
        d_ffn_in[i] = _matmul_tn(h2, dgu, f"ffn_dwin_{i}", D_MODEL, FFN_H // 2)
        dx, d_ffn_nw[i] = _matmul_rms_bwd([(dgu, "row")], [(w_ffn_in_t, i)], xm, ffn_nw[i], dx, f"ffn_dx_{i}", 512, 48)
        if i % 2 == 0:
            xr, h, qkv3, fl3, cp3, o3, lse4, last_q = mix_saved
            do3, dd = _attn_dout(dx, (w_attn_out_t, j), o3, f"attn_dout_{j}")
            d_attn_out[j] = _matmul_tn(o3, dx, f"attn_dwout_{j}", D_MODEL, D_MODEL, a_grp=True)
            to_blocks = lambda a: jnp.swapaxes(a.reshape(N_PAIR, 2, T // tb, tb), 1, 2)
            from_blocks = lambda a: jnp.swapaxes(a, 1, 2).reshape(N_PAIR, 2, T)
            dd4 = to_blocks(dd.T.reshape(N_PAIR, 2, T))
            st4 = jnp.concatenate([to_blocks(from_blocks(lse4)), dd4], axis=2)
            kst4 = jnp.swapaxes((qkv3[N_PAIR:2 * N_PAIR] * QK_SCALE).reshape(N_PAIR, T // tb, tb, LANES), 2, 3)
            dqt4, rs4, dk3, dv3, cs4 = _attn_bwd(qkv3, cp3, kst4, do3, st4, last_q, f"attn_bwd_{j}")
            dq3 = jnp.swapaxes(dqt4, 2, 3).reshape(N_PAIR, T, LANES).astype(BF16)
            dc_pair = from_blocks(rs4 - jnp.stack([cs4[:, :, 0], cs4[:, :, 8]], axis=2))
            dc_chunks = jnp.transpose(dc_pair.reshape(2 * N_PAIR, T // CHUNK, CHUNK), (1, 0, 2))
            dfl3, db = _fgate_bwd(dc_chunks, fl3, b_col[j], f"fgate_bwd_{j}")
            d_bf[j] = db[:, 0]
            dfl = jnp.transpose(dfl3, (0, 2, 1)).reshape(T, 2 * N_PAIR)
            dfl = jnp.pad(dfl.astype(BF16), ((0, 0), (0, LANES - 2 * N_PAIR)))
            d_qkv = [_matmul_tn(h, d3, f"attn_dw{nm}_{j}", D_MODEL, D_MODEL, g_grp=True)
                     for nm, d3 in (("q", dq3), ("k", dk3), ("v", dv3))]
            d_f = _matmul_tn(h, dfl, f"attn_dwf_{j}", D_MODEL, LANES)[:, :2 * N_PAIR]
            d_attn_in[j] = jnp.concatenate(d_qkv + [d_f], axis=1)
            wts = [(w_qkv_t6, 3 * j + k) for k in range(3)] + [(w_f_t, j)]
            dx, d_mixer_nw[i] = _matmul_rms_bwd(
                [(dq3, "grp"), (dk3, "grp"), (dv3, "grp"), (dfl, "row")], wts, xr, mixer_nw[i], dx,
                f"attn_dx_{j}", 256, 40)
        else:
            xr, h, a, gated, vn, mixed = mix_saved
            da, d_ws[j], dba, d_lg[j], d_lb[j] = _sgu_core_bwd(dx, (w_sgu_out_t, j), a, vn, mixed, ln_g[j], sgu_w_s[j],
                                                               f"sgu_core_bwd_{j}")
            d_bs[j] = jnp.sum(dba.reshape(CHUNK, SGU_G, CHUNK), axis=-1).T
            d_sgu_out[j] = _matmul_tn(gated, dx, f"sgu_dwout_{j}", D_MODEL, D_MODEL)
            d_sgu_in[j] = _matmul_tn(h, da, f"sgu_dwin_{j}", D_MODEL, 1024)
            dx, d_mixer_nw[i] = _matmul_rms_bwd([(da, "row")], [(w_sgu_in_t, j)], xr, mixer_nw[i], dx,
                                                f"sgu_dx_{j}", 256, 40)
    grad_x = dx[None]

    rows4 = lambda parts: jnp.concatenate(parts, axis=1).reshape(4, D_MODEL)
    small_g = _pack_small(rows4(d_mixer_nw), rows4(d_ffn_nw), d_final[0], jnp.stack(d_bf),
                          loss_part[0:1, 0:1] * jnp.ones((1, D_MODEL), F32), jnp.stack(d_ws), jnp.stack(d_bs))
    zero_row = jnp.zeros((1, D_MODEL), F32)
    pack = lambda pre: _pack_small(pre[0], pre[1], pre[2], pre[3], zero_row, pre[4], pre[5])
    small_w = pack((mixer_norm_w, ffn_norm_w, final_norm_w, attn_b_f, sgu_w_s, sgu_b_s))
    small_m = pack((m_mixer_norm_w, m_ffn_norm_w, m_final_norm_w, m_attn_b_f, m_sgu_w_s, m_sgu_b_s))
    small_v = pack((v_mixer_norm_w, v_ffn_norm_w, v_final_norm_w, v_attn_b_f, v_sgu_w_s, v_sgu_b_s))
    small_all = _all_gather([small_g], "gather_small_grads")[0]
    small_out = [_unpack_small(p) for p in _adam_sum(small_all, small_w, small_m, small_v, "adam_small")]
    loss = small_out[0][4][0]

    blocks = [
        _cols_to_blocks(jnp.stack(d_attn_in)), _rows_to_blocks(jnp.stack(d_attn_out)),
        _cols_to_blocks(jnp.stack(d_sgu_in)), _rows_to_blocks(jnp.stack(d_sgu_out)),
        _cols_to_blocks(_ffn_deinterleave(jnp.stack(d_ffn_in))), _rows_to_blocks(jnp.stack(d_ffn_out)),
        jnp.stack(d_lg).reshape(2, N_DEV, 1, SGU_W // N_DEV).transpose(1, 0, 2, 3).reshape(N_DEV, 2, SGU_W // N_DEV),
        jnp.stack(d_lb).reshape(2, N_DEV, 1, SGU_W // N_DEV).transpose(1, 0, 2, 3).reshape(N_DEV, 2, SGU_W // N_DEV),
    ]
    received = _exchange([b.astype(BF16) for b in blocks[:6]] + blocks[6:], "exchange_grads")
    names = ["attn_w_in", "attn_w_out", "sgu_w_in", "sgu_w_out", "ffn_w_in", "ffn_w_out"]
    ws = [attn_w_in, attn_w_out, sgu_w_in, sgu_w_out, ffn_w_in, ffn_w_out]
    ms = [m_attn_w_in, m_attn_w_out, m_sgu_w_in, m_sgu_w_out, m_ffn_w_in, m_ffn_w_out]
    vs = [v_attn_w_in, v_attn_w_out, v_sgu_w_in, v_sgu_w_out, v_ffn_w_in, v_ffn_w_out]
    big_out = {}
    for nm, rec, w, m, v in zip(names, received[:6], ws, ms, vs):
        flat = lambda a: a.reshape(-1, a.shape[-1])
        big_out[nm] = [o.reshape(w.shape) for o in _adam_sum(rec, flat(w), flat(m), flat(v), f"adam_{nm}")]
    pad8 = lambda a: jnp.pad(a, [(0, 0)] * (a.ndim - 2) + [(0, 8 - a.shape[-2]), (0, 0)])
    ln_parts = jnp.concatenate([pad8(received[6]), pad8(received[7])], axis=1)
    ln_pack = lambda g, b: jnp.concatenate([pad8(g), pad8(b)], axis=0)
    ln_out = _adam_sum(ln_parts, ln_pack(sgu_ln_g, sgu_ln_b), ln_pack(m_sgu_ln_g, m_sgu_ln_b),
                       ln_pack(v_sgu_ln_g, v_sgu_ln_b), "adam_sgu_ln")

    def leaf(kind):
        mixer, ffn, final, b_f, _, w_s, b_s = small_out[kind]
        o = lambda nm: big_out[nm][kind]
        return [mixer, o("attn_w_in"), b_f, o("attn_w_out"), o("sgu_w_in"), ln_out[kind][0:2], ln_out[kind][8:10],
                w_s, b_s, o("sgu_w_out"), ffn, o("ffn_w_in"), o("ffn_w_out"), final]

    return (loss, grad_x, *leaf(0), *leaf(1), *leaf(2), *leaf(3))
```

```python
import functools

import jax
import jax.numpy as jnp
from jax import lax
from jax.experimental import pallas as pl
from jax.experimental.pallas import tpu as pltpu

F32 = jnp.float32
BF16 = jnp.bfloat16

D_MODEL = 1024
HEAD_DIM = 64
N_PAIR = 8
LANES = 128
SGU_W = 2048
SGU_G = 16
CHUNK = 128
FFN_H = 2816
FFN_TILE = 256
N_FFN_TILE = FFN_H // FFN_TILE
NORM_EPS = 1e-6
LN_EPS = 1e-5
QK_SCALE = 0.125
ATT_BLOCK = 512
N_DEV = 8
ADAM_LR = 0.001
ADAM_B1 = 0.9
ADAM_B2 = 0.999
ADAM_EPS = 1e-08
ADAM_WD = 0.01
ADAM_STEP = 10
MESH = pl.DeviceIdType.MESH
SQRT_HALF = 0.7071067811865476
INV_SQRT_2PI = 0.3989422804014327

NT_DIMS = (((1,), (1,)), ((), ()))
TN_DIMS = (((0,), (0,)), ((), ()))


def _gelu(x):
    return 0.5 * x * (1.0 + lax.erf(x * SQRT_HALF))


def _gelu_grad(x):
    return 0.5 * (1.0 + lax.erf(x * SQRT_HALF)) + x * jnp.exp(-0.5 * x * x) * INV_SQRT_2PI


def _lane_col(v, lane):
    idx = lax.broadcasted_iota(jnp.int32, v.shape, 1)
    return jnp.sum(jnp.where(idx == lane, v, 0.0), axis=1, keepdims=True)


def _params(sem, vmem_mb):
    return pltpu.CompilerParams(dimension_semantics=sem, vmem_limit_bytes=vmem_mb << 20)


def _cat_groups(ref, n):
    if n == 1:
        return ref[0]
    return jnp.concatenate([ref[t] for t in range(n)], axis=1)


def _wkind(w):
    return (w[0], ("layer", w[1])) if isinstance(w, tuple) else (w, "full")


def _rowcall(name, body, T, tm, ins, outs, vmem_mb, scratch=()):
    def spec(shape, kind, resident_once=False):
        shape = tuple(shape)
        if isinstance(kind, tuple):
            layer = kind[1]
            return pl.BlockSpec((None,) + shape[1:], lambda i: (layer,) + (0,) * (len(shape) - 1),
                                pipeline_mode=pl.Buffered(1))
        if kind == "row":
            return pl.BlockSpec((tm,) + shape[1:], lambda i: (i,) + (0,) * (len(shape) - 1))
        if kind == "grp":
            return pl.BlockSpec((shape[0], tm, shape[2]), lambda i: (0, i, 0))
        if resident_once:
            return pl.BlockSpec(shape, lambda i: (0,) * len(shape), pipeline_mode=pl.Buffered(1))
        return pl.BlockSpec(shape, lambda i: (0,) * len(shape))

    return pl.pallas_call(
        body,
        name=name,
        grid=(T // tm,),
        in_specs=[spec(a.shape, k, True) for a, k in ins],
        out_specs=[spec(s, k) for s, _, k in outs],
        out_shape=[jax.ShapeDtypeStruct(tuple(s), d) for s, d, _ in outs],
        scratch_shapes=list(scratch),
        compiler_params=_params(("arbitrary",), vmem_mb),
    )(*[a for a, _ in ins])


def _norm_matmul(x, nw, w, out_dtype, name, tn, groups=False):
    w, layer = w
    T, N = x.shape[0], w.shape[2]
    tm = min(1024, T)

    def body(x_ref, nw_ref, w_ref, o_ref, h_ref, h_scr):
        @pl.when(pl.program_id(1) == 0)
        def _():
            xv = x_ref[...]
            r = lax.rsqrt(jnp.mean(xv * xv, axis=-1, keepdims=True) + NORM_EPS)
            hv = (xv * r * nw_ref[...]).astype(BF16)
            h_scr[...] = hv
            h_ref[...] = hv

        acc = jnp.dot(h_scr[...], w_ref[...], preferred_element_type=F32)
        if groups:
            for t in range(tn // LANES):
                o_ref[t] = acc[:, LANES * t:LANES * (t + 1)].astype(out_dtype)
        else:
            o_ref[...] = acc.astype(out_dtype)

    if groups:
        o_shape = (N // LANES, T, LANES)
        o_spec = pl.BlockSpec((tn // LANES, tm, LANES), lambda i, j: (j, i, 0))
    else:
        o_shape = (T, N)
        o_spec = pl.BlockSpec((tm, tn), lambda i, j: (i, j))
    return pl.pallas_call(
        body,
        name=name,
        grid=(T // tm, N // tn),
        in_specs=[
            pl.BlockSpec((tm, D_MODEL), lambda i, j: (i, 0)),
            pl.BlockSpec((1, D_MODEL), lambda i, j: (0, 0)),
            pl.BlockSpec((None, D_MODEL, tn), lambda i, j: (layer, 0, j)),
        ],
        out_specs=[o_spec, pl.BlockSpec((tm, D_MODEL), lambda i, j: (i, 0))],
        out_shape=[jax.ShapeDtypeStruct(o_shape, out_dtype), jax.ShapeDtypeStruct((T, D_MODEL), BF16)],
        scratch_shapes=[pltpu.VMEM((tm, D_MODEL), BF16)],
        compiler_params=_params(("arbitrary", "arbitrary"), 48),
    )(x, nw, w)


def _matmul_tn(a, g, name, tk, tn, a_grp=False, g_grp=False):
    T = a.shape[1] if a_grp else a.shape[0]
    K = a.shape[0] * LANES if a_grp else a.shape[1]
    N = g.shape[0] * LANES if g_grp else g.shape[1]
    tm = min(1024, T)

    def body(a_ref, g_ref, o_ref):
        @pl.when(pl.program_id(2) == 0)
        def _():
            o_ref[...] = jnp.zeros(o_ref.shape, F32)

        av = _cat_groups(a_ref, tk // LANES) if a_grp else a_ref[...]
        gv = _cat_groups(g_ref, tn // LANES) if g_grp else g_ref[...]
        o_ref[...] += lax.dot_general(av.astype(BF16), gv.astype(BF16), TN_DIMS, preferred_element_type=F32)

    if a_grp:
        a_spec = pl.BlockSpec((tk // LANES, tm, LANES), lambda k, n, m: (k, m, 0))
    else:
        a_spec = pl.BlockSpec((tm, tk), lambda k, n, m: (m, k))
    if g_grp:
        g_spec = pl.BlockSpec((tn // LANES, tm, LANES), lambda k, n, m: (n, m, 0))
    else:
        g_spec = pl.BlockSpec((tm, tn), lambda k, n, m: (m, n))
    return pl.pallas_call(
        body,
        name=name,
        grid=(K // tk, N // tn, T // tm),
        in_specs=[a_spec, g_spec],
        out_specs=pl.BlockSpec((tk, tn), lambda k, n, m: (k, n)),
        out_shape=jax.ShapeDtypeStruct((K, N), F32),
        compiler_params=_params(("parallel", "parallel", "arbitrary"), 48),
    )(a, g)


def _matmul_rms_bwd(a_list, wt_list, x, nw, dres, name, tm, vmem_mb):
    T = x.shape[0]
    n = len(a_list)

    def body(*refs):
        a_refs, w_refs = refs[:n], refs[n:2 * n]
        x_ref, nw_ref, dres_ref, dx_ref, dnw_ref = refs[2 * n:]

        @pl.when(pl.program_id(0) == 0)
        def _():
            dnw_ref[...] = jnp.zeros(dnw_ref.shape, F32)

        dh = None
        for (arr, kind), a_ref, w_ref in zip(a_list, a_refs, w_refs):
            av = _cat_groups(a_ref, arr.shape[0]) if kind == "grp" else a_ref[...]
            part = jnp.dot(av.astype(BF16), w_ref[...], preferred_element_type=F32)
            dh = part if dh is None else dh + part
        xv = x_ref[...]
        r = lax.rsqrt(jnp.mean(xv * xv, axis=-1, keepdims=True) + NORM_EPS)
        xn = xv * r
        dnw_ref[...] += jnp.sum(dh * xn, axis=0, keepdims=True)
        dyw = dh * nw_ref[...]
        dx_ref[...] = dres_ref[...] + r * (dyw - xn * jnp.mean(dyw * xn, axis=-1, keepdims=True))

    ins = list(a_list) + [_wkind(w) for w in wt_list] + [(x, "row"), (nw, "full"), (dres, "row")]
    outs = [((T, D_MODEL), F32, "row"), ((1, D_MODEL), F32, "full")]
    return _rowcall(name, body, T, tm, ins, outs, vmem_mb)


def _fgate_fwd(fl3, bcol, name):
    n_chunk = fl3.shape[0]

    def body(fl_ref, b_ref, c_ref):
        r = lax.broadcasted_iota(jnp.int32, (CHUNK, CHUNK), 0)
        t = lax.broadcasted_iota(jnp.int32, (CHUNK, CHUNK), 1)
        tri = jnp.where(r <= t, 1.0, 0.0).astype(BF16)

        def chunk(i, carry):
            z = fl_ref[i] + b_ref[...]
            lf = jnp.minimum(z, 0.0) - jnp.log(1.0 + jnp.exp(-jnp.abs(z)))
            hi = lf.astype(BF16)
            r1 = lf - hi.astype(F32)
            mid = r1.astype(BF16)
            low = (r1 - mid.astype(F32)).astype(BF16)
            cs = (jnp.dot(hi, tri, preferred_element_type=F32) + jnp.dot(mid, tri, preferred_element_type=F32)
                  + jnp.dot(low, tri, preferred_element_type=F32)) + carry
            c_ref[i] = cs
            return _lane_col(cs, CHUNK - 1)

        lax.fori_loop(0, n_chunk, chunk, jnp.zeros((2 * N_PAIR, 1), F32))

    return pl.pallas_call(
        body, name=name, out_shape=jax.ShapeDtypeStruct(fl3.shape, F32),
        compiler_params=pltpu.CompilerParams(vmem_limit_bytes=16 << 20),
    )(fl3, bcol)


def _fgate_bwd(dc3, fl3, bcol, name):
    n_chunk = fl3.shape[0]

    def body(dc_ref, fl_ref, b_ref, dfl_ref, db_ref):
        tt = lax.broadcasted_iota(jnp.int32, (CHUNK, CHUNK), 0)
        rr = lax.broadcasted_iota(jnp.int32, (CHUNK, CHUNK), 1)
        tri = jnp.where(tt >= rr, 1.0, 0.0).astype(BF16)

        def chunk(k, carry):
            tail, acc = carry
            i = n_chunk - 1 - k
            dc = dc_ref[i]
            hi = dc.astype(BF16)
            r1 = dc - hi.astype(F32)
            mid = r1.astype(BF16)
            low = (r1 - mid.astype(F32)).astype(BF16)
            dlf = (jnp.dot(hi, tri, preferred_element_type=F32) + jnp.dot(mid, tri, preferred_element_type=F32)
                   + jnp.dot(low, tri, preferred_element_type=F32)) + tail
            z = fl_ref[i] + b_ref[...]
            dfl = dlf / (1.0 + jnp.exp(z))
            dfl_ref[i] = dfl
            return _lane_col(dlf, 0), acc + dfl

        _, acc = lax.fori_loop(0, n_chunk, chunk,
                               (jnp.zeros((2 * N_PAIR, 1), F32), jnp.zeros((2 * N_PAIR, CHUNK), F32)))
        db_ref[...] = jnp.broadcast_to(jnp.sum(acc, axis=1, keepdims=True), db_ref.shape)

    return pl.pallas_call(
        body, name=name,
        out_shape=[jax.ShapeDtypeStruct(fl3.shape, F32), jax.ShapeDtypeStruct((2 * N_PAIR, LANES), F32)],
        compiler_params=pltpu.CompilerParams(vmem_limit_bytes=16 << 20),
    )(dc3, fl3, bcol)


BIAS_LANES = 3
ATT_Q_BLOCK = 2048


def _own_lanes(shape, hh, axis=1):
    idx = lax.broadcasted_iota(jnp.int32, shape, axis)
    return idx < HEAD_DIM if hh == 0 else idx >= HEAD_DIM


def _spare(hh):
    return HEAD_DIM * (1 - hh)


def _bias_pieces(c16):
    T = c16.shape[1]
    negc = -c16
    hi = negc.astype(BF16)
    r1 = negc - hi.astype(F32)
    mid = r1.astype(BF16)
    low = (r1 - mid.astype(F32)).astype(BF16)
    pieces = jnp.stack([hi, mid, low], axis=-1).reshape(N_PAIR, 2, T, BIAS_LANES)
    zpad = jnp.zeros((N_PAIR, T, HEAD_DIM - BIAS_LANES), BF16)
    return jnp.concatenate([pieces[:, 1], zpad, pieces[:, 0], zpad], axis=-1)


UNDERFLOW_BOUND = -110.0


def _attn_reach(qkv3, c16, tb, tq):
    T = qkv3.shape[1]
    nb = T // tb

    def block_norm(a):
        sq = jnp.sum(jnp.square(a.astype(F32)).reshape(N_PAIR, nb, tb, 2, HEAD_DIM), axis=-1)
        return jnp.transpose(jnp.sqrt(jnp.max(sq, axis=2)), (0, 2, 1))

    qn, kn = block_norm(qkv3[:N_PAIR]), block_norm(qkv3[N_PAIR:2 * N_PAIR])
    cb = c16.reshape(N_PAIR, 2, nb, tb)
    c_max, negc_max = jnp.max(cb, axis=-1), jnp.max(-cb, axis=-1)
    bound = (qn[:, :, :, None] * (kn[:, :, None, :] + kn[:, :, :, None]) * QK_SCALE
             + negc_max[:, :, None, :] + c_max[:, :, :, None])
    qi = lax.broadcasted_iota(jnp.int32, (nb, nb), 0)
    kj = lax.broadcasted_iota(jnp.int32, (nb, nb), 1)
    active = jnp.logical_or(bound > UNDERFLOW_BOUND, kj >= qi)
    last_q = jnp.max(jnp.where(active, qi, -1), axis=2)
    first_blk = jnp.min(jnp.where(active, kj, nb), axis=3)
    first_kv = jnp.min(first_blk.reshape(N_PAIR, 2, T // tq, tq // tb), axis=-1)
    return first_kv.astype(F32), last_q.astype(F32)


def _attn_fwd(qkv3, cp3, first_kv, name):
    T = qkv3.shape[1]
    tb = min(ATT_BLOCK, T)
    tq = min(ATT_Q_BLOCK, T)
    nb = T // tb
    per_q = tq // tb

    assert per_q % 2 == 0 or T == tq, (T, tq, tb)

    def body(q_ref, k_ref, v_ref, cp_ref, first_ref, o_ref, lse_ref, st_a, st_b):
        h, i = pl.program_id(0), pl.program_id(1)
        lane = lax.broadcasted_iota(jnp.int32, (tq, LANES), 1)
        lane_k = lax.broadcasted_iota(jnp.int32, (tb, LANES), 1)
        feat = lax.broadcasted_iota(jnp.int32, (LANES, tq), 0)
        q2 = q_ref[0] * QK_SCALE
        qa, own_k, one_k = [], [], []
        for hh in range(2):
            bias = jnp.logical_and(lane >= _spare(hh), lane < _spare(hh) + BIAS_LANES)
            qa.append(jnp.where(_own_lanes((tq, LANES), hh), q2, jnp.where(bias, 1.0, 0.0).astype(BF16)))
            own_k.append(_own_lanes((tb, LANES), hh))
            one_k.append(jnp.where(lane_k == _spare(hh), 1.0, 0.0).astype(BF16))

        def scores(j, scr, heads=(0, 1)):
            off = pl.multiple_of(j * tb, tb)
            kb, cb = k_ref[0, pl.ds(off, tb), :], cp_ref[0, pl.ds(off, tb), :]
            for hh in heads:
                scr[hh] = lax.dot_general(jnp.where(own_k[hh], kb, cb), qa[hh], NT_DIMS, preferred_element_type=F32)

        def step(j, carry, first, scr=None, heads=(0, 1)):
            off = pl.multiple_of(j * tb, tb)
            kb, vb, cb = k_ref[0, pl.ds(off, tb), :], v_ref[0, pl.ds(off, tb), :], cp_ref[0, pl.ds(off, tb), :]
            lo = 0 if first is None else first
            out = list(carry)
            for hh in heads:
                m_all, acc_all = carry[2 * hh], carry[2 * hh + 1]
                m_old, acc = m_all[:, lo:], acc_all[:, lo:]
                if scr is None:
                    st = lax.dot_general(jnp.where(own_k[hh], kb, cb), qa[hh][lo:], NT_DIMS,
                                         preferred_element_type=F32)
                else:
                    st = scr[hh]
                if first is not None:
                    key = lax.broadcasted_iota(jnp.int32, (tb, tq - lo), 0)
                    qry = lax.broadcasted_iota(jnp.int32, (tb, tq - lo), 1)
                    st = jnp.where(key <= qry, st, -jnp.inf)
                m = jnp.maximum(m_old, jnp.max(st, axis=0, keepdims=True))
                p = jnp.exp(st - m)
                acc = jnp.exp(m_old - m) * acc + lax.dot_general(
                    jnp.where(own_k[hh], vb, one_k[hh]), p.astype(BF16), TN_DIMS, preferred_element_type=F32)
                if lo:
                    m = jnp.concatenate([m_all[:, :lo], m], axis=1)
                    acc = jnp.concatenate([acc_all[:, :lo], acc], axis=1)
                out[2 * hh], out[2 * hh + 1] = m, acc
            return tuple(out)

        ninf = jnp.full((1, tq), -jnp.inf, F32)
        zacc = jnp.zeros((LANES, tq), F32)

        def make_pair(heads):
            def pair(jj, c):
                j = 2 * jj
                scores(j + 1, st_b, heads)
                c = step(j, c, None, st_a, heads)
                scores(j + 2, st_a, heads)
                return step(j + 1, c, None, st_b, heads)
            return pair

        end_pair = (i * per_q) // 2
        first_pair = [jnp.clip(first_ref[h, hh, i].astype(jnp.int32), 0, i * per_q) // 2 for hh in range(2)]
        both_pair = jnp.maximum(first_pair[0], first_pair[1])
        carry = (ninf, zacc, ninf, zacc)
        for hh in range(2):
            scores(2 * first_pair[hh], st_a, (hh,))
            carry = lax.fori_loop(first_pair[hh], both_pair, make_pair((hh,)), carry)
        scores(2 * both_pair, st_a)
        carry = lax.fori_loop(both_pair, end_pair, make_pair((0, 1)), carry)
        carry = step(i * per_q, carry, 0, st_a)
        for t in range(1, per_q):
            carry = step(i * per_q + t, carry, t * tb)
        outs = []
        for hh in range(2):
            m, acc = carry[2 * hh], carry[2 * hh + 1]
            l = jnp.sum(jnp.where(feat == _spare(hh), acc, 0.0), axis=0, keepdims=True)
            lse_ref[0, 0, hh:hh + 1, :] = m + jnp.log(l)
            outs.append(acc * (1.0 / l))
        o_ref[0] = jnp.where(feat < HEAD_DIM, outs[0], outs[1]).astype(BF16)

    res = lambda base: pl.BlockSpec((1, T, LANES), lambda h, i: (base + h, 0, 0), pipeline_mode=pl.Buffered(1))
    return pl.pallas_call(
        body,
        name=name,
        grid=(N_PAIR, T // tq),
        in_specs=[pl.BlockSpec((1, tq, LANES), lambda h, i: (h, i, 0)), res(N_PAIR), res(2 * N_PAIR), res(0),
                  pl.BlockSpec(memory_space=pltpu.SMEM)],
        out_specs=[pl.BlockSpec((1, LANES, tq), lambda h, i: (h, 0, i)),
                   pl.BlockSpec((1, 1, 2, tq), lambda h, i: (h, i, 0, 0))],
        out_shape=[jax.ShapeDtypeStruct((N_PAIR, LANES, T), BF16),
                   jax.ShapeDtypeStruct((N_PAIR, T // tq, 2, tq), F32)],
        scratch_shapes=[pltpu.VMEM((2, tb, tq), F32), pltpu.VMEM((2, tb, tq), F32)],
        compiler_params=_params(("parallel", "arbitrary"), 56),
    )(qkv3, qkv3, qkv3, cp3, first_kv)


def _attn_bwd(qkv3, cp3, kst4, do3, st4, last_q, name, ride=()):
    T = qkv3.shape[1]
    tb = min(ATT_BLOCK, T)
    nb = T // tb

    n_ride = len(ride)

    def body(*refs):
        q_ref, do_ref, st_ref, k_ref, v_ref, cp_ref, kst_ref, last_ref = refs[:8]
        ride_in = refs[8:8 + n_ride]
        dq_hbm, rs_ref, dk_ref, dv_ref, cs_ref = refs[8 + n_ride:13 + n_ride]
        ride_out = refs[13 + n_ride:13 + 2 * n_ride]
        dq_acc, dk_acc, dv_acc = refs[13 + 2 * n_ride:16 + 2 * n_ride]
        h, j = pl.program_id(0), pl.program_id(1)
        if n_ride:
            starts, waits = _exchange_copies(ride_in, ride_out, *refs[16 + 2 * n_ride:])

            @pl.when(jnp.logical_and(h == 0, j == 0))
            def _():
                for cp in starts:
                    cp.start()

        lane = lax.broadcasted_iota(jnp.int32, (tb, LANES), 1)
        own = [_own_lanes((tb, LANES), hh) for hh in range(2)]
        bias = [jnp.logical_and(lane >= _spare(hh), lane < _spare(hh) + BIAS_LANES) for hh in range(2)]
        key = lax.broadcasted_iota(jnp.int32, (tb, tb), 0)
        qry = lax.broadcasted_iota(jnp.int32, (tb, tb), 1)

        @pl.when(j == 0)
        def _():
            dq_acc[...] = jnp.zeros(dq_acc.shape, F32)
            rs_ref[...] = jnp.zeros(rs_ref.shape, F32)

        vb = v_ref[0]
        zero = jnp.zeros_like(vb)
        one = jnp.ones_like(vb)
        bias_one = [jnp.where(bias[hh], one, zero) for hh in range(2)]
        kb = [jnp.where(own[hh], k_ref[0], cp_ref[0]) for hh in range(2)]
        kst = kst_ref[0, 0]
        kst = [jnp.where(_own_lanes((LANES, tb), hh, axis=0), kst, jnp.zeros_like(kst)) for hh in range(2)]
        vm = [jnp.where(own[hh], vb, zero) for hh in range(2)]
        dk_acc[...] = jnp.zeros(dk_acc.shape, F32)
        dv_acc[...] = jnp.zeros(dv_acc.shape, F32)

        def step(i, masked, heads=(0, 1)):
            off = pl.multiple_of(i * tb, tb)
            qb = q_ref[0, pl.ds(off, tb), :] * QK_SCALE
            dob = do_ref[0, pl.ds(off, tb), :]
            dq = None
            for hh in heads:
                st = lax.dot_general(kb[hh], jnp.where(own[hh], qb, bias_one[hh]), NT_DIMS,
                                     preferred_element_type=F32)
                if masked:
                    st = jnp.where(key <= qry, st, -jnp.inf)
                p = jnp.exp(st - st_ref[0, i, hh:hh + 1, :])
                dp = lax.dot_general(vm[hh], dob, NT_DIMS, preferred_element_type=F32)
                dsb = (p * (dp - st_ref[0, i, 2 + hh:3 + hh, :])).astype(BF16)
                dv_acc[hh] += jnp.dot(p.astype(BF16), dob, preferred_element_type=F32)
                dk_acc[hh] += jnp.dot(dsb, jnp.where(own[hh], qb, one), preferred_element_type=F32)
                rs_ref[0, i, hh:hh + 1, :] += jnp.sum(dsb.astype(F32), axis=0, keepdims=True)
                d = jnp.dot(kst[hh], dsb, preferred_element_type=F32)
                dq = d if dq is None else dq + d
            dq_acc[i] += dq

        step(j, True)

        def make_body(heads):
            def loop_body(i, carry):
                step(i, False, heads)
                return carry
            return loop_body

        last = [jnp.clip(last_ref[h, hh, j].astype(jnp.int32), j, nb - 1) for hh in range(2)]
        both = jnp.minimum(last[0], last[1])
        lax.fori_loop(j + 1, both + 1, make_body((0, 1)), 0)
        for hh in range(2):
            lax.fori_loop(both + 1, last[hh] + 1, make_body((hh,)), 0)
        dk_ref[0] = jnp.where(own[0], dk_acc[0], dk_acc[1]).astype(BF16)
        dv_ref[0] = jnp.where(own[0], dv_acc[0], dv_acc[1]).astype(BF16)
        lane8 = lax.broadcasted_iota(jnp.int32, (8, LANES), 1)
        for hh in range(2):
            pick = jnp.where(lane8 == _spare(hh), 1.0, 0.0).astype(BF16)
            x = dk_acc[hh]
            hi = x.astype(BF16)
            r1 = x - hi.astype(F32)
            mid = r1.astype(BF16)
            low = (r1 - mid.astype(F32)).astype(BF16)
            cs_ref[0, 0, 8 * hh:8 * hh + 8, :] = (
                lax.dot_general(pick, hi, NT_DIMS, preferred_element_type=F32)
                + lax.dot_general(pick, mid, NT_DIMS, preferred_element_type=F32)
                + lax.dot_general(pick, low, NT_DIMS, preferred_element_type=F32))

        @pl.when(j == nb - 1)
        def _():
            pltpu.sync_copy(dq_acc, dq_hbm.at[h])

        if n_ride:
            @pl.when(jnp.logical_and(h == N_PAIR - 1, j == nb - 1))
            def _():
                for wait in waits:
                    wait()

    res = pl.BlockSpec((1, T, LANES), lambda h, j: (h, 0, 0), pipeline_mode=pl.Buffered(1))
    tile = lambda base: pl.BlockSpec((1, tb, LANES), lambda h, j: (base + h, j, 0))
    rows = lambda n: pl.BlockSpec((1, nb, n, tb), lambda h, j: (h, 0, 0, 0))
    hbm = pl.BlockSpec(memory_space=pl.ANY)
    return pl.pallas_call(
        body,
        name=name,
        grid=(N_PAIR, nb),
        in_specs=[res, res, rows(4), tile(N_PAIR), tile(2 * N_PAIR), tile(0),
                  pl.BlockSpec((1, 1, LANES, tb), lambda h, j: (h, j, 0, 0)),
                  pl.BlockSpec(memory_space=pltpu.SMEM)] + [hbm] * n_ride,
        out_specs=[hbm, rows(2), tile(0), tile(0),
                   pl.BlockSpec((1, 1, 16, tb), lambda h, j: (h, j, 0, 0))] + [hbm] * n_ride,
        out_shape=[
            jax.ShapeDtypeStruct((N_PAIR, nb, LANES, tb), F32),
            jax.ShapeDtypeStruct((N_PAIR, nb, 2, tb), F32),
            jax.ShapeDtypeStruct((N_PAIR, T, LANES), BF16),
            jax.ShapeDtypeStruct((N_PAIR, T, LANES), BF16),
            jax.ShapeDtypeStruct((N_PAIR, nb, 16, tb), F32),
        ] + [jax.ShapeDtypeStruct(a.shape, a.dtype) for a in ride],
        scratch_shapes=[pltpu.VMEM((nb, LANES, tb), F32), pltpu.VMEM((2, tb, LANES), F32),
                        pltpu.VMEM((2, tb, LANES), F32)] + (_exchange_sems(n_ride) if n_ride else []),
        compiler_params=_params(("arbitrary", "arbitrary"), 56),
    )(qkv3, do3, st4, qkv3, qkv3, cp3, kst4, last_q, *ride)


def _attn_out(o3, w, x, name):
    T = x.shape[0]

    def body(o_ref, w_ref, x_ref, out_ref):
        out_ref[...] = x_ref[...] + jnp.dot(_cat_groups(o_ref, N_PAIR), w_ref[...], preferred_element_type=F32)

    return _rowcall(name, body, T, min(512, T), [(o3, "grp"), _wkind(w), (x, "row")],
                    [((T, D_MODEL), F32, "row")], 24)[0]


def _attn_dout(dx, wt, o3, name):
    T = dx.shape[0]
    tm = min(512, T)

    def body(dx_ref, w_ref, o_ref, do_ref, dd_ref):
        do = jnp.dot(dx_ref[...].astype(BF16), w_ref[...], preferred_element_type=F32).astype(BF16)
        lo = _own_lanes((tm, LANES), 0)
        head = lax.broadcasted_iota(jnp.int32, (tm, 2 * N_PAIR), 1)
        dd = jnp.zeros((tm, 2 * N_PAIR), F32)
        for t in range(N_PAIR):
            d = do[:, LANES * t:LANES * (t + 1)]
            do_ref[t] = d
            prod = d.astype(F32) * o_ref[t].astype(F32)
            d0 = jnp.sum(jnp.where(lo, prod, 0.0), axis=1, keepdims=True)
            d1 = jnp.sum(jnp.where(lo, 0.0, prod), axis=1, keepdims=True)
            dd = jnp.where(head == 2 * t, d0, jnp.where(head == 2 * t + 1, d1, dd))
        dd_ref[...] = dd

    return _rowcall(name, body, T, tm, [(dx, "row"), _wkind(wt), (o3, "grp")],
                    [((N_PAIR, T, LANES), BF16, "grp"), ((T, 2 * N_PAIR), F32, "row")], 32)


def _ffn_out(gu, w, x, name):
    T = x.shape[0]
    tm = min(512, T)

    def body(gu_ref, w_ref, x_ref, out_ref, hid_ref):
        acc = x_ref[...]
        for j in range(N_FFN_TILE):
            g = gu_ref[:, 2 * FFN_TILE * j:2 * FFN_TILE * j + FFN_TILE].astype(F32)
            u = gu_ref[:, 2 * FFN_TILE * j + FFN_TILE:2 * FFN_TILE * (j + 1)].astype(F32)
            hj = (g * jax.nn.sigmoid(g) * u).astype(BF16)
            hid_ref[:, FFN_TILE * j:FFN_TILE * (j + 1)] = hj
            acc = acc + jnp.dot(hj, w_ref[FFN_TILE * j:FFN_TILE * (j + 1), :], preferred_element_type=F32)
        out_ref[...] = acc

    return _rowcall(name, body, T, tm, [(gu, "row"), _wkind(w), (x, "row")],
                    [((T, D_MODEL), F32, "row"), ((T, FFN_H), BF16, "row")], 48)


def _ffn_dgu(dx, wt, gu, name):
    T = dx.shape[0]
    tm = min(512, T)

    def body(dx_ref, w_ref, gu_ref, dgu_ref):
        dxb = dx_ref[...].astype(BF16)
        for j in range(N_FFN_TILE):
            dh = jnp.dot(dxb, w_ref[:, FFN_TILE * j:FFN_TILE * (j + 1)], preferred_element_type=F32)
            g = gu_ref[:, 2 * FFN_TILE * j:2 * FFN_TILE * j + FFN_TILE].astype(F32)
            u = gu_ref[:, 2 * FFN_TILE * j + FFN_TILE:2 * FFN_TILE * (j + 1)].astype(F32)
            sg = jax.nn.sigmoid(g)
            dgu_ref[:, 2 * FFN_TILE * j:2 * FFN_TILE * j + FFN_TILE] = (
                dh * u * (sg * (1.0 + g * (1.0 - sg)))).astype(BF16)
            dgu_ref[:, 2 * FFN_TILE * j + FFN_TILE:2 * FFN_TILE * (j + 1)] = (dh * (g * sg)).astype(BF16)

    return _rowcall(name, body, T, tm, [(dx, "row"), _wkind(wt), (gu, "row")],
                    [((T, 2 * FFN_H), BF16, "row")], 48)[0]


def _sgu_core(a, ln_g, ln_b, w_s, bst, w, x, name):
    T = x.shape[0]
    tm = min(256, T)

    def body(a_ref, lg_ref, lb_ref, ws_ref, bs_ref, w_ref, x_ref, out_ref, gated_ref, vn_ref, mixed_ref):
        v = _gelu(a_ref[:, SGU_W:].astype(F32))
        mu = jnp.mean(v, axis=-1, keepdims=True)
        vc = v - mu
        rstd = lax.rsqrt(jnp.mean(vc * vc, axis=-1, keepdims=True) + LN_EPS)
        vn_ref[...] = (vc * rstd * lg_ref[...] + lb_ref[...]).astype(BF16)
        tt = lax.broadcasted_iota(jnp.int32, (CHUNK, CHUNK), 0)
        ss = lax.broadcasted_iota(jnp.int32, (CHUNK, CHUNK), 1)
        for g in range(SGU_G):
            wg = jnp.where(tt >= ss, ws_ref[g], 0.0).astype(BF16)
            bcol = _lane_col(bs_ref[...], g)
            cols = slice(CHUNK * g, CHUNK * (g + 1))
            for c in range(tm // CHUNK):
                rows = slice(CHUNK * c, CHUNK * (c + 1))
                mixed = jnp.dot(wg, vn_ref[rows, cols], preferred_element_type=F32) + bcol
                u = _gelu(a_ref[rows, cols].astype(F32))
                mixed_ref[rows, cols] = mixed.astype(BF16)
                gated_ref[rows, cols] = (u * mixed).astype(BF16)
        out_ref[...] = x_ref[...] + jnp.dot(gated_ref[...], w_ref[...], preferred_element_type=F32)

    ins = [(a, "row"), (ln_g, "full"), (ln_b, "full"), (w_s, "full"), (bst, "full"), _wkind(w), (x, "row")]
    outs = [((T, D_MODEL), F32, "row")] + [((T, SGU_W), BF16, "row")] * 3
    return _rowcall(name, body, T, tm, ins, outs, 40)


def _sgu_core_bwd(dx, wt, a, vn, mixed, ln_g, w_s, name):
    T = dx.shape[0]
    tm = min(256, T)

    def body(dx_ref, wt_ref, a_ref, vn_ref, mx_ref, lg_ref, ws_ref,
             da_ref, dws_ref, dba_ref, dlg_ref, dlb_ref, dg_scr, dvn_scr):
        @pl.when(pl.program_id(0) == 0)
        def _():
            dws_ref[...] = jnp.zeros(dws_ref.shape, F32)
            dba_ref[...] = jnp.zeros(dba_ref.shape, F32)
            dlg_ref[...] = jnp.zeros(dlg_ref.shape, F32)
            dlb_ref[...] = jnp.zeros(dlb_ref.shape, F32)

        dg_scr[...] = jnp.dot(dx_ref[...].astype(BF16), wt_ref[...], preferred_element_type=F32)
        tt = lax.broadcasted_iota(jnp.int32, (CHUNK, CHUNK), 0)
        ss = lax.broadcasted_iota(jnp.int32, (CHUNK, CHUNK), 1)
        tril = tt >= ss
        for g in range(SGU_G):
            wg = jnp.where(tril, ws_ref[g], 0.0).astype(BF16)
            cols = slice(CHUNK * g, CHUNK * (g + 1))
            for c in range(tm // CHUNK):
                rows = slice(CHUNK * c, CHUNK * (c + 1))
                dgb = dg_scr[rows, cols]
                au = a_ref[rows, cols].astype(F32)
                dmx = dgb * _gelu(au)
                da_ref[rows, cols] = (dgb * mx_ref[rows, cols].astype(F32) * _gelu_grad(au)).astype(BF16)
                dmb = dmx.astype(BF16)
                dvn_scr[rows, cols] = lax.dot_general(wg, dmb, TN_DIMS, preferred_element_type=F32)
                dws_ref[g] += jnp.where(
                    tril, lax.dot_general(dmb, vn_ref[rows, cols], NT_DIMS, preferred_element_type=F32), 0.0)
                dba_ref[:, cols] += dmx
        av = a_ref[:, SGU_W:].astype(F32)
        v = _gelu(av)
        mu = jnp.mean(v, axis=-1, keepdims=True)
        vc = v - mu
        rstd = lax.rsqrt(jnp.mean(vc * vc, axis=-1, keepdims=True) + LN_EPS)
        xhat = vc * rstd
        dvn = dvn_scr[...]
        dlg_ref[...] += jnp.sum(dvn * xhat, axis=0, keepdims=True)
        dlb_ref[...] += jnp.sum(dvn, axis=0, keepdims=True)
        dxh = dvn * lg_ref[...]
        dv = rstd * (dxh - jnp.mean(dxh, axis=-1, keepdims=True)
                     - xhat * jnp.mean(dxh * xhat, axis=-1, keepdims=True))
        da_ref[:, SGU_W:] = (dv * _gelu_grad(av)).astype(BF16)

    ins = [(dx, "row"), _wkind(wt), (a, "row"), (vn, "row"), (mixed, "row"), (ln_g, "full"), (w_s, "full")]
    outs = [((T, 2 * SGU_W), BF16, "row"), ((SGU_G, CHUNK, CHUNK), F32, "full"), ((CHUNK, SGU_W), F32, "full"),
            ((1, SGU_W), F32, "full"), ((1, SGU_W), F32, "full")]
    return _rowcall(name, body, T, tm, ins, outs, 40,
                    scratch=[pltpu.VMEM((tm, SGU_W), F32), pltpu.VMEM((tm, SGU_W), F32)])


def _loss_head(x, wf, tgt, name):
    T = x.shape[0]
    tm = min(512, T)

    def body(x_ref, wf_ref, tgt_ref, dx_ref, dwf_ref, loss_ref):
        @pl.when(pl.program_id(0) == 0)
        def _():
            dwf_ref[...] = jnp.zeros(dwf_ref.shape, F32)
            loss_ref[...] = jnp.zeros(loss_ref.shape, F32)

        xv = x_ref[...]
        r = lax.rsqrt(jnp.mean(xv * xv, axis=-1, keepdims=True) + NORM_EPS)
        xn = xv * r
        err = xn * wf_ref[...] - tgt_ref[...]
        loss_ref[...] += 0.5 * jnp.sum(jnp.mean(err * err, axis=-1, keepdims=True), axis=0, keepdims=True)
        dy = err * (1.0 / D_MODEL)
        dwf_ref[...] += jnp.sum(dy * xn, axis=0, keepdims=True)
        dyw = dy * wf_ref[...]
        dx_ref[...] = r * (dyw - xn * jnp.mean(dyw * xn, axis=-1, keepdims=True))

    return _rowcall(name, body, T, tm, [(x, "row"), (wf, "full"), (tgt, "row")],
                    [((T, D_MODEL), F32, "row"), ((1, D_MODEL), F32, "full"), ((8, LANES), F32, "full")], 32)


def _peers():
    x, y, c = lax.axis_index("x"), lax.axis_index("y"), lax.axis_index("c")
    peers = []
    for p in range(1, N_DEV):
        px = 1 - x if p & 4 else x
        py = 1 - y if p & 2 else y
        pc = 1 - c if p & 1 else c
        peers.append((4 * px + 2 * py + pc, (px, py, pc)))
    return 4 * x + 2 * y + c, peers


def _all_gather(arrs, name):
    n = len(arrs)
    hbm = pl.BlockSpec(memory_space=pl.ANY)

    def body(*refs):
        ins, outs = refs[:n], refs[n:2 * n]
        send_sems, recv_sems, local_sems = refs[2 * n:]
        me, peers = _peers()
        sends, recvs, locals_ = [], [], []
        for t in range(n):
            cp = pltpu.make_async_copy(ins[t], outs[t].at[me], local_sems.at[t])
            cp.start()
            locals_.append(cp)
            for k, (pidx, pid) in enumerate(peers):
                s = t * (N_DEV - 1) + k
                send = pltpu.make_async_remote_copy(
                    src_ref=ins[t], dst_ref=outs[t].at[me], send_sem=send_sems.at[s], recv_sem=recv_sems.at[s],
                    device_id=pid, device_id_type=MESH)
                send.start()
                sends.append(send)
                recvs.append(pltpu.make_async_remote_copy(
                    src_ref=ins[t], dst_ref=outs[t].at[pidx], send_sem=send_sems.at[s], recv_sem=recv_sems.at[s],
                    device_id=pid, device_id_type=MESH))
        for r in recvs:
            r.wait_recv()
        for s in sends:
            s.wait_send()
        for cp in locals_:
            cp.wait()

    return pl.pallas_call(
        body,
        name=name,
        in_specs=[hbm] * n,
        out_specs=[hbm] * n,
        out_shape=[jax.ShapeDtypeStruct((N_DEV,) + a.shape, a.dtype) for a in arrs],
        scratch_shapes=[pltpu.SemaphoreType.DMA((n * (N_DEV - 1),)), pltpu.SemaphoreType.DMA((n * (N_DEV - 1),)),
                        pltpu.SemaphoreType.DMA((n,))],
    )(*arrs)


def _exchange_sems(n):
    return [pltpu.SemaphoreType.DMA((n * (N_DEV - 1),)), pltpu.SemaphoreType.DMA((n * (N_DEV - 1),)),
            pltpu.SemaphoreType.DMA((n,))]


def _exchange_copies(ins, outs, send_sems, recv_sems, local_sems):
    me, peers = _peers()
    starts, waits = [], []
    for t in range(len(ins)):
        local = pltpu.make_async_copy(ins[t].at[me], outs[t].at[me], local_sems.at[t])
        starts.append(local)
        waits.append(local.wait)
        for k, (pidx, pid) in enumerate(peers):
            s = t * (N_DEV - 1) + k
            send = pltpu.make_async_remote_copy(
                src_ref=ins[t].at[pidx], dst_ref=outs[t].at[me], send_sem=send_sems.at[s],
                recv_sem=recv_sems.at[s], device_id=pid, device_id_type=MESH)
            arrival = pltpu.make_async_remote_copy(
                src_ref=ins[t].at[pidx], dst_ref=outs[t].at[pidx], send_sem=send_sems.at[s],
                recv_sem=recv_sems.at[s], device_id=pid, device_id_type=MESH)
            starts.append(send)
            waits += [arrival.wait_recv, send.wait_send]
    return starts, waits


def _exchange(arrs, name):
    n = len(arrs)
    hbm = pl.BlockSpec(memory_space=pl.ANY)

    def body(*refs):
        starts, waits = _exchange_copies(refs[:n], refs[n:2 * n], *refs[2 * n:])
        for cp in starts:
            cp.start()
        for wait in waits:
            wait()

    return pl.pallas_call(
        body,
        name=name,
        in_specs=[hbm] * n,
        out_specs=[hbm] * n,
        out_shape=[jax.ShapeDtypeStruct(a.shape, a.dtype) for a in arrs],
        scratch_shapes=_exchange_sems(n),
    )(*arrs)


def _row_tile(rows, cap):
    best = None
    for t in range(16, cap + 1, 16):
        if rows % t == 0:
            best = t
    assert best is not None, rows
    return best


def _adam_sum(parts, w, m, v, name):
    R, C = w.shape
    tr = _row_tile(R, 128)

    def body(p_ref, w_ref, m_ref, v_ref, g_ref, d_ref, nm_ref, nv_ref):
        g = p_ref[0].astype(F32)
        for s in range(1, N_DEV):
            g = g + p_ref[s].astype(F32)
        mm = ADAM_B1 * m_ref[...] + (1.0 - ADAM_B1) * g
        vv = ADAM_B2 * v_ref[...] + (1.0 - ADAM_B2) * (g * g)
        m_hat = mm / (1.0 - ADAM_B1 ** ADAM_STEP)
        v_hat = vv / (1.0 - ADAM_B2 ** ADAM_STEP)
        g_ref[...] = g
        d_ref[...] = -ADAM_LR * (m_hat / (jnp.sqrt(v_hat) + ADAM_EPS) + ADAM_WD * w_ref[...])
        nm_ref[...] = mm
        nv_ref[...] = vv

    mat = pl.BlockSpec((tr, C), lambda i: (i, 0))
    return pl.pallas_call(
        body,
        name=name,
        grid=(R // tr,),
        in_specs=[pl.BlockSpec((N_DEV, tr, C), lambda i: (0, i, 0)), mat, mat, mat],
        out_specs=[mat] * 4,
        out_shape=[jax.ShapeDtypeStruct((R, C), F32)] * 4,
        compiler_params=_params(("parallel",), 32),
    )(parts, w, m, v)


def _cols_from_gathered(g):
    _, L, K, n = g.shape
    return jnp.transpose(g, (1, 2, 0, 3)).reshape(L, K, N_DEV * n)


def _rows_from_gathered(g):
    _, L, k, N = g.shape
    return jnp.transpose(g, (1, 0, 2, 3)).reshape(L, N_DEV * k, N)


def _cols_to_blocks(dw):
    L, K, N = dw.shape
    n = N // N_DEV
    return jnp.transpose(dw.reshape(L, K, N_DEV, n), (2, 0, 1, 3)).reshape(N_DEV, L * K, n)


def _rows_to_blocks(dw):
    L, K, N = dw.shape
    k = K // N_DEV
    return jnp.transpose(dw.reshape(L, N_DEV, k, N), (1, 0, 2, 3)).reshape(N_DEV, L * k, N)


def _ffn_interleave(w):
    lead = w.shape[:-1]
    t = w.reshape(lead + (2, N_FFN_TILE, FFN_TILE))
    return jnp.swapaxes(t, -3, -2).reshape(lead + (2 * FFN_H,))


def _ffn_deinterleave(w):
    lead = w.shape[:-1]
    t = w.reshape(lead + (N_FFN_TILE, 2, FFN_TILE))
    return jnp.swapaxes(t, -3, -2).reshape(lead + (2 * FFN_H,))


def _pad_rows(a, rows=8):
    a = a.reshape(-1, a.shape[-1])
    return jnp.pad(a, ((0, rows - a.shape[0]), (0, 0)))


SMALL_ROWS = 6 * 8 + 2 * SGU_G * CHUNK * CHUNK // D_MODEL


def _pack_small(mixer, ffn, final, b_f, extra, w_s, b_s):
    bf_row = jnp.pad(b_f.reshape(1, -1), ((0, 0), (0, D_MODEL - b_f.size)))
    bs_rows = jnp.pad(b_s.reshape(4, -1), ((0, 0), (0, D_MODEL - b_s.size // 4)))
    return jnp.concatenate([
        _pad_rows(mixer), _pad_rows(ffn), _pad_rows(final.reshape(1, -1)), _pad_rows(bf_row),
        _pad_rows(extra), _pad_rows(bs_rows), w_s.reshape(-1, D_MODEL)], axis=0)


def _unpack_small(p):
    mixer, ffn, final = p[0:4], p[8:12], p[16]
    b_f = p[24, :32].reshape(2, 2 * N_PAIR)
    extra = p[32]
    b_s = p[40:44, :2 * SGU_G * CHUNK // 4].reshape(2, SGU_G, CHUNK)
    w_s = p[48:].reshape(2, SGU_G, CHUNK, CHUNK)
    return mixer, ffn, final, b_f, extra, w_s, b_s


def kernel(x, mixer_norm_w, attn_w_in, attn_b_f, attn_w_out, sgu_w_in, sgu_ln_g, sgu_ln_b, sgu_w_s, sgu_b_s, sgu_w_out, ffn_norm_w, ffn_w_in, ffn_w_out, final_norm_w, loss_target, m_mixer_norm_w, m_attn_w_in, m_attn_b_f, m_attn_w_out, m_sgu_w_in, m_sgu_ln_g, m_sgu_ln_b, m_sgu_w_s, m_sgu_b_s, m_sgu_w_out, m_ffn_norm_w, m_ffn_w_in, m_ffn_w_out, m_final_norm_w, v_mixer_norm_w, v_attn_w_in, v_attn_b_f, v_attn_w_out, v_sgu_w_in, v_sgu_ln_g, v_sgu_ln_b, v_sgu_w_s, v_sgu_b_s, v_sgu_w_out, v_ffn_norm_w, v_ffn_w_in, v_ffn_w_out, v_final_norm_w):
    T = x.shape[1]
    tb = min(ATT_BLOCK, T)
    xs, tgt = x[0], loss_target[0]

    shards = [attn_w_in, attn_w_out, sgu_w_in, sgu_w_out, ffn_w_in, ffn_w_out, sgu_ln_g, sgu_ln_b]
    gathered = _all_gather([s.astype(BF16) for s in shards[:6]] + shards[6:], "gather_weights")
    w_attn_in = _cols_from_gathered(gathered[0])
    w_attn_out = _rows_from_gathered(gathered[1])
    w_sgu_in = _cols_from_gathered(gathered[2])
    w_sgu_out = _rows_from_gathered(gathered[3])
    w_ffn_in = _ffn_interleave(_cols_from_gathered(gathered[4]))
    w_ffn_out = _rows_from_gathered(gathered[5])
    ln_g = jnp.transpose(gathered[6], (1, 0, 2)).reshape(2, 1, SGU_W)
    ln_b = jnp.transpose(gathered[7], (1, 0, 2)).reshape(2, 1, SGU_W)
    w_qkv = w_attn_in[:, :, :3 * D_MODEL]
    w_f = jnp.pad(w_attn_in[:, :, 3 * D_MODEL:], ((0, 0), (0, 0), (0, LANES - 2 * N_PAIR)))
    tr = lambda w: jnp.swapaxes(w, -1, -2)
    w_qkv_t6, w_f_t, w_attn_out_t = tr(w_qkv).reshape(6, D_MODEL, D_MODEL), tr(w_f), tr(w_attn_out)
    w_sgu_in_t, w_sgu_out_t, w_ffn_in_t, w_ffn_out_t = tr(w_sgu_in), tr(w_sgu_out), tr(w_ffn_in), tr(w_ffn_out)
    mixer_nw = mixer_norm_w.reshape(4, 1, D_MODEL)
    ffn_nw = ffn_norm_w.reshape(4, 1, D_MODEL)
    b_col = attn_b_f.reshape(2, 2 * N_PAIR, 1)
    bs_t = jnp.swapaxes(sgu_b_s, 1, 2)

    saved = []
    xr = xs
    for i in range(4):
        j = i // 2
        if i % 2 == 0:
            qkv3, h = _norm_matmul(xr, mixer_nw[i], (w_qkv, j), BF16, f"attn_qkv_{j}", 1024, groups=True)
            fl, _ = _norm_matmul(xr, mixer_nw[i], (w_f, j), F32, f"attn_gate_{j}", LANES)
            fl3 = jnp.transpose(fl[:, :2 * N_PAIR].reshape(T // CHUNK, CHUNK, 2 * N_PAIR), (0, 2, 1))
            c_chunks = _fgate_fwd(fl3, b_col[j], f"fgate_fwd_{j}")
            c16 = jnp.transpose(c_chunks, (1, 0, 2)).reshape(2 * N_PAIR, T)
            cp3 = _bias_pieces(c16)
            first_kv, last_q = _attn_reach(qkv3, c16, tb, min(ATT_Q_BLOCK, T))
            ot3, lse4 = _attn_fwd(qkv3, cp3, first_kv, f"attn_fwd_{j}")
            o3 = jnp.swapaxes(ot3, 1, 2)
            xm = _attn_out(o3, (w_attn_out, j), xr, f"attn_out_{j}")
            mix_saved = (xr, h, qkv3, fl3, cp3, o3, lse4, last_q)
        else:
            a, h = _norm_matmul(xr, mixer_nw[i], (w_sgu_in, j), BF16, f"sgu_in_{j}", 1024)
            xm, gated, vn, mixed = _sgu_core(a, ln_g[j], ln_b[j], sgu_w_s[j], bs_t[j], (w_sgu_out, j), xr,
                                             f"sgu_core_{j}")
            mix_saved = (xr, h, a, gated, vn, mixed)
        gu, h2 = _norm_matmul(xm, ffn_nw[i], (w_ffn_in, i), BF16, f"ffn_in_{i}", FFN_H // 2)
        xo, hid = _ffn_out(gu, (w_ffn_out, i), xm, f"ffn_out_{i}")
        saved.append((mix_saved, (xm, h2, gu, hid)))
        xr = xo
    dx, d_final, loss_part = _loss_head(xr, final_norm_w.reshape(1, D_MODEL), tgt, "loss_head")

    d_mixer_nw, d_ffn_nw = [None] * 4, [None] * 4
    d_attn_in, d_attn_out, d_bf, d_sgu_in, d_sgu_out = [None] * 2, [None] * 2, [None] * 2, [None] * 2, [None] * 2
    d_ws, d_bs, d_lg, d_lb = [None] * 2, [None] * 2, [None] * 2, [None] * 2
    d_ffn_in, d_ffn_out = [None] * 4, [None] * 4
    for i in reversed(range(4)):
        j = i // 2
        mix_saved, (xm, h2, gu, hid) = saved[i]
        dgu = _ffn_dgu(dx, (w_ffn_out_t, i), gu, f"ffn_dgu_{i}")
        d_ffn_out[i] = _matmul_tn(hid, dx, f"ffn_dwout_{i}", FFN_H // 2, D_MODEL)
        d_ffn_in[i] = _matmul_tn(h2, dgu, f"ffn_dwin_{i}", D_MODEL, FFN_H // 2)
        dx, d_ffn_nw[i] = _matmul_rms_bwd([(dgu, "row")], [(w_ffn_in_t, i)], xm, ffn_nw[i], dx, f"ffn_dx_{i}", 512, 48)
        if i % 2 == 0:
            xr, h, qkv3, fl3, cp3, o3, lse4, last_q = mix_saved
            do3, dd = _attn_dout(dx, (w_attn_out_t, j), o3, f"attn_dout_{j}")
            d_attn_out[j] = _matmul_tn(o3, dx, f"attn_dwout_{j}", D_MODEL, D_MODEL, a_grp=True)
            to_blocks = lambda a: jnp.swapaxes(a.reshape(N_PAIR, 2, T // tb, tb), 1, 2)
            from_blocks = lambda a: jnp.swapaxes(a, 1, 2).reshape(N_PAIR, 2, T)
            dd4 = to_blocks(dd.T.reshape(N_PAIR, 2, T))
            st4 = jnp.concatenate([to_blocks(from_blocks(lse4)), dd4], axis=2)
            kst4 = jnp.swapaxes((qkv3[N_PAIR:2 * N_PAIR] * QK_SCALE).reshape(N_PAIR, T // tb, tb, LANES), 2, 3)
            ride = []
            if i == 0:
                ln_blocks = lambda d: jnp.transpose(jnp.stack(d).reshape(2, N_DEV, 1, SGU_W // N_DEV),
                                                    (1, 0, 2, 3)).reshape(N_DEV, 2, SGU_W // N_DEV)
                ride = [b.astype(BF16) for b in (
                    _cols_to_blocks(d_attn_in[1][None]), _rows_to_blocks(jnp.stack(d_attn_out)),
                    _cols_to_blocks(jnp.stack(d_sgu_in)), _rows_to_blocks(jnp.stack(d_sgu_out)),
                    _cols_to_blocks(_ffn_deinterleave(jnp.stack(d_ffn_in))), _rows_to_blocks(jnp.stack(d_ffn_out)))]
                ride += [ln_blocks(d_lg), ln_blocks(d_lb)]
            bwd_out = _attn_bwd(qkv3, cp3, kst4, do3, st4, last_q, f"attn_bwd_{j}", ride)
            dqt4, rs4, dk3, dv3, cs4 = bwd_out[:5]
            if i == 0:
                received = bwd_out[5:]
            dq3 = jnp.swapaxes(dqt4, 2, 3).reshape(N_PAIR, T, LANES).astype(BF16)
            dc_pair = from_blocks(rs4 - jnp.stack([cs4[:, :, 0], cs4[:, :, 8]], axis=2))
            dc_chunks = jnp.transpose(dc_pair.reshape(2 * N_PAIR, T // CHUNK, CHUNK), (1, 0, 2))
            dfl3, db = _fgate_bwd(dc_chunks, fl3, b_col[j], f"fgate_bwd_{j}")
            d_bf[j] = db[:, 0]
            dfl = jnp.transpose(dfl3, (0, 2, 1)).reshape(T, 2 * N_PAIR)
            dfl = jnp.pad(dfl.astype(BF16), ((0, 0), (0, LANES - 2 * N_PAIR)))
            d_qkv = [_matmul_tn(h, d3, f"attn_dw{nm}_{j}", D_MODEL, D_MODEL, g_grp=True)
                     for nm, d3 in (("q", dq3), ("k", dk3), ("v", dv3))]
            d_f = _matmul_tn(h, dfl, f"attn_dwf_{j}", D_MODEL, LANES)[:, :2 * N_PAIR]
            d_attn_in[j] = jnp.concatenate(d_qkv + [d_f], axis=1)
            wts = [(w_qkv_t6, 3 * j + k) for k in range(3)] + [(w_f_t, j)]
            dx, d_mixer_nw[i] = _matmul_rms_bwd(
                [(dq3, "grp"), (dk3, "grp"), (dv3, "grp"), (dfl, "row")], wts, xr, mixer_nw[i], dx,
                f"attn_dx_{j}", 256, 40)
        else:
            xr, h, a, gated, vn, mixed = mix_saved
            da, d_ws[j], dba, d_lg[j], d_lb[j] = _sgu_core_bwd(dx, (w_sgu_out_t, j), a, vn, mixed, ln_g[j], sgu_w_s[j],
                                                               f"sgu_core_bwd_{j}")
            d_bs[j] = jnp.sum(dba.reshape(CHUNK, SGU_G, CHUNK), axis=-1).T
            d_sgu_out[j] = _matmul_tn(gated, dx, f"sgu_dwout_{j}", D_MODEL, D_MODEL)
            d_sgu_in[j] = _matmul_tn(h, da, f"sgu_dwin_{j}", D_MODEL, 1024)
            dx, d_mixer_nw[i] = _matmul_rms_bwd([(da, "row")], [(w_sgu_in_t, j)], xr, mixer_nw[i], dx,
                                                f"sgu_dx_{j}", 256, 40)
    grad_x = dx[None]

    rows4 = lambda parts: jnp.concatenate(parts, axis=1).reshape(4, D_MODEL)
    small_g = _pack_small(rows4(d_mixer_nw), rows4(d_ffn_nw), d_final[0], jnp.stack(d_bf),
                          loss_part[0:1, 0:1] * jnp.ones((1, D_MODEL), F32), jnp.stack(d_ws), jnp.stack(d_bs))
    zero_row = jnp.zeros((1, D_MODEL), F32)
    pack = lambda pre: _pack_small(pre[0], pre[1], pre[2], pre[3], zero_row, pre[4], pre[5])
    small_w = pack((mixer_norm_w, ffn_norm_w, final_norm_w, attn_b_f, sgu_w_s, sgu_b_s))
    small_m = pack((m_mixer_norm_w, m_ffn_norm_w, m_final_norm_w, m_attn_b_f, m_sgu_w_s, m_sgu_b_s))
    small_v = pack((v_mixer_norm_w, v_ffn_norm_w, v_final_norm_w, v_attn_b_f, v_sgu_w_s, v_sgu_b_s))
    small_all = _all_gather([small_g], "gather_small_grads")[0]
    small_out = [_unpack_small(p) for p in _adam_sum(small_all, small_w, small_m, small_v, "adam_small")]
    loss = small_out[0][4][0]

    late = _exchange([_cols_to_blocks(d_attn_in[0][None]).astype(BF16)], "exchange_grads")[0]
    names = ["attn_w_out", "sgu_w_in", "sgu_w_out", "ffn_w_in", "ffn_w_out"]
    ws = [attn_w_out, sgu_w_in, sgu_w_out, ffn_w_in, ffn_w_out]
    ms = [m_attn_w_out, m_sgu_w_in, m_sgu_w_out, m_ffn_w_in, m_ffn_w_out]
    vs = [v_attn_w_out, v_sgu_w_in, v_sgu_w_out, v_ffn_w_in, v_ffn_w_out]
    big_out = {}
    for nm, rec, w, m, v in zip(names, received[1:6], ws, ms, vs):
        flat = lambda a: a.reshape(-1, a.shape[-1])
        big_out[nm] = [o.reshape(w.shape) for o in _adam_sum(rec, flat(w), flat(m), flat(v), f"adam_{nm}")]
    per_layer = [_adam_sum(rec, attn_w_in[l], m_attn_w_in[l], v_attn_w_in[l], f"adam_attn_w_in_{l}")
                 for l, rec in enumerate((late, received[0]))]
    big_out["attn_w_in"] = [jnp.stack([per_layer[0][k], per_layer[1][k]]) for k in range(4)]
    pad8 = lambda a: jnp.pad(a, [(0, 0)] * (a.ndim - 2) + [(0, 8 - a.shape[-2]), (0, 0)])
    ln_parts = jnp.concatenate([pad8(received[6]), pad8(received[7])], axis=1)
    ln_pack = lambda g, b: jnp.concatenate([pad8(g), pad8(b)], axis=0)
    ln_out = _adam_sum(ln_parts, ln_pack(sgu_ln_g, sgu_ln_b), ln_pack(m_sgu_ln_g, m_sgu_ln_b),
                       ln_pack(v_sgu_ln_g, v_sgu_ln_b), "adam_sgu_ln")

    def leaf(kind):
        mixer, ffn, final, b_f, _, w_s, b_s = small_out[kind]
        o = lambda nm: big_out[nm][kind]
        return [mixer, o("attn_w_in"), b_f, o("attn_w_out"), o("sgu_w_in"), ln_out[kind][0:2], ln_out[kind][8:10],
                w_s, b_s, o("sgu_w_out"), ffn, o("ffn_w_in"), o("ffn_w_out"), final]

    return (loss, grad_x, *leaf(0), *leaf(1), *leaf(2), *leaf(3))
```

```python
import functools

import jax
import jax.numpy as jnp
from jax import lax
from jax.experimental import pallas as pl
from jax.experimental.pallas import tpu as pltpu

F32 = jnp.float32
BF16 = jnp.bfloat16

D_MODEL = 1024
HEAD_DIM = 64
N_PAIR = 8
LANES = 128
SGU_W = 2048
SGU_G = 16
CHUNK = 128
FFN_H = 2816
FFN_TILE = 256
N_FFN_TILE = FFN_H // FFN_TILE
NORM_EPS = 1e-6
LN_EPS = 1e-5
QK_SCALE = 0.125
ATT_BLOCK = 512
N_DEV = 8
ADAM_LR = 0.001
ADAM_B1 = 0.9
ADAM_B2 = 0.999
ADAM_EPS = 1e-08
ADAM_WD = 0.01
ADAM_STEP = 10
MESH = pl.DeviceIdType.MESH
SQRT_HALF = 0.7071067811865476
INV_SQRT_2PI = 0.3989422804014327

NT_DIMS = (((1,), (1,)), ((), ()))
TN_DIMS = (((0,), (0,)), ((), ()))


def _gelu(x):
    return 0.5 * x * (1.0 + lax.erf(x * SQRT_HALF))


def _gelu_grad(x):
    return 0.5 * (1.0 + lax.erf(x * SQRT_HALF)) + x * jnp.exp(-0.5 * x * x) * INV_SQRT_2PI


def _lane_col(v, lane):
    idx = lax.broadcasted_iota(jnp.int32, v.shape, 1)
    return jnp.sum(jnp.where(idx == lane, v, 0.0), axis=1, keepdims=True)


def _params(sem, vmem_mb):
    return pltpu.CompilerParams(dimension_semantics=sem, vmem_limit_bytes=vmem_mb << 20)


def _cat_groups(ref, n):
    if n == 1:
        return ref[0]
    return jnp.concatenate([ref[t] for t in range(n)], axis=1)


def _wkind(w):
    return (w[0], ("layer", w[1])) if isinstance(w, tuple) else (w, "full")


def _rowcall(name, body, T, tm, ins, outs, vmem_mb, scratch=()):
    def spec(shape, kind, resident_once=False):
        shape = tuple(shape)
        if isinstance(kind, tuple):
            layer = kind[1]
            return pl.BlockSpec((None,) + shape[1:], lambda i: (layer,) + (0,) * (len(shape) - 1),
                                pipeline_mode=pl.Buffered(1))
        if kind == "row":
            return pl.BlockSpec((tm,) + shape[1:], lambda i: (i,) + (0,) * (len(shape) - 1))
        if kind == "grp":
            return pl.BlockSpec((shape[0], tm, shape[2]), lambda i: (0, i, 0))
        if resident_once:
            return pl.BlockSpec(shape, lambda i: (0,) * len(shape), pipeline_mode=pl.Buffered(1))
        return pl.BlockSpec(shape, lambda i: (0,) * len(shape))

    return pl.pallas_call(
        body,
        name=name,
        grid=(T // tm,),
        in_specs=[spec(a.shape, k, True) for a, k in ins],
        out_specs=[spec(s, k) for s, _, k in outs],
        out_shape=[jax.ShapeDtypeStruct(tuple(s), d) for s, d, _ in outs],
        scratch_shapes=list(scratch),
        compiler_params=_params(("arbitrary",), vmem_mb),
    )(*[a for a, _ in ins])


def _norm_matmul(x, nw, w, out_dtype, name, tn, groups=False):
    w, layer = w
    T, N = x.shape[0], w.shape[2]
    tm = min(1024, T)

    def body(x_ref, nw_ref, w_ref, o_ref, h_ref, h_scr):
        @pl.when(pl.program_id(1) == 0)
        def _():
            xv = x_ref[...]
            r = lax.rsqrt(jnp.mean(xv * xv, axis=-1, keepdims=True) + NORM_EPS)
            hv = (xv * r * nw_ref[...]).astype(BF16)
            h_scr[...] = hv
            h_ref[...] = hv

        acc = jnp.dot(h_scr[...], w_ref[...], preferred_element_type=F32)
        if groups:
            for t in range(tn // LANES):
                o_ref[t] = acc[:, LANES * t:LANES * (t + 1)].astype(out_dtype)
        else:
            o_ref[...] = acc.astype(out_dtype)

    if groups:
        o_shape = (N // LANES, T, LANES)
        o_spec = pl.BlockSpec((tn // LANES, tm, LANES), lambda i, j: (j, i, 0))
    else:
        o_shape = (T, N)
        o_spec = pl.BlockSpec((tm, tn), lambda i, j: (i, j))
    return pl.pallas_call(
        body,
        name=name,
        grid=(T // tm, N // tn),
        in_specs=[
            pl.BlockSpec((tm, D_MODEL), lambda i, j: (i, 0)),
            pl.BlockSpec((1, D_MODEL), lambda i, j: (0, 0)),
            pl.BlockSpec((None, D_MODEL, tn), lambda i, j: (layer, 0, j)),
        ],
        out_specs=[o_spec, pl.BlockSpec((tm, D_MODEL), lambda i, j: (i, 0))],
        out_shape=[jax.ShapeDtypeStruct(o_shape, out_dtype), jax.ShapeDtypeStruct((T, D_MODEL), BF16)],
        scratch_shapes=[pltpu.VMEM((tm, D_MODEL), BF16)],
        compiler_params=_params(("arbitrary", "arbitrary"), 48),
    )(x, nw, w)


def _matmul_tn(a, g, name, tk, tn, a_grp=False, g_grp=False):
    T = a.shape[1] if a_grp else a.shape[0]
    K = a.shape[0] * LANES if a_grp else a.shape[1]
    N = g.shape[0] * LANES if g_grp else g.shape[1]
    tm = min(1024, T)

    def body(a_ref, g_ref, o_ref):
        @pl.when(pl.program_id(2) == 0)
        def _():
            o_ref[...] = jnp.zeros(o_ref.shape, F32)

        av = _cat_groups(a_ref, tk // LANES) if a_grp else a_ref[...]
        gv = _cat_groups(g_ref, tn // LANES) if g_grp else g_ref[...]
        o_ref[...] += lax.dot_general(av.astype(BF16), gv.astype(BF16), TN_DIMS, preferred_element_type=F32)

    if a_grp:
        a_spec = pl.BlockSpec((tk // LANES, tm, LANES), lambda k, n, m: (k, m, 0))
    else:
        a_spec = pl.BlockSpec((tm, tk), lambda k, n, m: (m, k))
    if g_grp:
        g_spec = pl.BlockSpec((tn // LANES, tm, LANES), lambda k, n, m: (n, m, 0))
    else:
        g_spec = pl.BlockSpec((tm, tn), lambda k, n, m: (m, n))
    return pl.pallas_call(
        body,
        name=name,
        grid=(K // tk, N // tn, T // tm),
        in_specs=[a_spec, g_spec],
        out_specs=pl.BlockSpec((tk, tn), lambda k, n, m: (k, n)),
        out_shape=jax.ShapeDtypeStruct((K, N), F32),
        compiler_params=_params(("parallel", "parallel", "arbitrary"), 48),
    )(a, g)


def _matmul_rms_bwd(a_list, wt_list, x, nw, dres, name, tm, vmem_mb):
    T = x.shape[0]
    n = len(a_list)

    def body(*refs):
        a_refs, w_refs = refs[:n], refs[n:2 * n]
        x_ref, nw_ref, dres_ref, dx_ref, dnw_ref = refs[2 * n:]

        @pl.when(pl.program_id(0) == 0)
        def _():
            dnw_ref[...] = jnp.zeros(dnw_ref.shape, F32)

        dh = None
        for (arr, kind), a_ref, w_ref in zip(a_list, a_refs, w_refs):
            av = _cat_groups(a_ref, arr.shape[0]) if kind == "grp" else a_ref[...]
            part = jnp.dot(av.astype(BF16), w_ref[...], preferred_element_type=F32)
            dh = part if dh is None else dh + part
        xv = x_ref[...]
        r = lax.rsqrt(jnp.mean(xv * xv, axis=-1, keepdims=True) + NORM_EPS)
        xn = xv * r
        dnw_ref[...] += jnp.sum(dh * xn, axis=0, keepdims=True)
        dyw = dh * nw_ref[...]
        dx_ref[...] = dres_ref[...] + r * (dyw - xn * jnp.mean(dyw * xn, axis=-1, keepdims=True))

    ins = list(a_list) + [_wkind(w) for w in wt_list] + [(x, "row"), (nw, "full"), (dres, "row")]
    outs = [((T, D_MODEL), F32, "row"), ((1, D_MODEL), F32, "full")]
    return _rowcall(name, body, T, tm, ins, outs, vmem_mb)


def _fgate_fwd(fl3, bcol, name):
    n_chunk = fl3.shape[0]

    def body(fl_ref, b_ref, c_ref):
        r = lax.broadcasted_iota(jnp.int32, (CHUNK, CHUNK), 0)
        t = lax.broadcasted_iota(jnp.int32, (CHUNK, CHUNK), 1)
        tri = jnp.where(r <= t, 1.0, 0.0).astype(BF16)

        def chunk(i, carry):
            z = fl_ref[i] + b_ref[...]
            lf = jnp.minimum(z, 0.0) - jnp.log(1.0 + jnp.exp(-jnp.abs(z)))
            hi = lf.astype(BF16)
            r1 = lf - hi.astype(F32)
            mid = r1.astype(BF16)
            low = (r1 - mid.astype(F32)).astype(BF16)
            cs = (jnp.dot(hi, tri, preferred_element_type=F32) + jnp.dot(mid, tri, preferred_element_type=F32)
                  + jnp.dot(low, tri, preferred_element_type=F32)) + carry
            c_ref[i] = cs
            return _lane_col(cs, CHUNK - 1)

        lax.fori_loop(0, n_chunk, chunk, jnp.zeros((2 * N_PAIR, 1), F32))

    return pl.pallas_call(
        body, name=name, out_shape=jax.ShapeDtypeStruct(fl3.shape, F32),
        compiler_params=pltpu.CompilerParams(vmem_limit_bytes=16 << 20),
    )(fl3, bcol)


def _fgate_bwd(dc3, fl3, bcol, name):
    n_chunk = fl3.shape[0]

    def body(dc_ref, fl_ref, b_ref, dfl_ref, db_ref):
        tt = lax.broadcasted_iota(jnp.int32, (CHUNK, CHUNK), 0)
        rr = lax.broadcasted_iota(jnp.int32, (CHUNK, CHUNK), 1)
        tri = jnp.where(tt >= rr, 1.0, 0.0).astype(BF16)

        def chunk(k, carry):
            tail, acc = carry
            i = n_chunk - 1 - k
            dc = dc_ref[i]
            hi = dc.astype(BF16)
            r1 = dc - hi.astype(F32)
            mid = r1.astype(BF16)
            low = (r1 - mid.astype(F32)).astype(BF16)
            dlf = (jnp.dot(hi, tri, preferred_element_type=F32) + jnp.dot(mid, tri, preferred_element_type=F32)
                   + jnp.dot(low, tri, preferred_element_type=F32)) + tail
            z = fl_ref[i] + b_ref[...]
            dfl = dlf / (1.0 + jnp.exp(z))
            dfl_ref[i] = dfl
            return _lane_col(dlf, 0), acc + dfl

        _, acc = lax.fori_loop(0, n_chunk, chunk,
                               (jnp.zeros((2 * N_PAIR, 1), F32), jnp.zeros((2 * N_PAIR, CHUNK), F32)))
        db_ref[...] = jnp.broadcast_to(jnp.sum(acc, axis=1, keepdims=True), db_ref.shape)

    return pl.pallas_call(
        body, name=name,
        out_shape=[jax.ShapeDtypeStruct(fl3.shape, F32), jax.ShapeDtypeStruct((2 * N_PAIR, LANES), F32)],
        compiler_params=pltpu.CompilerParams(vmem_limit_bytes=16 << 20),
    )(dc3, fl3, bcol)


BIAS_LANES = 3
ATT_Q_BLOCK = 2048


def _own_lanes(shape, hh, axis=1):
    idx = lax.broadcasted_iota(jnp.int32, shape, axis)
    return idx < HEAD_DIM if hh == 0 else idx >= HEAD_DIM


def _spare(hh):
    return HEAD_DIM * (1 - hh)


def _bias_pieces(c16):
    T = c16.shape[1]
    negc = -c16
    hi = negc.astype(BF16)
    r1 = negc - hi.astype(F32)
    mid = r1.astype(BF16)
    low = (r1 - mid.astype(F32)).astype(BF16)
    pieces = jnp.stack([hi, mid, low], axis=-1).reshape(N_PAIR, 2, T, BIAS_LANES)
    zpad = jnp.zeros((N_PAIR, T, HEAD_DIM - BIAS_LANES), BF16)
    return jnp.concatenate([pieces[:, 1], zpad, pieces[:, 0], zpad], axis=-1)


UNDERFLOW_BOUND = -110.0


def _attn_reach(qkv3, c16, tb, tq):
    T = qkv3.shape[1]
    nb = T // tb

    def block_norm(a):
        sq = jnp.sum(jnp.square(a.astype(F32)).reshape(N_PAIR, nb, tb, 2, HEAD_DIM), axis=-1)
        return jnp.transpose(jnp.sqrt(jnp.max(sq, axis=2)), (0, 2, 1))

    qn, kn = block_norm(qkv3[:N_PAIR]), block_norm(qkv3[N_PAIR:2 * N_PAIR])
    cb = c16.reshape(N_PAIR, 2, nb, tb)
    c_max, negc_max = jnp.max(cb, axis=-1), jnp.max(-cb, axis=-1)
    bound = (qn[:, :, :, None] * (kn[:, :, None, :] + kn[:, :, :, None]) * QK_SCALE
             + negc_max[:, :, None, :] + c_max[:, :, :, None])
    qi = lax.broadcasted_iota(jnp.int32, (nb, nb), 0)
    kj = lax.broadcasted_iota(jnp.int32, (nb, nb), 1)
    active = jnp.logical_or(bound > UNDERFLOW_BOUND, kj >= qi)
    last_q = jnp.max(jnp.where(active, qi, -1), axis=2)
    first_blk = jnp.min(jnp.where(active, kj, nb), axis=3)
    first_kv = jnp.min(first_blk.reshape(N_PAIR, 2, T // tq, tq // tb), axis=-1)
    return first_kv.astype(F32), last_q.astype(F32)


def _attn_fwd(qkv3, cp3, first_kv, name, ride=()):
    T = qkv3.shape[1]
    tb = min(ATT_BLOCK, T)
    tq = min(ATT_Q_BLOCK, T)
    nb = T // tb
    per_q = tq // tb

    assert per_q % 2 == 0 or T == tq, (T, tq, tb)

    n_ride = len(ride)

    def body(*refs):
        q_ref, k_ref, v_ref, cp_ref, first_ref = refs[:5]
        o_ref, lse_ref = refs[5 + n_ride:7 + n_ride]
        st_a, st_b = refs[7 + 2 * n_ride:9 + 2 * n_ride]
        h, i = pl.program_id(0), pl.program_id(1)
        if n_ride:
            starts, waits = _gather_copies(refs[5:5 + n_ride], refs[7 + n_ride:7 + 2 * n_ride],
                                           *refs[9 + 2 * n_ride:])

            @pl.when(jnp.logical_and(h == 0, i == 0))
            def _():
                for cp in starts:
                    cp.start()

        lane = lax.broadcasted_iota(jnp.int32, (tq, LANES), 1)
        lane_k = lax.broadcasted_iota(jnp.int32, (tb, LANES), 1)
        feat = lax.broadcasted_iota(jnp.int32, (LANES, tq), 0)
        q2 = q_ref[0] * QK_SCALE
        qa, own_k, one_k = [], [], []
        for hh in range(2):
            bias = jnp.logical_and(lane >= _spare(hh), lane < _spare(hh) + BIAS_LANES)
            qa.append(jnp.where(_own_lanes((tq, LANES), hh), q2, jnp.where(bias, 1.0, 0.0).astype(BF16)))
            own_k.append(_own_lanes((tb, LANES), hh))
            one_k.append(jnp.where(lane_k == _spare(hh), 1.0, 0.0).astype(BF16))

        def scores(j, scr, heads=(0, 1)):
            off = pl.multiple_of(j * tb, tb)
            kb, cb = k_ref[0, pl.ds(off, tb), :], cp_ref[0, pl.ds(off, tb), :]
            for hh in heads:
                scr[hh] = lax.dot_general(jnp.where(own_k[hh], kb, cb), qa[hh], NT_DIMS, preferred_element_type=F32)

        def step(j, carry, first, scr=None, heads=(0, 1)):
            off = pl.multiple_of(j * tb, tb)
            kb, vb, cb = k_ref[0, pl.ds(off, tb), :], v_ref[0, pl.ds(off, tb), :], cp_ref[0, pl.ds(off, tb), :]
            lo = 0 if first is None else first
            out = list(carry)
            for hh in heads:
                m_all, acc_all = carry[2 * hh], carry[2 * hh + 1]
                m_old, acc = m_all[:, lo:], acc_all[:, lo:]
                if scr is None:
                    st = lax.dot_general(jnp.where(own_k[hh], kb, cb), qa[hh][lo:], NT_DIMS,
                                         preferred_element_type=F32)
                else:
                    st = scr[hh]
                if first is not None:
                    key = lax.broadcasted_iota(jnp.int32, (tb, tq - lo), 0)
                    qry = lax.broadcasted_iota(jnp.int32, (tb, tq - lo), 1)
                    st = jnp.where(key <= qry, st, -jnp.inf)
                m = jnp.maximum(m_old, jnp.max(st, axis=0, keepdims=True))
                p = jnp.exp(st - m)
                acc = jnp.exp(m_old - m) * acc + lax.dot_general(
                    jnp.where(own_k[hh], vb, one_k[hh]), p.astype(BF16), TN_DIMS, preferred_element_type=F32)
                if lo:
                    m = jnp.concatenate([m_all[:, :lo], m], axis=1)
                    acc = jnp.concatenate([acc_all[:, :lo], acc], axis=1)
                out[2 * hh], out[2 * hh + 1] = m, acc
            return tuple(out)

        ninf = jnp.full((1, tq), -jnp.inf, F32)
        zacc = jnp.zeros((LANES, tq), F32)

        def make_pair(heads):
            def pair(jj, c):
                j = 2 * jj
                scores(j + 1, st_b, heads)
                c = step(j, c, None, st_a, heads)
                scores(j + 2, st_a, heads)
                return step(j + 1, c, None, st_b, heads)
            return pair

        end_pair = (i * per_q) // 2
        first_pair = [jnp.clip(first_ref[h, hh, i].astype(jnp.int32), 0, i * per_q) // 2 for hh in range(2)]
        both_pair = jnp.maximum(first_pair[0], first_pair[1])
        carry = (ninf, zacc, ninf, zacc)
        for hh in range(2):
            scores(2 * first_pair[hh], st_a, (hh,))
            carry = lax.fori_loop(first_pair[hh], both_pair, make_pair((hh,)), carry)
        scores(2 * both_pair, st_a)
        carry = lax.fori_loop(both_pair, end_pair, make_pair((0, 1)), carry)
        carry = step(i * per_q, carry, 0, st_a)
        for t in range(1, per_q):
            carry = step(i * per_q + t, carry, t * tb)
        outs = []
        for hh in range(2):
            m, acc = carry[2 * hh], carry[2 * hh + 1]
            l = jnp.sum(jnp.where(feat == _spare(hh), acc, 0.0), axis=0, keepdims=True)
            lse_ref[0, 0, hh:hh + 1, :] = m + jnp.log(l)
            outs.append(acc * (1.0 / l))
        o_ref[0] = jnp.where(feat < HEAD_DIM, outs[0], outs[1]).astype(BF16)

        if n_ride:
            @pl.when(jnp.logical_and(h == N_PAIR - 1, i == T // tq - 1))
            def _():
                for wait in waits:
                    wait()

    res = lambda base: pl.BlockSpec((1, T, LANES), lambda h, i: (base + h, 0, 0), pipeline_mode=pl.Buffered(1))
    hbm = pl.BlockSpec(memory_space=pl.ANY)
    return pl.pallas_call(
        body,
        name=name,
        grid=(N_PAIR, T // tq),
        in_specs=[pl.BlockSpec((1, tq, LANES), lambda h, i: (h, i, 0)), res(N_PAIR), res(2 * N_PAIR), res(0),
                  pl.BlockSpec(memory_space=pltpu.SMEM)] + [hbm] * n_ride,
        out_specs=[pl.BlockSpec((1, LANES, tq), lambda h, i: (h, 0, i)),
                   pl.BlockSpec((1, 1, 2, tq), lambda h, i: (h, i, 0, 0))] + [hbm] * n_ride,
        out_shape=[jax.ShapeDtypeStruct((N_PAIR, LANES, T), BF16),
                   jax.ShapeDtypeStruct((N_PAIR, T // tq, 2, tq), F32)]
        + [jax.ShapeDtypeStruct((N_DEV,) + a.shape, a.dtype) for a in ride],
        scratch_shapes=[pltpu.VMEM((2, tb, tq), F32), pltpu.VMEM((2, tb, tq), F32)]
        + (_exchange_sems(n_ride) if n_ride else []),
        compiler_params=_params(("arbitrary", "arbitrary"), 56),
    )(qkv3, qkv3, qkv3, cp3, first_kv, *ride)


def _attn_bwd(qkv3, cp3, kst4, do3, st4, last_q, name, ride=()):
    T = qkv3.shape[1]
    tb = min(ATT_BLOCK, T)
    nb = T // tb

    n_ride = len(ride)

    def body(*refs):
        q_ref, do_ref, st_ref, k_ref, v_ref, cp_ref, kst_ref, last_ref = refs[:8]
        ride_in = refs[8:8 + n_ride]
        dq_hbm, rs_ref, dk_ref, dv_ref, cs_ref = refs[8 + n_ride:13 + n_ride]
        ride_out = refs[13 + n_ride:13 + 2 * n_ride]
        dq_acc, dk_acc, dv_acc = refs[13 + 2 * n_ride:16 + 2 * n_ride]
        h, j = pl.program_id(0), pl.program_id(1)
        if n_ride:
            starts, waits = _exchange_copies(ride_in, ride_out, *refs[16 + 2 * n_ride:])

            @pl.when(jnp.logical_and(h == 0, j == 0))
            def _():
                for cp in starts:
                    cp.start()

        lane = lax.broadcasted_iota(jnp.int32, (tb, LANES), 1)
        own = [_own_lanes((tb, LANES), hh) for hh in range(2)]
        bias = [jnp.logical_and(lane >= _spare(hh), lane < _spare(hh) + BIAS_LANES) for hh in range(2)]
        key = lax.broadcasted_iota(jnp.int32, (tb, tb), 0)
        qry = lax.broadcasted_iota(jnp.int32, (tb, tb), 1)

        @pl.when(j == 0)
        def _():
            dq_acc[...] = jnp.zeros(dq_acc.shape, F32)
            rs_ref[...] = jnp.zeros(rs_ref.shape, F32)

        vb = v_ref[0]
        zero = jnp.zeros_like(vb)
        one = jnp.ones_like(vb)
        bias_one = [jnp.where(bias[hh], one, zero) for hh in range(2)]
        kb = [jnp.where(own[hh], k_ref[0], cp_ref[0]) for hh in range(2)]
        kst = kst_ref[0, 0]
        kst = [jnp.where(_own_lanes((LANES, tb), hh, axis=0), kst, jnp.zeros_like(kst)) for hh in range(2)]
        vm = [jnp.where(own[hh], vb, zero) for hh in range(2)]
        dk_acc[...] = jnp.zeros(dk_acc.shape, F32)
        dv_acc[...] = jnp.zeros(dv_acc.shape, F32)

        def step(i, masked, heads=(0, 1)):
            off = pl.multiple_of(i * tb, tb)
            qb = q_ref[0, pl.ds(off, tb), :] * QK_SCALE
            dob = do_ref[0, pl.ds(off, tb), :]
            dq = None
            for hh in heads:
                st = lax.dot_general(kb[hh], jnp.where(own[hh], qb, bias_one[hh]), NT_DIMS,
                                     preferred_element_type=F32)
                if masked:
                    st = jnp.where(key <= qry, st, -jnp.inf)
                p = jnp.exp(st - st_ref[0, i, hh:hh + 1, :])
                dp = lax.dot_general(vm[hh], dob, NT_DIMS, preferred_element_type=F32)
                dsb = (p * (dp - st_ref[0, i, 2 + hh:3 + hh, :])).astype(BF16)
                dv_acc[hh] += jnp.dot(p.astype(BF16), dob, preferred_element_type=F32)
                dk_acc[hh] += jnp.dot(dsb, jnp.where(own[hh], qb, one), preferred_element_type=F32)
                rs_ref[0, i, hh:hh + 1, :] += jnp.sum(dsb.astype(F32), axis=0, keepdims=True)
                d = jnp.dot(kst[hh], dsb, preferred_element_type=F32)
                dq = d if dq is None else dq + d
            dq_acc[i] += dq

        step(j, True)

        def make_body(heads):
            def loop_body(i, carry):
                step(i, False, heads)
                return carry
            return loop_body

        last = [jnp.clip(last_ref[h, hh, j].astype(jnp.int32), j, nb - 1) for hh in range(2)]
        both = jnp.minimum(last[0], last[1])
        lax.fori_loop(j + 1, both + 1, make_body((0, 1)), 0)
        for hh in range(2):
            lax.fori_loop(both + 1, last[hh] + 1, make_body((hh,)), 0)
        dk_ref[0] = jnp.where(own[0], dk_acc[0], dk_acc[1]).astype(BF16)
        dv_ref[0] = jnp.where(own[0], dv_acc[0], dv_acc[1]).astype(BF16)
        lane8 = lax.broadcasted_iota(jnp.int32, (8, LANES), 1)
        for hh in range(2):
            pick = jnp.where(lane8 == _spare(hh), 1.0, 0.0).astype(BF16)
            x = dk_acc[hh]
            hi = x.astype(BF16)
            r1 = x - hi.astype(F32)
            mid = r1.astype(BF16)
            low = (r1 - mid.astype(F32)).astype(BF16)
            cs_ref[0, 0, 8 * hh:8 * hh + 8, :] = (
                lax.dot_general(pick, hi, NT_DIMS, preferred_element_type=F32)
                + lax.dot_general(pick, mid, NT_DIMS, preferred_element_type=F32)
                + lax.dot_general(pick, low, NT_DIMS, preferred_element_type=F32))

        @pl.when(j == nb - 1)
        def _():
            pltpu.sync_copy(dq_acc, dq_hbm.at[h])

        if n_ride:
            @pl.when(jnp.logical_and(h == N_PAIR - 1, j == nb - 1))
            def _():
                for wait in waits:
                    wait()

    res = pl.BlockSpec((1, T, LANES), lambda h, j: (h, 0, 0), pipeline_mode=pl.Buffered(1))
    tile = lambda base: pl.BlockSpec((1, tb, LANES), lambda h, j: (base + h, j, 0))
    rows = lambda n: pl.BlockSpec((1, nb, n, tb), lambda h, j: (h, 0, 0, 0))
    hbm = pl.BlockSpec(memory_space=pl.ANY)
    return pl.pallas_call(
        body,
        name=name,
        grid=(N_PAIR, nb),
        in_specs=[res, res, rows(4), tile(N_PAIR), tile(2 * N_PAIR), tile(0),
                  pl.BlockSpec((1, 1, LANES, tb), lambda h, j: (h, j, 0, 0)),
                  pl.BlockSpec(memory_space=pltpu.SMEM)] + [hbm] * n_ride,
        out_specs=[hbm, rows(2), tile(0), tile(0),
                   pl.BlockSpec((1, 1, 16, tb), lambda h, j: (h, j, 0, 0))] + [hbm] * n_ride,
        out_shape=[
            jax.ShapeDtypeStruct((N_PAIR, nb, LANES, tb), F32),
            jax.ShapeDtypeStruct((N_PAIR, nb, 2, tb), F32),
            jax.ShapeDtypeStruct((N_PAIR, T, LANES), BF16),
            jax.ShapeDtypeStruct((N_PAIR, T, LANES), BF16),
            jax.ShapeDtypeStruct((N_PAIR, nb, 16, tb), F32),
        ] + [jax.ShapeDtypeStruct(a.shape, a.dtype) for a in ride],
        scratch_shapes=[pltpu.VMEM((nb, LANES, tb), F32), pltpu.VMEM((2, tb, LANES), F32),
                        pltpu.VMEM((2, tb, LANES), F32)] + (_exchange_sems(n_ride) if n_ride else []),
        compiler_params=_params(("arbitrary", "arbitrary"), 56),
    )(qkv3, do3, st4, qkv3, qkv3, cp3, kst4, last_q, *ride)


def _attn_out(o3, w, x, name):
    T = x.shape[0]

    def body(o_ref, w_ref, x_ref, out_ref):
        out_ref[...] = x_ref[...] + jnp.dot(_cat_groups(o_ref, N_PAIR), w_ref[...], preferred_element_type=F32)

    return _rowcall(name, body, T, min(512, T), [(o3, "grp"), _wkind(w), (x, "row")],
                    [((T, D_MODEL), F32, "row")], 24)[0]


def _attn_dout(dx, wt, o3, name):
    T = dx.shape[0]
    tm = min(512, T)

    def body(dx_ref, w_ref, o_ref, do_ref, dd_ref):
        do = jnp.dot(dx_ref[...].astype(BF16), w_ref[...], preferred_element_type=F32).astype(BF16)
        lo = _own_lanes((tm, LANES), 0)
        head = lax.broadcasted_iota(jnp.int32, (tm, 2 * N_PAIR), 1)
        dd = jnp.zeros((tm, 2 * N_PAIR), F32)
        for t in range(N_PAIR):
            d = do[:, LANES * t:LANES * (t + 1)]
            do_ref[t] = d
            prod = d.astype(F32) * o_ref[t].astype(F32)
            d0 = jnp.sum(jnp.where(lo, prod, 0.0), axis=1, keepdims=True)
            d1 = jnp.sum(jnp.where(lo, 0.0, prod), axis=1, keepdims=True)
            dd = jnp.where(head == 2 * t, d0, jnp.where(head == 2 * t + 1, d1, dd))
        dd_ref[...] = dd

    return _rowcall(name, body, T, tm, [(dx, "row"), _wkind(wt), (o3, "grp")],
                    [((N_PAIR, T, LANES), BF16, "grp"), ((T, 2 * N_PAIR), F32, "row")], 32)


def _ffn_out(gu, w, x, name):
    T = x.shape[0]
    tm = min(512, T)

    def body(gu_ref, w_ref, x_ref, out_ref, hid_ref):
        acc = x_ref[...]
        for j in range(N_FFN_TILE):
            g = gu_ref[:, 2 * FFN_TILE * j:2 * FFN_TILE * j + FFN_TILE].astype(F32)
            u = gu_ref[:, 2 * FFN_TILE * j + FFN_TILE:2 * FFN_TILE * (j + 1)].astype(F32)
            hj = (g * jax.nn.sigmoid(g) * u).astype(BF16)
            hid_ref[:, FFN_TILE * j:FFN_TILE * (j + 1)] = hj
            acc = acc + jnp.dot(hj, w_ref[FFN_TILE * j:FFN_TILE * (j + 1), :], preferred_element_type=F32)
        out_ref[...] = acc

    return _rowcall(name, body, T, tm, [(gu, "row"), _wkind(w), (x, "row")],
                    [((T, D_MODEL), F32, "row"), ((T, FFN_H), BF16, "row")], 48)


def _ffn_dgu(dx, wt, gu, name):
    T = dx.shape[0]
    tm = min(512, T)

    def body(dx_ref, w_ref, gu_ref, dgu_ref):
        dxb = dx_ref[...].astype(BF16)
        for j in range(N_FFN_TILE):
            dh = jnp.dot(dxb, w_ref[:, FFN_TILE * j:FFN_TILE * (j + 1)], preferred_element_type=F32)
            g = gu_ref[:, 2 * FFN_TILE * j:2 * FFN_TILE * j + FFN_TILE].astype(F32)
            u = gu_ref[:, 2 * FFN_TILE * j + FFN_TILE:2 * FFN_TILE * (j + 1)].astype(F32)
            sg = jax.nn.sigmoid(g)
            dgu_ref[:, 2 * FFN_TILE * j:2 * FFN_TILE * j + FFN_TILE] = (
                dh * u * (sg * (1.0 + g * (1.0 - sg)))).astype(BF16)
            dgu_ref[:, 2 * FFN_TILE * j + FFN_TILE:2 * FFN_TILE * (j + 1)] = (dh * (g * sg)).astype(BF16)

    return _rowcall(name, body, T, tm, [(dx, "row"), _wkind(wt), (gu, "row")],
                    [((T, 2 * FFN_H), BF16, "row")], 48)[0]


def _sgu_core(a, ln_g, ln_b, w_s, bst, w, x, name):
    T = x.shape[0]
    tm = min(256, T)

    def body(a_ref, lg_ref, lb_ref, ws_ref, bs_ref, w_ref, x_ref, out_ref, gated_ref, vn_ref, mixed_ref):
        v = _gelu(a_ref[:, SGU_W:].astype(F32))
        mu = jnp.mean(v, axis=-1, keepdims=True)
        vc = v - mu
        rstd = lax.rsqrt(jnp.mean(vc * vc, axis=-1, keepdims=True) + LN_EPS)
        vn_ref[...] = (vc * rstd * lg_ref[...] + lb_ref[...]).astype(BF16)
        tt = lax.broadcasted_iota(jnp.int32, (CHUNK, CHUNK), 0)
        ss = lax.broadcasted_iota(jnp.int32, (CHUNK, CHUNK), 1)
        for g in range(SGU_G):
            wg = jnp.where(tt >= ss, ws_ref[g], 0.0).astype(BF16)
            bcol = _lane_col(bs_ref[...], g)
            cols = slice(CHUNK * g, CHUNK * (g + 1))
            for c in range(tm // CHUNK):
                rows = slice(CHUNK * c, CHUNK * (c + 1))
                mixed = jnp.dot(wg, vn_ref[rows, cols], preferred_element_type=F32) + bcol
                u = _gelu(a_ref[rows, cols].astype(F32))
                mixed_ref[rows, cols] = mixed.astype(BF16)
                gated_ref[rows, cols] = (u * mixed).astype(BF16)
        out_ref[...] = x_ref[...] + jnp.dot(gated_ref[...], w_ref[...], preferred_element_type=F32)

    ins = [(a, "row"), (ln_g, "full"), (ln_b, "full"), (w_s, "full"), (bst, "full"), _wkind(w), (x, "row")]
    outs = [((T, D_MODEL), F32, "row")] + [((T, SGU_W), BF16, "row")] * 3
    return _rowcall(name, body, T, tm, ins, outs, 40)


def _sgu_core_bwd(dx, wt, a, vn, mixed, ln_g, w_s, name):
    T = dx.shape[0]
    tm = min(256, T)

    def body(dx_ref, wt_ref, a_ref, vn_ref, mx_ref, lg_ref, ws_ref,
             da_ref, dws_ref, dba_ref, dlg_ref, dlb_ref, dg_scr, dvn_scr):
        @pl.when(pl.program_id(0) == 0)
        def _():
            dws_ref[...] = jnp.zeros(dws_ref.shape, F32)
            dba_ref[...] = jnp.zeros(dba_ref.shape, F32)
            dlg_ref[...] = jnp.zeros(dlg_ref.shape, F32)
            dlb_ref[...] = jnp.zeros(dlb_ref.shape, F32)

        dg_scr[...] = jnp.dot(dx_ref[...].astype(BF16), wt_ref[...], preferred_element_type=F32)
        tt = lax.broadcasted_iota(jnp.int32, (CHUNK, CHUNK), 0)
        ss = lax.broadcasted_iota(jnp.int32, (CHUNK, CHUNK), 1)
        tril = tt >= ss
        for g in range(SGU_G):
            wg = jnp.where(tril, ws_ref[g], 0.0).astype(BF16)
            cols = slice(CHUNK * g, CHUNK * (g + 1))
            for c in range(tm // CHUNK):
                rows = slice(CHUNK * c, CHUNK * (c + 1))
                dgb = dg_scr[rows, cols]
                au = a_ref[rows, cols].astype(F32)
                dmx = dgb * _gelu(au)
                da_ref[rows, cols] = (dgb * mx_ref[rows, cols].astype(F32) * _gelu_grad(au)).astype(BF16)
                dmb = dmx.astype(BF16)
                dvn_scr[rows, cols] = lax.dot_general(wg, dmb, TN_DIMS, preferred_element_type=F32)
                dws_ref[g] += jnp.where(
                    tril, lax.dot_general(dmb, vn_ref[rows, cols], NT_DIMS, preferred_element_type=F32), 0.0)
                dba_ref[:, cols] += dmx
        av = a_ref[:, SGU_W:].astype(F32)
        v = _gelu(av)
        mu = jnp.mean(v, axis=-1, keepdims=True)
        vc = v - mu
        rstd = lax.rsqrt(jnp.mean(vc * vc, axis=-1, keepdims=True) + LN_EPS)
        xhat = vc * rstd
        dvn = dvn_scr[...]
        dlg_ref[...] += jnp.sum(dvn * xhat, axis=0, keepdims=True)
        dlb_ref[...] += jnp.sum(dvn, axis=0, keepdims=True)
        dxh = dvn * lg_ref[...]
        dv = rstd * (dxh - jnp.mean(dxh, axis=-1, keepdims=True)
                     - xhat * jnp.mean(dxh * xhat, axis=-1, keepdims=True))
        da_ref[:, SGU_W:] = (dv * _gelu_grad(av)).astype(BF16)

    ins = [(dx, "row"), _wkind(wt), (a, "row"), (vn, "row"), (mixed, "row"), (ln_g, "full"), (w_s, "full")]
    outs = [((T, 2 * SGU_W), BF16, "row"), ((SGU_G, CHUNK, CHUNK), F32, "full"), ((CHUNK, SGU_W), F32, "full"),
            ((1, SGU_W), F32, "full"), ((1, SGU_W), F32, "full")]
    return _rowcall(name, body, T, tm, ins, outs, 40,
                    scratch=[pltpu.VMEM((tm, SGU_W), F32), pltpu.VMEM((tm, SGU_W), F32)])


def _loss_head(x, wf, tgt, name):
    T = x.shape[0]
    tm = min(512, T)

    def body(x_ref, wf_ref, tgt_ref, dx_ref, dwf_ref, loss_ref):
        @pl.when(pl.program_id(0) == 0)
        def _():
            dwf_ref[...] = jnp.zeros(dwf_ref.shape, F32)
            loss_ref[...] = jnp.zeros(loss_ref.shape, F32)

        xv = x_ref[...]
        r = lax.rsqrt(jnp.mean(xv * xv, axis=-1, keepdims=True) + NORM_EPS)
        xn = xv * r
        err = xn * wf_ref[...] - tgt_ref[...]
        loss_ref[...] += 0.5 * jnp.sum(jnp.mean(err * err, axis=-1, keepdims=True), axis=0, keepdims=True)
        dy = err * (1.0 / D_MODEL)
        dwf_ref[...] += jnp.sum(dy * xn, axis=0, keepdims=True)
        dyw = dy * wf_ref[...]
        dx_ref[...] = r * (dyw - xn * jnp.mean(dyw * xn, axis=-1, keepdims=True))

    return _rowcall(name, body, T, tm, [(x, "row"), (wf, "full"), (tgt, "row")],
                    [((T, D_MODEL), F32, "row"), ((1, D_MODEL), F32, "full"), ((8, LANES), F32, "full")], 32)


def _peers():
    x, y, c = lax.axis_index("x"), lax.axis_index("y"), lax.axis_index("c")
    peers = []
    for p in range(1, N_DEV):
        px = 1 - x if p & 4 else x
        py = 1 - y if p & 2 else y
        pc = 1 - c if p & 1 else c
        peers.append((4 * px + 2 * py + pc, (px, py, pc)))
    return 4 * x + 2 * y + c, peers


def _all_gather(arrs, name):
    n = len(arrs)
    hbm = pl.BlockSpec(memory_space=pl.ANY)

    def body(*refs):
        starts, waits = _gather_copies(refs[:n], refs[n:2 * n], *refs[2 * n:])
        for cp in starts:
            cp.start()
        for wait in waits:
            wait()

    return pl.pallas_call(
        body,
        name=name,
        in_specs=[hbm] * n,
        out_specs=[hbm] * n,
        out_shape=[jax.ShapeDtypeStruct((N_DEV,) + a.shape, a.dtype) for a in arrs],
        scratch_shapes=_exchange_sems(n),
    )(*arrs)


def _gather_copies(ins, outs, send_sems, recv_sems, local_sems):
    me, peers = _peers()
    starts, waits = [], []
    for t in range(len(ins)):
        local = pltpu.make_async_copy(ins[t], outs[t].at[me], local_sems.at[t])
        starts.append(local)
        waits.append(local.wait)
        for k, (pidx, pid) in enumerate(peers):
            s = t * (N_DEV - 1) + k
            send = pltpu.make_async_remote_copy(
                src_ref=ins[t], dst_ref=outs[t].at[me], send_sem=send_sems.at[s], recv_sem=recv_sems.at[s],
                device_id=pid, device_id_type=MESH)
            arrival = pltpu.make_async_remote_copy(
                src_ref=ins[t], dst_ref=outs[t].at[pidx], send_sem=send_sems.at[s], recv_sem=recv_sems.at[s],
                device_id=pid, device_id_type=MESH)
            starts.append(send)
            waits += [arrival.wait_recv, send.wait_send]
    return starts, waits


def _exchange_sems(n):
    return [pltpu.SemaphoreType.DMA((n * (N_DEV - 1),)), pltpu.SemaphoreType.DMA((n * (N_DEV - 1),)),
            pltpu.SemaphoreType.DMA((n,))]


def _exchange_copies(ins, outs, send_sems, recv_sems, local_sems):
    me, peers = _peers()
    starts, waits = [], []
    for t in range(len(ins)):
        local = pltpu.make_async_copy(ins[t].at[me], outs[t].at[me], local_sems.at[t])
        starts.append(local)
        waits.append(local.wait)
        for k, (pidx, pid) in enumerate(peers):
            s = t * (N_DEV - 1) + k
            send = pltpu.make_async_remote_copy(
                src_ref=ins[t].at[pidx], dst_ref=outs[t].at[me], send_sem=send_sems.at[s],
                recv_sem=recv_sems.at[s], device_id=pid, device_id_type=MESH)
            arrival = pltpu.make_async_remote_copy(
                src_ref=ins[t].at[pidx], dst_ref=outs[t].at[pidx], send_sem=send_sems.at[s],
                recv_sem=recv_sems.at[s], device_id=pid, device_id_type=MESH)
            starts.append(send)
            waits += [arrival.wait_recv, send.wait_send]
    return starts, waits


def _exchange(arrs, name):
    n = len(arrs)
    hbm = pl.BlockSpec(memory_space=pl.ANY)

    def body(*refs):
        starts, waits = _exchange_copies(refs[:n], refs[n:2 * n], *refs[2 * n:])
        for cp in starts:
            cp.start()
        for wait in waits:
            wait()

    return pl.pallas_call(
        body,
        name=name,
        in_specs=[hbm] * n,
        out_specs=[hbm] * n,
        out_shape=[jax.ShapeDtypeStruct(a.shape, a.dtype) for a in arrs],
        scratch_shapes=_exchange_sems(n),
    )(*arrs)


def _row_tile(rows, cap):
    best = None
    for t in range(16, cap + 1, 16):
        if rows % t == 0:
            best = t
    assert best is not None, rows
    return best


def _adam_sum(parts, w, m, v, name):
    R, C = w.shape
    tr = _row_tile(R, 128)

    def body(p_ref, w_ref, m_ref, v_ref, g_ref, d_ref, nm_ref, nv_ref):
        g = p_ref[0].astype(F32)
        for s in range(1, N_DEV):
            g = g + p_ref[s].astype(F32)
        mm = ADAM_B1 * m_ref[...] + (1.0 - ADAM_B1) * g
        vv = ADAM_B2 * v_ref[...] + (1.0 - ADAM_B2) * (g * g)
        m_hat = mm / (1.0 - ADAM_B1 ** ADAM_STEP)
        v_hat = vv / (1.0 - ADAM_B2 ** ADAM_STEP)
        g_ref[...] = g
        d_ref[...] = -ADAM_LR * (m_hat / (jnp.sqrt(v_hat) + ADAM_EPS) + ADAM_WD * w_ref[...])
        nm_ref[...] = mm
        nv_ref[...] = vv

    mat = pl.BlockSpec((tr, C), lambda i: (i, 0))
    return pl.pallas_call(
        body,
        name=name,
        grid=(R // tr,),
        in_specs=[pl.BlockSpec((N_DEV, tr, C), lambda i: (0, i, 0)), mat, mat, mat],
        out_specs=[mat] * 4,
        out_shape=[jax.ShapeDtypeStruct((R, C), F32)] * 4,
        compiler_params=_params(("parallel",), 32),
    )(parts, w, m, v)


def _cols_from_gathered(g):
    _, L, K, n = g.shape
    return jnp.transpose(g, (1, 2, 0, 3)).reshape(L, K, N_DEV * n)


def _rows_from_gathered(g):
    _, L, k, N = g.shape
    return jnp.transpose(g, (1, 0, 2, 3)).reshape(L, N_DEV * k, N)


def _cols_to_blocks(dw):
    L, K, N = dw.shape
    n = N // N_DEV
    return jnp.transpose(dw.reshape(L, K, N_DEV, n), (2, 0, 1, 3)).reshape(N_DEV, L * K, n)


def _rows_to_blocks(dw):
    L, K, N = dw.shape
    k = K // N_DEV
    return jnp.transpose(dw.reshape(L, N_DEV, k, N), (1, 0, 2, 3)).reshape(N_DEV, L * k, N)


def _ffn_interleave(w):
    lead = w.shape[:-1]
    t = w.reshape(lead + (2, N_FFN_TILE, FFN_TILE))
    return jnp.swapaxes(t, -3, -2).reshape(lead + (2 * FFN_H,))


def _ffn_deinterleave(w):
    lead = w.shape[:-1]
    t = w.reshape(lead + (N_FFN_TILE, 2, FFN_TILE))
    return jnp.swapaxes(t, -3, -2).reshape(lead + (2 * FFN_H,))


def _pad_rows(a, rows=8):
    a = a.reshape(-1, a.shape[-1])
    return jnp.pad(a, ((0, rows - a.shape[0]), (0, 0)))


SMALL_ROWS = 6 * 8 + 2 * SGU_G * CHUNK * CHUNK // D_MODEL


def _pack_small(mixer, ffn, final, b_f, extra, w_s, b_s):
    bf_row = jnp.pad(b_f.reshape(1, -1), ((0, 0), (0, D_MODEL - b_f.size)))
    bs_rows = jnp.pad(b_s.reshape(4, -1), ((0, 0), (0, D_MODEL - b_s.size // 4)))
    return jnp.concatenate([
        _pad_rows(mixer), _pad_rows(ffn), _pad_rows(final.reshape(1, -1)), _pad_rows(bf_row),
        _pad_rows(extra), _pad_rows(bs_rows), w_s.reshape(-1, D_MODEL)], axis=0)


def _unpack_small(p):
    mixer, ffn, final = p[0:4], p[8:12], p[16]
    b_f = p[24, :32].reshape(2, 2 * N_PAIR)
    extra = p[32]
    b_s = p[40:44, :2 * SGU_G * CHUNK // 4].reshape(2, SGU_G, CHUNK)
    w_s = p[48:].reshape(2, SGU_G, CHUNK, CHUNK)
    return mixer, ffn, final, b_f, extra, w_s, b_s


def kernel(x, mixer_norm_w, attn_w_in, attn_b_f, attn_w_out, sgu_w_in, sgu_ln_g, sgu_ln_b, sgu_w_s, sgu_b_s, sgu_w_out, ffn_norm_w, ffn_w_in, ffn_w_out, final_norm_w, loss_target, m_mixer_norm_w, m_attn_w_in, m_attn_b_f, m_attn_w_out, m_sgu_w_in, m_sgu_ln_g, m_sgu_ln_b, m_sgu_w_s, m_sgu_b_s, m_sgu_w_out, m_ffn_norm_w, m_ffn_w_in, m_ffn_w_out, m_final_norm_w, v_mixer_norm_w, v_attn_w_in, v_attn_b_f, v_attn_w_out, v_sgu_w_in, v_sgu_ln_g, v_sgu_ln_b, v_sgu_w_s, v_sgu_b_s, v_sgu_w_out, v_ffn_norm_w, v_ffn_w_in, v_ffn_w_out, v_final_norm_w):
    T = x.shape[1]
    tb = min(ATT_BLOCK, T)
    xs, tgt = x[0], loss_target[0]

    tr = lambda w: jnp.swapaxes(w, -1, -2)
    bf = lambda a: a.astype(BF16)

    def attn_weights(g_in, g_out):
        w_in = _cols_from_gathered(g_in[:, None])
        w_qkv = w_in[:, :, :3 * D_MODEL]
        w_f = jnp.pad(w_in[:, :, 3 * D_MODEL:], ((0, 0), (0, 0), (0, LANES - 2 * N_PAIR)))
        w_out = _rows_from_gathered(g_out[:, None])
        return dict(qkv=w_qkv, f=w_f, qkv_t=tr(w_qkv).reshape(3, D_MODEL, D_MODEL), f_t=tr(w_f), out=w_out,
                    out_t=tr(w_out))

    attn_w = [attn_weights(*_all_gather([bf(attn_w_in[0]), bf(attn_w_out[0])], "gather_weights")), None]
    later = [bf(attn_w_in[1]), bf(attn_w_out[1]), bf(sgu_w_in), bf(sgu_w_out), bf(ffn_w_in), bf(ffn_w_out),
             sgu_ln_g, sgu_ln_b]
    mixer_nw = mixer_norm_w.reshape(4, 1, D_MODEL)
    ffn_nw = ffn_norm_w.reshape(4, 1, D_MODEL)
    b_col = attn_b_f.reshape(2, 2 * N_PAIR, 1)
    bs_t = jnp.swapaxes(sgu_b_s, 1, 2)

    saved = []
    xr = xs
    for i in range(4):
        j = i // 2
        if i % 2 == 0:
            qkv3, h = _norm_matmul(xr, mixer_nw[i], (attn_w[j]["qkv"], 0), BF16, f"attn_qkv_{j}", 1024, groups=True)
            fl, _ = _norm_matmul(xr, mixer_nw[i], (attn_w[j]["f"], 0), F32, f"attn_gate_{j}", LANES)
            fl3 = jnp.transpose(fl[:, :2 * N_PAIR].reshape(T // CHUNK, CHUNK, 2 * N_PAIR), (0, 2, 1))
            c_chunks = _fgate_fwd(fl3, b_col[j], f"fgate_fwd_{j}")
            c16 = jnp.transpose(c_chunks, (1, 0, 2)).reshape(2 * N_PAIR, T)
            cp3 = _bias_pieces(c16)
            first_kv, last_q = _attn_reach(qkv3, c16, tb, min(ATT_Q_BLOCK, T))
            fwd_out = _attn_fwd(qkv3, cp3, first_kv, f"attn_fwd_{j}", later if i == 0 else ())
            ot3, lse4 = fwd_out[:2]
            if i == 0:
                g = fwd_out[2:]
                attn_w[1] = attn_weights(g[0], g[1])
                w_sgu_in = _cols_from_gathered(g[2])
                w_sgu_out = _rows_from_gathered(g[3])
                w_ffn_in = _ffn_interleave(_cols_from_gathered(g[4]))
                w_ffn_out = _rows_from_gathered(g[5])
                ln_g = jnp.transpose(g[6], (1, 0, 2)).reshape(2, 1, SGU_W)
                ln_b = jnp.transpose(g[7], (1, 0, 2)).reshape(2, 1, SGU_W)
                w_sgu_in_t, w_sgu_out_t, w_ffn_in_t, w_ffn_out_t = (
                    tr(w_sgu_in), tr(w_sgu_out), tr(w_ffn_in), tr(w_ffn_out))
            o3 = jnp.swapaxes(ot3, 1, 2)
            xm = _attn_out(o3, (attn_w[j]["out"], 0), xr, f"attn_out_{j}")
            mix_saved = (xr, h, qkv3, fl3, cp3, o3, lse4, last_q)
        else:
            a, h = _norm_matmul(xr, mixer_nw[i], (w_sgu_in, j), BF16, f"sgu_in_{j}", 1024)
            xm, gated, vn, mixed = _sgu_core(a, ln_g[j], ln_b[j], sgu_w_s[j], bs_t[j], (w_sgu_out, j), xr,
                                             f"sgu_core_{j}")
            mix_saved = (xr, h, a, gated, vn, mixed)
        gu, h2 = _norm_matmul(xm, ffn_nw[i], (w_ffn_in, i), BF16, f"ffn_in_{i}", FFN_H // 2)
        xo, hid = _ffn_out(gu, (w_ffn_out, i), xm, f"ffn_out_{i}")
        saved.append((mix_saved, (xm, h2, gu, hid)))
        xr = xo
    dx, d_final, loss_part = _loss_head(xr, final_norm_w.reshape(1, D_MODEL), tgt, "loss_head")

    d_mixer_nw, d_ffn_nw = [None] * 4, [None] * 4
    d_attn_in, d_attn_out, d_bf, d_sgu_in, d_sgu_out = [None] * 2, [None] * 2, [None] * 2, [None] * 2, [None] * 2
    d_ws, d_bs, d_lg, d_lb = [None] * 2, [None] * 2, [None] * 2, [None] * 2
    d_ffn_in, d_ffn_out = [None] * 4, [None] * 4
    for i in reversed(range(4)):
        j = i // 2
        mix_saved, (xm, h2, gu, hid) = saved[i]
        dgu = _ffn_dgu(dx, (w_ffn_out_t, i), gu, f"ffn_dgu_{i}")
        d_ffn_out[i] = _matmul_tn(hid, dx, f"ffn_dwout_{i}", FFN_H // 2, D_MODEL)
        d_ffn_in[i] = _matmul_tn(h2, dgu, f"ffn_dwin_{i}", D_MODEL, FFN_H // 2)
        dx, d_ffn_nw[i] = _matmul_rms_bwd([(dgu, "row")], [(w_ffn_in_t, i)], xm, ffn_nw[i], dx, f"ffn_dx_{i}", 512, 48)
        if i % 2 == 0:
            xr, h, qkv3, fl3, cp3, o3, lse4, last_q = mix_saved
            do3, dd = _attn_dout(dx, (attn_w[j]["out_t"], 0), o3, f"attn_dout_{j}")
            d_attn_out[j] = _matmul_tn(o3, dx, f"attn_dwout_{j}", D_MODEL, D_MODEL, a_grp=True)
            to_blocks = lambda a: jnp.swapaxes(a.reshape(N_PAIR, 2, T // tb, tb), 1, 2)
            from_blocks = lambda a: jnp.swapaxes(a, 1, 2).reshape(N_PAIR, 2, T)
            dd4 = to_blocks(dd.T.reshape(N_PAIR, 2, T))
            st4 = jnp.concatenate([to_blocks(from_blocks(lse4)), dd4], axis=2)
            kst4 = jnp.swapaxes((qkv3[N_PAIR:2 * N_PAIR] * QK_SCALE).reshape(N_PAIR, T // tb, tb, LANES), 2, 3)
            ride = []
            if i == 0:
                ln_blocks = lambda d: jnp.transpose(jnp.stack(d).reshape(2, N_DEV, 1, SGU_W // N_DEV),
                                                    (1, 0, 2, 3)).reshape(N_DEV, 2, SGU_W // N_DEV)
                ride = [b.astype(BF16) for b in (
                    _cols_to_blocks(d_attn_in[1][None]), _rows_to_blocks(jnp.stack(d_attn_out)),
                    _cols_to_blocks(jnp.stack(d_sgu_in)), _rows_to_blocks(jnp.stack(d_sgu_out)),
                    _cols_to_blocks(_ffn_deinterleave(jnp.stack(d_ffn_in))), _rows_to_blocks(jnp.stack(d_ffn_out)))]
                ride += [ln_blocks(d_lg), ln_blocks(d_lb)]
            bwd_out = _attn_bwd(qkv3, cp3, kst4, do3, st4, last_q, f"attn_bwd_{j}", ride)
            dqt4, rs4, dk3, dv3, cs4 = bwd_out[:5]
            if i == 0:
                received = bwd_out[5:]
            dq3 = jnp.swapaxes(dqt4, 2, 3).reshape(N_PAIR, T, LANES).astype(BF16)
            dc_pair = from_blocks(rs4 - jnp.stack([cs4[:, :, 0], cs4[:, :, 8]], axis=2))
            dc_chunks = jnp.transpose(dc_pair.reshape(2 * N_PAIR, T // CHUNK, CHUNK), (1, 0, 2))
            dfl3, db = _fgate_bwd(dc_chunks, fl3, b_col[j], f"fgate_bwd_{j}")
            d_bf[j] = db[:, 0]
            dfl = jnp.transpose(dfl3, (0, 2, 1)).reshape(T, 2 * N_PAIR)
            dfl = jnp.pad(dfl.astype(BF16), ((0, 0), (0, LANES - 2 * N_PAIR)))
            d_qkv = [_matmul_tn(h, d3, f"attn_dw{nm}_{j}", D_MODEL, D_MODEL, g_grp=True)
                     for nm, d3 in (("q", dq3), ("k", dk3), ("v", dv3))]
            d_f = _matmul_tn(h, dfl, f"attn_dwf_{j}", D_MODEL, LANES)[:, :2 * N_PAIR]
            d_attn_in[j] = jnp.concatenate(d_qkv + [d_f], axis=1)
            wts = [(attn_w[j]["qkv_t"], k) for k in range(3)] + [(attn_w[j]["f_t"], 0)]
            dx, d_mixer_nw[i] = _matmul_rms_bwd(
                [(dq3, "grp"), (dk3, "grp"), (dv3, "grp"), (dfl, "row")], wts, xr, mixer_nw[i], dx,
                f"attn_dx_{j}", 256, 40)
        else:
            xr, h, a, gated, vn, mixed = mix_saved
            da, d_ws[j], dba, d_lg[j], d_lb[j] = _sgu_core_bwd(dx, (w_sgu_out_t, j), a, vn, mixed, ln_g[j], sgu_w_s[j],
                                                               f"sgu_core_bwd_{j}")
            d_bs[j] = jnp.sum(dba.reshape(CHUNK, SGU_G, CHUNK), axis=-1).T
            d_sgu_out[j] = _matmul_tn(gated, dx, f"sgu_dwout_{j}", D_MODEL, D_MODEL)
            d_sgu_in[j] = _matmul_tn(h, da, f"sgu_dwin_{j}", D_MODEL, 1024)
            dx, d_mixer_nw[i] = _matmul_rms_bwd([(da, "row")], [(w_sgu_in_t, j)], xr, mixer_nw[i], dx,
                                                f"sgu_dx_{j}", 256, 40)
    grad_x = dx[None]

    rows4 = lambda parts: jnp.concatenate(parts, axis=1).reshape(4, D_MODEL)
    small_g = _pack_small(rows4(d_mixer_nw), rows4(d_ffn_nw), d_final[0], jnp.stack(d_bf),
                          loss_part[0:1, 0:1] * jnp.ones((1, D_MODEL), F32), jnp.stack(d_ws), jnp.stack(d_bs))
    zero_row = jnp.zeros((1, D_MODEL), F32)
    pack = lambda pre: _pack_small(pre[0], pre[1], pre[2], pre[3], zero_row, pre[4], pre[5])
    small_w = pack((mixer_norm_w, ffn_norm_w, final_norm_w, attn_b_f, sgu_w_s, sgu_b_s))
    small_m = pack((m_mixer_norm_w, m_ffn_norm_w, m_final_norm_w, m_attn_b_f, m_sgu_w_s, m_sgu_b_s))
    small_v = pack((v_mixer_norm_w, v_ffn_norm_w, v_final_norm_w, v_attn_b_f, v_sgu_w_s, v_sgu_b_s))
    small_all = _all_gather([small_g], "gather_small_grads")[0]
    small_out = [_unpack_small(p) for p in _adam_sum(small_all, small_w, small_m, small_v, "adam_small")]
    loss = small_out[0][4][0]

    late = _exchange([_cols_to_blocks(d_attn_in[0][None]).astype(BF16)], "exchange_grads")[0]
    names = ["attn_w_out", "sgu_w_in", "sgu_w_out", "ffn_w_in", "ffn_w_out"]
    ws = [attn_w_out, sgu_w_in, sgu_w_out, ffn_w_in, ffn_w_out]
    ms = [m_attn_w_out, m_sgu_w_in, m_sgu_w_out, m_ffn_w_in, m_ffn_w_out]
    vs = [v_attn_w_out, v_sgu_w_in, v_sgu_w_out, v_ffn_w_in, v_ffn_w_out]
    big_out = {}
    for nm, rec, w, m, v in zip(names, received[1:6], ws, ms, vs):
        flat = lambda a: a.reshape(-1, a.shape[-1])
        big_out[nm] = [o.reshape(w.shape) for o in _adam_sum(rec, flat(w), flat(m), flat(v), f"adam_{nm}")]
    per_layer = [_adam_sum(rec, attn_w_in[l], m_attn_w_in[l], v_attn_w_in[l], f"adam_attn_w_in_{l}")
                 for l, rec in enumerate((late, received[0]))]
    big_out["attn_w_in"] = [jnp.stack([per_layer[0][k], per_layer[1][k]]) for k in range(4)]
    pad8 = lambda a: jnp.pad(a, [(0, 0)] * (a.ndim - 2) + [(0, 8 - a.shape[-2]), (0, 0)])
    ln_parts = jnp.concatenate([pad8(received[6]), pad8(received[7])], axis=1)
    ln_pack = lambda g, b: jnp.concatenate([pad8(g), pad8(b)], axis=0)
    ln_out = _adam_sum(ln_parts, ln_pack(sgu_ln_g, sgu_ln_b), ln_pack(m_sgu_ln_g, m_sgu_ln_b),
                       ln_pack(v_sgu_ln_g, v_sgu_ln_b), "adam_sgu_ln")

    def leaf(kind):
        mixer, ffn, final, b_f, _, w_s, b_s = small_out[kind]
        o = lambda nm: big_out[nm][kind]
        return [mixer, o("attn_w_in"), b_f, o("attn_w_out"), o("sgu_w_in"), ln_out[kind][0:2], ln_out[kind][8:10],
                w_s, b_s, o("sgu_w_out"), ffn, o("ffn_w_in"), o("ffn_w_out"), final]

    return (loss, grad_x, *leaf(0), *leaf(1), *leaf(2), *leaf(3))
```

```python
import functools

import jax
import jax.numpy as jnp
from jax import lax
from jax.experimental import pallas as pl
from jax.experimental.pallas import tpu as pltpu

F32 = jnp.float32
BF16 = jnp.bfloat16

D_MODEL = 1024
HEAD_DIM = 64
N_PAIR = 8
LANES = 128
SGU_W = 2048
SGU_G = 16
CHUNK = 128
FFN_H = 2816
FFN_TILE = 256
N_FFN_TILE = FFN_H // FFN_TILE
NORM_EPS = 1e-6
LN_EPS = 1e-5
QK_SCALE = 0.125
ATT_BLOCK = 512
N_DEV = 8
ADAM_LR = 0.001
ADAM_B1 = 0.9
ADAM_B2 = 0.999
ADAM_EPS = 1e-08
ADAM_WD = 0.01
ADAM_STEP = 10
MESH = pl.DeviceIdType.MESH
SQRT_HALF = 0.7071067811865476
INV_SQRT_2PI = 0.3989422804014327

NT_DIMS = (((1,), (1,)), ((), ()))
TN_DIMS = (((0,), (0,)), ((), ()))


def _gelu(x):
    return 0.5 * x * (1.0 + lax.erf(x * SQRT_HALF))


def _gelu_grad(x):
    return 0.5 * (1.0 + lax.erf(x * SQRT_HALF)) + x * jnp.exp(-0.5 * x * x) * INV_SQRT_2PI


def _lane_col(v, lane):
    idx = lax.broadcasted_iota(jnp.int32, v.shape, 1)
    return jnp.sum(jnp.where(idx == lane, v, 0.0), axis=1, keepdims=True)


def _params(sem, vmem_mb):
    return pltpu.CompilerParams(dimension_semantics=sem, vmem_limit_bytes=vmem_mb << 20)


def _cat_groups(ref, n):
    if n == 1:
        return ref[0]
    return jnp.concatenate([ref[t] for t in range(n)], axis=1)


def _wkind(w):
    return (w[0], ("layer", w[1])) if isinstance(w, tuple) else (w, "full")


def _rowcall(name, body, T, tm, ins, outs, vmem_mb, scratch=()):
    def spec(shape, kind, resident_once=False):
        shape = tuple(shape)
        if isinstance(kind, tuple):
            layer = kind[1]
            return pl.BlockSpec((None,) + shape[1:], lambda i: (layer,) + (0,) * (len(shape) - 1),
                                pipeline_mode=pl.Buffered(1))
        if kind == "row":
            return pl.BlockSpec((tm,) + shape[1:], lambda i: (i,) + (0,) * (len(shape) - 1))
        if kind == "grp":
            return pl.BlockSpec((shape[0], tm, shape[2]), lambda i: (0, i, 0))
        if resident_once:
            return pl.BlockSpec(shape, lambda i: (0,) * len(shape), pipeline_mode=pl.Buffered(1))
        return pl.BlockSpec(shape, lambda i: (0,) * len(shape))

    return pl.pallas_call(
        body,
        name=name,
        grid=(T // tm,),
        in_specs=[spec(a.shape, k, True) for a, k in ins],
        out_specs=[spec(s, k) for s, _, k in outs],
        out_shape=[jax.ShapeDtypeStruct(tuple(s), d) for s, d, _ in outs],
        scratch_shapes=list(scratch),
        compiler_params=_params(("arbitrary",), vmem_mb),
    )(*[a for a, _ in ins])


def _norm_matmul(x, nw, w, out_dtype, name, tn, groups=False):
    w, layer = w
    T, N = x.shape[0], w.shape[2]
    tm = min(1024, T)

    def body(x_ref, nw_ref, w_ref, o_ref, h_ref, h_scr):
        @pl.when(pl.program_id(1) == 0)
        def _():
            xv = x_ref[...]
            r = lax.rsqrt(jnp.mean(xv * xv, axis=-1, keepdims=True) + NORM_EPS)
            hv = (xv * r * nw_ref[...]).astype(BF16)
            h_scr[...] = hv
            h_ref[...] = hv

        acc = jnp.dot(h_scr[...], w_ref[...], preferred_element_type=F32)
        if groups:
            for t in range(tn // LANES):
                o_ref[t] = acc[:, LANES * t:LANES * (t + 1)].astype(out_dtype)
        else:
            o_ref[...] = acc.astype(out_dtype)

    if groups:
        o_shape = (N // LANES, T, LANES)
        o_spec = pl.BlockSpec((tn // LANES, tm, LANES), lambda i, j: (j, i, 0))
    else:
        o_shape = (T, N)
        o_spec = pl.BlockSpec((tm, tn), lambda i, j: (i, j))
    return pl.pallas_call(
        body,
        name=name,
        grid=(T // tm, N // tn),
        in_specs=[
            pl.BlockSpec((tm, D_MODEL), lambda i, j: (i, 0)),
            pl.BlockSpec((1, D_MODEL), lambda i, j: (0, 0)),
            pl.BlockSpec((None, D_MODEL, tn), lambda i, j: (layer, 0, j)),
        ],
        out_specs=[o_spec, pl.BlockSpec((tm, D_MODEL), lambda i, j: (i, 0))],
        out_shape=[jax.ShapeDtypeStruct(o_shape, out_dtype), jax.ShapeDtypeStruct((T, D_MODEL), BF16)],
        scratch_shapes=[pltpu.VMEM((tm, D_MODEL), BF16)],
        compiler_params=_params(("arbitrary", "arbitrary"), 48),
    )(x, nw, w)


def _matmul_tn(a, g, name, tk, tn, a_grp=False, g_grp=False):
    T = a.shape[1] if a_grp else a.shape[0]
    K = a.shape[0] * LANES if a_grp else a.shape[1]
    N = g.shape[0] * LANES if g_grp else g.shape[1]
    tm = min(1024, T)

    def body(a_ref, g_ref, o_ref):
        @pl.when(pl.program_id(2) == 0)
        def _():
            o_ref[...] = jnp.zeros(o_ref.shape, F32)

        av = _cat_groups(a_ref, tk // LANES) if a_grp else a_ref[...]
        gv = _cat_groups(g_ref, tn // LANES) if g_grp else g_ref[...]
        o_ref[...] += lax.dot_general(av.astype(BF16), gv.astype(BF16), TN_DIMS, preferred_element_type=F32)

    if a_grp:
        a_spec = pl.BlockSpec((tk // LANES, tm, LANES), lambda k, n, m: (k, m, 0))
    else:
        a_spec = pl.BlockSpec((tm, tk), lambda k, n, m: (m, k))
    if g_grp:
        g_spec = pl.BlockSpec((tn // LANES, tm, LANES), lambda k, n, m: (n, m, 0))
    else:
        g_spec = pl.BlockSpec((tm, tn), lambda k, n, m: (m, n))
    return pl.pallas_call(
        body,
        name=name,
        grid=(K // tk, N // tn, T // tm),
        in_specs=[a_spec, g_spec],
        out_specs=pl.BlockSpec((tk, tn), lambda k, n, m: (k, n)),
        out_shape=jax.ShapeDtypeStruct((K, N), F32),
        compiler_params=_params(("parallel", "parallel", "arbitrary"), 48),
    )(a, g)


def _matmul_rms_bwd(a_list, wt_list, x, nw, dres, name, tm, vmem_mb):
    T = x.shape[0]
    n = len(a_list)

    def body(*refs):
        a_refs, w_refs = refs[:n], refs[n:2 * n]
        x_ref, nw_ref, dres_ref, dx_ref, dxb_ref, dnw_ref = refs[2 * n:]

        @pl.when(pl.program_id(0) == 0)
        def _():
            dnw_ref[...] = jnp.zeros(dnw_ref.shape, F32)

        dh = None
        for (arr, kind), a_ref, w_ref in zip(a_list, a_refs, w_refs):
            av = _cat_groups(a_ref, arr.shape[0]) if kind == "grp" else a_ref[...]
            part = jnp.dot(av.astype(BF16), w_ref[...], preferred_element_type=F32)
            dh = part if dh is None else dh + part
        xv = x_ref[...]
        r = lax.rsqrt(jnp.mean(xv * xv, axis=-1, keepdims=True) + NORM_EPS)
        xn = xv * r
        dnw_ref[...] += jnp.sum(dh * xn, axis=0, keepdims=True)
        dyw = dh * nw_ref[...]
        dxv = dres_ref[...] + r * (dyw - xn * jnp.mean(dyw * xn, axis=-1, keepdims=True))
        dx_ref[...] = dxv
        dxb_ref[...] = dxv.astype(BF16)

    ins = list(a_list) + [_wkind(w) for w in wt_list] + [(x, "row"), (nw, "full"), (dres, "row")]
    outs = [((T, D_MODEL), F32, "row"), ((T, D_MODEL), BF16, "row"), ((1, D_MODEL), F32, "full")]
    return _rowcall(name, body, T, tm, ins, outs, vmem_mb)


def _fgate_fwd(fl3, bcol, name):
    n_chunk = fl3.shape[0]

    def body(fl_ref, b_ref, c_ref):
        r = lax.broadcasted_iota(jnp.int32, (CHUNK, CHUNK), 0)
        t = lax.broadcasted_iota(jnp.int32, (CHUNK, CHUNK), 1)
        tri = jnp.where(r <= t, 1.0, 0.0).astype(BF16)

        def chunk(i, carry):
            z = fl_ref[i] + b_ref[...]
            lf = jnp.minimum(z, 0.0) - jnp.log(1.0 + jnp.exp(-jnp.abs(z)))
            hi = lf.astype(BF16)
            r1 = lf - hi.astype(F32)
            mid = r1.astype(BF16)
            low = (r1 - mid.astype(F32)).astype(BF16)
            cs = (jnp.dot(hi, tri, preferred_element_type=F32) + jnp.dot(mid, tri, preferred_element_type=F32)
                  + jnp.dot(low, tri, preferred_element_type=F32)) + carry
            c_ref[i] = cs
            return _lane_col(cs, CHUNK - 1)

        lax.fori_loop(0, n_chunk, chunk, jnp.zeros((2 * N_PAIR, 1), F32))

    return pl.pallas_call(
        body, name=name, out_shape=jax.ShapeDtypeStruct(fl3.shape, F32),
        compiler_params=pltpu.CompilerParams(vmem_limit_bytes=16 << 20),
    )(fl3, bcol)


def _fgate_bwd(dc3, fl3, bcol, name):
    n_chunk = fl3.shape[0]

    def body(dc_ref, fl_ref, b_ref, dfl_ref, db_ref):
        tt = lax.broadcasted_iota(jnp.int32, (CHUNK, CHUNK), 0)
        rr = lax.broadcasted_iota(jnp.int32, (CHUNK, CHUNK), 1)
        tri = jnp.where(tt >= rr, 1.0, 0.0).astype(BF16)

        def chunk(k, carry):
            tail, acc = carry
            i = n_chunk - 1 - k
            dc = dc_ref[i]
            hi = dc.astype(BF16)
            r1 = dc - hi.astype(F32)
            mid = r1.astype(BF16)
            low = (r1 - mid.astype(F32)).astype(BF16)
            dlf = (jnp.dot(hi, tri, preferred_element_type=F32) + jnp.dot(mid, tri, preferred_element_type=F32)
                   + jnp.dot(low, tri, preferred_element_type=F32)) + tail
            z = fl_ref[i] + b_ref[...]
            dfl = dlf / (1.0 + jnp.exp(z))
            dfl_ref[i] = dfl
            return _lane_col(dlf, 0), acc + dfl

        _, acc = lax.fori_loop(0, n_chunk, chunk,
                               (jnp.zeros((2 * N_PAIR, 1), F32), jnp.zeros((2 * N_PAIR, CHUNK), F32)))
        db_ref[...] = jnp.broadcast_to(jnp.sum(acc, axis=1, keepdims=True), db_ref.shape)

    return pl.pallas_call(
        body, name=name,
        out_shape=[jax.ShapeDtypeStruct(fl3.shape, F32), jax.ShapeDtypeStruct((2 * N_PAIR, LANES), F32)],
        compiler_params=pltpu.CompilerParams(vmem_limit_bytes=16 << 20),
    )(dc3, fl3, bcol)


BIAS_LANES = 3
ATT_Q_BLOCK = 2048


def _own_lanes(shape, hh, axis=1):
    idx = lax.broadcasted_iota(jnp.int32, shape, axis)
    return idx < HEAD_DIM if hh == 0 else idx >= HEAD_DIM


def _spare(hh):
    return HEAD_DIM * (1 - hh)


def _bias_pieces(c16):
    T = c16.shape[1]
    negc = -c16
    hi = negc.astype(BF16)
    r1 = negc - hi.astype(F32)
    mid = r1.astype(BF16)
    low = (r1 - mid.astype(F32)).astype(BF16)
    pieces = jnp.stack([hi, mid, low], axis=-1).reshape(N_PAIR, 2, T, BIAS_LANES)
    zpad = jnp.zeros((N_PAIR, T, HEAD_DIM - BIAS_LANES), BF16)
    return jnp.concatenate([pieces[:, 1], zpad, pieces[:, 0], zpad], axis=-1)


UNDERFLOW_BOUND = -110.0


def _attn_reach(qkv3, c16, tb, tq):
    T = qkv3.shape[1]
    nb = T // tb

    def block_norm(a):
        sq = jnp.sum(jnp.square(a.astype(F32)).reshape(N_PAIR, nb, tb, 2, HEAD_DIM), axis=-1)
        return jnp.transpose(jnp.sqrt(jnp.max(sq, axis=2)), (0, 2, 1))

    qn, kn = block_norm(qkv3[:N_PAIR]), block_norm(qkv3[N_PAIR:2 * N_PAIR])
    cb = c16.reshape(N_PAIR, 2, nb, tb)
    c_max, negc_max = jnp.max(cb, axis=-1), jnp.max(-cb, axis=-1)
    bound = (qn[:, :, :, None] * (kn[:, :, None, :] + kn[:, :, :, None]) * QK_SCALE
             + negc_max[:, :, None, :] + c_max[:, :, :, None])
    qi = lax.broadcasted_iota(jnp.int32, (nb, nb), 0)
    kj = lax.broadcasted_iota(jnp.int32, (nb, nb), 1)
    active = jnp.logical_or(bound > UNDERFLOW_BOUND, kj >= qi)
    last_q = jnp.max(jnp.where(active, qi, -1), axis=2)
    first_blk = jnp.min(jnp.where(active, kj, nb), axis=3)
    first_kv = jnp.min(first_blk.reshape(N_PAIR, 2, T // tq, tq // tb), axis=-1)
    return first_kv.astype(F32), last_q.astype(F32)


def _attn_fwd(qkv3, cp3, first_kv, name, ride=()):
    T = qkv3.shape[1]
    tb = min(ATT_BLOCK, T)
    tq = min(ATT_Q_BLOCK, T)
    nb = T // tb
    per_q = tq // tb

    assert per_q % 2 == 0 or T == tq, (T, tq, tb)

    n_ride = len(ride)

    def body(*refs):
        q_ref, k_ref, v_ref, cp_ref, first_ref = refs[:5]
        o_ref, lse_ref = refs[5 + n_ride:7 + n_ride]
        st_a, st_b = refs[7 + 2 * n_ride:9 + 2 * n_ride]
        h, i = pl.program_id(0), pl.program_id(1)
        if n_ride:
            starts, waits = _gather_copies(refs[5:5 + n_ride], refs[7 + n_ride:7 + 2 * n_ride],
                                           *refs[9 + 2 * n_ride:])

            @pl.when(jnp.logical_and(h == 0, i == 0))
            def _():
                for cp in starts:
                    cp.start()

        lane = lax.broadcasted_iota(jnp.int32, (tq, LANES), 1)
        lane_k = lax.broadcasted_iota(jnp.int32, (tb, LANES), 1)
        feat = lax.broadcasted_iota(jnp.int32, (LANES, tq), 0)
        q2 = q_ref[0] * QK_SCALE
        qa, own_k, one_k = [], [], []
        for hh in range(2):
            bias = jnp.logical_and(lane >= _spare(hh), lane < _spare(hh) + BIAS_LANES)
            qa.append(jnp.where(_own_lanes((tq, LANES), hh), q2, jnp.where(bias, 1.0, 0.0).astype(BF16)))
            own_k.append(_own_lanes((tb, LANES), hh))
            one_k.append(jnp.where(lane_k == _spare(hh), 1.0, 0.0).astype(BF16))

        def scores(j, scr, heads=(0, 1)):
            off = pl.multiple_of(j * tb, tb)
            kb, cb = k_ref[0, pl.ds(off, tb), :], cp_ref[0, pl.ds(off, tb), :]
            for hh in heads:
                scr[hh] = lax.dot_general(jnp.where(own_k[hh], kb, cb), qa[hh], NT_DIMS, preferred_element_type=F32)

        def step(j, carry, first, scr=None, heads=(0, 1)):
            off = pl.multiple_of(j * tb, tb)
            kb, vb, cb = k_ref[0, pl.ds(off, tb), :], v_ref[0, pl.ds(off, tb), :], cp_ref[0, pl.ds(off, tb), :]
            lo = 0 if first is None else first
            out = list(carry)
            for hh in heads:
                m_all, acc_all = carry[2 * hh], carry[2 * hh + 1]
                m_old, acc = m_all[:, lo:], acc_all[:, lo:]
                if scr is None:
                    st = lax.dot_general(jnp.where(own_k[hh], kb, cb), qa[hh][lo:], NT_DIMS,
                                         preferred_element_type=F32)
                else:
                    st = scr[hh]
                if first is not None:
                    key = lax.broadcasted_iota(jnp.int32, (tb, tq - lo), 0)
                    qry = lax.broadcasted_iota(jnp.int32, (tb, tq - lo), 1)
                    st = jnp.where(key <= qry, st, -jnp.inf)
                m = jnp.maximum(m_old, jnp.max(st, axis=0, keepdims=True))
                p = jnp.exp(st - m)
                acc = jnp.exp(m_old - m) * acc + lax.dot_general(
                    jnp.where(own_k[hh], vb, one_k[hh]), p.astype(BF16), TN_DIMS, preferred_element_type=F32)
                if lo:
                    m = jnp.concatenate([m_all[:, :lo], m], axis=1)
                    acc = jnp.concatenate([acc_all[:, :lo], acc], axis=1)
                out[2 * hh], out[2 * hh + 1] = m, acc
            return tuple(out)

        ninf = jnp.full((1, tq), -jnp.inf, F32)
        zacc = jnp.zeros((LANES, tq), F32)

        def make_pair(heads):
            def pair(jj, c):
                j = 2 * jj
                scores(j + 1, st_b, heads)
                c = step(j, c, None, st_a, heads)
                scores(j + 2, st_a, heads)
                return step(j + 1, c, None, st_b, heads)
            return pair

        end_pair = (i * per_q) // 2
        first_pair = [jnp.clip(first_ref[h, hh, i].astype(jnp.int32), 0, i * per_q) // 2 for hh in range(2)]
        both_pair = jnp.maximum(first_pair[0], first_pair[1])
        carry = (ninf, zacc, ninf, zacc)
        for hh in range(2):
            scores(2 * first_pair[hh], st_a, (hh,))
            carry = lax.fori_loop(first_pair[hh], both_pair, make_pair((hh,)), carry)
        scores(2 * both_pair, st_a)
        carry = lax.fori_loop(both_pair, end_pair, make_pair((0, 1)), carry)
        carry = step(i * per_q, carry, 0, st_a)
        for t in range(1, per_q):
            carry = step(i * per_q + t, carry, t * tb)
        outs = []
        for hh in range(2):
            m, acc = carry[2 * hh], carry[2 * hh + 1]
            l = jnp.sum(jnp.where(feat == _spare(hh), acc, 0.0), axis=0, keepdims=True)
            lse_ref[0, 0, hh:hh + 1, :] = m + jnp.log(l)
            outs.append(acc * (1.0 / l))
        o_ref[0] = jnp.where(feat < HEAD_DIM, outs[0], outs[1]).astype(BF16)

        if n_ride:
            @pl.when(jnp.logical_and(h == N_PAIR - 1, i == T // tq - 1))
            def _():
                for wait in waits:
                    wait()

    res = lambda base: pl.BlockSpec((1, T, LANES), lambda h, i: (base + h, 0, 0), pipeline_mode=pl.Buffered(1))
    hbm = pl.BlockSpec(memory_space=pl.ANY)
    return pl.pallas_call(
        body,
        name=name,
        grid=(N_PAIR, T // tq),
        in_specs=[pl.BlockSpec((1, tq, LANES), lambda h, i: (h, i, 0)), res(N_PAIR), res(2 * N_PAIR), res(0),
                  pl.BlockSpec(memory_space=pltpu.SMEM)] + [hbm] * n_ride,
        out_specs=[pl.BlockSpec((1, LANES, tq), lambda h, i: (h, 0, i)),
                   pl.BlockSpec((1, 1, 2, tq), lambda h, i: (h, i, 0, 0))] + [hbm] * n_ride,
        out_shape=[jax.ShapeDtypeStruct((N_PAIR, LANES, T), BF16),
                   jax.ShapeDtypeStruct((N_PAIR, T // tq, 2, tq), F32)]
        + [jax.ShapeDtypeStruct((N_DEV,) + a.shape, a.dtype) for a in ride],
        scratch_shapes=[pltpu.VMEM((2, tb, tq), F32), pltpu.VMEM((2, tb, tq), F32)]
        + (_exchange_sems(n_ride) if n_ride else []),
        compiler_params=_params(("arbitrary", "arbitrary"), 56),
    )(qkv3, qkv3, qkv3, cp3, first_kv, *ride)


def _attn_bwd(qkv3, cp3, kst4, do3, st4, last_q, name, ride=()):
    T = qkv3.shape[1]
    tb = min(ATT_BLOCK, T)
    nb = T // tb

    n_ride = len(ride)

    def body(*refs):
        q_ref, do_ref, st_ref, k_ref, v_ref, cp_ref, kst_ref, last_ref = refs[:8]
        ride_in = refs[8:8 + n_ride]
        dq_hbm, rs_ref, dk_ref, dv_ref, cs_ref = refs[8 + n_ride:13 + n_ride]
        ride_out = refs[13 + n_ride:13 + 2 * n_ride]
        dq_acc, dk_acc, dv_acc = refs[13 + 2 * n_ride:16 + 2 * n_ride]
        h, j = pl.program_id(0), pl.program_id(1)
        if n_ride:
            starts, waits = _exchange_copies(ride_in, ride_out, *refs[16 + 2 * n_ride:])

            @pl.when(jnp.logical_and(h == 0, j == 0))
            def _():
                for cp in starts:
                    cp.start()

        lane = lax.broadcasted_iota(jnp.int32, (tb, LANES), 1)
        own = [_own_lanes((tb, LANES), hh) for hh in range(2)]
        bias = [jnp.logical_and(lane >= _spare(hh), lane < _spare(hh) + BIAS_LANES) for hh in range(2)]
        key = lax.broadcasted_iota(jnp.int32, (tb, tb), 0)
        qry = lax.broadcasted_iota(jnp.int32, (tb, tb), 1)

        @pl.when(j == 0)
        def _():
            dq_acc[...] = jnp.zeros(dq_acc.shape, F32)
            rs_ref[...] = jnp.zeros(rs_ref.shape, F32)

        vb = v_ref[0]
        zero = jnp.zeros_like(vb)
        one = jnp.ones_like(vb)
        bias_one = [jnp.where(bias[hh], one, zero) for hh in range(2)]
        kb = [jnp.where(own[hh], k_ref[0], cp_ref[0]) for hh in range(2)]
        kst = kst_ref[0, 0]
        kst = [jnp.where(_own_lanes((LANES, tb), hh, axis=0), kst, jnp.zeros_like(kst)) for hh in range(2)]
        vm = [jnp.where(own[hh], vb, zero) for hh in range(2)]
        dk_acc[...] = jnp.zeros(dk_acc.shape, F32)
        dv_acc[...] = jnp.zeros(dv_acc.shape, F32)

        def step(i, masked, heads=(0, 1)):
            off = pl.multiple_of(i * tb, tb)
            qb = q_ref[0, pl.ds(off, tb), :] * QK_SCALE
            dob = do_ref[0, pl.ds(off, tb), :]
            dq = None
            for hh in heads:
                st = lax.dot_general(kb[hh], jnp.where(own[hh], qb, bias_one[hh]), NT_DIMS,
                                     preferred_element_type=F32)
                if masked:
                    st = jnp.where(key <= qry, st, -jnp.inf)
                p = jnp.exp(st - st_ref[0, i, hh:hh + 1, :])
                dp = lax.dot_general(vm[hh], dob, NT_DIMS, preferred_element_type=F32)
                dsb = (p * (dp - st_ref[0, i, 2 + hh:3 + hh, :])).astype(BF16)
                dv_acc[hh] += jnp.dot(p.astype(BF16), dob, preferred_element_type=F32)
                dk_acc[hh] += jnp.dot(dsb, jnp.where(own[hh], qb, one), preferred_element_type=F32)
                rs_ref[0, i, hh:hh + 1, :] += jnp.sum(dsb.astype(F32), axis=0, keepdims=True)
                d = jnp.dot(kst[hh], dsb, preferred_element_type=F32)
                dq = d if dq is None else dq + d
            dq_acc[i] += dq

        step(j, True)

        def make_body(heads):
            def loop_body(i, carry):
                step(i, False, heads)
                return carry
            return loop_body

        last = [jnp.clip(last_ref[h, hh, j].astype(jnp.int32), j, nb - 1) for hh in range(2)]
        both = jnp.minimum(last[0], last[1])
        lax.fori_loop(j + 1, both + 1, make_body((0, 1)), 0)
        for hh in range(2):
            lax.fori_loop(both + 1, last[hh] + 1, make_body((hh,)), 0)
        dk_ref[0] = jnp.where(own[0], dk_acc[0], dk_acc[1]).astype(BF16)
        dv_ref[0] = jnp.where(own[0], dv_acc[0], dv_acc[1]).astype(BF16)
        lane8 = lax.broadcasted_iota(jnp.int32, (8, LANES), 1)
        for hh in range(2):
            pick = jnp.where(lane8 == _spare(hh), 1.0, 0.0).astype(BF16)
            x = dk_acc[hh]
            hi = x.astype(BF16)
            r1 = x - hi.astype(F32)
            mid = r1.astype(BF16)
            low = (r1 - mid.astype(F32)).astype(BF16)
            cs_ref[0, 0, 8 * hh:8 * hh + 8, :] = (
                lax.dot_general(pick, hi, NT_DIMS, preferred_element_type=F32)
                + lax.dot_general(pick, mid, NT_DIMS, preferred_element_type=F32)
                + lax.dot_general(pick, low, NT_DIMS, preferred_element_type=F32))

        @pl.when(j == nb - 1)
        def _():
            pltpu.sync_copy(dq_acc, dq_hbm.at[h])

        if n_ride:
            @pl.when(jnp.logical_and(h == N_PAIR - 1, j == nb - 1))
            def _():
                for wait in waits:
                    wait()

    res = pl.BlockSpec((1, T, LANES), lambda h, j: (h, 0, 0), pipeline_mode=pl.Buffered(1))
    tile = lambda base: pl.BlockSpec((1, tb, LANES), lambda h, j: (base + h, j, 0))
    rows = lambda n: pl.BlockSpec((1, nb, n, tb), lambda h, j: (h, 0, 0, 0))
    hbm = pl.BlockSpec(memory_space=pl.ANY)
    return pl.pallas_call(
        body,
        name=name,
        grid=(N_PAIR, nb),
        in_specs=[res, res, rows(4), tile(N_PAIR), tile(2 * N_PAIR), tile(0),
                  pl.BlockSpec((1, 1, LANES, tb), lambda h, j: (h, j, 0, 0)),
                  pl.BlockSpec(memory_space=pltpu.SMEM)] + [hbm] * n_ride,
        out_specs=[hbm, rows(2), tile(0), tile(0),
                   pl.BlockSpec((1, 1, 16, tb), lambda h, j: (h, j, 0, 0))] + [hbm] * n_ride,
        out_shape=[
            jax.ShapeDtypeStruct((N_PAIR, nb, LANES, tb), F32),
            jax.ShapeDtypeStruct((N_PAIR, nb, 2, tb), F32),
            jax.ShapeDtypeStruct((N_PAIR, T, LANES), BF16),
            jax.ShapeDtypeStruct((N_PAIR, T, LANES), BF16),
            jax.ShapeDtypeStruct((N_PAIR, nb, 16, tb), F32),
        ] + [jax.ShapeDtypeStruct(a.shape, a.dtype) for a in ride],
        scratch_shapes=[pltpu.VMEM((nb, LANES, tb), F32), pltpu.VMEM((2, tb, LANES), F32),
                        pltpu.VMEM((2, tb, LANES), F32)] + (_exchange_sems(n_ride) if n_ride else []),
        compiler_params=_params(("arbitrary", "arbitrary"), 56),
    )(qkv3, do3, st4, qkv3, qkv3, cp3, kst4, last_q, *ride)


def _attn_out(o3, w, x, name):
    T = x.shape[0]

    def body(o_ref, w_ref, x_ref, out_ref):
        out_ref[...] = x_ref[...] + jnp.dot(_cat_groups(o_ref, N_PAIR), w_ref[...], preferred_element_type=F32)

    return _rowcall(name, body, T, min(512, T), [(o3, "grp"), _wkind(w), (x, "row")],
                    [((T, D_MODEL), F32, "row")], 24)[0]


def _attn_dout(dx, wt, o3, name):
    T = dx.shape[0]
    tm = min(512, T)

    def body(dx_ref, w_ref, o_ref, do_ref, dd_ref):
        do = jnp.dot(dx_ref[...].astype(BF16), w_ref[...], preferred_element_type=F32).astype(BF16)
        lo = _own_lanes((tm, LANES), 0)
        head = lax.broadcasted_iota(jnp.int32, (tm, 2 * N_PAIR), 1)
        dd = jnp.zeros((tm, 2 * N_PAIR), F32)
        for t in range(N_PAIR):
            d = do[:, LANES * t:LANES * (t + 1)]
            do_ref[t] = d
            prod = d.astype(F32) * o_ref[t].astype(F32)
            d0 = jnp.sum(jnp.where(lo, prod, 0.0), axis=1, keepdims=True)
            d1 = jnp.sum(jnp.where(lo, 0.0, prod), axis=1, keepdims=True)
            dd = jnp.where(head == 2 * t, d0, jnp.where(head == 2 * t + 1, d1, dd))
        dd_ref[...] = dd

    return _rowcall(name, body, T, tm, [(dx, "row"), _wkind(wt), (o3, "grp")],
                    [((N_PAIR, T, LANES), BF16, "grp"), ((T, 2 * N_PAIR), F32, "row")], 32)


def _ffn_out(gu, w, x, name):
    T = x.shape[0]
    tm = min(512, T)

    def body(gu_ref, w_ref, x_ref, out_ref, hid_ref):
        acc = x_ref[...]
        for j in range(N_FFN_TILE):
            g = gu_ref[:, 2 * FFN_TILE * j:2 * FFN_TILE * j + FFN_TILE].astype(F32)
            u = gu_ref[:, 2 * FFN_TILE * j + FFN_TILE:2 * FFN_TILE * (j + 1)].astype(F32)
            hj = (g * jax.nn.sigmoid(g) * u).astype(BF16)
            hid_ref[:, FFN_TILE * j:FFN_TILE * (j + 1)] = hj
            acc = acc + jnp.dot(hj, w_ref[FFN_TILE * j:FFN_TILE * (j + 1), :], preferred_element_type=F32)
        out_ref[...] = acc

    return _rowcall(name, body, T, tm, [(gu, "row"), _wkind(w), (x, "row")],
                    [((T, D_MODEL), F32, "row"), ((T, FFN_H), BF16, "row")], 48)


def _ffn_dgu(dx, wt, gu, name):
    T = dx.shape[0]
    tm = min(512, T)

    def body(dx_ref, w_ref, gu_ref, dgu_ref):
        dxb = dx_ref[...].astype(BF16)
        for j in range(N_FFN_TILE):
            dh = jnp.dot(dxb, w_ref[:, FFN_TILE * j:FFN_TILE * (j + 1)], preferred_element_type=F32)
            g = gu_ref[:, 2 * FFN_TILE * j:2 * FFN_TILE * j + FFN_TILE].astype(F32)
            u = gu_ref[:, 2 * FFN_TILE * j + FFN_TILE:2 * FFN_TILE * (j + 1)].astype(F32)
            sg = jax.nn.sigmoid(g)
            dgu_ref[:, 2 * FFN_TILE * j:2 * FFN_TILE * j + FFN_TILE] = (
                dh * u * (sg * (1.0 + g * (1.0 - sg)))).astype(BF16)
            dgu_ref[:, 2 * FFN_TILE * j + FFN_TILE:2 * FFN_TILE * (j + 1)] = (dh * (g * sg)).astype(BF16)

    return _rowcall(name, body, T, tm, [(dx, "row"), _wkind(wt), (gu, "row")],
                    [((T, 2 * FFN_H), BF16, "row")], 48)[0]


def _sgu_core(a, ln_g, ln_b, w_s, bst, w, x, name):
    T = x.shape[0]
    tm = min(256, T)

    def body(a_ref, lg_ref, lb_ref, ws_ref, bs_ref, w_ref, x_ref, out_ref, gated_ref, vn_ref, mixed_ref):
        v = _gelu(a_ref[:, SGU_W:].astype(F32))
        mu = jnp.mean(v, axis=-1, keepdims=True)
        vc = v - mu
        rstd = lax.rsqrt(jnp.mean(vc * vc, axis=-1, keepdims=True) + LN_EPS)
        vn_ref[...] = (vc * rstd * lg_ref[...] + lb_ref[...]).astype(BF16)
        tt = lax.broadcasted_iota(jnp.int32, (CHUNK, CHUNK), 0)
        ss = lax.broadcasted_iota(jnp.int32, (CHUNK, CHUNK), 1)
        for g in range(SGU_G):
            wg = jnp.where(tt >= ss, ws_ref[g], 0.0).astype(BF16)
            bcol = _lane_col(bs_ref[...], g)
            cols = slice(CHUNK * g, CHUNK * (g + 1))
            for c in range(tm // CHUNK):
                rows = slice(CHUNK * c, CHUNK * (c + 1))
                mixed = jnp.dot(wg, vn_ref[rows, cols], preferred_element_type=F32) + bcol
                u = _gelu(a_ref[rows, cols].astype(F32))
                mixed_ref[rows, cols] = mixed.astype(BF16)
                gated_ref[rows, cols] = (u * mixed).astype(BF16)
        out_ref[...] = x_ref[...] + jnp.dot(gated_ref[...], w_ref[...], preferred_element_type=F32)

    ins = [(a, "row"), (ln_g, "full"), (ln_b, "full"), (w_s, "full"), (bst, "full"), _wkind(w), (x, "row")]
    outs = [((T, D_MODEL), F32, "row")] + [((T, SGU_W), BF16, "row")] * 3
    return _rowcall(name, body, T, tm, ins, outs, 40)


def _sgu_core_bwd(dx, wt, a, vn, mixed, ln_g, w_s, name):
    T = dx.shape[0]
    tm = min(256, T)

    def body(dx_ref, wt_ref, a_ref, vn_ref, mx_ref, lg_ref, ws_ref,
             da_ref, dws_ref, dba_ref, dlg_ref, dlb_ref, dg_scr, dvn_scr):
        @pl.when(pl.program_id(0) == 0)
        def _():
            dws_ref[...] = jnp.zeros(dws_ref.shape, F32)
            dba_ref[...] = jnp.zeros(dba_ref.shape, F32)
            dlg_ref[...] = jnp.zeros(dlg_ref.shape, F32)
            dlb_ref[...] = jnp.zeros(dlb_ref.shape, F32)

        dg_scr[...] = jnp.dot(dx_ref[...].astype(BF16), wt_ref[...], preferred_element_type=F32)
        tt = lax.broadcasted_iota(jnp.int32, (CHUNK, CHUNK), 0)
        ss = lax.broadcasted_iota(jnp.int32, (CHUNK, CHUNK), 1)
        tril = tt >= ss
        for g in range(SGU_G):
            wg = jnp.where(tril, ws_ref[g], 0.0).astype(BF16)
            cols = slice(CHUNK * g, CHUNK * (g + 1))
            for c in range(tm // CHUNK):
                rows = slice(CHUNK * c, CHUNK * (c + 1))
                dgb = dg_scr[rows, cols]
                au = a_ref[rows, cols].astype(F32)
                dmx = dgb * _gelu(au)
                da_ref[rows, cols] = (dgb * mx_ref[rows, cols].astype(F32) * _gelu_grad(au)).astype(BF16)
                dmb = dmx.astype(BF16)
                dvn_scr[rows, cols] = lax.dot_general(wg, dmb, TN_DIMS, preferred_element_type=F32)
                dws_ref[g] += jnp.where(
                    tril, lax.dot_general(dmb, vn_ref[rows, cols], NT_DIMS, preferred_element_type=F32), 0.0)
                dba_ref[:, cols] += dmx
        av = a_ref[:, SGU_W:].astype(F32)
        v = _gelu(av)
        mu = jnp.mean(v, axis=-1, keepdims=True)
        vc = v - mu
        rstd = lax.rsqrt(jnp.mean(vc * vc, axis=-1, keepdims=True) + LN_EPS)
        xhat = vc * rstd
        dvn = dvn_scr[...]
        dlg_ref[...] += jnp.sum(dvn * xhat, axis=0, keepdims=True)
        dlb_ref[...] += jnp.sum(dvn, axis=0, keepdims=True)
        dxh = dvn * lg_ref[...]
        dv = rstd * (dxh - jnp.mean(dxh, axis=-1, keepdims=True)
                     - xhat * jnp.mean(dxh * xhat, axis=-1, keepdims=True))
        da_ref[:, SGU_W:] = (dv * _gelu_grad(av)).astype(BF16)

    ins = [(dx, "row"), _wkind(wt), (a, "row"), (vn, "row"), (mixed, "row"), (ln_g, "full"), (w_s, "full")]
    outs = [((T, 2 * SGU_W), BF16, "row"), ((SGU_G, CHUNK, CHUNK), F32, "full"), ((CHUNK, SGU_W), F32, "full"),
            ((1, SGU_W), F32, "full"), ((1, SGU_W), F32, "full")]
    return _rowcall(name, body, T, tm, ins, outs, 40,
                    scratch=[pltpu.VMEM((tm, SGU_W), F32), pltpu.VMEM((tm, SGU_W), F32)])


def _loss_head(x, wf, tgt, name):
    T = x.shape[0]
    tm = min(512, T)

    def body(x_ref, wf_ref, tgt_ref, dx_ref, dxb_ref, dwf_ref, loss_ref):
        @pl.when(pl.program_id(0) == 0)
        def _():
            dwf_ref[...] = jnp.zeros(dwf_ref.shape, F32)
            loss_ref[...] = jnp.zeros(loss_ref.shape, F32)

        xv = x_ref[...]
        r = lax.rsqrt(jnp.mean(xv * xv, axis=-1, keepdims=True) + NORM_EPS)
        xn = xv * r
        err = xn * wf_ref[...] - tgt_ref[...]
        loss_ref[...] += 0.5 * jnp.sum(jnp.mean(err * err, axis=-1, keepdims=True), axis=0, keepdims=True)
        dy = err * (1.0 / D_MODEL)
        dwf_ref[...] += jnp.sum(dy * xn, axis=0, keepdims=True)
        dyw = dy * wf_ref[...]
        dxv = r * (dyw - xn * jnp.mean(dyw * xn, axis=-1, keepdims=True))
        dx_ref[...] = dxv
        dxb_ref[...] = dxv.astype(BF16)

    return _rowcall(name, body, T, tm, [(x, "row"), (wf, "full"), (tgt, "row")],
                    [((T, D_MODEL), F32, "row"), ((T, D_MODEL), BF16, "row"), ((1, D_MODEL), F32, "full"),
                     ((8, LANES), F32, "full")], 32)


def _peers():
    x, y, c = lax.axis_index("x"), lax.axis_index("y"), lax.axis_index("c")
    peers = []
    for p in range(1, N_DEV):
        px = 1 - x if p & 4 else x
        py = 1 - y if p & 2 else y
        pc = 1 - c if p & 1 else c
        peers.append((4 * px + 2 * py + pc, (px, py, pc)))
    return 4 * x + 2 * y + c, peers


def _all_gather(arrs, name):
    n = len(arrs)
    hbm = pl.BlockSpec(memory_space=pl.ANY)

    def body(*refs):
        starts, waits = _gather_copies(refs[:n], refs[n:2 * n], *refs[2 * n:])
        for cp in starts:
            cp.start()
        for wait in waits:
            wait()

    return pl.pallas_call(
        body,
        name=name,
        in_specs=[hbm] * n,
        out_specs=[hbm] * n,
        out_shape=[jax.ShapeDtypeStruct((N_DEV,) + a.shape, a.dtype) for a in arrs],
        scratch_shapes=_exchange_sems(n),
    )(*arrs)


def _gather_copies(ins, outs, send_sems, recv_sems, local_sems):
    me, peers = _peers()
    starts, waits = [], []
    for t in range(len(ins)):
        local = pltpu.make_async_copy(ins[t], outs[t].at[me], local_sems.at[t])
        starts.append(local)
        waits.append(local.wait)
        for k, (pidx, pid) in enumerate(peers):
            s = t * (N_DEV - 1) + k
            send = pltpu.make_async_remote_copy(
                src_ref=ins[t], dst_ref=outs[t].at[me], send_sem=send_sems.at[s], recv_sem=recv_sems.at[s],
                device_id=pid, device_id_type=MESH)
            arrival = pltpu.make_async_remote_copy(
                src_ref=ins[t], dst_ref=outs[t].at[pidx], send_sem=send_sems.at[s], recv_sem=recv_sems.at[s],
                device_id=pid, device_id_type=MESH)
            starts.append(send)
            waits += [arrival.wait_recv, send.wait_send]
    return starts, waits


def _exchange_sems(n):
    return [pltpu.SemaphoreType.DMA((n * (N_DEV - 1),)), pltpu.SemaphoreType.DMA((n * (N_DEV - 1),)),
            pltpu.SemaphoreType.DMA((n,))]


def _exchange_copies(ins, outs, send_sems, recv_sems, local_sems):
    me, peers = _peers()
    starts, waits = [], []
    for t in range(len(ins)):
        local = pltpu.make_async_copy(ins[t].at[me], outs[t].at[me], local_sems.at[t])
        starts.append(local)
        waits.append(local.wait)
        for k, (pidx, pid) in enumerate(peers):
            s = t * (N_DEV - 1) + k
            send = pltpu.make_async_remote_copy(
                src_ref=ins[t].at[pidx], dst_ref=outs[t].at[me], send_sem=send_sems.at[s],
                recv_sem=recv_sems.at[s], device_id=pid, device_id_type=MESH)
            arrival = pltpu.make_async_remote_copy(
                src_ref=ins[t].at[pidx], dst_ref=outs[t].at[pidx], send_sem=send_sems.at[s],
                recv_sem=recv_sems.at[s], device_id=pid, device_id_type=MESH)
            starts.append(send)
            waits += [arrival.wait_recv, send.wait_send]
    return starts, waits


def _exchange(arrs, name):
    n = len(arrs)
    hbm = pl.BlockSpec(memory_space=pl.ANY)

    def body(*refs):
        starts, waits = _exchange_copies(refs[:n], refs[n:2 * n], *refs[2 * n:])
        for cp in starts:
            cp.start()
        for wait in waits:
            wait()

    return pl.pallas_call(
        body,
        name=name,
        in_specs=[hbm] * n,
        out_specs=[hbm] * n,
        out_shape=[jax.ShapeDtypeStruct(a.shape, a.dtype) for a in arrs],
        scratch_shapes=_exchange_sems(n),
    )(*arrs)


def _row_tile(rows, cap):
    best = None
    for t in range(16, cap + 1, 16):
        if rows % t == 0:
            best = t
    assert best is not None, rows
    return best


def _adam_sum(parts, w, m, v, name):
    R, C = w.shape
    tr = _row_tile(R, 128)

    def body(p_ref, w_ref, m_ref, v_ref, g_ref, d_ref, nm_ref, nv_ref):
        g = p_ref[0].astype(F32)
        for s in range(1, N_DEV):
            g = g + p_ref[s].astype(F32)
        mm = ADAM_B1 * m_ref[...] + (1.0 - ADAM_B1) * g
        vv = ADAM_B2 * v_ref[...] + (1.0 - ADAM_B2) * (g * g)
        m_hat = mm / (1.0 - ADAM_B1 ** ADAM_STEP)
        v_hat = vv / (1.0 - ADAM_B2 ** ADAM_STEP)
        g_ref[...] = g
        d_ref[...] = -ADAM_LR * (m_hat / (jnp.sqrt(v_hat) + ADAM_EPS) + ADAM_WD * w_ref[...])
        nm_ref[...] = mm
        nv_ref[...] = vv

    mat = pl.BlockSpec((tr, C), lambda i: (i, 0))
    return pl.pallas_call(
        body,
        name=name,
        grid=(R // tr,),
        in_specs=[pl.BlockSpec((N_DEV, tr, C), lambda i: (0, i, 0)), mat, mat, mat],
        out_specs=[mat] * 4,
        out_shape=[jax.ShapeDtypeStruct((R, C), F32)] * 4,
        compiler_params=_params(("parallel",), 32),
    )(parts, w, m, v)


def _cols_from_gathered(g):
    _, L, K, n = g.shape
    return jnp.transpose(g, (1, 2, 0, 3)).reshape(L, K, N_DEV * n)


def _rows_from_gathered(g):
    _, L, k, N = g.shape
    return jnp.transpose(g, (1, 0, 2, 3)).reshape(L, N_DEV * k, N)


def _cols_to_blocks(dw):
    L, K, N = dw.shape
    n = N // N_DEV
    return jnp.transpose(dw.reshape(L, K, N_DEV, n), (2, 0, 1, 3)).reshape(N_DEV, L * K, n)


def _rows_to_blocks(dw):
    L, K, N = dw.shape
    k = K // N_DEV
    return jnp.transpose(dw.reshape(L, N_DEV, k, N), (1, 0, 2, 3)).reshape(N_DEV, L * k, N)


def _ffn_interleave(w):
    lead = w.shape[:-1]
    t = w.reshape(lead + (2, N_FFN_TILE, FFN_TILE))
    return jnp.swapaxes(t, -3, -2).reshape(lead + (2 * FFN_H,))


def _ffn_deinterleave(w):
    lead = w.shape[:-1]
    t = w.reshape(lead + (N_FFN_TILE, 2, FFN_TILE))
    return jnp.swapaxes(t, -3, -2).reshape(lead + (2 * FFN_H,))


def _pad_rows(a, rows=8):
    a = a.reshape(-1, a.shape[-1])
    return jnp.pad(a, ((0, rows - a.shape[0]), (0, 0)))


SMALL_ROWS = 6 * 8 + 2 * SGU_G * CHUNK * CHUNK // D_MODEL


def _pack_small(mixer, ffn, final, b_f, extra, w_s, b_s):
    bf_row = jnp.pad(b_f.reshape(1, -1), ((0, 0), (0, D_MODEL - b_f.size)))
    bs_rows = jnp.pad(b_s.reshape(4, -1), ((0, 0), (0, D_MODEL - b_s.size // 4)))
    return jnp.concatenate([
        _pad_rows(mixer), _pad_rows(ffn), _pad_rows(final.reshape(1, -1)), _pad_rows(bf_row),
        _pad_rows(extra), _pad_rows(bs_rows), w_s.reshape(-1, D_MODEL)], axis=0)


def _unpack_small(p):
    mixer, ffn, final = p[0:4], p[8:12], p[16]
    b_f = p[24, :32].reshape(2, 2 * N_PAIR)
    extra = p[32]
    b_s = p[40:44, :2 * SGU_G * CHUNK // 4].reshape(2, SGU_G, CHUNK)
    w_s = p[48:].reshape(2, SGU_G, CHUNK, CHUNK)
    return mixer, ffn, final, b_f, extra, w_s, b_s


def kernel(x, mixer_norm_w, attn_w_in, attn_b_f, attn_w_out, sgu_w_in, sgu_ln_g, sgu_ln_b, sgu_w_s, sgu_b_s, sgu_w_out, ffn_norm_w, ffn_w_in, ffn_w_out, final_norm_w, loss_target, m_mixer_norm_w, m_attn_w_in, m_attn_b_f, m_attn_w_out, m_sgu_w_in, m_sgu_ln_g, m_sgu_ln_b, m_sgu_w_s, m_sgu_b_s, m_sgu_w_out, m_ffn_norm_w, m_ffn_w_in, m_ffn_w_out, m_final_norm_w, v_mixer_norm_w, v_attn_w_in, v_attn_b_f, v_attn_w_out, v_sgu_w_in, v_sgu_ln_g, v_sgu_ln_b, v_sgu_w_s, v_sgu_b_s, v_sgu_w_out, v_ffn_norm_w, v_ffn_w_in, v_ffn_w_out, v_final_norm_w):
    T = x.shape[1]
    tb = min(ATT_BLOCK, T)
    xs, tgt = x[0], loss_target[0]

    tr = lambda w: jnp.swapaxes(w, -1, -2)
    bf = lambda a: a.astype(BF16)

    def attn_weights(g_in, g_out):
        w_in = _cols_from_gathered(g_in[:, None])
        w_qkv = w_in[:, :, :3 * D_MODEL]
        w_f = jnp.pad(w_in[:, :, 3 * D_MODEL:], ((0, 0), (0, 0), (0, LANES - 2 * N_PAIR)))
        w_out = _rows_from_gathered(g_out[:, None])
        return dict(qkv=w_qkv, f=w_f, qkv_t=tr(w_qkv).reshape(3, D_MODEL, D_MODEL), f_t=tr(w_f), out=w_out,
                    out_t=tr(w_out))

    attn_w = [attn_weights(*_all_gather([bf(attn_w_in[0]), bf(attn_w_out[0])], "gather_weights")), None]
    later = [bf(attn_w_in[1]), bf(attn_w_out[1]), bf(sgu_w_in), bf(sgu_w_out), bf(ffn_w_in), bf(ffn_w_out),
             sgu_ln_g, sgu_ln_b]
    mixer_nw = mixer_norm_w.reshape(4, 1, D_MODEL)
    ffn_nw = ffn_norm_w.reshape(4, 1, D_MODEL)
    b_col = attn_b_f.reshape(2, 2 * N_PAIR, 1)
    bs_t = jnp.swapaxes(sgu_b_s, 1, 2)

    saved = []
    xr = xs
    for i in range(4):
        j = i // 2
        if i % 2 == 0:
            qkv3, h = _norm_matmul(xr, mixer_nw[i], (attn_w[j]["qkv"], 0), BF16, f"attn_qkv_{j}", 1024, groups=True)
            fl, _ = _norm_matmul(xr, mixer_nw[i], (attn_w[j]["f"], 0), F32, f"attn_gate_{j}", LANES)
            fl3 = jnp.transpose(fl[:, :2 * N_PAIR].reshape(T // CHUNK, CHUNK, 2 * N_PAIR), (0, 2, 1))
            c_chunks = _fgate_fwd(fl3, b_col[j], f"fgate_fwd_{j}")
            c16 = jnp.transpose(c_chunks, (1, 0, 2)).reshape(2 * N_PAIR, T)
            cp3 = _bias_pieces(c16)
            first_kv, last_q = _attn_reach(qkv3, c16, tb, min(ATT_Q_BLOCK, T))
            fwd_out = _attn_fwd(qkv3, cp3, first_kv, f"attn_fwd_{j}", later if i == 0 else ())
            ot3, lse4 = fwd_out[:2]
            if i == 0:
                g = fwd_out[2:]
                attn_w[1] = attn_weights(g[0], g[1])
                w_sgu_in = _cols_from_gathered(g[2])
                w_sgu_out = _rows_from_gathered(g[3])
                w_ffn_in = _ffn_interleave(_cols_from_gathered(g[4]))
                w_ffn_out = _rows_from_gathered(g[5])
                ln_g = jnp.transpose(g[6], (1, 0, 2)).reshape(2, 1, SGU_W)
                ln_b = jnp.transpose(g[7], (1, 0, 2)).reshape(2, 1, SGU_W)
                w_sgu_in_t, w_sgu_out_t, w_ffn_in_t, w_ffn_out_t = (
                    tr(w_sgu_in), tr(w_sgu_out), tr(w_ffn_in), tr(w_ffn_out))
            o3 = jnp.swapaxes(ot3, 1, 2)
            xm = _attn_out(o3, (attn_w[j]["out"], 0), xr, f"attn_out_{j}")
            mix_saved = (xr, h, qkv3, fl3, cp3, o3, lse4, last_q)
        else:
            a, h = _norm_matmul(xr, mixer_nw[i], (w_sgu_in, j), BF16, f"sgu_in_{j}", 1024)
            xm, gated, vn, mixed = _sgu_core(a, ln_g[j], ln_b[j], sgu_w_s[j], bs_t[j], (w_sgu_out, j), xr,
                                             f"sgu_core_{j}")
            mix_saved = (xr, h, a, gated, vn, mixed)
        gu, h2 = _norm_matmul(xm, ffn_nw[i], (w_ffn_in, i), BF16, f"ffn_in_{i}", FFN_H // 2)
        xo, hid = _ffn_out(gu, (w_ffn_out, i), xm, f"ffn_out_{i}")
        saved.append((mix_saved, (xm, h2, gu, hid)))
        xr = xo
    dx, dxb, d_final, loss_part = _loss_head(xr, final_norm_w.reshape(1, D_MODEL), tgt, "loss_head")

    d_mixer_nw, d_ffn_nw = [None] * 4, [None] * 4
    d_attn_in, d_attn_out, d_bf, d_sgu_in, d_sgu_out = [None] * 2, [None] * 2, [None] * 2, [None] * 2, [None] * 2
    d_ws, d_bs, d_lg, d_lb = [None] * 2, [None] * 2, [None] * 2, [None] * 2
    d_ffn_in, d_ffn_out = [None] * 4, [None] * 4
    for i in reversed(range(4)):
        j = i // 2
        mix_saved, (xm, h2, gu, hid) = saved[i]
        dgu = _ffn_dgu(dxb, (w_ffn_out_t, i), gu, f"ffn_dgu_{i}")
        d_ffn_out[i] = _matmul_tn(hid, dxb, f"ffn_dwout_{i}", FFN_H // 2, D_MODEL)
        d_ffn_in[i] = _matmul_tn(h2, dgu, f"ffn_dwin_{i}", D_MODEL, FFN_H // 2)
        dx, dxb, d_ffn_nw[i] = _matmul_rms_bwd([(dgu, "row")], [(w_ffn_in_t, i)], xm, ffn_nw[i], dx, f"ffn_dx_{i}",
                                               512, 48)
        if i % 2 == 0:
            xr, h, qkv3, fl3, cp3, o3, lse4, last_q = mix_saved
            do3, dd = _attn_dout(dxb, (attn_w[j]["out_t"], 0), o3, f"attn_dout_{j}")
            d_attn_out[j] = _matmul_tn(o3, dxb, f"attn_dwout_{j}", D_MODEL, D_MODEL, a_grp=True)
            to_blocks = lambda a: jnp.swapaxes(a.reshape(N_PAIR, 2, T // tb, tb), 1, 2)
            from_blocks = lambda a: jnp.swapaxes(a, 1, 2).reshape(N_PAIR, 2, T)
            dd4 = to_blocks(dd.T.reshape(N_PAIR, 2, T))
            st4 = jnp.concatenate([to_blocks(from_blocks(lse4)), dd4], axis=2)
            kst4 = jnp.swapaxes((qkv3[N_PAIR:2 * N_PAIR] * QK_SCALE).reshape(N_PAIR, T // tb, tb, LANES), 2, 3)
            ride = []
            if i == 0:
                ln_blocks = lambda d: jnp.transpose(jnp.stack(d).reshape(2, N_DEV, 1, SGU_W // N_DEV),
                                                    (1, 0, 2, 3)).reshape(N_DEV, 2, SGU_W // N_DEV)
                ride = [b.astype(BF16) for b in (
                    _cols_to_blocks(d_attn_in[1][None]), _rows_to_blocks(jnp.stack(d_attn_out)),
                    _cols_to_blocks(jnp.stack(d_sgu_in)), _rows_to_blocks(jnp.stack(d_sgu_out)),
                    _cols_to_blocks(_ffn_deinterleave(jnp.stack(d_ffn_in))), _rows_to_blocks(jnp.stack(d_ffn_out)))]
                ride += [ln_blocks(d_lg), ln_blocks(d_lb)]
            bwd_out = _attn_bwd(qkv3, cp3, kst4, do3, st4, last_q, f"attn_bwd_{j}", ride)
            dqt4, rs4, dk3, dv3, cs4 = bwd_out[:5]
            if i == 0:
                received = bwd_out[5:]
            dq3 = jnp.swapaxes(dqt4, 2, 3).reshape(N_PAIR, T, LANES).astype(BF16)
            dc_pair = from_blocks(rs4 - jnp.stack([cs4[:, :, 0], cs4[:, :, 8]], axis=2))
            dc_chunks = jnp.transpose(dc_pair.reshape(2 * N_PAIR, T // CHUNK, CHUNK), (1, 0, 2))
            dfl3, db = _fgate_bwd(dc_chunks, fl3, b_col[j], f"fgate_bwd_{j}")
            d_bf[j] = db[:, 0]
            dfl = jnp.transpose(dfl3, (0, 2, 1)).reshape(T, 2 * N_PAIR)
            dfl = jnp.pad(dfl.astype(BF16), ((0, 0), (0, LANES - 2 * N_PAIR)))
            d_qkv = [_matmul_tn(h, d3, f"attn_dw{nm}_{j}", D_MODEL, D_MODEL, g_grp=True)
                     for nm, d3 in (("q", dq3), ("k", dk3), ("v", dv3))]
            d_f = _matmul_tn(h, dfl, f"attn_dwf_{j}", D_MODEL, LANES)[:, :2 * N_PAIR]
            d_attn_in[j] = jnp.concatenate(d_qkv + [d_f], axis=1)
            wts = [(attn_w[j]["qkv_t"], k) for k in range(3)] + [(attn_w[j]["f_t"], 0)]
            dx, dxb, d_mixer_nw[i] = _matmul_rms_bwd(
                [(dq3, "grp"), (dk3, "grp"), (dv3, "grp"), (dfl, "row")], wts, xr, mixer_nw[i], dx,
                f"attn_dx_{j}", 256, 40)
        else:
            xr, h, a, gated, vn, mixed = mix_saved
            da, d_ws[j], dba, d_lg[j], d_lb[j] = _sgu_core_bwd(dxb, (w_sgu_out_t, j), a, vn, mixed, ln_g[j], sgu_w_s[j],
                                                               f"sgu_core_bwd_{j}")
            d_bs[j] = jnp.sum(dba.reshape(CHUNK, SGU_G, CHUNK), axis=-1).T
            d_sgu_out[j] = _matmul_tn(gated, dxb, f"sgu_dwout_{j}", D_MODEL, D_MODEL)
            d_sgu_in[j] = _matmul_tn(h, da, f"sgu_dwin_{j}", D_MODEL, 1024)
            dx, dxb, d_mixer_nw[i] = _matmul_rms_bwd([(da, "row")], [(w_sgu_in_t, j)], xr, mixer_nw[i], dx,
                                                f"sgu_dx_{j}", 256, 40)
    grad_x = dx[None]

    rows4 = lambda parts: jnp.concatenate(parts, axis=1).reshape(4, D_MODEL)
    small_g = _pack_small(rows4(d_mixer_nw), rows4(d_ffn_nw), d_final[0], jnp.stack(d_bf),
                          loss_part[0:1, 0:1] * jnp.ones((1, D_MODEL), F32), jnp.stack(d_ws), jnp.stack(d_bs))
    zero_row = jnp.zeros((1, D_MODEL), F32)
    pack = lambda pre: _pack_small(pre[0], pre[1], pre[2], pre[3], zero_row, pre[4], pre[5])
    small_w = pack((mixer_norm_w, ffn_norm_w, final_norm_w, attn_b_f, sgu_w_s, sgu_b_s))
    small_m = pack((m_mixer_norm_w, m_ffn_norm_w, m_final_norm_w, m_attn_b_f, m_sgu_w_s, m_sgu_b_s))
    small_v = pack((v_mixer_norm_w, v_ffn_norm_w, v_final_norm_w, v_attn_b_f, v_sgu_w_s, v_sgu_b_s))
    small_all = _all_gather([small_g], "gather_small_grads")[0]
    small_out = [_unpack_small(p) for p in _adam_sum(small_all, small_w, small_m, small_v, "adam_small")]
    loss = small_out[0][4][0]

    late = _exchange([_cols_to_blocks(d_attn_in[0][None]).astype(BF16)], "exchange_grads")[0]
    names = ["attn_w_out", "sgu_w_in", "sgu_w_out", "ffn_w_in", "ffn_w_out"]
    ws = [attn_w_out, sgu_w_in, sgu_w_out, ffn_w_in, ffn_w_out]
    ms = [m_attn_w_out, m_sgu_w_in, m_sgu_w_out, m_ffn_w_in, m_ffn_w_out]
    vs = [v_attn_w_out, v_sgu_w_in, v_sgu_w_out, v_ffn_w_in, v_ffn_w_out]
    big_out = {}
    for nm, rec, w, m, v in zip(names, received[1:6], ws, ms, vs):
        flat = lambda a: a.reshape(-1, a.shape[-1])
        big_out[nm] = [o.reshape(w.shape) for o in _adam_sum(rec, flat(w), flat(m), flat(v), f"adam_{nm}")]
    per_layer = [_adam_sum(rec, attn_w_in[l], m_attn_w_in[l], v_attn_w_in[l], f"adam_attn_w_in_{l}")
                 for l, rec in enumerate((late, received[0]))]
    big_out["attn_w_in"] = [jnp.stack([per_layer[0][k], per_layer[1][k]]) for k in range(4)]
    pad8 = lambda a: jnp.pad(a, [(0, 0)] * (a.ndim - 2) + [(0, 8 - a.shape[-2]), (0, 0)])
    ln_parts = jnp.concatenate([pad8(received[6]), pad8(received[7])], axis=1)
    ln_pack = lambda g, b: jnp.concatenate([pad8(g), pad8(b)], axis=0)
    ln_out = _adam_sum(ln_parts, ln_pack(sgu_ln_g, sgu_ln_b), ln_pack(m_sgu_ln_g, m_sgu_ln_b),
                       ln_pack(v_sgu_ln_g, v_sgu_ln_b), "adam_sgu_ln")

    def leaf(kind):
        mixer, ffn, final, b_f, _, w_s, b_s = small_out[kind]
        o = lambda nm: big_out[nm][kind]
        return [mixer, o("attn_w_in"), b_f, o("attn_w_out"), o("sgu_w_in"), ln_out[kind][0:2], ln_out[kind][8:10],
                w_s, b_s, o("sgu_w_out"), ffn, o("ffn_w_in"), o("ffn_w_out"), final]

    return (loss, grad_x, *leaf(0), *leaf(1), *leaf(2), *leaf(3))
```
